```python
import math
import jax
import jax.numpy as jnp
from jax import lax
import numpy as np


D_MODEL = 2048
BATCH = 8
SEQ = 4096
DEPTH = 2

D_MIX = D_MODEL
N_MIXERS = 4
GROUP_WIDTH = D_MIX // N_MIXERS
HEAD_DIM = 128
N_HEADS = GROUP_WIDTH // HEAD_DIM
CHUNK = 64
SHORT_CONV = 4
MIX_CHUNK = 128
CONV_WIDTH = 31
D_FF = 4 * D_MODEL
D_IN_PROJ = 12 * GROUP_WIDTH + 2 * N_HEADS
EPS = 1e-6
NEG_BIG = -1e30
TINY = 1e-30

kernel_name = 'hybrid_parallel_group_trunk'


def rms_norm(x, w):
    xf = x.astype(jnp.float32)
    y = xf * lax.rsqrt(jnp.mean(xf * xf, axis=-1, keepdims=True) + EPS)
    return (y * w.astype(jnp.float32)).astype(x.dtype)


def layer_norm(x, w, b):
    xf = x.astype(jnp.float32)
    mu = jnp.mean(xf, axis=-1, keepdims=True)
    var = jnp.mean(jnp.square(xf - mu), axis=-1, keepdims=True)
    y = (xf - mu) * lax.rsqrt(var + EPS)
    return (y * w.astype(jnp.float32) + b.astype(jnp.float32)).astype(x.dtype)


def l2_norm(x):
    return x * lax.rsqrt(jnp.sum(x * x, axis=-1, keepdims=True) + EPS)


def causal_dwconv(x, w):
    width, ch = w.shape
    return lax.conv_general_dilated(
        x, w[:, None, :].astype(x.dtype), window_strides=(1,), padding=[(width - 1, 0)],
        dimension_numbers=('NWC', 'WIO', 'NWC'), feature_group_count=ch)


def split_heads(t):
    return t.reshape(t.shape[0], t.shape[1], N_HEADS, HEAD_DIM)


def split_in_proj(p):
    gw = GROUP_WIDTH
    sizes = [gw] * 4 + [gw] * 4 + [N_HEADS, N_HEADS] + [gw] * 2 + [gw] * 2
    points, acc = [], 0
    for s in sizes[:-1]:
        acc += s
        points.append(acc)
    return jnp.split(p, points, axis=-1)


def hgrn2_recurrence(q, k, v, log_f):
    bsz, seq, nh, dk = q.shape
    dv = v.shape[-1]
    n = seq // CHUNK

    def chunks(t):
        return t.reshape(bsz, n, CHUNK, nh, t.shape[-1]).transpose(1, 0, 3, 2, 4)

    qc, kc, vc = chunks(q), chunks(k), chunks(v)
    bc = jnp.cumsum(chunks(log_f), axis=3)
    causal = jnp.tril(jnp.ones((CHUNK, CHUNK), dtype=bool))[:, :, None]

    def step(state, inp):
        q_c, k_c, v_c, b_c = inp
        rel = b_c[:, :, :, None, :] - b_c[:, :, None, :, :]
        decay = jnp.exp(jnp.where(causal, rel, NEG_BIG))
        scores = jnp.einsum('bhijd,bhjd->bhij', decay * q_c[:, :, :, None, :], k_c)
        b_end = b_c[:, :, -1:, :]
        out = (jnp.einsum('bhij,bhje->bhie', scores, v_c)
               + jnp.einsum('bhid,bhde->bhie', q_c * jnp.exp(b_c), state))
        state = (state * jnp.exp(b_end)[:, :, 0, :, None]
                 + jnp.einsum('bhjd,bhje->bhde', k_c * jnp.exp(b_end - b_c), v_c))
        return state, out

    state0 = jnp.zeros((bsz, nh, dk, dv), jnp.float32)
    _, out = lax.scan(step, state0, (qc, kc, vc, bc))
    return out.transpose(1, 0, 3, 2, 4).reshape(bsz, seq, nh, dv)


def gated_delta_rule(q, k, v, beta, g):
    bsz, seq, nh, dk = q.shape
    dv = v.shape[-1]
    n = seq // CHUNK

    def chunks(t):
        t = t.reshape((bsz, n, CHUNK, nh) + t.shape[3:])
        return jnp.moveaxis(t, 3, 1)

    qc, kc, vc = chunks(q), chunks(k), chunks(v)
    bc = chunks(beta)
    gc = jnp.cumsum(chunks(g), axis=-1)
    idx = jnp.arange(CHUNK)
    incl = idx[:, None] >= idx[None, :]
    strict = idx[:, None] > idx[None, :]
    gamma = jnp.exp(jnp.where(incl, gc[..., :, None] - gc[..., None, :], NEG_BIG))
    k_beta = kc * bc[..., None]
    m = jnp.where(strict, jnp.einsum('bhnid,bhnjd->bhnij', k_beta, kc) * gamma, 0.0)
    t_mat = m + jnp.eye(CHUNK, dtype=m.dtype)

    def solve(rhs):
        return lax.linalg.triangular_solve(t_mat, rhs, left_side=True, lower=True, unit_diagonal=True)

    u = solve(vc * bc[..., None])
    w = solve(k_beta * jnp.exp(gc)[..., None])
    qk = jnp.einsum('bhnid,bhnjd->bhnij', qc, kc) * gamma
    q_dec = qc * jnp.exp(gc)[..., None]
    k_dec = kc * jnp.exp(gc[..., -1:] - gc)[..., None]
    g_end = jnp.exp(gc[..., -1])

    def step(state, inp):
        u_c, w_c, qk_c, qd_c, kd_c, ge_c = inp
        v_new = u_c - jnp.einsum('bhik,bhkv->bhiv', w_c, state)
        out = (jnp.einsum('bhik,bhkv->bhiv', qd_c, state)
               + jnp.einsum('bhij,bhjv->bhiv', qk_c, v_new))
        state = state * ge_c[..., None, None] + jnp.einsum('bhjk,bhjv->bhkv', kd_c, v_new)
        return state, out

    xs = tuple(jnp.moveaxis(t, 2, 0) for t in (u, w, qk, q_dec, k_dec, g_end))
    _, out = lax.scan(step, jnp.zeros((bsz, nh, dk, dv), jnp.float32), xs)
    return out.transpose(1, 0, 3, 2, 4).reshape(bsz, seq, nh, dv)


def spatial_gating(u, v, ln_w, ln_b, w_s, b_s):
    bsz, seq, _ = v.shape
    n = seq // MIX_CHUNK
    v = layer_norm(v, ln_w, ln_b).reshape(bsz, n, MIX_CHUNK, N_HEADS, HEAD_DIM)
    w_causal = jnp.where(jnp.tril(jnp.ones((MIX_CHUNK, MIX_CHUNK), dtype=bool)), w_s, 0.0)
    mixed = jnp.einsum('hij,bnjhd->bnihd', w_causal, v) + b_s.T[:, :, None]
    return u * mixed.reshape(bsz, seq, GROUP_WIDTH)


def conformer_conv(a, gate, dw_w, dw_b, ln_w, ln_b):
    y = a * jax.nn.sigmoid(gate)
    y = causal_dwconv(y, dw_w) + dw_b
    return jax.nn.silu(layer_norm(y, ln_w, ln_b))


def _fwd_setup_inputs(seed: int = 0) -> dict:
    key = jax.random.key(seed)
    ks = jax.random.split(key, 24)
    f32 = jnp.float32

    def normal(k, shape, scale):
        return jax.random.normal(k, shape, f32) * scale

    def gain(k, shape):
        return 1.0 + 0.02 * jax.random.normal(k, shape, f32)

    dt = jnp.exp(jax.random.uniform(ks[10], (DEPTH, N_HEADS), f32, math.log(1e-3), math.log(1e-1)))
    return {
        'x': normal(ks[0], (BATCH, SEQ, D_MODEL), 1.0),
        'lower_bounds': normal(ks[1], (DEPTH, GROUP_WIDTH), 0.1),
        'norm_mix_pre': gain(ks[2], (DEPTH, D_MODEL)),
        'norm_mix_post': gain(ks[3], (DEPTH, D_MODEL)),
        'norm_ff_pre': gain(ks[4], (DEPTH, D_MODEL)),
        'norm_ff_post': gain(ks[5], (DEPTH, D_MODEL)),
        'w_in': normal(ks[6], (DEPTH, D_MODEL, D_IN_PROJ), D_MODEL ** -0.5),
        'w_out': normal(ks[7], (DEPTH, D_MIX, D_MODEL), D_MIX ** -0.5),
        'hgrn_norm_w': gain(ks[8], (DEPTH, HEAD_DIM)),
        'gdn_conv_w': normal(ks[9], (DEPTH, SHORT_CONV, 3 * GROUP_WIDTH), SHORT_CONV ** -0.5),
        'gdn_a_log': jnp.log(jax.random.uniform(ks[11], (DEPTH, N_HEADS), f32, 1.0, 16.0)),
        'gdn_dt_bias': dt + jnp.log(-jnp.expm1(-dt)),
        'gdn_norm_w': gain(ks[12], (DEPTH, HEAD_DIM)),
        'gmlp_ln_w': gain(ks[13], (DEPTH, GROUP_WIDTH)),
        'gmlp_ln_b': normal(ks[14], (DEPTH, GROUP_WIDTH), 0.02),
        'gmlp_w_s': normal(ks[15], (DEPTH, N_HEADS, MIX_CHUNK, MIX_CHUNK), MIX_CHUNK ** -0.5),
        'gmlp_b_s': gain(ks[16], (DEPTH, N_HEADS, MIX_CHUNK)),
        'conv_dw_w': normal(ks[17], (DEPTH, CONV_WIDTH, GROUP_WIDTH), CONV_WIDTH ** -0.5),
        'conv_dw_b': normal(ks[18], (DEPTH, GROUP_WIDTH), 0.02),
        'conv_ln_w': gain(ks[19], (DEPTH, GROUP_WIDTH)),
        'conv_ln_b': normal(ks[20], (DEPTH, GROUP_WIDTH), 0.02),
        'w_ff1': normal(ks[21], (DEPTH, D_MODEL, D_FF), D_MODEL ** -0.5),
        'w_ff2': normal(ks[22], (DEPTH, D_FF, D_MODEL), D_FF ** -0.5),
    }


def _fwd_reference(x, lower_bounds, norm_mix_pre, norm_mix_post, norm_ff_pre, norm_ff_post,
              w_in, w_out, hgrn_norm_w, gdn_conv_w, gdn_a_log, gdn_dt_bias, gdn_norm_w,
              gmlp_ln_w, gmlp_ln_b, gmlp_w_s, gmlp_b_s, conv_dw_w, conv_dw_b, conv_ln_w,
              conv_ln_b, w_ff1, w_ff2):
    f32 = jnp.float32
    bsz, seq, _ = x.shape
    lb_soft = jax.nn.softmax(lower_bounds.astype(f32), axis=0)
    lb_all = jnp.cumsum(lb_soft, axis=0) - lb_soft[0]

    for l in range(DEPTH):
        h = rms_norm(x, norm_mix_pre[l])
        proj = (h @ w_in[l]).astype(f32)
        (a_q, a_f, a_i, a_g, b_q, b_k, b_v, b_z, b_beta, b_a,
         c_u, c_v, d_a, d_gate) = split_in_proj(proj)

        lb = lb_all[l]
        f_a = lb + (1.0 - lb) * jax.nn.sigmoid(a_f)
        log_f = jnp.log(jnp.maximum(f_a, TINY))
        k_a = (1.0 - lb) * jax.nn.sigmoid(-a_f)
        q_a = jax.nn.silu(a_q)
        o_a = hgrn2_recurrence(split_heads(q_a), split_heads(k_a), split_heads(a_i), split_heads(log_f))
        o_a = rms_norm(o_a, hgrn_norm_w[l]).reshape(bsz, seq, GROUP_WIDTH) * jax.nn.silu(a_g)

        qkv = jax.nn.silu(causal_dwconv(jnp.concatenate([b_q, b_k, b_v], axis=-1), gdn_conv_w[l]))
        q_b, k_b, v_b = jnp.split(qkv, 3, axis=-1)
        q_b = l2_norm(split_heads(q_b)) * (HEAD_DIM ** -0.5)
        k_b = l2_norm(split_heads(k_b))
        beta = jax.nn.sigmoid(b_beta)
        g_b = -jnp.exp(gdn_a_log[l].astype(f32)) * jax.nn.softplus(b_a + gdn_dt_bias[l].astype(f32))
        o_b = gated_delta_rule(q_b, k_b, split_heads(v_b), beta, g_b)
        o_b = rms_norm(o_b, gdn_norm_w[l]).reshape(bsz, seq, GROUP_WIDTH) * jax.nn.silu(b_z)

        o_c = spatial_gating(jax.nn.gelu(c_u, approximate=False), jax.nn.gelu(c_v, approximate=False),
                             gmlp_ln_w[l], gmlp_ln_b[l], gmlp_w_s[l].astype(f32), gmlp_b_s[l].astype(f32))

        o_d = conformer_conv(d_a, d_gate, conv_dw_w[l], conv_dw_b[l], conv_ln_w[l], conv_ln_b[l])

        mix = jnp.concatenate([o_a, o_b, o_c, o_d], axis=-1).astype(x.dtype)
        x = x + rms_norm(mix @ w_out[l], norm_mix_post[l])

        h = rms_norm(x, norm_ff_pre[l])
        y = jnp.square(jax.nn.relu(h @ w_ff1[l])) @ w_ff2[l]
        x = x + rms_norm(y, norm_ff_post[l])
    return x


import jax as _jax
import jax.numpy as _jnp

TWIN_FORMAT = 'train_step'
FWD_PARAMS = ['x', 'lower_bounds', 'norm_mix_pre', 'norm_mix_post', 'norm_ff_pre', 'norm_ff_post', 'w_in', 'w_out', 'hgrn_norm_w', 'gdn_conv_w', 'gdn_a_log', 'gdn_dt_bias', 'gdn_norm_w', 'gmlp_ln_w', 'gmlp_ln_b', 'gmlp_w_s', 'gmlp_b_s', 'conv_dw_w', 'conv_dw_b', 'conv_ln_w', 'conv_ln_b', 'w_ff1', 'w_ff2']
TWIN_WEIGHTS = ['lower_bounds', 'norm_mix_pre', 'norm_mix_post', 'norm_ff_pre', 'norm_ff_post', 'w_in', 'w_out', 'hgrn_norm_w', 'gdn_conv_w', 'gdn_a_log', 'gdn_dt_bias', 'gdn_norm_w', 'gmlp_ln_w', 'gmlp_ln_b', 'gmlp_w_s', 'gmlp_b_s', 'conv_dw_w', 'conv_dw_b', 'conv_ln_w', 'conv_ln_b', 'w_ff1', 'w_ff2']
TWIN_DIFF_INPUT = 'x'
TWIN_INPUTS = ['x', 'lower_bounds', 'norm_mix_pre', 'norm_mix_post', 'norm_ff_pre', 'norm_ff_post', 'w_in', 'w_out', 'hgrn_norm_w', 'gdn_conv_w', 'gdn_a_log', 'gdn_dt_bias', 'gdn_norm_w', 'gmlp_ln_w', 'gmlp_ln_b', 'gmlp_w_s', 'gmlp_b_s', 'conv_dw_w', 'conv_dw_b', 'conv_ln_w', 'conv_ln_b', 'w_ff1', 'w_ff2', 'loss_target', 'm_lower_bounds', 'm_norm_mix_pre', 'm_norm_mix_post', 'm_norm_ff_pre', 'm_norm_ff_post', 'm_w_in', 'm_w_out', 'm_hgrn_norm_w', 'm_gdn_conv_w', 'm_gdn_a_log', 'm_gdn_dt_bias', 'm_gdn_norm_w', 'm_gmlp_ln_w', 'm_gmlp_ln_b', 'm_gmlp_w_s', 'm_gmlp_b_s', 'm_conv_dw_w', 'm_conv_dw_b', 'm_conv_ln_w', 'm_conv_ln_b', 'm_w_ff1', 'm_w_ff2', 'v_lower_bounds', 'v_norm_mix_pre', 'v_norm_mix_post', 'v_norm_ff_pre', 'v_norm_ff_post', 'v_w_in', 'v_w_out', 'v_hgrn_norm_w', 'v_gdn_conv_w', 'v_gdn_a_log', 'v_gdn_dt_bias', 'v_gdn_norm_w', 'v_gmlp_ln_w', 'v_gmlp_ln_b', 'v_gmlp_w_s', 'v_gmlp_b_s', 'v_conv_dw_w', 'v_conv_dw_b', 'v_conv_ln_w', 'v_conv_ln_b', 'v_w_ff1', 'v_w_ff2']
TWIN_OUTPUTS = ['loss', 'grad_x', 'grad_lower_bounds', 'grad_norm_mix_pre', 'grad_norm_mix_post', 'grad_norm_ff_pre', 'grad_norm_ff_post', 'grad_w_in', 'grad_w_out', 'grad_hgrn_norm_w', 'grad_gdn_conv_w', 'grad_gdn_a_log', 'grad_gdn_dt_bias', 'grad_gdn_norm_w', 'grad_gmlp_ln_w', 'grad_gmlp_ln_b', 'grad_gmlp_w_s', 'grad_gmlp_b_s', 'grad_conv_dw_w', 'grad_conv_dw_b', 'grad_conv_ln_w', 'grad_conv_ln_b', 'grad_w_ff1', 'grad_w_ff2', 'delta_lower_bounds', 'delta_norm_mix_pre', 'delta_norm_mix_post', 'delta_norm_ff_pre', 'delta_norm_ff_post', 'delta_w_in', 'delta_w_out', 'delta_hgrn_norm_w', 'delta_gdn_conv_w', 'delta_gdn_a_log', 'delta_gdn_dt_bias', 'delta_gdn_norm_w', 'delta_gmlp_ln_w', 'delta_gmlp_ln_b', 'delta_gmlp_w_s', 'delta_gmlp_b_s', 'delta_conv_dw_w', 'delta_conv_dw_b', 'delta_conv_ln_w', 'delta_conv_ln_b', 'delta_w_ff1', 'delta_w_ff2', 'new_m_lower_bounds', 'new_m_norm_mix_pre', 'new_m_norm_mix_post', 'new_m_norm_ff_pre', 'new_m_norm_ff_post', 'new_m_w_in', 'new_m_w_out', 'new_m_hgrn_norm_w', 'new_m_gdn_conv_w', 'new_m_gdn_a_log', 'new_m_gdn_dt_bias', 'new_m_gdn_norm_w', 'new_m_gmlp_ln_w', 'new_m_gmlp_ln_b', 'new_m_gmlp_w_s', 'new_m_gmlp_b_s', 'new_m_conv_dw_w', 'new_m_conv_dw_b', 'new_m_conv_ln_w', 'new_m_conv_ln_b', 'new_m_w_ff1', 'new_m_w_ff2', 'new_v_lower_bounds', 'new_v_norm_mix_pre', 'new_v_norm_mix_post', 'new_v_norm_ff_pre', 'new_v_norm_ff_post', 'new_v_w_in', 'new_v_w_out', 'new_v_hgrn_norm_w', 'new_v_gdn_conv_w', 'new_v_gdn_a_log', 'new_v_gdn_dt_bias', 'new_v_gdn_norm_w', 'new_v_gmlp_ln_w', 'new_v_gmlp_ln_b', 'new_v_gmlp_w_s', 'new_v_gmlp_b_s', 'new_v_conv_dw_w', 'new_v_conv_dw_b', 'new_v_conv_ln_w', 'new_v_conv_ln_b', 'new_v_w_ff1', 'new_v_w_ff2']
TWIN_LEAF_KINDS = {'loss': 'loss', 'grad_x': 'grad_x', 'grad_lower_bounds': 'grad_w', 'grad_norm_mix_pre': 'grad_w', 'grad_norm_mix_post': 'grad_w', 'grad_norm_ff_pre': 'grad_w', 'grad_norm_ff_post': 'grad_w', 'grad_w_in': 'grad_w', 'grad_w_out': 'grad_w', 'grad_hgrn_norm_w': 'grad_w', 'grad_gdn_conv_w': 'grad_w', 'grad_gdn_a_log': 'grad_w', 'grad_gdn_dt_bias': 'grad_w', 'grad_gdn_norm_w': 'grad_w', 'grad_gmlp_ln_w': 'grad_w', 'grad_gmlp_ln_b': 'grad_w', 'grad_gmlp_w_s': 'grad_w', 'grad_gmlp_b_s': 'grad_w', 'grad_conv_dw_w': 'grad_w', 'grad_conv_dw_b': 'grad_w', 'grad_conv_ln_w': 'grad_w', 'grad_conv_ln_b': 'grad_w', 'grad_w_ff1': 'grad_w', 'grad_w_ff2': 'grad_w', 'delta_lower_bounds': 'delta_w', 'delta_norm_mix_pre': 'delta_w', 'delta_norm_mix_post': 'delta_w', 'delta_norm_ff_pre': 'delta_w', 'delta_norm_ff_post': 'delta_w', 'delta_w_in': 'delta_w', 'delta_w_out': 'delta_w', 'delta_hgrn_norm_w': 'delta_w', 'delta_gdn_conv_w': 'delta_w', 'delta_gdn_a_log': 'delta_w', 'delta_gdn_dt_bias': 'delta_w', 'delta_gdn_norm_w': 'delta_w', 'delta_gmlp_ln_w': 'delta_w', 'delta_gmlp_ln_b': 'delta_w', 'delta_gmlp_w_s': 'delta_w', 'delta_gmlp_b_s': 'delta_w', 'delta_conv_dw_w': 'delta_w', 'delta_conv_dw_b': 'delta_w', 'delta_conv_ln_w': 'delta_w', 'delta_conv_ln_b': 'delta_w', 'delta_w_ff1': 'delta_w', 'delta_w_ff2': 'delta_w', 'new_m_lower_bounds': 'new_m', 'new_m_norm_mix_pre': 'new_m', 'new_m_norm_mix_post': 'new_m', 'new_m_norm_ff_pre': 'new_m', 'new_m_norm_ff_post': 'new_m', 'new_m_w_in': 'new_m', 'new_m_w_out': 'new_m', 'new_m_hgrn_norm_w': 'new_m', 'new_m_gdn_conv_w': 'new_m', 'new_m_gdn_a_log': 'new_m', 'new_m_gdn_dt_bias': 'new_m', 'new_m_gdn_norm_w': 'new_m', 'new_m_gmlp_ln_w': 'new_m', 'new_m_gmlp_ln_b': 'new_m', 'new_m_gmlp_w_s': 'new_m', 'new_m_gmlp_b_s': 'new_m', 'new_m_conv_dw_w': 'new_m', 'new_m_conv_dw_b': 'new_m', 'new_m_conv_ln_w': 'new_m', 'new_m_conv_ln_b': 'new_m', 'new_m_w_ff1': 'new_m', 'new_m_w_ff2': 'new_m', 'new_v_lower_bounds': 'new_v', 'new_v_norm_mix_pre': 'new_v', 'new_v_norm_mix_post': 'new_v', 'new_v_norm_ff_pre': 'new_v', 'new_v_norm_ff_post': 'new_v', 'new_v_w_in': 'new_v', 'new_v_w_out': 'new_v', 'new_v_hgrn_norm_w': 'new_v', 'new_v_gdn_conv_w': 'new_v', 'new_v_gdn_a_log': 'new_v', 'new_v_gdn_dt_bias': 'new_v', 'new_v_gdn_norm_w': 'new_v', 'new_v_gmlp_ln_w': 'new_v', 'new_v_gmlp_ln_b': 'new_v', 'new_v_gmlp_w_s': 'new_v', 'new_v_gmlp_b_s': 'new_v', 'new_v_conv_dw_w': 'new_v', 'new_v_conv_dw_b': 'new_v', 'new_v_conv_ln_w': 'new_v', 'new_v_conv_ln_b': 'new_v', 'new_v_w_ff1': 'new_v', 'new_v_w_ff2': 'new_v'}


def _forward(args):
    return _fwd_reference(*[args[k] for k in FWD_PARAMS])


def _output_shape():
    def fwd():
        inp = _fwd_setup_inputs(0)
        return _fwd_reference(*[inp[k] for k in FWD_PARAMS])
    out = _jax.eval_shape(fwd)
    return out.shape, out.dtype

N_MICROBATCH = 1
ADAM_LR = 0.001
ADAM_B1 = 0.9
ADAM_B2 = 0.999
ADAM_EPS = 1e-08
ADAM_WD = 0.01
ADAM_STEP = 10
PER_EXAMPLE_BATCH_AXIS = {'x': 0, 'loss_target': 0}
SHARED_INPUTS = []
_WEIGHT_DTYPES = {'lower_bounds': _jnp.float32, 'norm_mix_pre': _jnp.float32, 'norm_mix_post': _jnp.float32, 'norm_ff_pre': _jnp.float32, 'norm_ff_post': _jnp.float32, 'w_in': _jnp.float32, 'w_out': _jnp.float32, 'hgrn_norm_w': _jnp.float32, 'gdn_conv_w': _jnp.float32, 'gdn_a_log': _jnp.float32, 'gdn_dt_bias': _jnp.float32, 'gdn_norm_w': _jnp.float32, 'gmlp_ln_w': _jnp.float32, 'gmlp_ln_b': _jnp.float32, 'gmlp_w_s': _jnp.float32, 'gmlp_b_s': _jnp.float32, 'conv_dw_w': _jnp.float32, 'conv_dw_b': _jnp.float32, 'conv_ln_w': _jnp.float32, 'conv_ln_b': _jnp.float32, 'w_ff1': _jnp.float32, 'w_ff2': _jnp.float32}
MOMENT_SCALE = {'lower_bounds': 3.447601e-02, 'norm_mix_pre': 1.946431e+00, 'norm_mix_post': 1.671623e+01, 'norm_ff_pre': 1.993021e+00, 'norm_ff_post': 1.741676e+01, 'w_in': 1.122573e+00, 'w_out': 4.201251e+00, 'hgrn_norm_w': 4.418057e+00, 'gdn_conv_w': 1.126176e+00, 'gdn_a_log': 1.536742e+00, 'gdn_dt_bias': 1.493960e+00, 'gdn_norm_w': 5.182758e+00, 'gmlp_ln_w': 2.687078e-01, 'gmlp_ln_b': 3.014040e-01, 'gmlp_w_s': 2.185895e-01, 'gmlp_b_s': 6.052636e-01, 'conv_dw_w': 2.089292e+00, 'conv_dw_b': 1.811347e+01, 'conv_ln_w': 7.359408e+00, 'conv_ln_b': 1.023677e+01, 'w_ff1': 1.020965e+00, 'w_ff2': 5.039630e+00}


def _to_microbatches(a, axis):
    t = _jnp.moveaxis(a, axis, 0)
    t = t.reshape((N_MICROBATCH, t.shape[0] // N_MICROBATCH) + t.shape[1:])
    return _jnp.moveaxis(t, 1, axis + 1)


def setup_inputs(seed: int = 0) -> dict:
    inp = _fwd_setup_inputs(seed)
    key = _jax.random.fold_in(_jax.random.key(seed), 7919)
    shape, _ = _output_shape()
    out = dict(inp)
    out["loss_target"] = _jax.random.normal(_jax.random.fold_in(key, 0), shape, _jnp.float32)
    for i, name in enumerate(TWIN_WEIGHTS):
        w = inp[name].astype(_jnp.float32)
        if MOMENT_SCALE is None:
            s = _jnp.sqrt(_jnp.mean(_jnp.square(w)) + 1e-30)
        else:
            s = MOMENT_SCALE[name]
        km, kv = _jax.random.split(_jax.random.fold_in(key, i + 1))
        out[name] = w
        out["m_" + name] = s * _jax.random.normal(km, w.shape, _jnp.float32)
        out["v_" + name] = (s * s) * _jax.random.uniform(kv, w.shape, _jnp.float32, 0.5, 1.5)
    if N_MICROBATCH > 1:
        for name, axis in PER_EXAMPLE_BATCH_AXIS.items():
            out[name] = _to_microbatches(out[name], axis)
    return {'x': out['x'], 'lower_bounds': out['lower_bounds'], 'norm_mix_pre': out['norm_mix_pre'], 'norm_mix_post': out['norm_mix_post'], 'norm_ff_pre': out['norm_ff_pre'], 'norm_ff_post': out['norm_ff_post'], 'w_in': out['w_in'], 'w_out': out['w_out'], 'hgrn_norm_w': out['hgrn_norm_w'], 'gdn_conv_w': out['gdn_conv_w'], 'gdn_a_log': out['gdn_a_log'], 'gdn_dt_bias': out['gdn_dt_bias'], 'gdn_norm_w': out['gdn_norm_w'], 'gmlp_ln_w': out['gmlp_ln_w'], 'gmlp_ln_b': out['gmlp_ln_b'], 'gmlp_w_s': out['gmlp_w_s'], 'gmlp_b_s': out['gmlp_b_s'], 'conv_dw_w': out['conv_dw_w'], 'conv_dw_b': out['conv_dw_b'], 'conv_ln_w': out['conv_ln_w'], 'conv_ln_b': out['conv_ln_b'], 'w_ff1': out['w_ff1'], 'w_ff2': out['w_ff2'], 'loss_target': out['loss_target'], 'm_lower_bounds': out['m_lower_bounds'], 'm_norm_mix_pre': out['m_norm_mix_pre'], 'm_norm_mix_post': out['m_norm_mix_post'], 'm_norm_ff_pre': out['m_norm_ff_pre'], 'm_norm_ff_post': out['m_norm_ff_post'], 'm_w_in': out['m_w_in'], 'm_w_out': out['m_w_out'], 'm_hgrn_norm_w': out['m_hgrn_norm_w'], 'm_gdn_conv_w': out['m_gdn_conv_w'], 'm_gdn_a_log': out['m_gdn_a_log'], 'm_gdn_dt_bias': out['m_gdn_dt_bias'], 'm_gdn_norm_w': out['m_gdn_norm_w'], 'm_gmlp_ln_w': out['m_gmlp_ln_w'], 'm_gmlp_ln_b': out['m_gmlp_ln_b'], 'm_gmlp_w_s': out['m_gmlp_w_s'], 'm_gmlp_b_s': out['m_gmlp_b_s'], 'm_conv_dw_w': out['m_conv_dw_w'], 'm_conv_dw_b': out['m_conv_dw_b'], 'm_conv_ln_w': out['m_conv_ln_w'], 'm_conv_ln_b': out['m_conv_ln_b'], 'm_w_ff1': out['m_w_ff1'], 'm_w_ff2': out['m_w_ff2'], 'v_lower_bounds': out['v_lower_bounds'], 'v_norm_mix_pre': out['v_norm_mix_pre'], 'v_norm_mix_post': out['v_norm_mix_post'], 'v_norm_ff_pre': out['v_norm_ff_pre'], 'v_norm_ff_post': out['v_norm_ff_post'], 'v_w_in': out['v_w_in'], 'v_w_out': out['v_w_out'], 'v_hgrn_norm_w': out['v_hgrn_norm_w'], 'v_gdn_conv_w': out['v_gdn_conv_w'], 'v_gdn_a_log': out['v_gdn_a_log'], 'v_gdn_dt_bias': out['v_gdn_dt_bias'], 'v_gdn_norm_w': out['v_gdn_norm_w'], 'v_gmlp_ln_w': out['v_gmlp_ln_w'], 'v_gmlp_ln_b': out['v_gmlp_ln_b'], 'v_gmlp_w_s': out['v_gmlp_w_s'], 'v_gmlp_b_s': out['v_gmlp_b_s'], 'v_conv_dw_w': out['v_conv_dw_w'], 'v_conv_dw_b': out['v_conv_dw_b'], 'v_conv_ln_w': out['v_conv_ln_w'], 'v_conv_ln_b': out['v_conv_ln_b'], 'v_w_ff1': out['v_w_ff1'], 'v_w_ff2': out['v_w_ff2']}


def _loss(weights, diff, rest, loss_target):
    with _jax.named_scope("forward"):
        args = {**rest, TWIN_DIFF_INPUT: diff, **{k: w.astype(_WEIGHT_DTYPES[k]) for k, w in weights.items()}}
        y = _forward(args)
    with _jax.named_scope("loss_head"):
        err = _jnp.square(y.astype(_jnp.float32) - loss_target)
        return 0.5 * _jnp.sum(_jnp.mean(err, axis=-1)) if err.ndim else 0.5 * err


def _adamw(w, g, m, v):
    m = ADAM_B1 * m + (1.0 - ADAM_B1) * g
    v = ADAM_B2 * v + (1.0 - ADAM_B2) * _jnp.square(g)
    m_hat = m / (1.0 - ADAM_B1 ** ADAM_STEP)
    v_hat = v / (1.0 - ADAM_B2 ** ADAM_STEP)
    delta = -ADAM_LR * (m_hat / (_jnp.sqrt(v_hat) + ADAM_EPS) + ADAM_WD * w)
    return delta, m, v


def reference(x, lower_bounds, norm_mix_pre, norm_mix_post, norm_ff_pre, norm_ff_post, w_in, w_out, hgrn_norm_w, gdn_conv_w, gdn_a_log, gdn_dt_bias, gdn_norm_w, gmlp_ln_w, gmlp_ln_b, gmlp_w_s, gmlp_b_s, conv_dw_w, conv_dw_b, conv_ln_w, conv_ln_b, w_ff1, w_ff2, loss_target, m_lower_bounds, m_norm_mix_pre, m_norm_mix_post, m_norm_ff_pre, m_norm_ff_post, m_w_in, m_w_out, m_hgrn_norm_w, m_gdn_conv_w, m_gdn_a_log, m_gdn_dt_bias, m_gdn_norm_w, m_gmlp_ln_w, m_gmlp_ln_b, m_gmlp_w_s, m_gmlp_b_s, m_conv_dw_w, m_conv_dw_b, m_conv_ln_w, m_conv_ln_b, m_w_ff1, m_w_ff2, v_lower_bounds, v_norm_mix_pre, v_norm_mix_post, v_norm_ff_pre, v_norm_ff_post, v_w_in, v_w_out, v_hgrn_norm_w, v_gdn_conv_w, v_gdn_a_log, v_gdn_dt_bias, v_gdn_norm_w, v_gmlp_ln_w, v_gmlp_ln_b, v_gmlp_w_s, v_gmlp_b_s, v_conv_dw_w, v_conv_dw_b, v_conv_ln_w, v_conv_ln_b, v_w_ff1, v_w_ff2):
    given = dict(x=x, lower_bounds=lower_bounds, norm_mix_pre=norm_mix_pre, norm_mix_post=norm_mix_post, norm_ff_pre=norm_ff_pre, norm_ff_post=norm_ff_post, w_in=w_in, w_out=w_out, hgrn_norm_w=hgrn_norm_w, gdn_conv_w=gdn_conv_w, gdn_a_log=gdn_a_log, gdn_dt_bias=gdn_dt_bias, gdn_norm_w=gdn_norm_w, gmlp_ln_w=gmlp_ln_w, gmlp_ln_b=gmlp_ln_b, gmlp_w_s=gmlp_w_s, gmlp_b_s=gmlp_b_s, conv_dw_w=conv_dw_w, conv_dw_b=conv_dw_b, conv_ln_w=conv_ln_w, conv_ln_b=conv_ln_b, w_ff1=w_ff1, w_ff2=w_ff2, loss_target=loss_target, m_lower_bounds=m_lower_bounds, m_norm_mix_pre=m_norm_mix_pre, m_norm_mix_post=m_norm_mix_post, m_norm_ff_pre=m_norm_ff_pre, m_norm_ff_post=m_norm_ff_post, m_w_in=m_w_in, m_w_out=m_w_out, m_hgrn_norm_w=m_hgrn_norm_w, m_gdn_conv_w=m_gdn_conv_w, m_gdn_a_log=m_gdn_a_log, m_gdn_dt_bias=m_gdn_dt_bias, m_gdn_norm_w=m_gdn_norm_w, m_gmlp_ln_w=m_gmlp_ln_w, m_gmlp_ln_b=m_gmlp_ln_b, m_gmlp_w_s=m_gmlp_w_s, m_gmlp_b_s=m_gmlp_b_s, m_conv_dw_w=m_conv_dw_w, m_conv_dw_b=m_conv_dw_b, m_conv_ln_w=m_conv_ln_w, m_conv_ln_b=m_conv_ln_b, m_w_ff1=m_w_ff1, m_w_ff2=m_w_ff2, v_lower_bounds=v_lower_bounds, v_norm_mix_pre=v_norm_mix_pre, v_norm_mix_post=v_norm_mix_post, v_norm_ff_pre=v_norm_ff_pre, v_norm_ff_post=v_norm_ff_post, v_w_in=v_w_in, v_w_out=v_w_out, v_hgrn_norm_w=v_hgrn_norm_w, v_gdn_conv_w=v_gdn_conv_w, v_gdn_a_log=v_gdn_a_log, v_gdn_dt_bias=v_gdn_dt_bias, v_gdn_norm_w=v_gdn_norm_w, v_gmlp_ln_w=v_gmlp_ln_w, v_gmlp_ln_b=v_gmlp_ln_b, v_gmlp_w_s=v_gmlp_w_s, v_gmlp_b_s=v_gmlp_b_s, v_conv_dw_w=v_conv_dw_w, v_conv_dw_b=v_conv_dw_b, v_conv_ln_w=v_conv_ln_w, v_conv_ln_b=v_conv_ln_b, v_w_ff1=v_w_ff1, v_w_ff2=v_w_ff2)
    weights = {n: given[n] for n in TWIN_WEIGHTS}
    shared = {n: given[n] for n in SHARED_INPUTS}
    per_example = {n: given[n] for n in ['x']}
    grad_fn = _jax.value_and_grad(_loss, argnums=(0, 1))

    def one_microbatch(ex, loss_target):
        ex = dict(ex)
        diff = ex.pop(TWIN_DIFF_INPUT)
        return grad_fn(weights, diff, {**shared, **ex}, loss_target)

    if N_MICROBATCH == 1:
        loss, (grad_w, grad_x) = one_microbatch(per_example, given["loss_target"])
    else:
        def body(carry, xs):
            loss_sum, grad_sum = carry
            l_k, (gw_k, gx_k) = one_microbatch(xs[0], xs[1])
            with _jax.named_scope("update"):
                return (loss_sum + l_k, _jax.tree.map(_jnp.add, grad_sum, gw_k)), gx_k

        init = (_jnp.zeros((), _jnp.float32), _jax.tree.map(_jnp.zeros_like, weights))
        (loss, grad_w), grad_x = _jax.lax.scan(body, init, (per_example, given["loss_target"]))
    with _jax.named_scope("update"):
        delta_w, new_m, new_v = {}, {}, {}
        for n in TWIN_WEIGHTS:
            delta_w[n], new_m[n], new_v[n] = _adamw(weights[n], grad_w[n], given["m_" + n], given["v_" + n])
    return (loss, grad_x, *[grad_w[n] for n in TWIN_WEIGHTS], *[delta_w[n] for n in TWIN_WEIGHTS],
            *[new_m[n] for n in TWIN_WEIGHTS], *[new_v[n] for n in TWIN_WEIGHTS])
```

```python
import functools
import math

import jax
import jax.numpy as jnp
from jax import lax
from jax.experimental import pallas as pl
from jax.experimental.pallas import tpu as pltpu

F32 = jnp.float32
BF16 = jnp.bfloat16

DEPTH = 2
D_MODEL = 2048
GROUP_WIDTH = 512
HEAD_DIM = 128
N_HEADS = 4
CHUNK = 64
SHORT_CONV = 4
MIX_CHUNK = 128
CONV_WIDTH = 31
D_FF = 4 * D_MODEL
D_IN_PROJ = 12 * GROUP_WIDTH + 2 * N_HEADS
EPS = 1e-6
NEG_BIG = -1e30
TINY = 1e-30
ADAM_LR = 0.001
ADAM_B1 = 0.9
ADAM_B2 = 0.999
ADAM_EPS = 1e-08
ADAM_WD = 0.01
ADAM_STEP = 10

LANES = 128
P_IN = 12 * GROUP_WIDTH + LANES
SUB_BLOCK = 16
CONV_TILE = 128
CONV_HALO = 32
VMEM_LIMIT = 56 * 1024 * 1024
N_CHIPS = 4
N_DEV = 8
MESH = pl.DeviceIdType.MESH


_DIMS = {
    "nn": (((1,), (0,)), ((), ())),
    "nt": (((1,), (1,)), ((), ())),
    "tn": (((0,), (0,)), ((), ())),
}


def _raw_mm(a, b, mode, exact):
    if exact:
        return lax.dot_general(a, b, _DIMS[mode], precision=lax.Precision.HIGHEST, preferred_element_type=F32)
    return lax.dot_general(a.astype(BF16), b.astype(BF16), _DIMS[mode], preferred_element_type=F32)


@functools.partial(jax.custom_vjp, nondiff_argnums=(2, 3))
def _mm(a, b, mode, exact):
    return _raw_mm(a, b, mode, exact)


def _mm_fwd(a, b, mode, exact):
    return _raw_mm(a, b, mode, exact), (a, b)


def _mm_bwd(mode, exact, res, g):
    a, b = res
    if mode == "nn":
        return _raw_mm(g, b, "nt", exact), _raw_mm(a, g, "tn", exact)
    if mode == "nt":
        return _raw_mm(g, b, "nn", exact), _raw_mm(g, a, "tn", exact)
    return _raw_mm(b, g, "nt", exact), _raw_mm(a, g, "nn", exact)


_mm.defvjp(_mm_fwd, _mm_bwd)


def _sig(x):
    return jax.nn.sigmoid(x)


def _silu(x):
    return x * jax.nn.sigmoid(x)


def _gelu(x):
    return 0.5 * x * (1.0 + lax.erf(x * (1.0 / math.sqrt(2.0))))


def _rms(x, w):
    return x * lax.rsqrt(jnp.mean(x * x, axis=-1, keepdims=True) + EPS) * w


def _ln(x, w, b):
    mu = jnp.mean(x, axis=-1, keepdims=True)
    xc = x - mu
    var = jnp.mean(xc * xc, axis=-1, keepdims=True)
    return xc * lax.rsqrt(var + EPS) * w + b


def _iota(shape, dim):
    return lax.broadcasted_iota(jnp.int32, shape, dim)


def _cumsum_rows(x):
    n = x.shape[0]
    tri = (_iota((n, n), 0) >= _iota((n, n), 1)).astype(F32)
    return _mm(tri, x, "nn", True)


def _hgrn_head(q, k, v, b, st):
    n = q.shape[0]
    ii = _iota((n, 1), 0)
    zpad = jnp.zeros((SUB_BLOCK, HEAD_DIM), F32)
    k_ext = jnp.concatenate([zpad, k], axis=0)
    b_ext = jnp.concatenate([zpad, b], axis=0)
    v_ext = jnp.concatenate([zpad, v], axis=0)
    o = jnp.zeros((n, HEAD_DIM), F32)
    for d in range(SUB_BLOCK):
        ks = k_ext[SUB_BLOCK - d:SUB_BLOCK - d + n]
        bs = b_ext[SUB_BLOCK - d:SUB_BLOCK - d + n]
        vs = v_ext[SUB_BLOCK - d:SUB_BLOCK - d + n]
        e = jnp.exp(jnp.where((ii % SUB_BLOCK) >= d, b - bs, NEG_BIG))
        o = o + jnp.sum(q * ks * e, axis=-1, keepdims=True) * vs
    blocks = [o[0:SUB_BLOCK]]
    for blk in range(1, n // SUB_BLOCK):
        lo = SUB_BLOCK * blk
        r = b[lo - 1:lo]
        a_q = q[lo:lo + SUB_BLOCK] * jnp.exp(b[lo:lo + SUB_BLOCK] - r)
        b_k = jnp.where(ii < lo, k * jnp.exp(jnp.minimum(r - b, 0.0)), 0.0)
        sc = _mm(a_q, b_k, "nt", False)
        blocks.append(o[lo:lo + SUB_BLOCK] + _mm(sc, v, "nn", False))
    o = jnp.concatenate(blocks, axis=0)
    o = o + _mm(q * jnp.exp(b), st, "nt", False)
    b_end = b[n - 1:n]
    st_new = st * jnp.exp(b_end) + _mm(v, k * jnp.exp(b_end - b), "tn", False)
    return o, st_new


def _hgrn_chunk(layer, lbp, nw, aq, af, ai, ag, states):
    rows = [lbp[i:i + 1, :] for i in range(DEPTH)]
    mx = functools.reduce(jnp.maximum, rows)
    es = [jnp.exp(r - mx) for r in rows]
    den = functools.reduce(lambda p, s: p + s, es)
    soft = [e / den for e in es]
    lb = functools.reduce(lambda p, s: p + s, soft[:layer + 1]) - soft[0]
    f = lb + (1.0 - lb) * _sig(af)
    logf = jnp.log(jnp.maximum(f, TINY))
    k = (1.0 - lb) * _sig(-af)
    q = _silu(aq)
    b = _cumsum_rows(logf)
    outs, new_states = [], []
    for h in range(N_HEADS):
        sl = slice(HEAD_DIM * h, HEAD_DIM * (h + 1))
        o, st = _hgrn_head(q[:, sl], k[:, sl], ai[:, sl], b[:, sl], states[h])
        outs.append(_rms(o, nw) * _silu(ag[:, sl]))
        new_states.append(st)
    return jnp.concatenate(outs, axis=1), new_states


def _short_conv(prev, cur, w):
    n = cur.shape[0]
    ext = jnp.concatenate([prev[n - 8:n], cur], axis=0)
    y = jnp.zeros_like(cur)
    for t in range(SHORT_CONV):
        off = 8 - (SHORT_CONV - 1) + t
        y = y + w[t:t + 1, :] * ext[off:off + n]
    return _silu(y)


def _gdn_head(q, k, v, beta, gc, gcr, st):
    n = q.shape[0]
    ii = _iota((n, n), 0)
    jj = _iota((n, n), 1)
    gamma = jnp.exp(jnp.where(ii >= jj, gc - gcr, NEG_BIG))
    kb = k * beta
    m = jnp.where(ii > jj, _mm(kb, k, "nt", False) * gamma, 0.0)
    eye = (ii == jj).astype(F32)
    inv = eye - m
    p = m
    steps = max(1, int(math.ceil(math.log2(n))) - 1)
    for _ in range(steps):
        p = _mm(p, p, "nn", True)
        inv = inv + _mm(inv, p, "nn", True)
    eg = jnp.exp(gc)
    u = _mm(inv, v * beta, "nn", True)
    w = _mm(inv, kb * eg, "nn", True)
    qk = _mm(q, k, "nt", False) * gamma
    qd = q * eg
    g_end = gc[n - 1:n]
    kd = k * jnp.exp(g_end - gc)
    v_new = u - _mm(w, st, "nt", False)
    out = _mm(qd, st, "nt", False) + _mm(qk, v_new, "nn", False)
    st_new = st * jnp.exp(g_end) + _mm(v_new, kd, "tn", False)
    return out, st_new


def _gdn_chunk(cw, alog, dtb, nw, pq, pk, pv, cq, ck, cv, bz, ab, states):
    beta_all = _sig(ab)
    g_all = -jnp.exp(alog) * jax.nn.softplus(ab + dtb)
    gc_all = _cumsum_rows(g_all)
    gc_t = gc_all.T
    outs, new_states = [], []
    for h in range(N_HEADS):
        sl = slice(HEAD_DIM * h, HEAD_DIM * (h + 1))
        wq = cw[:, HEAD_DIM * h:HEAD_DIM * (h + 1)]
        wk = cw[:, GROUP_WIDTH + HEAD_DIM * h:GROUP_WIDTH + HEAD_DIM * (h + 1)]
        wv = cw[:, 2 * GROUP_WIDTH + HEAD_DIM * h:2 * GROUP_WIDTH + HEAD_DIM * (h + 1)]
        q = _short_conv(pq[:, sl], cq[:, sl], wq)
        k = _short_conv(pk[:, sl], ck[:, sl], wk)
        v = _short_conv(pv[:, sl], cv[:, sl], wv)
        q = q * lax.rsqrt(jnp.sum(q * q, axis=-1, keepdims=True) + EPS) * (HEAD_DIM ** -0.5)
        k = k * lax.rsqrt(jnp.sum(k * k, axis=-1, keepdims=True) + EPS)
        beta = beta_all[:, h:h + 1]
        gc = gc_all[:, N_HEADS + h:N_HEADS + h + 1]
        gcr = gc_t[N_HEADS + h:N_HEADS + h + 1, :]
        o, st = _gdn_head(q, k, v, beta, gc, gcr, states[h])
        outs.append(_rms(o, nw) * _silu(bz[:, sl]))
        new_states.append(st)
    return jnp.concatenate(outs, axis=1), new_states


def _gmlp_tile(ln_w, ln_b, ws, bs_cols, cu, cv):
    u = _gelu(cu)
    v = _ln(_gelu(cv), ln_w, ln_b)
    n = cu.shape[0]
    tril = _iota((n, n), 0) >= _iota((n, n), 1)
    outs = []
    for h in range(N_HEADS):
        sl = slice(HEAD_DIM * h, HEAD_DIM * (h + 1))
        wc = jnp.where(tril, ws[h], 0.0)
        outs.append(_mm(wc, v[:, sl], "nn", False) + bs_cols[:, h:h + 1])
    return u * jnp.concatenate(outs, axis=1)


def _conformer_tile(dw_w, dw_b, ln_w, ln_b, pa, pg, ca, cg):
    n = ca.shape[0]
    yp = pa[n - CONV_HALO:n] * _sig(pg[n - CONV_HALO:n])
    ext = jnp.concatenate([yp, ca * _sig(cg)], axis=0)
    acc = jnp.zeros_like(ca)
    for t in range(CONV_WIDTH):
        off = CONV_HALO - (CONV_WIDTH - 1) + t
        acc = acc + dw_w[t:t + 1, :] * ext[off:off + n]
    return _silu(_ln(acc + dw_b, ln_w, ln_b))


def _seq_call(name, body, *, steps, ins, outs, accs=(), carries=(), reverse=False, prefetch=None):
    n_in, n_out, n_acc, n_car = len(ins), len(outs), len(accs), len(carries)
    n_pre = 0 if prefetch is None else 1

    def logical(g):
        return (steps - 1 - g) if reverse else g

    def kern(*refs):
        refs = refs[n_pre:]
        in_refs = refs[:n_in]
        out_refs = refs[n_in:n_in + n_out]
        acc_refs = refs[n_in + n_out:n_in + n_out + n_acc]
        car_refs = refs[n_in + n_out + n_acc:]
        g = pl.program_id(0)

        @pl.when(g == 0)
        def _():
            for r in list(acc_refs) + list(car_refs):
                r[...] = jnp.zeros(r.shape, r.dtype)

        o, a, c = body(logical(g), [r[...] for r in in_refs], [r[...] for r in car_refs])
        for r, v in zip(out_refs, o, strict=True):
            r[...] = v.astype(r.dtype)
        for r, v in zip(acc_refs, a, strict=True):
            r[...] += v
        for r, v in zip(car_refs, c, strict=True):
            r[...] = v

    def spec(block, fn):
        return pl.BlockSpec(block, lambda g, *pre: fn(logical(g), *pre))

    in_specs = [spec(bs, fn) for (_, bs, fn) in ins]
    out_specs = [spec(bs, fn) for (_, _, bs, fn) in outs]
    out_specs += [pl.BlockSpec(shape, lambda g, *pre, _n=len(shape): (0,) * _n) for (shape, _) in accs]
    out_shape = [jax.ShapeDtypeStruct(s, d) for (s, d, _, _) in outs]
    out_shape += [jax.ShapeDtypeStruct(s, d) for (s, d) in accs]
    grid_spec = pltpu.PrefetchScalarGridSpec(
        num_scalar_prefetch=n_pre, grid=(steps,), in_specs=in_specs, out_specs=out_specs,
        scratch_shapes=[pltpu.VMEM(s, d) for (s, d) in carries])
    args = ([] if prefetch is None else [prefetch]) + [a for (a, _, _) in ins]
    return pl.pallas_call(
        kern, name=name, grid_spec=grid_spec, out_shape=out_shape,
        compiler_params=pltpu.CompilerParams(dimension_semantics=("arbitrary",), vmem_limit_bytes=VMEM_LIMIT),
    )(*args)


def _whole(a):
    nd = a.ndim
    return (a, a.shape, lambda i, *pre: (0,) * nd)


def _rows(a, tile, col=0, width=None, shift=0):
    width = a.shape[1] if width is None else width
    if shift:
        return (a, (tile, width), lambda i, *pre: (jnp.maximum(i + shift, 0), col))
    return (a, (tile, width), lambda i, *pre: (i, col))


def _row_out(n_rows, width, dtype, tile):
    return ((n_rows, width), dtype, (tile, width), lambda i, *pre: (i, 0))


def _pick_tile(n, prefs):
    for t in prefs:
        if n % t == 0:
            return t
    return n


def _matmul(name, a, b, mode, out_dtypes, epilogue=None, extras=()):
    if mode == "nn":
        (m, k), n = a.shape, b.shape[1]
    elif mode == "nt":
        (m, k), n = a.shape, b.shape[0]
    else:
        (k, m), n = a.shape, b.shape[1]
    tm = _pick_tile(m, (1024, 512, 256, 128))
    tn = _pick_tile(n, (1024, 896, 512, 256, 128))
    tk = _pick_tile(k, (1024, 896, 512, 256, 128))
    nk = k // tk
    n_ex = len(extras)
    n_out = len(out_dtypes)

    def kern(*refs):
        a_ref, b_ref = refs[0], refs[1]
        ex_refs = refs[2:2 + n_ex]
        out_refs = refs[2 + n_ex:2 + n_ex + n_out]
        acc_ref = refs[2 + n_ex + n_out]
        kk = pl.program_id(2)

        @pl.when(kk == 0)
        def _():
            acc_ref[...] = jnp.zeros(acc_ref.shape, F32)

        acc_ref[...] += lax.dot_general(a_ref[...], b_ref[...], _DIMS[mode], preferred_element_type=F32)

        @pl.when(kk == nk - 1)
        def _():
            acc = acc_ref[...]
            vals = (acc,) if epilogue is None else epilogue(acc, *[r[...] for r in ex_refs])
            for r, v in zip(out_refs, vals, strict=True):
                r[...] = v.astype(r.dtype)

    if mode == "tn":
        a_spec = pl.BlockSpec((tk, tm), lambda i, j, kk: (kk, i))
    else:
        a_spec = pl.BlockSpec((tm, tk), lambda i, j, kk: (i, kk))
    if mode == "nt":
        b_spec = pl.BlockSpec((tn, tk), lambda i, j, kk: (j, kk))
    else:
        b_spec = pl.BlockSpec((tk, tn), lambda i, j, kk: (kk, j))
    tile = pl.BlockSpec((tm, tn), lambda i, j, kk: (i, j))
    return pl.pallas_call(
        kern, name=name, grid=(m // tm, n // tn, nk),
        in_specs=[a_spec, b_spec] + [tile] * n_ex,
        out_specs=[tile] * n_out,
        out_shape=[jax.ShapeDtypeStruct((m, n), d) for d in out_dtypes],
        scratch_shapes=[pltpu.VMEM((tm, tn), F32)],
        compiler_params=pltpu.CompilerParams(
            dimension_semantics=("parallel", "parallel", "arbitrary"), vmem_limit_bytes=VMEM_LIMIT),
    )(a, b, *extras)


ROW_TILE = 256


def _rms_fwd(name, x, w):
    s, d = x.shape
    t = _pick_tile(s, (ROW_TILE,))

    def body(i, v, c):
        return [_rms(v[0], v[1])], [], []

    return _seq_call(name, body, steps=s // t, ins=[_rows(x, t), _whole(w)], outs=[_row_out(s, d, BF16, t)])[0]


def _resid_rms_fwd(name, x, y, w):
    s, d = x.shape
    t = _pick_tile(s, (ROW_TILE,))

    def body(i, v, c):
        return [v[0] + _rms(v[1], v[2])], [], []

    return _seq_call(name, body, steps=s // t, ins=[_rows(x, t), _rows(y, t), _whole(w)],
                     outs=[_row_out(s, d, F32, t)])[0]


def _rms_bwd(name, x, w, dh, dres):
    s, d = x.shape
    t = _pick_tile(s, (ROW_TILE,))

    def body(i, v, c):
        _, vjp = jax.vjp(_rms, v[0], v[1])
        dx, dw = vjp(v[2])
        return [dx + v[3]], [dw], []

    return _seq_call(name, body, steps=s // t, ins=[_rows(x, t), _whole(w), _rows(dh, t), _rows(dres, t)],
                     outs=[_row_out(s, d, F32, t)], accs=[((1, d), F32)])


def _resid_rms_bwd(name, y, w, dxo):
    s, d = y.shape
    t = _pick_tile(s, (ROW_TILE,))

    def body(i, v, c):
        _, vjp = jax.vjp(_rms, v[0], v[1])
        dy, dw = vjp(v[2])
        return [dy], [dw], []

    return _seq_call(name, body, steps=s // t, ins=[_rows(y, t), _whole(w), _rows(dxo, t)],
                     outs=[_row_out(s, d, BF16, t)], accs=[((1, d), F32)])


def _loss_head(name, y, target):
    s, d = y.shape
    t = _pick_tile(s, (ROW_TILE,))

    def body(i, v, c):
        err = v[0] - v[1]
        part = 0.5 * jnp.sum(jnp.mean(err * err, axis=-1, keepdims=True))
        return [err * (1.0 / d)], [jnp.full((1, LANES), part, F32)], []

    return _seq_call(name, body, steps=s // t, ins=[_rows(y, t), _rows(target, t)],
                     outs=[_row_out(s, d, F32, t)], accs=[((1, LANES), F32)])


SEG = {n: i for i, n in enumerate(
    ["a_q", "a_f", "a_i", "a_g", "b_q", "b_k", "b_v", "b_z", "c_u", "c_v", "d_a", "d_gate"])}
AB_COL = 12 * GROUP_WIDTH // LANES


def _seg(proj, name, tile, shift=0):
    return _rows(proj, tile, col=SEG[name], width=GROUP_WIDTH, shift=shift)


def _state_block():
    return (1, N_HEADS * HEAD_DIM, HEAD_DIM), lambda i, *pre: (i, 0, 0)


def _split_states(blk):
    return [blk[0, HEAD_DIM * h:HEAD_DIM * (h + 1), :] for h in range(N_HEADS)]


STATE_CARRIES = [((HEAD_DIM, HEAD_DIM), F32)] * N_HEADS


def _hgrn_fwd(layer, proj, lbp, nw):
    s = proj.shape[0]
    n = s // CHUNK
    sb, sf = _state_block()

    def body(i, v, st):
        o, new = _hgrn_chunk(layer, v[0], v[1], v[2], v[3], v[4], v[5], st)
        return [o, jnp.concatenate(st, axis=0)[None]], [], new

    return _seq_call(
        f"hgrn_fwd{layer}", body, steps=n,
        ins=[_whole(lbp), _whole(nw)] + [_seg(proj, k, CHUNK) for k in ("a_q", "a_f", "a_i", "a_g")],
        outs=[_row_out(s, GROUP_WIDTH, BF16, CHUNK), ((n, N_HEADS * HEAD_DIM, HEAD_DIM), F32, sb, sf)],
        carries=STATE_CARRIES)


def _hgrn_bwd(layer, proj, lbp, nw, states, dmix):
    s = proj.shape[0]
    n = s // CHUNK
    sb, sf = _state_block()

    def body(i, v, dst):
        st = _split_states(v[6])

        def f(lbp_, nw_, aq, af, ai, ag, *st_):
            return _hgrn_chunk(layer, lbp_, nw_, aq, af, ai, ag, list(st_))

        _, vjp = jax.vjp(f, v[0], v[1], v[2], v[3], v[4], v[5], *st)
        g = vjp((v[7], list(dst)))
        return [jnp.concatenate(g[2:6], axis=1)], [g[0], g[1]], list(g[6:])

    return _seq_call(
        f"hgrn_bwd{layer}", body, steps=n, reverse=True,
        ins=[_whole(lbp), _whole(nw)] + [_seg(proj, k, CHUNK) for k in ("a_q", "a_f", "a_i", "a_g")]
        + [(states, sb, sf), _rows(dmix, CHUNK, col=0, width=GROUP_WIDTH)],
        outs=[_row_out(s, 4 * GROUP_WIDTH, BF16, CHUNK)],
        accs=[(lbp.shape, F32), (nw.shape, F32)], carries=STATE_CARRIES)


def _gdn_ins(proj, cw, alog, dtb, nw):
    return ([_whole(cw), _whole(alog), _whole(dtb), _whole(nw)]
            + [_seg(proj, k, CHUNK, shift=-1) for k in ("b_q", "b_k", "b_v")]
            + [_seg(proj, k, CHUNK) for k in ("b_q", "b_k", "b_v", "b_z")]
            + [_rows(proj, CHUNK, col=AB_COL, width=LANES)])


def _mask_prev(i, vals):
    keep = (i > 0).astype(F32)
    return [p * keep for p in vals]


def _gdn_fwd(layer, proj, cw, alog, dtb, nw):
    s = proj.shape[0]
    n = s // CHUNK
    sb, sf = _state_block()

    def body(i, v, st):
        prev = _mask_prev(i, v[4:7])
        o, new = _gdn_chunk(v[0], v[1], v[2], v[3], *prev, *v[7:12], st)
        return [o, jnp.concatenate(st, axis=0)[None]], [], new

    return _seq_call(
        f"gdn_fwd{layer}", body, steps=n, ins=_gdn_ins(proj, cw, alog, dtb, nw),
        outs=[_row_out(s, GROUP_WIDTH, BF16, CHUNK), ((n, N_HEADS * HEAD_DIM, HEAD_DIM), F32, sb, sf)],
        carries=STATE_CARRIES)


def _gdn_bwd(layer, proj, cw, alog, dtb, nw, states, dmix):
    s = proj.shape[0]
    n = s // CHUNK
    sb, sf = _state_block()

    def body(i, v, car):
        dst, dprev = car[:N_HEADS], car[N_HEADS:]
        prev = _mask_prev(i, v[4:7])
        st = _split_states(v[12])

        def f(cw_, alog_, dtb_, nw_, pq, pk, pv, cq, ck, cv, bz, ab, *st_):
            return _gdn_chunk(cw_, alog_, dtb_, nw_, pq, pk, pv, cq, ck, cv, bz, ab, list(st_))

        _, vjp = jax.vjp(f, v[0], v[1], v[2], v[3], *prev, *v[7:12], *st)
        g = vjp((v[13], list(dst)))
        dcur = [g[7] + dprev[0], g[8] + dprev[1], g[9] + dprev[2], g[10]]
        return ([jnp.concatenate(dcur, axis=1), g[11]], list(g[0:4]), list(g[12:]) + list(g[4:7]))

    return _seq_call(
        f"gdn_bwd{layer}", body, steps=n, reverse=True,
        ins=_gdn_ins(proj, cw, alog, dtb, nw) + [(states, sb, sf), _rows(dmix, CHUNK, col=1, width=GROUP_WIDTH)],
        outs=[_row_out(s, 4 * GROUP_WIDTH, BF16, CHUNK), _row_out(s, LANES, BF16, CHUNK)],
        accs=[(cw.shape, F32), (alog.shape, F32), (dtb.shape, F32), (nw.shape, F32)],
        carries=STATE_CARRIES + [((CHUNK, GROUP_WIDTH), F32)] * 3)


def _gmlp_fwd(layer, proj, ln_w, ln_b, ws, bs_cols):
    s = proj.shape[0]

    def body(i, v, c):
        return [_gmlp_tile(*v)], [], []

    return _seq_call(
        f"gmlp_fwd{layer}", body, steps=s // MIX_CHUNK,
        ins=[_whole(ln_w), _whole(ln_b), _whole(ws), _whole(bs_cols),
             _seg(proj, "c_u", MIX_CHUNK), _seg(proj, "c_v", MIX_CHUNK)],
        outs=[_row_out(s, GROUP_WIDTH, BF16, MIX_CHUNK)])[0]


def _gmlp_bwd(layer, proj, ln_w, ln_b, ws, bs_cols, dmix):
    s = proj.shape[0]

    def body(i, v, c):
        _, vjp = jax.vjp(_gmlp_tile, *v[:6])
        g = vjp(v[6])
        return [jnp.concatenate(g[4:6], axis=1)], list(g[0:4]), []

    return _seq_call(
        f"gmlp_bwd{layer}", body, steps=s // MIX_CHUNK,
        ins=[_whole(ln_w), _whole(ln_b), _whole(ws), _whole(bs_cols),
             _seg(proj, "c_u", MIX_CHUNK), _seg(proj, "c_v", MIX_CHUNK),
             _rows(dmix, MIX_CHUNK, col=2, width=GROUP_WIDTH)],
        outs=[_row_out(s, 2 * GROUP_WIDTH, BF16, MIX_CHUNK)],
        accs=[(ln_w.shape, F32), (ln_b.shape, F32), (ws.shape, F32), (bs_cols.shape, F32)])


def _conformer_ins(proj, dw_w, dw_b, ln_w, ln_b):
    return ([_whole(dw_w), _whole(dw_b), _whole(ln_w), _whole(ln_b)]
            + [_seg(proj, k, CONV_TILE, shift=-1) for k in ("d_a", "d_gate")]
            + [_seg(proj, k, CONV_TILE) for k in ("d_a", "d_gate")])


def _conformer_fwd(layer, proj, dw_w, dw_b, ln_w, ln_b):
    s = proj.shape[0]

    def body(i, v, c):
        prev = _mask_prev(i, v[4:6])
        return [_conformer_tile(v[0], v[1], v[2], v[3], *prev, v[6], v[7])], [], []

    return _seq_call(
        f"conformer_fwd{layer}", body, steps=s // CONV_TILE, ins=_conformer_ins(proj, dw_w, dw_b, ln_w, ln_b),
        outs=[_row_out(s, GROUP_WIDTH, BF16, CONV_TILE)])[0]


def _conformer_bwd(layer, proj, dw_w, dw_b, ln_w, ln_b, dmix):
    s = proj.shape[0]

    def body(i, v, dprev):
        prev = _mask_prev(i, v[4:6])
        _, vjp = jax.vjp(_conformer_tile, v[0], v[1], v[2], v[3], *prev, v[6], v[7])
        g = vjp(v[8])
        return [jnp.concatenate([g[6] + dprev[0], g[7] + dprev[1]], axis=1)], list(g[0:4]), [g[4], g[5]]

    return _seq_call(
        f"conformer_bwd{layer}", body, steps=s // CONV_TILE, reverse=True,
        ins=_conformer_ins(proj, dw_w, dw_b, ln_w, ln_b) + [_rows(dmix, CONV_TILE, col=3, width=GROUP_WIDTH)],
        outs=[_row_out(s, 2 * GROUP_WIDTH, BF16, CONV_TILE)],
        accs=[(dw_w.shape, F32), (dw_b.shape, F32), (ln_w.shape, F32), (ln_b.shape, F32)],
        carries=[((CONV_TILE, GROUP_WIDTH), F32)] * 2)


SMALL = ["lower_bounds", "norm_mix_pre", "norm_mix_post", "norm_ff_pre", "norm_ff_post", "hgrn_norm_w",
         "gdn_conv_w", "gdn_a_log", "gdn_dt_bias", "gdn_norm_w", "gmlp_ln_w", "gmlp_ln_b", "gmlp_w_s",
         "gmlp_b_s", "conv_dw_w", "conv_dw_b", "conv_ln_w", "conv_ln_b"]


def _gate_row(v):
    return jnp.pad(v.reshape(1, N_HEADS), ((0, 0), (N_HEADS, LANES - 2 * N_HEADS)))


def _relu2(acc):
    r = jnp.maximum(acc, 0.0)
    return acc, r * r


def _relu2_bwd(acc, u):
    return (2.0 * jnp.maximum(u, 0.0) * acc,)


def _local_step(x, target, sp, w_in, w_out, w_ff1, w_ff2):
    row = lambda v: v.reshape(1, -1)
    saved = []
    for l in range(DEPTH):
        par = dict(
            lbp=sp["lower_bounds"], hn=row(sp["hgrn_norm_w"][l]), cw=sp["gdn_conv_w"][l],
            alog=_gate_row(sp["gdn_a_log"][l]), dtb=_gate_row(sp["gdn_dt_bias"][l]), gn=row(sp["gdn_norm_w"][l]),
            glw=row(sp["gmlp_ln_w"][l]), glb=row(sp["gmlp_ln_b"][l]), ws=sp["gmlp_w_s"][l],
            bsc=jnp.pad(sp["gmlp_b_s"][l].T, ((0, 0), (0, LANES - N_HEADS))),
            dww=sp["conv_dw_w"][l], dwb=row(sp["conv_dw_b"][l]), clw=row(sp["conv_ln_w"][l]),
            clb=row(sp["conv_ln_b"][l]), n1=row(sp["norm_mix_pre"][l]), n2=row(sp["norm_mix_post"][l]),
            n3=row(sp["norm_ff_pre"][l]), n4=row(sp["norm_ff_post"][l]))
        h = _rms_fwd(f"norm_mix_pre{l}", x, par["n1"])
        proj = _matmul(f"in_proj{l}", h, w_in[l], "nn", [F32])[0]
        o_a, st_a = _hgrn_fwd(l, proj, par["lbp"], par["hn"])
        o_b, st_b = _gdn_fwd(l, proj, par["cw"], par["alog"], par["dtb"], par["gn"])
        o_c = _gmlp_fwd(l, proj, par["glw"], par["glb"], par["ws"], par["bsc"])
        o_d = _conformer_fwd(l, proj, par["dww"], par["dwb"], par["clw"], par["clb"])
        mix = jnp.concatenate([o_a, o_b, o_c, o_d], axis=1)
        y = _matmul(f"out_proj{l}", mix, w_out[l], "nn", [F32])[0]
        x1 = _resid_rms_fwd(f"norm_mix_post{l}", x, y, par["n2"])
        h2 = _rms_fwd(f"norm_ff_pre{l}", x1, par["n3"])
        u, act = _matmul(f"ff1_{l}", h2, w_ff1[l], "nn", [F32, BF16], epilogue=_relu2)
        y2 = _matmul(f"ff2_{l}", act, w_ff2[l], "nn", [F32])[0]
        x2 = _resid_rms_fwd(f"norm_ff_post{l}", x1, y2, par["n4"])
        saved.append(dict(par=par, x=x, h=h, proj=proj, st_a=st_a, st_b=st_b, mix=mix, y=y, x1=x1, h2=h2,
                          u=u, act=act, y2=y2))
        x = x2

    dx, loss_acc = _loss_head("loss_head", x, target)
    loss_part = loss_acc[0, 0]

    gs = {k: [None] * DEPTH for k in SMALL if k != "lower_bounds"}
    g_lb = jnp.zeros((DEPTH, GROUP_WIDTH), F32)
    g_in, g_out, g_ff1, g_ff2 = [None] * DEPTH, [None] * DEPTH, [None] * DEPTH, [None] * DEPTH
    for l in reversed(range(DEPTH)):
        sv = saved[l]
        par = sv["par"]
        dy2, dn4 = _resid_rms_bwd(f"norm_ff_post_bwd{l}", sv["y2"], par["n4"], dx)
        du = _matmul(f"ff2_dx{l}", dy2, w_ff2[l], "nt", [BF16], epilogue=_relu2_bwd, extras=(sv["u"],))[0]
        g_ff2[l] = _matmul(f"ff2_dw{l}", sv["act"], dy2, "tn", [F32])[0]
        g_ff1[l] = _matmul(f"ff1_dw{l}", sv["h2"], du, "tn", [F32])[0]
        dh2 = _matmul(f"ff1_dx{l}", du, w_ff1[l], "nt", [F32])[0]
        dx1, dn3 = _rms_bwd(f"norm_ff_pre_bwd{l}", sv["x1"], par["n3"], dh2, dx)
        dy, dn2 = _resid_rms_bwd(f"norm_mix_post_bwd{l}", sv["y"], par["n2"], dx1)
        dmix = _matmul(f"out_proj_dx{l}", dy, w_out[l], "nt", [F32])[0]
        g_out[l] = _matmul(f"out_proj_dw{l}", sv["mix"], dy, "tn", [F32])[0]
        proj = sv["proj"]
        dp_a, dlb, dhn = _hgrn_bwd(l, proj, par["lbp"], par["hn"], sv["st_a"], dmix)
        dp_b, dp_ab, dcw, dalog, ddtb, dgn = _gdn_bwd(
            l, proj, par["cw"], par["alog"], par["dtb"], par["gn"], sv["st_b"], dmix)
        dp_c, dglw, dglb, dws, dbsc = _gmlp_bwd(l, proj, par["glw"], par["glb"], par["ws"], par["bsc"], dmix)
        dp_d, ddww, ddwb, dclw, dclb = _conformer_bwd(
            l, proj, par["dww"], par["dwb"], par["clw"], par["clb"], dmix)
        dproj = jnp.concatenate([dp_a, dp_b, dp_c, dp_d, dp_ab], axis=1)
        g_in[l] = _matmul(f"in_proj_dw{l}", sv["h"], dproj, "tn", [F32])[0]
        dh = _matmul(f"in_proj_dx{l}", dproj, w_in[l], "nt", [F32])[0]
        dx, dn1 = _rms_bwd(f"norm_mix_pre_bwd{l}", sv["x"], par["n1"], dh, dx1)
        g_lb = g_lb + dlb
        for k, v in dict(
                norm_mix_pre=dn1[0], norm_mix_post=dn2[0], norm_ff_pre=dn3[0], norm_ff_post=dn4[0],
                hgrn_norm_w=dhn[0], gdn_conv_w=dcw, gdn_a_log=dalog[0, N_HEADS:2 * N_HEADS],
                gdn_dt_bias=ddtb[0, N_HEADS:2 * N_HEADS], gdn_norm_w=dgn[0], gmlp_ln_w=dglw[0],
                gmlp_ln_b=dglb[0], gmlp_w_s=dws, gmlp_b_s=dbsc[:, :N_HEADS].T, conv_dw_w=ddww,
                conv_dw_b=ddwb[0], conv_ln_w=dclw[0], conv_ln_b=dclb[0]).items():
            gs[k][l] = v
    small_grads = {k: jnp.stack(v) for k, v in gs.items()}
    small_grads["lower_bounds"] = g_lb
    return loss_part, dx, small_grads, g_in, g_out, g_ff1, g_ff2


HBM = pl.BlockSpec(memory_space=pl.ANY)


def _place():
    return lax.axis_index("x"), lax.axis_index("y"), lax.axis_index("c")


def _other_chips(x, y):
    chips = [(1 - x, y), (x, 1 - y), (1 - x, 1 - y)]
    return [(px, py, 2 * px + py) for px, py in chips]


def _gather_weights(name, blk):
    def body(blk_ref, out_ref, ici_send, ici_recv, d2d_send, d2d_recv, local_sem):
        x, y, c = _place()
        mine = 2 * x + y
        local = pltpu.make_async_copy(blk_ref, out_ref.at[mine], local_sem)
        local.start()

        def ici(j, chip_of_slab, to):
            return pltpu.make_async_remote_copy(
                src_ref=blk_ref.at[c], dst_ref=out_ref.at[chip_of_slab, c], send_sem=ici_send.at[j],
                recv_sem=ici_recv.at[j], device_id=to, device_id_type=MESH)

        def d2d(j, chip_of_slab, layer):
            return pltpu.make_async_remote_copy(
                src_ref=out_ref.at[chip_of_slab, layer], dst_ref=out_ref.at[chip_of_slab, layer],
                send_sem=d2d_send.at[j], recv_sem=d2d_recv.at[j], device_id=(x, y, 1 - c), device_id_type=MESH)

        peers = _other_chips(x, y)
        sends = [ici(j, mine, (px, py, c)) for j, (px, py, _) in enumerate(peers)]
        for cp in sends:
            cp.start()
        passed = []
        for j, (px, py, k) in enumerate(peers):
            ici(j, k, (px, py, c)).wait_recv()
            fwd = d2d(j, k, c)
            fwd.start()
            passed.append(fwd)
        for j, (px, py, k) in enumerate(peers):
            d2d(j, k, 1 - c).wait_recv()
        for cp in sends + passed:
            cp.wait_send()
        local.wait()

    return pl.pallas_call(
        body, name=name, out_shape=jax.ShapeDtypeStruct((N_CHIPS,) + blk.shape, blk.dtype),
        in_specs=[HBM], out_specs=HBM,
        scratch_shapes=[pltpu.SemaphoreType.DMA((3,))] * 4 + [pltpu.SemaphoreType.DMA],
    )(blk)


def _pair_exchange(name, g):
    def body(g_ref, got_ref, send_sem, recv_sem):
        x, y, c = _place()
        cp = pltpu.make_async_remote_copy(
            src_ref=g_ref.at[1 - c], dst_ref=got_ref, send_sem=send_sem, recv_sem=recv_sem,
            device_id=(x, y, 1 - c), device_id_type=MESH)
        cp.start()
        cp.wait()

    return pl.pallas_call(
        body, name=name, out_shape=jax.ShapeDtypeStruct(g.shape[1:], g.dtype), in_specs=[HBM], out_specs=HBM,
        scratch_shapes=[pltpu.SemaphoreType.DMA, pltpu.SemaphoreType.DMA],
    )(g)


def _scatter_partials(name, p):
    def body(p_ref, out_ref, send_sems, recv_sems, local_sem):
        x, y, c = _place()
        mine = 2 * x + y
        local = pltpu.make_async_copy(p_ref.at[mine], out_ref.at[mine], local_sem)
        local.start()

        def cp(j, slab, slot, to):
            return pltpu.make_async_remote_copy(
                src_ref=p_ref.at[slab], dst_ref=out_ref.at[slot], send_sem=send_sems.at[j],
                recv_sem=recv_sems.at[j], device_id=to, device_id_type=MESH)

        peers = _other_chips(x, y)
        sends = [cp(j, k, mine, (px, py, c)) for j, (px, py, k) in enumerate(peers)]
        for s in sends:
            s.start()
        for j, (px, py, k) in enumerate(peers):
            cp(j, mine, k, (px, py, c)).wait_recv()
        for s in sends:
            s.wait_send()
        local.wait()

    return pl.pallas_call(
        body, name=name, out_shape=jax.ShapeDtypeStruct(p.shape, p.dtype), in_specs=[HBM], out_specs=HBM,
        scratch_shapes=[pltpu.SemaphoreType.DMA((3,)), pltpu.SemaphoreType.DMA((3,)), pltpu.SemaphoreType.DMA],
    )(p)


def _pair_share(name, r):
    def body(r_ref, out_ref, send_sem, recv_sem, local_sem):
        x, y, c = _place()
        local = pltpu.make_async_copy(r_ref, out_ref.at[c], local_sem)
        local.start()
        cp = pltpu.make_async_remote_copy(
            src_ref=r_ref, dst_ref=out_ref.at[c], send_sem=send_sem, recv_sem=recv_sem,
            device_id=(x, y, 1 - c), device_id_type=MESH)
        cp.start()
        pltpu.make_async_remote_copy(
            src_ref=r_ref, dst_ref=out_ref.at[1 - c], send_sem=send_sem, recv_sem=recv_sem,
            device_id=(x, y, 1 - c), device_id_type=MESH).wait_recv()
        cp.wait_send()
        local.wait()

    return pl.pallas_call(
        body, name=name, out_shape=jax.ShapeDtypeStruct((DEPTH,) + r.shape, r.dtype), in_specs=[HBM],
        out_specs=HBM, scratch_shapes=[pltpu.SemaphoreType.DMA] * 3,
    )(r)


def _gather_small(name, pack):
    def body(p_ref, out_ref, send_sems, recv_sems, local_sem):
        x, y, c = _place()
        me = 4 * x + 2 * y + c
        local = pltpu.make_async_copy(p_ref, out_ref.at[me], local_sem)
        local.start()
        flips = [(fx, fy, fc) for fx in (0, 1) for fy in (0, 1) for fc in (0, 1)][1:]
        peers = [((1 - x) if fx else x, (1 - y) if fy else y, (1 - c) if fc else c) for fx, fy, fc in flips]

        def cp(j, slot, to):
            return pltpu.make_async_remote_copy(
                src_ref=p_ref, dst_ref=out_ref.at[slot], send_sem=send_sems.at[j], recv_sem=recv_sems.at[j],
                device_id=to, device_id_type=MESH)

        sends = [cp(j, me, to) for j, to in enumerate(peers)]
        for s in sends:
            s.start()
        for j, (px, py, pc) in enumerate(peers):
            cp(j, 4 * px + 2 * py + pc, (px, py, pc)).wait_recv()
        for s in sends:
            s.wait_send()
        local.wait()

    return pl.pallas_call(
        body, name=name, out_shape=jax.ShapeDtypeStruct((N_DEV,) + pack.shape, pack.dtype), in_specs=[HBM],
        out_specs=HBM,
        scratch_shapes=[pltpu.SemaphoreType.DMA((7,)), pltpu.SemaphoreType.DMA((7,)), pltpu.SemaphoreType.DMA],
    )(pack)


SLAB_ROWS = 256


def _pair_sum(name, g, got, core):
    _, nch, r, c = g.shape
    t = _pick_tile(r, (SLAB_ROWS, 128, 64, 8))
    per = r // t

    def body(i, v, car):
        return [v[0] + v[1]], [], []

    return _seq_call(
        name, body, steps=nch * per, prefetch=core,
        ins=[(g, (None, None, t, c), lambda i, pre: (pre[0], i // per, i % per, 0)),
             (got, (None, t, c), lambda i, pre: (i // per, i % per, 0))],
        outs=[((nch, r, c), F32, (None, t, c), lambda i, pre: (i // per, i % per, 0))])[0]


def _chip_sum(name, parts):
    nch, r, c = parts.shape
    t = _pick_tile(r, (SLAB_ROWS, 128, 64, 8))

    def body(i, v, car):
        acc = v[0]
        for k in range(1, nch):
            acc = acc + v[k]
        return [acc], [], []

    return _seq_call(
        name, body, steps=r // t,
        ins=[(parts, (None, t, c), (lambda i, _k=k: (_k, i, 0))) for k in range(nch)],
        outs=[((r, c), F32, (t, c), lambda i: (i, 0))])[0]


def _adamw_math(w, g, m, v):
    m = ADAM_B1 * m + (1.0 - ADAM_B1) * g
    v = ADAM_B2 * v + (1.0 - ADAM_B2) * (g * g)
    m_hat = m / (1.0 - ADAM_B1 ** ADAM_STEP)
    v_hat = v / (1.0 - ADAM_B2 ** ADAM_STEP)
    delta = -ADAM_LR * (m_hat / (jnp.sqrt(v_hat) + ADAM_EPS) + ADAM_WD * w)
    return delta, m, v


def _adamw(name, w, g, m, v):
    n, r, c = w.shape
    t = _pick_tile(r, (SLAB_ROWS, 128, 64, 8))
    per = r // t

    def body(i, vals, car):
        return list(_adamw_math(*vals)), [], []

    blk = lambda a: (a, (None, t, c), lambda i: (i // per, i % per, 0))
    out = ((n, r, c), F32, (None, t, c), lambda i: (i // per, i % per, 0))
    return _seq_call(name, body, steps=n * per, ins=[blk(w), blk(g), blk(m), blk(v)], outs=[out] * 3)


def _ordered_sum(name, packs):
    n, r, c = packs.shape

    def body(i, v, car):
        acc = v[0][0]
        for k in range(1, n):
            acc = acc + v[0][k]
        return [acc], [], []

    return _seq_call(name, body, steps=1, ins=[_whole(packs)], outs=[((r, c), F32, (r, c), lambda i: (0, 0))])[0]


def _pack(arrays):
    flat = []
    for a in arrays:
        a = a.reshape(-1).astype(F32)
        pad = (-a.shape[0]) % LANES
        flat.append(jnp.pad(a, (0, pad)) if pad else a)
    v = jnp.concatenate(flat)
    pad = (-v.shape[0]) % (64 * LANES)
    if pad:
        v = jnp.pad(v, (0, pad))
    return v.reshape(-1, LANES)


def _unpack(pack, shapes):
    v = pack.reshape(-1)
    out, off = [], 0
    for s in shapes:
        n = math.prod(s)
        out.append(v[off:off + n].reshape(s))
        off += n + ((-n) % LANES)
    return out


def _reorder_in(full):
    g0 = 8 * GROUP_WIDTH
    pad = jnp.zeros(full.shape[:-1] + (LANES - 2 * N_HEADS,), full.dtype)
    return jnp.concatenate([full[..., :g0], full[..., g0 + 2 * N_HEADS:], full[..., g0:g0 + 2 * N_HEADS], pad], axis=-1)


def _restore_in(padded):
    g0 = 8 * GROUP_WIDTH
    wide = 12 * GROUP_WIDTH
    return jnp.concatenate([padded[..., :g0], padded[..., wide:wide + 2 * N_HEADS], padded[..., g0:wide]], axis=-1)


def _reduce_big(tag, g, core):
    got = _pair_exchange(f"pair_exchange_{tag}", g)
    pair = _pair_sum(f"pair_sum_{tag}", g, got, core)
    parts = _scatter_partials(f"scatter_partials_{tag}", pair)
    mine = _chip_sum(f"chip_sum_{tag}", parts)
    return _pair_share(f"pair_share_{tag}", mine)


def kernel(x, lower_bounds, norm_mix_pre, norm_mix_post, norm_ff_pre, norm_ff_post, w_in, w_out, hgrn_norm_w, gdn_conv_w, gdn_a_log, gdn_dt_bias, gdn_norm_w, gmlp_ln_w, gmlp_ln_b, gmlp_w_s, gmlp_b_s, conv_dw_w, conv_dw_b, conv_ln_w, conv_ln_b, w_ff1, w_ff2, loss_target, m_lower_bounds, m_norm_mix_pre, m_norm_mix_post, m_norm_ff_pre, m_norm_ff_post, m_w_in, m_w_out, m_hgrn_norm_w, m_gdn_conv_w, m_gdn_a_log, m_gdn_dt_bias, m_gdn_norm_w, m_gmlp_ln_w, m_gmlp_ln_b, m_gmlp_w_s, m_gmlp_b_s, m_conv_dw_w, m_conv_dw_b, m_conv_ln_w, m_conv_ln_b, m_w_ff1, m_w_ff2, v_lower_bounds, v_norm_mix_pre, v_norm_mix_post, v_norm_ff_pre, v_norm_ff_post, v_w_in, v_w_out, v_hgrn_norm_w, v_gdn_conv_w, v_gdn_a_log, v_gdn_dt_bias, v_gdn_norm_w, v_gmlp_ln_w, v_gmlp_ln_b, v_gmlp_w_s, v_gmlp_b_s, v_conv_dw_w, v_conv_dw_b, v_conv_ln_w, v_conv_ln_b, v_w_ff1, v_w_ff2):
    args = dict(locals())
    chip = 2 * lax.axis_index("x") + lax.axis_index("y")
    core = lax.axis_index("c").astype(jnp.int32).reshape(1)

    win_all = _gather_weights("gather_w_in", w_in.astype(BF16))
    wout_all = _gather_weights("gather_w_out", w_out.astype(BF16))
    ff1_all = _gather_weights("gather_w_ff1", w_ff1.astype(BF16))
    ff2_all = _gather_weights("gather_w_ff2", w_ff2.astype(BF16))
    cut_shapes = [gdn_conv_w.shape, conv_dw_w.shape]
    cuts = _gather_small("gather_cut_small", _pack([gdn_conv_w, conv_dw_w]))
    cut_parts = [_unpack(cuts[2 * k], cut_shapes) for k in range(N_CHIPS)]
    sp = {k: args[k] for k in SMALL}
    sp["gdn_conv_w"] = jnp.concatenate([p[0] for p in cut_parts], axis=-1)
    sp["conv_dw_w"] = jnp.concatenate([p[1] for p in cut_parts], axis=-1)

    w_in_l, w_out_l, w_ff1_l, w_ff2_l = [], [], [], []
    for l in range(DEPTH):
        full = jnp.concatenate([win_all[k, l] for k in range(N_CHIPS)], axis=-1)
        w_in_l.append(_reorder_in(full))
        w_out_l.append(wout_all[:, l].reshape(D_MODEL, D_MODEL))
        w_ff1_l.append(jnp.concatenate([ff1_all[k, l] for k in range(N_CHIPS)], axis=-1))
        w_ff2_l.append(ff2_all[:, l].reshape(D_FF, D_MODEL))

    loss_part, grad_x, small_g, g_in, g_out, g_ff1, g_ff2 = _local_step(
        x[0], loss_target[0], sp, w_in_l, w_out_l, w_ff1_l, w_ff2_l)

    cut_cols = lambda g, n: jnp.stack(jnp.split(g, n, axis=-1))
    gi = jnp.stack([cut_cols(_restore_in(g_in[l]), N_CHIPS) for l in range(DEPTH)])
    go = jnp.stack([g_out[l].reshape(N_CHIPS, GROUP_WIDTH, D_MODEL) for l in range(DEPTH)])
    g1 = jnp.stack([cut_cols(g_ff1[l], N_CHIPS) for l in range(DEPTH)])
    g2 = jnp.stack([g_ff2[l].reshape(N_CHIPS, D_FF // N_CHIPS, D_MODEL) for l in range(DEPTH)])
    big_g = dict(w_in=_reduce_big("w_in", gi, core), w_out=_reduce_big("w_out", go, core),
                 w_ff1=_reduce_big("w_ff1", g1, core), w_ff2=_reduce_big("w_ff2", g2, core))

    names = SMALL + ["loss"]
    small_g["loss"] = loss_part.reshape(1)
    shapes = [small_g[k].shape for k in names]
    total = _ordered_sum("sum_small", _gather_small("gather_small", _pack([small_g[k] for k in names])))
    summed = dict(zip(names, _unpack(total, shapes)))
    loss = summed.pop("loss")[0]
    for k, width in (("gdn_conv_w", gdn_conv_w.shape[-1]), ("conv_dw_w", conv_dw_w.shape[-1])):
        summed[k] = lax.dynamic_slice_in_dim(summed[k], chip * width, width, axis=-1)

    grads, deltas, new_m, new_v = {}, {}, {}, {}
    for k in ("w_in", "w_out", "w_ff1", "w_ff2"):
        grads[k] = big_g[k]
        deltas[k], new_m[k], new_v[k] = _adamw(f"adamw_{k}", args[k], big_g[k], args["m_" + k], args["v_" + k])
    local_shapes = [args[k].shape for k in SMALL]
    packs = [_pack([src[k] for k in SMALL]) for src in (
        {k: args[k] for k in SMALL}, summed, {k: args["m_" + k] for k in SMALL}, {k: args["v_" + k] for k in SMALL})]
    d_s, m_s, v_s = _adamw("adamw_small", *[p[None] for p in packs])
    for k, d, mm, vv in zip(SMALL, _unpack(d_s[0], local_shapes), _unpack(m_s[0], local_shapes),
                            _unpack(v_s[0], local_shapes)):
        grads[k], deltas[k], new_m[k], new_v[k] = summed[k], d, mm, vv

    order = ["lower_bounds", "norm_mix_pre", "norm_mix_post", "norm_ff_pre", "norm_ff_post", "w_in", "w_out",
             "hgrn_norm_w", "gdn_conv_w", "gdn_a_log", "gdn_dt_bias", "gdn_norm_w", "gmlp_ln_w", "gmlp_ln_b",
             "gmlp_w_s", "gmlp_b_s", "conv_dw_w", "conv_dw_b", "conv_ln_w", "conv_ln_b", "w_ff1", "w_ff2"]
    return (loss, grad_x[None], *[grads[k] for k in order], *[deltas[k] for k in order],
            *[new_m[k] for k in order], *[new_v[k] for k in order])
```

```python
import functools
import math

import jax
import jax.numpy as jnp
from jax import lax
from jax.experimental import pallas as pl
from jax.experimental.pallas import tpu as pltpu

F32 = jnp.float32
BF16 = jnp.bfloat16

DEPTH = 2
D_MODEL = 2048
GROUP_WIDTH = 512
HEAD_DIM = 128
N_HEADS = 4
CHUNK = 64
SHORT_CONV = 4
MIX_CHUNK = 128
CONV_WIDTH = 31
D_FF = 4 * D_MODEL
D_IN_PROJ = 12 * GROUP_WIDTH + 2 * N_HEADS
EPS = 1e-6
NEG_BIG = -1e30
TINY = 1e-30
ADAM_LR = 0.001
ADAM_B1 = 0.9
ADAM_B2 = 0.999
ADAM_EPS = 1e-08
ADAM_WD = 0.01
ADAM_STEP = 10

LANES = 128
P_IN = 12 * GROUP_WIDTH + LANES
SUB_BLOCK = 16
HGRN_TILE = 128
GDN_TILE = 128
CONV_TILE = 128
CONV_HALO = 32
VMEM_LIMIT = 56 * 1024 * 1024
N_CHIPS = 4
N_DEV = 8
MESH = pl.DeviceIdType.MESH


_DIMS = {
    "nn": (((1,), (0,)), ((), ())),
    "nt": (((1,), (1,)), ((), ())),
    "tn": (((0,), (0,)), ((), ())),
}


def _split2(a):
    hi = a.astype(BF16)
    return hi, (a - hi.astype(F32)).astype(BF16)


def _raw_mm(a, b, mode, exact):
    dot = lambda p, q: lax.dot_general(p, q, _DIMS[mode], preferred_element_type=F32)
    if not exact:
        return dot(a.astype(BF16), b.astype(BF16))
    a_hi, a_lo = _split2(a)
    b_hi, b_lo = _split2(b)
    return dot(a_hi, b_hi) + (dot(a_hi, b_lo) + dot(a_lo, b_hi))


@functools.partial(jax.custom_vjp, nondiff_argnums=(2, 3))
def _mm(a, b, mode, exact):
    return _raw_mm(a, b, mode, exact)


def _mm_fwd(a, b, mode, exact):
    return _raw_mm(a, b, mode, exact), (a, b)


def _mm_bwd(mode, exact, res, g):
    a, b = res
    if mode == "nn":
        return _raw_mm(g, b, "nt", exact), _raw_mm(a, g, "tn", exact)
    if mode == "nt":
        return _raw_mm(g, b, "nn", exact), _raw_mm(g, a, "tn", exact)
    return _raw_mm(b, g, "nt", exact), _raw_mm(a, g, "nn", exact)


_mm.defvjp(_mm_fwd, _mm_bwd)


def _sig(x):
    return jax.nn.sigmoid(x)


def _silu(x):
    return x * jax.nn.sigmoid(x)


def _gelu(x):
    return 0.5 * x * (1.0 + lax.erf(x * (1.0 / math.sqrt(2.0))))


def _rms(x, w):
    return x * lax.rsqrt(jnp.mean(x * x, axis=-1, keepdims=True) + EPS) * w


def _ln(x, w, b):
    mu = jnp.mean(x, axis=-1, keepdims=True)
    xc = x - mu
    var = jnp.mean(xc * xc, axis=-1, keepdims=True)
    return xc * lax.rsqrt(var + EPS) * w + b


def _iota(shape, dim):
    return lax.broadcasted_iota(jnp.int32, shape, dim)


def _tri_mm(x, mode):
    n = x.shape[0]
    tri = (_iota((n, n), 0) >= _iota((n, n), 1)).astype(BF16)
    x1 = x.astype(BF16)
    r1 = x - x1.astype(F32)
    x2 = r1.astype(BF16)
    x3 = (r1 - x2.astype(F32)).astype(BF16)
    dot = lambda q: lax.dot_general(tri, q, _DIMS[mode], preferred_element_type=F32)
    return dot(x1) + (dot(x2) + dot(x3))


@jax.custom_vjp
def _cumsum_rows(x):
    return _tri_mm(x, "nn")


_cumsum_rows.defvjp(lambda x: (_tri_mm(x, "nn"), None), lambda _, g: (_tri_mm(g, "tn"),))


def _hgrn_head(q, k, v, b, st):
    n = q.shape[0]
    ii = _iota((n, 1), 0)
    zpad = jnp.zeros((SUB_BLOCK, HEAD_DIM), F32)
    k_ext = jnp.concatenate([zpad, k], axis=0)
    b_ext = jnp.concatenate([zpad, b], axis=0)
    v_ext = jnp.concatenate([zpad, v], axis=0)
    o = jnp.zeros((n, HEAD_DIM), F32)
    for d in range(SUB_BLOCK):
        ks = k_ext[SUB_BLOCK - d:SUB_BLOCK - d + n]
        bs = b_ext[SUB_BLOCK - d:SUB_BLOCK - d + n]
        vs = v_ext[SUB_BLOCK - d:SUB_BLOCK - d + n]
        e = jnp.exp(jnp.where((ii % SUB_BLOCK) >= d, b - bs, NEG_BIG))
        o = o + jnp.sum(q * ks * e, axis=-1, keepdims=True) * vs
    blocks = [o[0:SUB_BLOCK]]
    for blk in range(1, n // SUB_BLOCK):
        lo = SUB_BLOCK * blk
        r = b[lo - 1:lo]
        a_q = q[lo:lo + SUB_BLOCK] * jnp.exp(b[lo:lo + SUB_BLOCK] - r)
        b_k = jnp.where(ii < lo, k * jnp.exp(jnp.minimum(r - b, 0.0)), 0.0)
        sc = _mm(a_q, b_k, "nt", False)
        blocks.append(o[lo:lo + SUB_BLOCK] + _mm(sc, v, "nn", False))
    o = jnp.concatenate(blocks, axis=0)
    o = o + _mm(q * jnp.exp(b), st, "nt", False)
    b_end = b[n - 1:n]
    st_new = st * jnp.exp(b_end) + _mm(v, k * jnp.exp(b_end - b), "tn", False)
    return o, st_new


def _hgrn_chunk(layer, lbp, nw, aq, af, ai, ag, states):
    rows = [lbp[i:i + 1, :] for i in range(DEPTH)]
    mx = functools.reduce(jnp.maximum, rows)
    es = [jnp.exp(r - mx) for r in rows]
    den = functools.reduce(lambda p, s: p + s, es)
    soft = [e / den for e in es]
    lb = functools.reduce(lambda p, s: p + s, soft[:layer + 1]) - soft[0]
    f = lb + (1.0 - lb) * _sig(af)
    logf = jnp.log(jnp.maximum(f, TINY))
    k = (1.0 - lb) * _sig(-af)
    q = _silu(aq)
    b = _cumsum_rows(logf)
    outs, new_states = [], []
    for h in range(N_HEADS):
        sl = slice(HEAD_DIM * h, HEAD_DIM * (h + 1))
        o, st = _hgrn_head(q[:, sl], k[:, sl], ai[:, sl], b[:, sl], states[h])
        outs.append(_rms(o, nw) * _silu(ag[:, sl]))
        new_states.append(st)
    return jnp.concatenate(outs, axis=1), new_states


def _hgrn_tile(layer, lbp, nw, aq, af, ai, ag, states):
    outs = []
    for ci in range(aq.shape[0] // CHUNK):
        rs = slice(CHUNK * ci, CHUNK * (ci + 1))
        o, states = _hgrn_chunk(layer, lbp, nw, aq[rs], af[rs], ai[rs], ag[rs], states)
        outs.append(o)
    return jnp.concatenate(outs, axis=0), states


def _short_conv(prev, cur, w):
    n = cur.shape[0]
    ext = jnp.concatenate([prev[n - 8:n], cur], axis=0)
    y = jnp.zeros_like(cur)
    for t in range(SHORT_CONV):
        off = 8 - (SHORT_CONV - 1) + t
        y = y + w[t:t + 1, :] * ext[off:off + n]
    return _silu(y)


def _gdn_head(q, k, v, beta, gc, gcr, st):
    n = q.shape[0]
    ii = _iota((n, n), 0)
    jj = _iota((n, n), 1)
    gamma = jnp.exp(jnp.where(ii >= jj, gc - gcr, NEG_BIG))
    kb = k * beta
    m = jnp.where(ii > jj, _mm(kb, k, "nt", False) * gamma, 0.0)
    eye = (ii == jj).astype(F32)
    inv = eye - m
    p = m
    steps = max(1, int(math.ceil(math.log2(n))) - 1)
    for _ in range(steps):
        p = _mm(p, p, "nn", True)
        inv = inv + _mm(inv, p, "nn", True)
    eg = jnp.exp(gc)
    u = _mm(inv, v * beta, "nn", True)
    w = _mm(inv, kb * eg, "nn", True)
    qk = _mm(q, k, "nt", False) * gamma
    qd = q * eg
    g_end = gc[n - 1:n]
    kd = k * jnp.exp(g_end - gc)
    v_new = u - _mm(w, st, "nt", False)
    out = _mm(qd, st, "nt", False) + _mm(qk, v_new, "nn", False)
    st_new = st * jnp.exp(g_end) + _mm(v_new, kd, "tn", False)
    return out, st_new


def _gdn_chunk(cw, alog, dtb, nw, pq, pk, pv, cq, ck, cv, bz, ab, states):
    beta_all = _sig(ab)
    g_all = -jnp.exp(alog) * jax.nn.softplus(ab + dtb)
    gc_all = _cumsum_rows(g_all)
    gc_t = gc_all.T
    outs, new_states = [], []
    for h in range(N_HEADS):
        sl = slice(HEAD_DIM * h, HEAD_DIM * (h + 1))
        wq = cw[:, HEAD_DIM * h:HEAD_DIM * (h + 1)]
        wk = cw[:, GROUP_WIDTH + HEAD_DIM * h:GROUP_WIDTH + HEAD_DIM * (h + 1)]
        wv = cw[:, 2 * GROUP_WIDTH + HEAD_DIM * h:2 * GROUP_WIDTH + HEAD_DIM * (h + 1)]
        q = _short_conv(pq[:, sl], cq[:, sl], wq)
        k = _short_conv(pk[:, sl], ck[:, sl], wk)
        v = _short_conv(pv[:, sl], cv[:, sl], wv)
        q = q * lax.rsqrt(jnp.sum(q * q, axis=-1, keepdims=True) + EPS) * (HEAD_DIM ** -0.5)
        k = k * lax.rsqrt(jnp.sum(k * k, axis=-1, keepdims=True) + EPS)
        beta = beta_all[:, h:h + 1]
        gc = gc_all[:, N_HEADS + h:N_HEADS + h + 1]
        gcr = gc_t[N_HEADS + h:N_HEADS + h + 1, :]
        o, st = _gdn_head(q, k, v, beta, gc, gcr, states[h])
        outs.append(_rms(o, nw) * _silu(bz[:, sl]))
        new_states.append(st)
    return jnp.concatenate(outs, axis=1), new_states


def _gdn_tile(cw, alog, dtb, nw, pq, pk, pv, cq, ck, cv, bz, ab, states):
    n = cq.shape[0]
    outs = []
    for ci in range(n // CHUNK):
        rs = slice(CHUNK * ci, CHUNK * (ci + 1))
        if ci == 0:
            prev = [p[n - CHUNK:n] for p in (pq, pk, pv)]
        else:
            ps = slice(CHUNK * (ci - 1), CHUNK * ci)
            prev = [cq[ps], ck[ps], cv[ps]]
        o, states = _gdn_chunk(cw, alog, dtb, nw, *prev, cq[rs], ck[rs], cv[rs], bz[rs], ab[rs], states)
        outs.append(o)
    return jnp.concatenate(outs, axis=0), states


def _gmlp_tile(ln_w, ln_b, ws, bs_cols, cu, cv):
    u = _gelu(cu)
    v = _ln(_gelu(cv), ln_w, ln_b)
    n = cu.shape[0]
    tril = _iota((n, n), 0) >= _iota((n, n), 1)
    outs = []
    for h in range(N_HEADS):
        sl = slice(HEAD_DIM * h, HEAD_DIM * (h + 1))
        wc = jnp.where(tril, ws[h], 0.0)
        outs.append(_mm(wc, v[:, sl], "nn", False) + bs_cols[:, h:h + 1])
    return u * jnp.concatenate(outs, axis=1)


def _conformer_tile(dw_w, dw_b, ln_w, ln_b, pa, pg, ca, cg):
    n = ca.shape[0]
    yp = pa[n - CONV_HALO:n] * _sig(pg[n - CONV_HALO:n])
    ext = jnp.concatenate([yp, ca * _sig(cg)], axis=0)
    acc = jnp.zeros_like(ca)
    for t in range(CONV_WIDTH):
        off = CONV_HALO - (CONV_WIDTH - 1) + t
        acc = acc + dw_w[t:t + 1, :] * ext[off:off + n]
    return _silu(_ln(acc + dw_b, ln_w, ln_b))


def _seq_call(name, body, *, steps, ins, outs, accs=(), carries=(), reverse=False, prefetch=None):
    n_in, n_out, n_acc, n_car = len(ins), len(outs), len(accs), len(carries)
    n_pre = 0 if prefetch is None else len(prefetch)

    def logical(g):
        return (steps - 1 - g) if reverse else g

    def kern(*refs):
        pre = refs[:n_pre]
        refs = refs[n_pre:]
        in_refs = refs[:n_in]
        out_refs = refs[n_in:n_in + n_out]
        acc_refs = refs[n_in + n_out:n_in + n_out + n_acc]
        car_refs = refs[n_in + n_out + n_acc:]
        g = pl.program_id(0)

        @pl.when(g == 0)
        def _():
            for r in list(acc_refs) + list(car_refs):
                r[...] = jnp.zeros(r.shape, r.dtype)

        o, a, c = body(logical(g), [r[...] for r in in_refs], [r[...] for r in car_refs], *((pre,) if n_pre else ()))
        for r, v in zip(out_refs, o, strict=True):
            r[...] = v.astype(r.dtype)
        for r, v in zip(acc_refs, a, strict=True):
            r[...] += v
        for r, v in zip(car_refs, c, strict=True):
            r[...] = v

    def spec(block, fn):
        return pl.BlockSpec(block, lambda g, *pre: fn(logical(g), *((pre,) if n_pre else ())))

    in_specs = [spec(bs, fn) for (_, bs, fn) in ins]
    out_specs = [spec(bs, fn) for (_, _, bs, fn) in outs]
    out_specs += [pl.BlockSpec(shape, lambda g, *pre, _n=len(shape): (0,) * _n) for (shape, _) in accs]
    out_shape = [jax.ShapeDtypeStruct(s, d) for (s, d, _, _) in outs]
    out_shape += [jax.ShapeDtypeStruct(s, d) for (s, d) in accs]
    grid_spec = pltpu.PrefetchScalarGridSpec(
        num_scalar_prefetch=n_pre, grid=(steps,), in_specs=in_specs, out_specs=out_specs,
        scratch_shapes=[pltpu.VMEM(s, d) for (s, d) in carries])
    args = ([] if prefetch is None else list(prefetch)) + [a for (a, _, _) in ins]
    return pl.pallas_call(
        kern, name=name, grid_spec=grid_spec, out_shape=out_shape,
        compiler_params=pltpu.CompilerParams(dimension_semantics=("arbitrary",), vmem_limit_bytes=VMEM_LIMIT),
    )(*args)


def _whole(a):
    nd = a.ndim
    return (a, a.shape, lambda i, *pre: (0,) * nd)


def _rows(a, tile, col=0, width=None, shift=0):
    width = a.shape[1] if width is None else width
    if shift:
        return (a, (tile, width), lambda i, *pre: (jnp.maximum(i + shift, 0), col))
    return (a, (tile, width), lambda i, *pre: (i, col))


def _row_out(n_rows, width, dtype, tile):
    return ((n_rows, width), dtype, (tile, width), lambda i, *pre: (i, 0))


def _pick_tile(n, prefs):
    for t in prefs:
        if n % t == 0:
            return t
    return n


def _matmul(name, a, b, mode, out_dtypes, epilogue=None, extras=(), place=None, into=None, max_tm=1024):
    if mode == "nn":
        (m, k), n = a.shape, b.shape[1]
    elif mode == "nt":
        (m, k), n = a.shape, b.shape[0]
    else:
        (k, m), n = a.shape, b.shape[1]
    tm = _pick_tile(m, tuple(t for t in (1024, 512, 256, 128) if t <= max_tm))
    tn = _pick_tile(n, (1024, 896, 512, 256, 128))
    tk = _pick_tile(k, (1024, 896, 512, 256, 128))
    nk = k // tk
    n_ex = len(extras)
    n_out = len(out_dtypes)

    n_into = 0 if into is None else 1

    def kern(*refs):
        a_ref, b_ref = refs[0], refs[1]
        ex_refs = refs[2:2 + n_ex]
        out_refs = refs[2 + n_ex + n_into:2 + n_ex + n_into + n_out]
        acc_ref = refs[2 + n_ex + n_into + n_out]
        kk = pl.program_id(2)

        @pl.when(kk == 0)
        def _():
            acc_ref[...] = jnp.zeros(acc_ref.shape, F32)

        acc_ref[...] += lax.dot_general(a_ref[...], b_ref[...], _DIMS[mode], preferred_element_type=F32)

        @pl.when(kk == nk - 1)
        def _():
            acc = acc_ref[...]
            vals = (acc,) if epilogue is None else epilogue(acc, *[r[...] for r in ex_refs])
            for r, v in zip(out_refs, vals, strict=True):
                r[...] = v.astype(r.dtype)

    if mode == "tn":
        a_spec = pl.BlockSpec((tk, tm), lambda i, j, kk: (kk, i))
    else:
        a_spec = pl.BlockSpec((tm, tk), lambda i, j, kk: (i, kk))
    if mode == "nt":
        b_spec = pl.BlockSpec((tn, tk), lambda i, j, kk: (j, kk))
    else:
        b_spec = pl.BlockSpec((tk, tn), lambda i, j, kk: (kk, j))
    tile = pl.BlockSpec((tm, tn), lambda i, j, kk: (i, j))
    out_specs = [tile] * n_out
    out_shape = [jax.ShapeDtypeStruct((m, n), d) for d in out_dtypes]
    if place is not None:
        shape, block_fn, index_fn = place
        out_specs = [pl.BlockSpec(block_fn(tm, tn), lambda i, j, kk: index_fn(i, j, tm, tn))]
        out_shape = [jax.ShapeDtypeStruct(shape, out_dtypes[0])]
    return pl.pallas_call(
        kern, name=name, grid=(m // tm, n // tn, nk),
        in_specs=[a_spec, b_spec] + [tile] * n_ex + [HBM] * n_into,
        out_specs=out_specs, out_shape=out_shape,
        input_output_aliases=({2 + n_ex: 0} if n_into else {}),
        scratch_shapes=[pltpu.VMEM((tm, tn), F32)],
        compiler_params=pltpu.CompilerParams(
            dimension_semantics=("parallel", "parallel", "arbitrary"), vmem_limit_bytes=VMEM_LIMIT),
    )(a, b, *extras, *([into] if n_into else []))


ROW_TILE = 256


def _rms_fwd(name, x, w):
    s, d = x.shape
    t = _pick_tile(s, (ROW_TILE,))

    def body(i, v, c):
        return [_rms(v[0], v[1])], [], []

    return _seq_call(name, body, steps=s // t, ins=[_rows(x, t), _whole(w)], outs=[_row_out(s, d, BF16, t)])[0]


def _resid_rms_fwd(name, x, y, w):
    s, d = x.shape
    t = _pick_tile(s, (ROW_TILE,))

    def body(i, v, c):
        return [v[0] + _rms(v[1], v[2])], [], []

    return _seq_call(name, body, steps=s // t, ins=[_rows(x, t), _rows(y, t), _whole(w)],
                     outs=[_row_out(s, d, F32, t)])[0]


def _rms_bwd(name, x, w, dh, dres):
    s, d = x.shape
    t = _pick_tile(s, (ROW_TILE,))

    def body(i, v, c):
        _, vjp = jax.vjp(_rms, v[0], v[1])
        dx, dw = vjp(v[2])
        return [dx + v[3]], [dw], []

    return _seq_call(name, body, steps=s // t, ins=[_rows(x, t), _whole(w), _rows(dh, t), _rows(dres, t)],
                     outs=[_row_out(s, d, F32, t)], accs=[((1, d), F32)])


def _resid_rms_bwd(name, y, w, dxo):
    s, d = y.shape
    t = _pick_tile(s, (ROW_TILE,))

    def body(i, v, c):
        _, vjp = jax.vjp(_rms, v[0], v[1])
        dy, dw = vjp(v[2])
        return [dy], [dw], []

    return _seq_call(name, body, steps=s // t, ins=[_rows(y, t), _whole(w), _rows(dxo, t)],
                     outs=[_row_out(s, d, BF16, t)], accs=[((1, d), F32)])


def _loss_head(name, y, target):
    s, d = y.shape
    t = _pick_tile(s, (ROW_TILE,))

    def body(i, v, c):
        err = v[0] - v[1]
        part = 0.5 * jnp.sum(jnp.mean(err * err, axis=-1, keepdims=True))
        return [err * (1.0 / d)], [jnp.full((1, LANES), part, F32)], []

    return _seq_call(name, body, steps=s // t, ins=[_rows(y, t), _rows(target, t)],
                     outs=[_row_out(s, d, F32, t)], accs=[((1, LANES), F32)])


SEG = {n: i for i, n in enumerate(
    ["a_q", "a_f", "a_i", "a_g", "b_q", "b_k", "b_v", "b_z", "c_u", "c_v", "d_a", "d_gate"])}
AB_COL = 12 * GROUP_WIDTH // LANES


def _seg(proj, name, tile, shift=0):
    return _rows(proj, tile, col=SEG[name], width=GROUP_WIDTH, shift=shift)


def _state_block():
    return (1, N_HEADS * HEAD_DIM, HEAD_DIM), lambda i, *pre: (i, 0, 0)


def _split_states(blk):
    return [blk[0, HEAD_DIM * h:HEAD_DIM * (h + 1), :] for h in range(N_HEADS)]


STATE_CARRIES = [((HEAD_DIM, HEAD_DIM), F32)] * N_HEADS


def _hgrn_fwd(layer, proj, lbp, nw):
    s = proj.shape[0]
    n = s // HGRN_TILE
    sb, sf = _state_block()

    def body(i, v, st):
        o, new = _hgrn_tile(layer, v[0], v[1], v[2], v[3], v[4], v[5], st)
        return [o, jnp.concatenate(st, axis=0)[None]], [], new

    return _seq_call(
        f"hgrn_fwd{layer}", body, steps=n,
        ins=[_whole(lbp), _whole(nw)] + [_seg(proj, k, HGRN_TILE) for k in ("a_q", "a_f", "a_i", "a_g")],
        outs=[_row_out(s, GROUP_WIDTH, BF16, HGRN_TILE), ((n, N_HEADS * HEAD_DIM, HEAD_DIM), F32, sb, sf)],
        carries=STATE_CARRIES)


def _hgrn_bwd(layer, proj, lbp, nw, states, dmix):
    s = proj.shape[0]
    n = s // HGRN_TILE
    sb, sf = _state_block()

    def body(i, v, dst):
        st = _split_states(v[6])

        def f(lbp_, nw_, aq, af, ai, ag, *st_):
            return _hgrn_tile(layer, lbp_, nw_, aq, af, ai, ag, list(st_))

        _, vjp = jax.vjp(f, v[0], v[1], v[2], v[3], v[4], v[5], *st)
        g = vjp((v[7], list(dst)))
        return [jnp.concatenate(g[2:6], axis=1)], [g[0], g[1]], list(g[6:])

    return _seq_call(
        f"hgrn_bwd{layer}", body, steps=n, reverse=True,
        ins=[_whole(lbp), _whole(nw)] + [_seg(proj, k, HGRN_TILE) for k in ("a_q", "a_f", "a_i", "a_g")]
        + [(states, sb, sf), _rows(dmix, HGRN_TILE, col=0, width=GROUP_WIDTH)],
        outs=[_row_out(s, 4 * GROUP_WIDTH, BF16, HGRN_TILE)],
        accs=[(lbp.shape, F32), (nw.shape, F32)], carries=STATE_CARRIES)


def _gdn_ins(proj, cw, alog, dtb, nw):
    return ([_whole(cw), _whole(alog), _whole(dtb), _whole(nw)]
            + [_seg(proj, k, GDN_TILE, shift=-1) for k in ("b_q", "b_k", "b_v")]
            + [_seg(proj, k, GDN_TILE) for k in ("b_q", "b_k", "b_v", "b_z")]
            + [_rows(proj, GDN_TILE, col=AB_COL, width=LANES)])


def _mask_prev(i, vals):
    keep = (i > 0).astype(F32)
    return [p * keep for p in vals]


def _gdn_fwd(layer, proj, cw, alog, dtb, nw):
    s = proj.shape[0]
    n = s // GDN_TILE
    sb, sf = _state_block()

    def body(i, v, st):
        prev = _mask_prev(i, v[4:7])
        o, new = _gdn_tile(v[0], v[1], v[2], v[3], *prev, *v[7:12], st)
        return [o, jnp.concatenate(st, axis=0)[None]], [], new

    return _seq_call(
        f"gdn_fwd{layer}", body, steps=n, ins=_gdn_ins(proj, cw, alog, dtb, nw),
        outs=[_row_out(s, GROUP_WIDTH, BF16, GDN_TILE), ((n, N_HEADS * HEAD_DIM, HEAD_DIM), F32, sb, sf)],
        carries=STATE_CARRIES)


def _gdn_bwd(layer, proj, cw, alog, dtb, nw, states, dmix):
    s = proj.shape[0]
    n = s // GDN_TILE
    sb, sf = _state_block()

    def body(i, v, car):
        dst, dprev = car[:N_HEADS], car[N_HEADS:]
        prev = _mask_prev(i, v[4:7])
        st = _split_states(v[12])

        def f(cw_, alog_, dtb_, nw_, pq, pk, pv, cq, ck, cv, bz, ab, *st_):
            return _gdn_tile(cw_, alog_, dtb_, nw_, pq, pk, pv, cq, ck, cv, bz, ab, list(st_))

        _, vjp = jax.vjp(f, v[0], v[1], v[2], v[3], *prev, *v[7:12], *st)
        g = vjp((v[13], list(dst)))
        dcur = [g[7] + dprev[0], g[8] + dprev[1], g[9] + dprev[2], g[10]]
        return ([jnp.concatenate(dcur, axis=1), g[11]], list(g[0:4]), list(g[12:]) + list(g[4:7]))

    return _seq_call(
        f"gdn_bwd{layer}", body, steps=n, reverse=True,
        ins=_gdn_ins(proj, cw, alog, dtb, nw) + [(states, sb, sf), _rows(dmix, GDN_TILE, col=1, width=GROUP_WIDTH)],
        outs=[_row_out(s, 4 * GROUP_WIDTH, BF16, GDN_TILE), _row_out(s, LANES, BF16, GDN_TILE)],
        accs=[(cw.shape, F32), (alog.shape, F32), (dtb.shape, F32), (nw.shape, F32)],
        carries=STATE_CARRIES + [((GDN_TILE, GROUP_WIDTH), F32)] * 3)


def _gmlp_fwd(layer, proj, ln_w, ln_b, ws, bs_cols):
    s = proj.shape[0]

    def body(i, v, c):
        return [_gmlp_tile(*v)], [], []

    return _seq_call(
        f"gmlp_fwd{layer}", body, steps=s // MIX_CHUNK,
        ins=[_whole(ln_w), _whole(ln_b), _whole(ws), _whole(bs_cols),
             _seg(proj, "c_u", MIX_CHUNK), _seg(proj, "c_v", MIX_CHUNK)],
        outs=[_row_out(s, GROUP_WIDTH, BF16, MIX_CHUNK)])[0]


def _gmlp_bwd(layer, proj, ln_w, ln_b, ws, bs_cols, dmix):
    s = proj.shape[0]

    def body(i, v, c):
        _, vjp = jax.vjp(_gmlp_tile, *v[:6])
        g = vjp(v[6])
        return [jnp.concatenate(g[4:6], axis=1)], list(g[0:4]), []

    return _seq_call(
        f"gmlp_bwd{layer}", body, steps=s // MIX_CHUNK,
        ins=[_whole(ln_w), _whole(ln_b), _whole(ws), _whole(bs_cols),
             _seg(proj, "c_u", MIX_CHUNK), _seg(proj, "c_v", MIX_CHUNK),
             _rows(dmix, MIX_CHUNK, col=2, width=GROUP_WIDTH)],
        outs=[_row_out(s, 2 * GROUP_WIDTH, BF16, MIX_CHUNK)],
        accs=[(ln_w.shape, F32), (ln_b.shape, F32), (ws.shape, F32), (bs_cols.shape, F32)])


def _conformer_ins(proj, dw_w, dw_b, ln_w, ln_b):
    return ([_whole(dw_w), _whole(dw_b), _whole(ln_w), _whole(ln_b)]
            + [_seg(proj, k, CONV_TILE, shift=-1) for k in ("d_a", "d_gate")]
            + [_seg(proj, k, CONV_TILE) for k in ("d_a", "d_gate")])


def _conformer_fwd(layer, proj, dw_w, dw_b, ln_w, ln_b):
    s = proj.shape[0]

    def body(i, v, c):
        prev = _mask_prev(i, v[4:6])
        return [_conformer_tile(v[0], v[1], v[2], v[3], *prev, v[6], v[7])], [], []

    return _seq_call(
        f"conformer_fwd{layer}", body, steps=s // CONV_TILE, ins=_conformer_ins(proj, dw_w, dw_b, ln_w, ln_b),
        outs=[_row_out(s, GROUP_WIDTH, BF16, CONV_TILE)])[0]


def _conformer_bwd(layer, proj, dw_w, dw_b, ln_w, ln_b, dmix):
    s = proj.shape[0]

    def body(i, v, dprev):
        prev = _mask_prev(i, v[4:6])
        _, vjp = jax.vjp(_conformer_tile, v[0], v[1], v[2], v[3], *prev, v[6], v[7])
        g = vjp(v[8])
        return [jnp.concatenate([g[6] + dprev[0], g[7] + dprev[1]], axis=1)], list(g[0:4]), [g[4], g[5]]

    return _seq_call(
        f"conformer_bwd{layer}", body, steps=s // CONV_TILE, reverse=True,
        ins=_conformer_ins(proj, dw_w, dw_b, ln_w, ln_b) + [_rows(dmix, CONV_TILE, col=3, width=GROUP_WIDTH)],
        outs=[_row_out(s, 2 * GROUP_WIDTH, BF16, CONV_TILE)],
        accs=[(dw_w.shape, F32), (dw_b.shape, F32), (ln_w.shape, F32), (ln_b.shape, F32)],
        carries=[((CONV_TILE, GROUP_WIDTH), F32)] * 2)


SMALL = ["lower_bounds", "norm_mix_pre", "norm_mix_post", "norm_ff_pre", "norm_ff_post", "hgrn_norm_w",
         "gdn_conv_w", "gdn_a_log", "gdn_dt_bias", "gdn_norm_w", "gmlp_ln_w", "gmlp_ln_b", "gmlp_w_s",
         "gmlp_b_s", "conv_dw_w", "conv_dw_b", "conv_ln_w", "conv_ln_b"]


def _gate_row(v):
    return jnp.pad(v.reshape(1, N_HEADS), ((0, 0), (N_HEADS, LANES - 2 * N_HEADS)))


def _cut_rows(layer, rows, cols):
    return ((DEPTH, N_CHIPS, rows, cols), lambda tm, tn: (None, None, tm, tn),
            lambda i, j, tm, tn: (layer, i // (rows // tm), i % (rows // tm), j))


def _cut_cols(layer, rows, cols):
    return ((DEPTH, N_CHIPS, rows, cols), lambda tm, tn: (None, None, tm, tn),
            lambda i, j, tm, tn: (layer, j // (cols // tn), i, j % (cols // tn)))


def _relu2(acc):
    r = jnp.maximum(acc, 0.0)
    return acc, r * r


def _relu2_bwd(acc, u):
    return (2.0 * jnp.maximum(u, 0.0) * acc,)


def _local_step(x, target, sp, w_in, w_out, w_ff1, w_ff2):
    row = lambda v: v.reshape(1, -1)
    saved = []
    for l in range(DEPTH):
        par = dict(
            lbp=sp["lower_bounds"], hn=row(sp["hgrn_norm_w"][l]), cw=sp["gdn_conv_w"][l],
            alog=_gate_row(sp["gdn_a_log"][l]), dtb=_gate_row(sp["gdn_dt_bias"][l]), gn=row(sp["gdn_norm_w"][l]),
            glw=row(sp["gmlp_ln_w"][l]), glb=row(sp["gmlp_ln_b"][l]), ws=sp["gmlp_w_s"][l],
            bsc=jnp.pad(sp["gmlp_b_s"][l].T, ((0, 0), (0, LANES - N_HEADS))),
            dww=sp["conv_dw_w"][l], dwb=row(sp["conv_dw_b"][l]), clw=row(sp["conv_ln_w"][l]),
            clb=row(sp["conv_ln_b"][l]), n1=row(sp["norm_mix_pre"][l]), n2=row(sp["norm_mix_post"][l]),
            n3=row(sp["norm_ff_pre"][l]), n4=row(sp["norm_ff_post"][l]))
        h = _rms_fwd(f"norm_mix_pre{l}", x, par["n1"])
        proj = _matmul(f"in_proj{l}", h, w_in[l], "nn", [F32])[0]
        o_a, st_a = _hgrn_fwd(l, proj, par["lbp"], par["hn"])
        o_b, st_b = _gdn_fwd(l, proj, par["cw"], par["alog"], par["dtb"], par["gn"])
        o_c = _gmlp_fwd(l, proj, par["glw"], par["glb"], par["ws"], par["bsc"])
        o_d = _conformer_fwd(l, proj, par["dww"], par["dwb"], par["clw"], par["clb"])
        mix = jnp.concatenate([o_a, o_b, o_c, o_d], axis=1)
        y = _matmul(f"out_proj{l}", mix, w_out[l], "nn", [F32])[0]
        x1 = _resid_rms_fwd(f"norm_mix_post{l}", x, y, par["n2"])
        h2 = _rms_fwd(f"norm_ff_pre{l}", x1, par["n3"])
        u, act = _matmul(f"ff1_{l}", h2, w_ff1[l], "nn", [F32, BF16], epilogue=_relu2)
        y2 = _matmul(f"ff2_{l}", act, w_ff2[l], "nn", [F32])[0]
        x2 = _resid_rms_fwd(f"norm_ff_post{l}", x1, y2, par["n4"])
        saved.append(dict(par=par, x=x, h=h, proj=proj, st_a=st_a, st_b=st_b, mix=mix, y=y, x1=x1, h2=h2,
                          u=u, act=act, y2=y2))
        x = x2

    dx, loss_acc = _loss_head("loss_head", x, target)
    loss_part = loss_acc[0, 0]

    gs = {k: [None] * DEPTH for k in SMALL if k != "lower_bounds"}
    g_lb = jnp.zeros((DEPTH, GROUP_WIDTH), F32)
    g_in = [None] * DEPTH
    g_out = g_ff1 = g_ff2 = None
    for l in reversed(range(DEPTH)):
        sv = saved[l]
        par = sv["par"]
        dy2, dn4 = _resid_rms_bwd(f"norm_ff_post_bwd{l}", sv["y2"], par["n4"], dx)
        du = _matmul(f"ff2_dx{l}", dy2, w_ff2[l], "nt", [BF16], epilogue=_relu2_bwd, extras=(sv["u"],))[0]
        g_ff2 = _matmul(f"ff2_dw{l}", sv["act"], dy2, "tn", [F32], into=g_ff2,
                        place=_cut_rows(l, D_FF // N_CHIPS, D_MODEL))[0]
        g_ff1 = _matmul(f"ff1_dw{l}", sv["h2"], du, "tn", [F32], into=g_ff1,
                        place=_cut_cols(l, D_MODEL, D_FF // N_CHIPS))[0]
        dh2 = _matmul(f"ff1_dx{l}", du, w_ff1[l], "nt", [F32])[0]
        dx1, dn3 = _rms_bwd(f"norm_ff_pre_bwd{l}", sv["x1"], par["n3"], dh2, dx)
        dy, dn2 = _resid_rms_bwd(f"norm_mix_post_bwd{l}", sv["y"], par["n2"], dx1)
        dmix = _matmul(f"out_proj_dx{l}", dy, w_out[l], "nt", [F32])[0]
        g_out = _matmul(f"out_proj_dw{l}", sv["mix"], dy, "tn", [F32], into=g_out, max_tm=GROUP_WIDTH,
                        place=_cut_rows(l, GROUP_WIDTH, D_MODEL))[0]
        proj = sv["proj"]
        dp_a, dlb, dhn = _hgrn_bwd(l, proj, par["lbp"], par["hn"], sv["st_a"], dmix)
        dp_b, dp_ab, dcw, dalog, ddtb, dgn = _gdn_bwd(
            l, proj, par["cw"], par["alog"], par["dtb"], par["gn"], sv["st_b"], dmix)
        dp_c, dglw, dglb, dws, dbsc = _gmlp_bwd(l, proj, par["glw"], par["glb"], par["ws"], par["bsc"], dmix)
        dp_d, ddww, ddwb, dclw, dclb = _conformer_bwd(
            l, proj, par["dww"], par["dwb"], par["clw"], par["clb"], dmix)
        dproj = jnp.concatenate([dp_a, dp_b, dp_c, dp_d, dp_ab], axis=1)
        g_in[l] = _matmul(f"in_proj_dw{l}", sv["h"], dproj, "tn", [F32])[0]
        dh = _matmul(f"in_proj_dx{l}", dproj, w_in[l], "nt", [F32])[0]
        dx, dn1 = _rms_bwd(f"norm_mix_pre_bwd{l}", sv["x"], par["n1"], dh, dx1)
        g_lb = g_lb + dlb
        for k, v in dict(
                norm_mix_pre=dn1[0], norm_mix_post=dn2[0], norm_ff_pre=dn3[0], norm_ff_post=dn4[0],
                hgrn_norm_w=dhn[0], gdn_conv_w=dcw, gdn_a_log=dalog[0, N_HEADS:2 * N_HEADS],
                gdn_dt_bias=ddtb[0, N_HEADS:2 * N_HEADS], gdn_norm_w=dgn[0], gmlp_ln_w=dglw[0],
                gmlp_ln_b=dglb[0], gmlp_w_s=dws, gmlp_b_s=dbsc[:, :N_HEADS].T, conv_dw_w=ddww,
                conv_dw_b=ddwb[0], conv_ln_w=dclw[0], conv_ln_b=dclb[0]).items():
            gs[k][l] = v
    small_grads = {k: jnp.stack(v) for k, v in gs.items()}
    small_grads["lower_bounds"] = g_lb
    return loss_part, dx, small_grads, g_in, g_out, g_ff1, g_ff2


HBM = pl.BlockSpec(memory_space=pl.ANY)


def _place():
    return lax.axis_index("x"), lax.axis_index("y"), lax.axis_index("c")


def _other_chips(x, y):
    chips = [(1 - x, y), (x, 1 - y), (1 - x, 1 - y)]
    return [(px, py, 2 * px + py) for px, py in chips]


def _gather_weights(name, blk):
    def body(blk_ref, out_ref, ici_send, ici_recv, d2d_send, d2d_recv):
        x, y, c = _place()
        mine = 2 * x + y

        def ici(j, chip_of_slab, to):
            return pltpu.make_async_remote_copy(
                src_ref=blk_ref.at[c], dst_ref=out_ref.at[chip_of_slab, c], send_sem=ici_send.at[j],
                recv_sem=ici_recv.at[j], device_id=to, device_id_type=MESH)

        def d2d(j, chip_of_slab, layer):
            return pltpu.make_async_remote_copy(
                src_ref=out_ref.at[chip_of_slab, layer], dst_ref=out_ref.at[chip_of_slab, layer],
                send_sem=d2d_send.at[j], recv_sem=d2d_recv.at[j], device_id=(x, y, 1 - c), device_id_type=MESH)

        peers = _other_chips(x, y)
        sends = [ici(j, mine, (px, py, c)) for j, (px, py, _) in enumerate(peers)]
        for cp in sends:
            cp.start()
        passed = []
        for j, (px, py, k) in enumerate(peers):
            ici(j, k, (px, py, c)).wait_recv()
            fwd = d2d(j, k, c)
            fwd.start()
            passed.append(fwd)
        for j, (px, py, k) in enumerate(peers):
            d2d(j, k, 1 - c).wait_recv()
        for cp in sends + passed:
            cp.wait_send()

    return pl.pallas_call(
        body, name=name, out_shape=jax.ShapeDtypeStruct((N_CHIPS,) + blk.shape, blk.dtype),
        in_specs=[HBM], out_specs=HBM, scratch_shapes=[pltpu.SemaphoreType.DMA((3,))] * 4,
    )(blk)


def _pair_exchange(name, g, by_layer):
    def body(g_ref, got_ref, send_sem, recv_sem):
        x, y, c = _place()
        cp = pltpu.make_async_remote_copy(
            src_ref=g_ref.at[1 - c] if by_layer else g_ref, dst_ref=got_ref, send_sem=send_sem,
            recv_sem=recv_sem, device_id=(x, y, 1 - c), device_id_type=MESH)
        cp.start()
        cp.wait()

    return pl.pallas_call(
        body, name=name, out_shape=jax.ShapeDtypeStruct(g.shape[1:] if by_layer else g.shape, g.dtype),
        in_specs=[HBM], out_specs=HBM, scratch_shapes=[pltpu.SemaphoreType.DMA, pltpu.SemaphoreType.DMA],
    )(g)


def _scatter_partials(name, p):
    def body(p_ref, out_ref, send_sems, recv_sems):
        x, y, c = _place()

        def cp(j, to):
            return pltpu.make_async_remote_copy(
                src_ref=p_ref.at[j], dst_ref=out_ref.at[j], send_sem=send_sems.at[j],
                recv_sem=recv_sems.at[j], device_id=to, device_id_type=MESH)

        copies = [cp(j, (px, py, c)) for j, (px, py, _) in enumerate(_other_chips(x, y))]
        for s in copies:
            s.start()
        for s in copies:
            s.wait_recv()
        for s in copies:
            s.wait_send()

    return pl.pallas_call(
        body, name=name, out_shape=jax.ShapeDtypeStruct(p.shape, p.dtype), in_specs=[HBM], out_specs=HBM,
        scratch_shapes=[pltpu.SemaphoreType.DMA((3,)), pltpu.SemaphoreType.DMA((3,))],
    )(p)


def _gather_small(name, pack):
    def body(p_ref, out_ref, send_sems, recv_sems, local_sem):
        x, y, c = _place()
        me = 4 * x + 2 * y + c
        local = pltpu.make_async_copy(p_ref, out_ref.at[me], local_sem)
        local.start()
        flips = [(fx, fy, fc) for fx in (0, 1) for fy in (0, 1) for fc in (0, 1)][1:]
        peers = [((1 - x) if fx else x, (1 - y) if fy else y, (1 - c) if fc else c) for fx, fy, fc in flips]

        def cp(j, slot, to):
            return pltpu.make_async_remote_copy(
                src_ref=p_ref, dst_ref=out_ref.at[slot], send_sem=send_sems.at[j], recv_sem=recv_sems.at[j],
                device_id=to, device_id_type=MESH)

        sends = [cp(j, me, to) for j, to in enumerate(peers)]
        for s in sends:
            s.start()
        for j, (px, py, pc) in enumerate(peers):
            cp(j, 4 * px + 2 * py + pc, (px, py, pc)).wait_recv()
        for s in sends:
            s.wait_send()
        local.wait()

    return pl.pallas_call(
        body, name=name, out_shape=jax.ShapeDtypeStruct((N_DEV,) + pack.shape, pack.dtype), in_specs=[HBM],
        out_specs=HBM,
        scratch_shapes=[pltpu.SemaphoreType.DMA((7,)), pltpu.SemaphoreType.DMA((7,)), pltpu.SemaphoreType.DMA],
    )(pack)


SLAB_ROWS = 256


I_CORE, I_CHIP, I_PEER = 0, 1, 2


def _peer_chip(pre, j):
    return jnp.where(j == 0, pre[I_PEER][0], jnp.where(j == 1, pre[I_PEER + 1][0], pre[I_PEER + 2][0]))


def _pair_sum(name, g, got, where):
    _, _, r, c = g.shape
    t = _pick_tile(r, (SLAB_ROWS, 128, 64, 8))
    per = r // t

    def body(i, v, car, pre):
        return [v[0] + v[1]], [], []

    return _seq_call(
        name, body, steps=3 * per, prefetch=where,
        ins=[(g, (None, None, t, c), lambda i, pre: (pre[I_CORE][0], _peer_chip(pre, i // per), i % per, 0)),
             (got, (None, t, c), lambda i, pre: (_peer_chip(pre, i // per), i % per, 0))],
        outs=[((3, r, c), BF16, (None, t, c), lambda i, pre: (i // per, i % per, 0))])[0]


def _chip_sum(name, g, got, rcv, where):
    _, _, r, c = g.shape
    t = _pick_tile(r, (SLAB_ROWS, 128, 64, 8))

    def body(i, v, car, pre):
        acc = v[0] + v[1]
        for part in v[2:]:
            acc = acc + part.astype(F32)
        return [acc], [], []

    return _seq_call(
        name, body, steps=r // t, prefetch=where,
        ins=[(g, (None, None, t, c), lambda i, pre: (pre[I_CORE][0], pre[I_CHIP][0], i, 0)),
             (got, (None, t, c), lambda i, pre: (pre[I_CHIP][0], i, 0))]
        + [(rcv, (None, t, c), (lambda i, pre, _j=j: (_j, i, 0))) for j in range(3)],
        outs=[((r, c), F32, (t, c), lambda i, pre: (i, 0))])[0]


def _adamw_math(w, g, m, v):
    m = ADAM_B1 * m + (1.0 - ADAM_B1) * g
    v = ADAM_B2 * v + (1.0 - ADAM_B2) * (g * g)
    m_hat = m / (1.0 - ADAM_B1 ** ADAM_STEP)
    v_hat = v / (1.0 - ADAM_B2 ** ADAM_STEP)
    delta = -ADAM_LR * (m_hat / (jnp.sqrt(v_hat) + ADAM_EPS) + ADAM_WD * w)
    return delta, m, v


def _adamw(name, w, g, m, v):
    n, r, c = w.shape
    t = _pick_tile(r, (SLAB_ROWS, 128, 64, 8))
    per = r // t

    def body(i, vals, car):
        return list(_adamw_math(*vals)), [], []

    blk = lambda a: (a, (None, t, c), lambda i: (i // per, i % per, 0))
    out = ((n, r, c), F32, (None, t, c), lambda i: (i // per, i % per, 0))
    return _seq_call(name, body, steps=n * per, ins=[blk(w), blk(g), blk(m), blk(v)], outs=[out] * 3)


def _adamw_cut(name, w, mine, sib, m, v, where):
    n, r, c = w.shape
    t = _pick_tile(r, (SLAB_ROWS, 128, 64, 8))
    per = r // t

    def body(i, vals, car, pre):
        g = jnp.where(i // per == pre[I_CORE][0], vals[1], vals[2])
        return [g] + list(_adamw_math(vals[0], g, vals[3], vals[4])), [], []

    by_layer = lambda a: (a, (None, t, c), lambda i, pre: (i // per, i % per, 0))
    flat = lambda a: (a, (t, c), lambda i, pre: (i % per, 0))
    out = ((n, r, c), F32, (None, t, c), lambda i, pre: (i // per, i % per, 0))
    return _seq_call(name, body, steps=n * per, prefetch=where,
                     ins=[by_layer(w), flat(mine), flat(sib), by_layer(m), by_layer(v)], outs=[out] * 4)


def _ordered_sum(name, packs):
    n, r, c = packs.shape

    def body(i, v, car):
        acc = v[0][0]
        for k in range(1, n):
            acc = acc + v[0][k]
        return [acc], [], []

    return _seq_call(name, body, steps=1, ins=[_whole(packs)], outs=[((r, c), F32, (r, c), lambda i: (0, 0))])[0]


def _pack(arrays):
    flat = []
    for a in arrays:
        a = a.reshape(-1).astype(F32)
        pad = (-a.shape[0]) % LANES
        flat.append(jnp.pad(a, (0, pad)) if pad else a)
    v = jnp.concatenate(flat)
    pad = (-v.shape[0]) % (64 * LANES)
    if pad:
        v = jnp.pad(v, (0, pad))
    return v.reshape(-1, LANES)


def _unpack(pack, shapes):
    v = pack.reshape(-1)
    out, off = [], 0
    for s in shapes:
        n = math.prod(s)
        out.append(v[off:off + n].reshape(s))
        off += n + ((-n) % LANES)
    return out


def _reorder_in(full):
    g0 = 8 * GROUP_WIDTH
    pad = jnp.zeros(full.shape[:-1] + (LANES - 2 * N_HEADS,), full.dtype)
    return jnp.concatenate([full[..., :g0], full[..., g0 + 2 * N_HEADS:], full[..., g0:g0 + 2 * N_HEADS], pad], axis=-1)


def _restore_in(padded):
    g0 = 8 * GROUP_WIDTH
    wide = 12 * GROUP_WIDTH
    return jnp.concatenate([padded[..., :g0], padded[..., wide:wide + 2 * N_HEADS], padded[..., g0:wide]], axis=-1)


def _reduce_big(tag, g, where):
    got = _pair_exchange(f"pair_exchange_{tag}", g, True)
    pair = _pair_sum(f"pair_sum_{tag}", g, got, where)
    rcv = _scatter_partials(f"scatter_partials_{tag}", pair)
    mine = _chip_sum(f"chip_sum_{tag}", g, got, rcv, where)
    return mine, _pair_exchange(f"pair_send_{tag}", mine, False)


def kernel(x, lower_bounds, norm_mix_pre, norm_mix_post, norm_ff_pre, norm_ff_post, w_in, w_out, hgrn_norm_w, gdn_conv_w, gdn_a_log, gdn_dt_bias, gdn_norm_w, gmlp_ln_w, gmlp_ln_b, gmlp_w_s, gmlp_b_s, conv_dw_w, conv_dw_b, conv_ln_w, conv_ln_b, w_ff1, w_ff2, loss_target, m_lower_bounds, m_norm_mix_pre, m_norm_mix_post, m_norm_ff_pre, m_norm_ff_post, m_w_in, m_w_out, m_hgrn_norm_w, m_gdn_conv_w, m_gdn_a_log, m_gdn_dt_bias, m_gdn_norm_w, m_gmlp_ln_w, m_gmlp_ln_b, m_gmlp_w_s, m_gmlp_b_s, m_conv_dw_w, m_conv_dw_b, m_conv_ln_w, m_conv_ln_b, m_w_ff1, m_w_ff2, v_lower_bounds, v_norm_mix_pre, v_norm_mix_post, v_norm_ff_pre, v_norm_ff_post, v_w_in, v_w_out, v_hgrn_norm_w, v_gdn_conv_w, v_gdn_a_log, v_gdn_dt_bias, v_gdn_norm_w, v_gmlp_ln_w, v_gmlp_ln_b, v_gmlp_w_s, v_gmlp_b_s, v_conv_dw_w, v_conv_dw_b, v_conv_ln_w, v_conv_ln_b, v_w_ff1, v_w_ff2):
    args = dict(locals())
    chip = 2 * lax.axis_index("x") + lax.axis_index("y")
    where = tuple(jnp.asarray(v, jnp.int32).reshape(1)
                  for v in (lax.axis_index("c"), chip, chip ^ 2, chip ^ 1, chip ^ 3))

    def gathered(name, w):
        blk = w.astype(BF16)
        return lax.dynamic_update_slice(_gather_weights(name, blk), blk[None], (chip, 0, 0, 0))

    win_all = gathered("gather_w_in", w_in)
    wout_all = gathered("gather_w_out", w_out)
    ff1_all = gathered("gather_w_ff1", w_ff1)
    ff2_all = gathered("gather_w_ff2", w_ff2)
    cut_shapes = [gdn_conv_w.shape, conv_dw_w.shape]
    cuts = _gather_small("gather_cut_small", _pack([gdn_conv_w, conv_dw_w]))
    cut_parts = [_unpack(cuts[2 * k], cut_shapes) for k in range(N_CHIPS)]
    sp = {k: args[k] for k in SMALL}
    sp["gdn_conv_w"] = jnp.concatenate([p[0] for p in cut_parts], axis=-1)
    sp["conv_dw_w"] = jnp.concatenate([p[1] for p in cut_parts], axis=-1)

    w_in_l, w_out_l, w_ff1_l, w_ff2_l = [], [], [], []
    for l in range(DEPTH):
        full = jnp.concatenate([win_all[k, l] for k in range(N_CHIPS)], axis=-1)
        w_in_l.append(_reorder_in(full))
        w_out_l.append(wout_all[:, l].reshape(D_MODEL, D_MODEL))
        w_ff1_l.append(jnp.concatenate([ff1_all[k, l] for k in range(N_CHIPS)], axis=-1))
        w_ff2_l.append(ff2_all[:, l].reshape(D_FF, D_MODEL))

    loss_part, grad_x, small_g, g_in, g_out, g_ff1, g_ff2 = _local_step(
        x[0], loss_target[0], sp, w_in_l, w_out_l, w_ff1_l, w_ff2_l)

    gi = jnp.stack([jnp.stack(jnp.split(_restore_in(g_in[l]), N_CHIPS, axis=-1)) for l in range(DEPTH)])
    big_g = dict(w_in=_reduce_big("w_in", gi, where), w_out=_reduce_big("w_out", g_out, where),
                 w_ff1=_reduce_big("w_ff1", g_ff1, where), w_ff2=_reduce_big("w_ff2", g_ff2, where))

    names = SMALL + ["loss"]
    small_g["loss"] = loss_part.reshape(1)
    shapes = [small_g[k].shape for k in names]
    total = _ordered_sum("sum_small", _gather_small("gather_small", _pack([small_g[k] for k in names])))
    summed = dict(zip(names, _unpack(total, shapes)))
    loss = summed.pop("loss")[0]
    for k, width in (("gdn_conv_w", gdn_conv_w.shape[-1]), ("conv_dw_w", conv_dw_w.shape[-1])):
        summed[k] = lax.dynamic_slice_in_dim(summed[k], chip * width, width, axis=-1)

    grads, deltas, new_m, new_v = {}, {}, {}, {}
    for k in ("w_in", "w_out", "w_ff1", "w_ff2"):
        grads[k], deltas[k], new_m[k], new_v[k] = _adamw_cut(
            f"adamw_{k}", args[k], *big_g[k], args["m_" + k], args["v_" + k], where)
    local_shapes = [args[k].shape for k in SMALL]
    packs = [_pack([src[k] for k in SMALL]) for src in (
        {k: args[k] for k in SMALL}, summed, {k: args["m_" + k] for k in SMALL}, {k: args["v_" + k] for k in SMALL})]
    d_s, m_s, v_s = _adamw("adamw_small", *[p[None] for p in packs])
    for k, d, mm, vv in zip(SMALL, _unpack(d_s[0], local_shapes), _unpack(m_s[0], local_shapes),
                            _unpack(v_s[0], local_shapes)):
        grads[k], deltas[k], new_m[k], new_v[k] = summed[k], d, mm, vv

    order = ["lower_bounds", "norm_mix_pre", "norm_mix_post", "norm_ff_pre", "norm_ff_post", "w_in", "w_out",
             "hgrn_norm_w", "gdn_conv_w", "gdn_a_log", "gdn_dt_bias", "gdn_norm_w", "gmlp_ln_w", "gmlp_ln_b",
             "gmlp_w_s", "gmlp_b_s", "conv_dw_w", "conv_dw_b", "conv_ln_w", "conv_ln_b", "w_ff1", "w_ff2"]
    return (loss, grad_x[None], *[grads[k] for k in order], *[deltas[k] for k in order],
            *[new_m[k] for k in order], *[new_v[k] for k in order])
```

```python
import functools
import math

import jax
import jax.numpy as jnp
from jax import lax
from jax.experimental import pallas as pl
from jax.experimental.pallas import tpu as pltpu

F32 = jnp.float32
BF16 = jnp.bfloat16

DEPTH = 2
D_MODEL = 2048
GROUP_WIDTH = 512
HEAD_DIM = 128
N_HEADS = 4
CHUNK = 64
SHORT_CONV = 4
MIX_CHUNK = 128
CONV_WIDTH = 31
D_FF = 4 * D_MODEL
D_IN_PROJ = 12 * GROUP_WIDTH + 2 * N_HEADS
EPS = 1e-6
NEG_BIG = -1e30
TINY = 1e-30
ADAM_LR = 0.001
ADAM_B1 = 0.9
ADAM_B2 = 0.999
ADAM_EPS = 1e-08
ADAM_WD = 0.01
ADAM_STEP = 10

LANES = 128
P_IN = 12 * GROUP_WIDTH + LANES
SUB_BLOCK = 16
HGRN_TILE = 128
GDN_TILE = 128
CONV_TILE = 128
CONV_HALO = 32
VMEM_LIMIT = 56 * 1024 * 1024
N_CHIPS = 4
N_DEV = 8
MESH = pl.DeviceIdType.MESH


_DIMS = {
    "nn": (((1,), (0,)), ((), ())),
    "nt": (((1,), (1,)), ((), ())),
    "tn": (((0,), (0,)), ((), ())),
}


def _split2(a):
    hi = a.astype(BF16)
    return hi, (a - hi.astype(F32)).astype(BF16)


def _raw_mm(a, b, mode, exact):
    dot = lambda p, q: lax.dot_general(p, q, _DIMS[mode], preferred_element_type=F32)
    if not exact:
        return dot(a.astype(BF16), b.astype(BF16))
    a_hi, a_lo = _split2(a)
    b_hi, b_lo = _split2(b)
    return dot(a_hi, b_hi) + (dot(a_hi, b_lo) + dot(a_lo, b_hi))


@functools.partial(jax.custom_vjp, nondiff_argnums=(2, 3))
def _mm(a, b, mode, exact):
    return _raw_mm(a, b, mode, exact)


def _mm_fwd(a, b, mode, exact):
    return _raw_mm(a, b, mode, exact), (a, b)


def _mm_bwd(mode, exact, res, g):
    a, b = res
    if mode == "nn":
        return _raw_mm(g, b, "nt", exact), _raw_mm(a, g, "tn", exact)
    if mode == "nt":
        return _raw_mm(g, b, "nn", exact), _raw_mm(g, a, "tn", exact)
    return _raw_mm(b, g, "nt", exact), _raw_mm(a, g, "nn", exact)


_mm.defvjp(_mm_fwd, _mm_bwd)


def _sig(x):
    return jax.nn.sigmoid(x)


def _silu(x):
    return x * jax.nn.sigmoid(x)


def _gelu(x):
    return 0.5 * x * (1.0 + lax.erf(x * (1.0 / math.sqrt(2.0))))


def _rms(x, w):
    return x * lax.rsqrt(jnp.mean(x * x, axis=-1, keepdims=True) + EPS) * w


def _ln(x, w, b):
    mu = jnp.mean(x, axis=-1, keepdims=True)
    xc = x - mu
    var = jnp.mean(xc * xc, axis=-1, keepdims=True)
    return xc * lax.rsqrt(var + EPS) * w + b


def _iota(shape, dim):
    return lax.broadcasted_iota(jnp.int32, shape, dim)


def _tri_mm(x, mode):
    n = x.shape[0]
    tri = (_iota((n, n), 0) >= _iota((n, n), 1)).astype(BF16)
    x1 = x.astype(BF16)
    r1 = x - x1.astype(F32)
    x2 = r1.astype(BF16)
    x3 = (r1 - x2.astype(F32)).astype(BF16)
    dot = lambda q: lax.dot_general(tri, q, _DIMS[mode], preferred_element_type=F32)
    return dot(x1) + (dot(x2) + dot(x3))


@jax.custom_vjp
def _cumsum_rows(x):
    return _tri_mm(x, "nn")


_cumsum_rows.defvjp(lambda x: (_tri_mm(x, "nn"), None), lambda _, g: (_tri_mm(g, "tn"),))


def _hgrn_head(q, k, v, b, st):
    n = q.shape[0]
    ii = _iota((n, 1), 0)
    zpad = jnp.zeros((SUB_BLOCK, HEAD_DIM), F32)
    k_ext = jnp.concatenate([zpad, k], axis=0)
    b_ext = jnp.concatenate([zpad, b], axis=0)
    v_ext = jnp.concatenate([zpad, v], axis=0)
    o = jnp.zeros((n, HEAD_DIM), F32)
    for d in range(SUB_BLOCK):
        ks = k_ext[SUB_BLOCK - d:SUB_BLOCK - d + n]
        bs = b_ext[SUB_BLOCK - d:SUB_BLOCK - d + n]
        vs = v_ext[SUB_BLOCK - d:SUB_BLOCK - d + n]
        e = jnp.exp(jnp.where((ii % SUB_BLOCK) >= d, b - bs, NEG_BIG))
        o = o + jnp.sum(q * ks * e, axis=-1, keepdims=True) * vs
    blocks = [o[0:SUB_BLOCK]]
    for blk in range(1, n // SUB_BLOCK):
        lo = SUB_BLOCK * blk
        r = b[lo - 1:lo]
        a_q = q[lo:lo + SUB_BLOCK] * jnp.exp(b[lo:lo + SUB_BLOCK] - r)
        b_k = jnp.where(ii < lo, k * jnp.exp(jnp.minimum(r - b, 0.0)), 0.0)
        sc = _mm(a_q, b_k, "nt", False)
        blocks.append(o[lo:lo + SUB_BLOCK] + _mm(sc, v, "nn", False))
    o = jnp.concatenate(blocks, axis=0)
    o = o + _mm(q * jnp.exp(b), st, "nt", False)
    b_end = b[n - 1:n]
    st_new = st * jnp.exp(b_end) + _mm(v, k * jnp.exp(b_end - b), "tn", False)
    return o, st_new


def _hgrn_chunk(layer, lbp, nw, aq, af, ai, ag, states):
    rows = [lbp[i:i + 1, :] for i in range(DEPTH)]
    mx = functools.reduce(jnp.maximum, rows)
    es = [jnp.exp(r - mx) for r in rows]
    den = functools.reduce(lambda p, s: p + s, es)
    soft = [e / den for e in es]
    lb = functools.reduce(lambda p, s: p + s, soft[:layer + 1]) - soft[0]
    f = lb + (1.0 - lb) * _sig(af)
    logf = jnp.log(jnp.maximum(f, TINY))
    k = (1.0 - lb) * _sig(-af)
    q = _silu(aq)
    b = _cumsum_rows(logf)
    outs, new_states = [], []
    for h in range(N_HEADS):
        sl = slice(HEAD_DIM * h, HEAD_DIM * (h + 1))
        o, st = _hgrn_head(q[:, sl], k[:, sl], ai[:, sl], b[:, sl], states[h])
        outs.append(_rms(o, nw) * _silu(ag[:, sl]))
        new_states.append(st)
    return jnp.concatenate(outs, axis=1), new_states


def _hgrn_tile(layer, lbp, nw, aq, af, ai, ag, states):
    outs = []
    for ci in range(aq.shape[0] // CHUNK):
        rs = slice(CHUNK * ci, CHUNK * (ci + 1))
        o, states = _hgrn_chunk(layer, lbp, nw, aq[rs], af[rs], ai[rs], ag[rs], states)
        outs.append(o)
    return jnp.concatenate(outs, axis=0), states


def _short_conv(prev, cur, w):
    n = cur.shape[0]
    ext = jnp.concatenate([prev[n - 8:n], cur], axis=0)
    y = jnp.zeros_like(cur)
    for t in range(SHORT_CONV):
        off = 8 - (SHORT_CONV - 1) + t
        y = y + w[t:t + 1, :] * ext[off:off + n]
    return _silu(y)


def _gdn_tile(cw, alog, dtb, nw, pq, pk, pv, cq, ck, cv, bz, ab, states):
    n = cq.shape[0]
    q_all = _short_conv(pq, cq, cw[:, 0:GROUP_WIDTH])
    k_all = _short_conv(pk, ck, cw[:, GROUP_WIDTH:2 * GROUP_WIDTH])
    v_all = _short_conv(pv, cv, cw[:, 2 * GROUP_WIDTH:3 * GROUP_WIDTH])
    beta_all = _sig(ab)
    g_all = -jnp.exp(alog) * jax.nn.softplus(ab + dtb)
    units = [(ci, h) for ci in range(n // CHUNK) for h in range(N_HEADS)]
    gc_all = [_cumsum_rows(g_all[CHUNK * ci:CHUNK * (ci + 1)]) for ci in range(n // CHUNK)]
    gc_t = [g.T for g in gc_all]
    ii = _iota((CHUNK, CHUNK), 0)
    jj = _iota((CHUNK, CHUNK), 1)
    eye = (ii == jj).astype(F32)

    def cut(a, ci, h):
        return a[CHUNK * ci:CHUNK * (ci + 1), HEAD_DIM * h:HEAD_DIM * (h + 1)]

    q = [cut(q_all, ci, h) for ci, h in units]
    k = [cut(k_all, ci, h) for ci, h in units]
    v = [cut(v_all, ci, h) for ci, h in units]
    q = [t * lax.rsqrt(jnp.sum(t * t, axis=-1, keepdims=True) + EPS) * (HEAD_DIM ** -0.5) for t in q]
    k = [t * lax.rsqrt(jnp.sum(t * t, axis=-1, keepdims=True) + EPS) for t in k]
    beta = [beta_all[CHUNK * ci:CHUNK * (ci + 1), h:h + 1] for ci, h in units]
    gc = [gc_all[ci][:, N_HEADS + h:N_HEADS + h + 1] for ci, h in units]
    gcr = [gc_t[ci][N_HEADS + h:N_HEADS + h + 1, :] for ci, h in units]
    gamma = [jnp.exp(jnp.where(ii >= jj, a - b, NEG_BIG)) for a, b in zip(gc, gcr)]
    kb = [a * b for a, b in zip(k, beta)]
    m = [jnp.where(ii > jj, _mm(a, b, "nt", False) * g, 0.0) for a, b, g in zip(kb, k, gamma)]
    inv = [eye - t for t in m]
    p = m
    for _ in range(max(1, int(math.ceil(math.log2(CHUNK))) - 1)):
        p = [_mm(t, t, "nn", True) for t in p]
        inv = [a + _mm(a, t, "nn", True) for a, t in zip(inv, p)]
    eg = [jnp.exp(t) for t in gc]
    u = [_mm(a, b * c, "nn", True) for a, b, c in zip(inv, v, beta)]
    w = [_mm(a, b * c, "nn", True) for a, b, c in zip(inv, kb, eg)]
    qk = [_mm(a, b, "nt", False) * g for a, b, g in zip(q, k, gamma)]
    qd = [a * b for a, b in zip(q, eg)]
    g_end = [t[CHUNK - 1:CHUNK] for t in gc]
    kd = [a * jnp.exp(e - g) for a, e, g in zip(k, g_end, gc)]
    states = list(states)
    outs = {}
    for i, (ci, h) in enumerate(units):
        st = states[h]
        v_new = u[i] - _mm(w[i], st, "nt", False)
        outs[ci, h] = _mm(qd[i], st, "nt", False) + _mm(qk[i], v_new, "nn", False)
        states[h] = st * jnp.exp(g_end[i]) + _mm(v_new, kd[i], "tn", False)
    rows = []
    for ci in range(n // CHUNK):
        rows.append(jnp.concatenate(
            [_rms(outs[ci, h], nw) * _silu(cut(bz, ci, h)) for h in range(N_HEADS)], axis=1))
    return jnp.concatenate(rows, axis=0), states


def _gmlp_tile(ln_w, ln_b, ws, bs_cols, cu, cv):
    u = _gelu(cu)
    v = _ln(_gelu(cv), ln_w, ln_b)
    n = cu.shape[0]
    tril = _iota((n, n), 0) >= _iota((n, n), 1)
    outs = []
    for h in range(N_HEADS):
        sl = slice(HEAD_DIM * h, HEAD_DIM * (h + 1))
        wc = jnp.where(tril, ws[h], 0.0)
        outs.append(_mm(wc, v[:, sl], "nn", False) + bs_cols[:, h:h + 1])
    return u * jnp.concatenate(outs, axis=1)


def _conformer_tile(dw_w, dw_b, ln_w, ln_b, pa, pg, ca, cg):
    n = ca.shape[0]
    yp = pa[n - CONV_HALO:n] * _sig(pg[n - CONV_HALO:n])
    ext = jnp.concatenate([yp, ca * _sig(cg)], axis=0)
    acc = jnp.zeros_like(ca)
    for t in range(CONV_WIDTH):
        off = CONV_HALO - (CONV_WIDTH - 1) + t
        acc = acc + dw_w[t:t + 1, :] * ext[off:off + n]
    return _silu(_ln(acc + dw_b, ln_w, ln_b))


def _seq_call(name, body, *, steps, ins, outs, accs=(), carries=(), reverse=False, prefetch=None):
    n_in, n_out, n_acc, n_car = len(ins), len(outs), len(accs), len(carries)
    n_pre = 0 if prefetch is None else len(prefetch)

    def logical(g):
        return (steps - 1 - g) if reverse else g

    def kern(*refs):
        pre = refs[:n_pre]
        refs = refs[n_pre:]
        in_refs = refs[:n_in]
        out_refs = refs[n_in:n_in + n_out]
        acc_refs = refs[n_in + n_out:n_in + n_out + n_acc]
        car_refs = refs[n_in + n_out + n_acc:]
        g = pl.program_id(0)

        @pl.when(g == 0)
        def _():
            for r in list(acc_refs) + list(car_refs):
                r[...] = jnp.zeros(r.shape, r.dtype)

        o, a, c = body(logical(g), [r[...] for r in in_refs], [r[...] for r in car_refs], *((pre,) if n_pre else ()))
        for r, v in zip(out_refs, o, strict=True):
            r[...] = v.astype(r.dtype)
        for r, v in zip(acc_refs, a, strict=True):
            r[...] += v
        for r, v in zip(car_refs, c, strict=True):
            r[...] = v

    def spec(block, fn):
        return pl.BlockSpec(block, lambda g, *pre: fn(logical(g), *((pre,) if n_pre else ())))

    in_specs = [spec(bs, fn) for (_, bs, fn) in ins]
    out_specs = [spec(bs, fn) for (_, _, bs, fn) in outs]
    out_specs += [pl.BlockSpec(shape, lambda g, *pre, _n=len(shape): (0,) * _n) for (shape, _) in accs]
    out_shape = [jax.ShapeDtypeStruct(s, d) for (s, d, _, _) in outs]
    out_shape += [jax.ShapeDtypeStruct(s, d) for (s, d) in accs]
    grid_spec = pltpu.PrefetchScalarGridSpec(
        num_scalar_prefetch=n_pre, grid=(steps,), in_specs=in_specs, out_specs=out_specs,
        scratch_shapes=[pltpu.VMEM(s, d) for (s, d) in carries])
    args = ([] if prefetch is None else list(prefetch)) + [a for (a, _, _) in ins]
    return pl.pallas_call(
        kern, name=name, grid_spec=grid_spec, out_shape=out_shape,
        compiler_params=pltpu.CompilerParams(dimension_semantics=("arbitrary",), vmem_limit_bytes=VMEM_LIMIT),
    )(*args)


def _whole(a):
    nd = a.ndim
    return (a, a.shape, lambda i, *pre: (0,) * nd)


def _rows(a, tile, col=0, width=None, shift=0):
    width = a.shape[1] if width is None else width
    if shift:
        return (a, (tile, width), lambda i, *pre: (jnp.maximum(i + shift, 0), col))
    return (a, (tile, width), lambda i, *pre: (i, col))


def _row_out(n_rows, width, dtype, tile):
    return ((n_rows, width), dtype, (tile, width), lambda i, *pre: (i, 0))


def _pick_tile(n, prefs):
    for t in prefs:
        if n % t == 0:
            return t
    return n


def _matmul(name, a, b, mode, out_dtypes, epilogue=None, extras=(), place=None, into=None, max_tm=1024):
    if mode == "nn":
        (m, k), n = a.shape, b.shape[1]
    elif mode == "nt":
        (m, k), n = a.shape, b.shape[0]
    else:
        (k, m), n = a.shape, b.shape[1]
    tm = _pick_tile(m, tuple(t for t in (1024, 512, 256, 128) if t <= max_tm))
    tn = _pick_tile(n, (1024, 896, 512, 256, 128))
    tk = _pick_tile(k, (2048, 896, 512, 256, 128))
    nk = k // tk
    n_ex = len(extras)
    n_out = len(out_dtypes)

    n_into = 0 if into is None else 1

    def kern(*refs):
        a_ref, b_ref = refs[0], refs[1]
        ex_refs = refs[2:2 + n_ex]
        out_refs = refs[2 + n_ex + n_into:2 + n_ex + n_into + n_out]
        acc_ref = refs[2 + n_ex + n_into + n_out]
        kk = pl.program_id(2)

        @pl.when(kk == 0)
        def _():
            acc_ref[...] = jnp.zeros(acc_ref.shape, F32)

        acc_ref[...] += lax.dot_general(a_ref[...], b_ref[...], _DIMS[mode], preferred_element_type=F32)

        @pl.when(kk == nk - 1)
        def _():
            acc = acc_ref[...]
            vals = (acc,) if epilogue is None else epilogue(acc, *[r[...] for r in ex_refs])
            for r, v in zip(out_refs, vals, strict=True):
                r[...] = v.astype(r.dtype)

    if mode == "tn":
        a_spec = pl.BlockSpec((tk, tm), lambda i, j, kk: (kk, i))
    else:
        a_spec = pl.BlockSpec((tm, tk), lambda i, j, kk: (i, kk))
    if mode == "nt":
        b_spec = pl.BlockSpec((tn, tk), lambda i, j, kk: (j, kk))
    else:
        b_spec = pl.BlockSpec((tk, tn), lambda i, j, kk: (kk, j))
    tile = pl.BlockSpec((tm, tn), lambda i, j, kk: (i, j))
    out_specs = [tile] * n_out
    out_shape = [jax.ShapeDtypeStruct((m, n), d) for d in out_dtypes]
    if place is not None:
        shape, block_fn, index_fn = place
        out_specs = [pl.BlockSpec(block_fn(tm, tn), lambda i, j, kk: index_fn(i, j, tm, tn))]
        out_shape = [jax.ShapeDtypeStruct(shape, out_dtypes[0])]
    return pl.pallas_call(
        kern, name=name, grid=(m // tm, n // tn, nk),
        in_specs=[a_spec, b_spec] + [tile] * n_ex + [HBM] * n_into,
        out_specs=out_specs, out_shape=out_shape,
        input_output_aliases=({2 + n_ex: 0} if n_into else {}),
        scratch_shapes=[pltpu.VMEM((tm, tn), F32)],
        compiler_params=pltpu.CompilerParams(
            dimension_semantics=("parallel", "parallel", "arbitrary"), vmem_limit_bytes=VMEM_LIMIT),
    )(a, b, *extras, *([into] if n_into else []))


ROW_TILE = 256


def _rms_fwd(name, x, w):
    s, d = x.shape
    t = _pick_tile(s, (ROW_TILE,))

    def body(i, v, c):
        return [_rms(v[0], v[1])], [], []

    return _seq_call(name, body, steps=s // t, ins=[_rows(x, t), _whole(w)], outs=[_row_out(s, d, BF16, t)])[0]


def _resid_rms_fwd(name, x, y, w):
    s, d = x.shape
    t = _pick_tile(s, (ROW_TILE,))

    def body(i, v, c):
        return [v[0] + _rms(v[1], v[2])], [], []

    return _seq_call(name, body, steps=s // t, ins=[_rows(x, t), _rows(y, t), _whole(w)],
                     outs=[_row_out(s, d, F32, t)])[0]


def _rms_bwd(name, x, w, dh, dres):
    s, d = x.shape
    t = _pick_tile(s, (ROW_TILE,))

    def body(i, v, c):
        _, vjp = jax.vjp(_rms, v[0], v[1])
        dx, dw = vjp(v[2])
        return [dx + v[3]], [dw], []

    return _seq_call(name, body, steps=s // t, ins=[_rows(x, t), _whole(w), _rows(dh, t), _rows(dres, t)],
                     outs=[_row_out(s, d, F32, t)], accs=[((1, d), F32)])


def _resid_rms_bwd(name, y, w, dxo):
    s, d = y.shape
    t = _pick_tile(s, (ROW_TILE,))

    def body(i, v, c):
        _, vjp = jax.vjp(_rms, v[0], v[1])
        dy, dw = vjp(v[2])
        return [dy], [dw], []

    return _seq_call(name, body, steps=s // t, ins=[_rows(y, t), _whole(w), _rows(dxo, t)],
                     outs=[_row_out(s, d, BF16, t)], accs=[((1, d), F32)])


def _loss_head(name, y, target):
    s, d = y.shape
    t = _pick_tile(s, (ROW_TILE,))

    def body(i, v, c):
        err = v[0] - v[1]
        part = 0.5 * jnp.sum(jnp.mean(err * err, axis=-1, keepdims=True))
        return [err * (1.0 / d)], [jnp.full((1, LANES), part, F32)], []

    return _seq_call(name, body, steps=s // t, ins=[_rows(y, t), _rows(target, t)],
                     outs=[_row_out(s, d, F32, t)], accs=[((1, LANES), F32)])


SEG = {n: i for i, n in enumerate(
    ["a_q", "a_f", "a_i", "a_g", "b_q", "b_k", "b_v", "b_z", "c_u", "c_v", "d_a", "d_gate"])}
AB_COL = 12 * GROUP_WIDTH // LANES


def _seg(proj, name, tile, shift=0):
    return _rows(proj, tile, col=SEG[name], width=GROUP_WIDTH, shift=shift)


def _state_block():
    return (1, N_HEADS * HEAD_DIM, HEAD_DIM), lambda i, *pre: (i, 0, 0)


def _split_states(blk):
    return [blk[0, HEAD_DIM * h:HEAD_DIM * (h + 1), :] for h in range(N_HEADS)]


STATE_CARRIES = [((HEAD_DIM, HEAD_DIM), F32)] * N_HEADS


def _hgrn_fwd(layer, proj, lbp, nw):
    s = proj.shape[0]
    n = s // HGRN_TILE
    sb, sf = _state_block()

    def body(i, v, st):
        o, new = _hgrn_tile(layer, v[0], v[1], v[2], v[3], v[4], v[5], st)
        return [o, jnp.concatenate(st, axis=0)[None]], [], new

    return _seq_call(
        f"hgrn_fwd{layer}", body, steps=n,
        ins=[_whole(lbp), _whole(nw)] + [_seg(proj, k, HGRN_TILE) for k in ("a_q", "a_f", "a_i", "a_g")],
        outs=[_row_out(s, GROUP_WIDTH, BF16, HGRN_TILE), ((n, N_HEADS * HEAD_DIM, HEAD_DIM), F32, sb, sf)],
        carries=STATE_CARRIES)


def _hgrn_bwd(layer, proj, lbp, nw, states, dmix):
    s = proj.shape[0]
    n = s // HGRN_TILE
    sb, sf = _state_block()

    def body(i, v, dst):
        st = _split_states(v[6])

        def f(lbp_, nw_, aq, af, ai, ag, *st_):
            return _hgrn_tile(layer, lbp_, nw_, aq, af, ai, ag, list(st_))

        _, vjp = jax.vjp(f, v[0], v[1], v[2], v[3], v[4], v[5], *st)
        g = vjp((v[7], list(dst)))
        return [jnp.concatenate(g[2:6], axis=1)], [g[0], g[1]], list(g[6:])

    return _seq_call(
        f"hgrn_bwd{layer}", body, steps=n, reverse=True,
        ins=[_whole(lbp), _whole(nw)] + [_seg(proj, k, HGRN_TILE) for k in ("a_q", "a_f", "a_i", "a_g")]
        + [(states, sb, sf), _rows(dmix, HGRN_TILE, col=0, width=GROUP_WIDTH)],
        outs=[_row_out(s, 4 * GROUP_WIDTH, BF16, HGRN_TILE)],
        accs=[(lbp.shape, F32), (nw.shape, F32)], carries=STATE_CARRIES)


def _gdn_ins(proj, cw, alog, dtb, nw):
    return ([_whole(cw), _whole(alog), _whole(dtb), _whole(nw)]
            + [_seg(proj, k, GDN_TILE, shift=-1) for k in ("b_q", "b_k", "b_v")]
            + [_seg(proj, k, GDN_TILE) for k in ("b_q", "b_k", "b_v", "b_z")]
            + [_rows(proj, GDN_TILE, col=AB_COL, width=LANES)])


def _mask_prev(i, vals):
    keep = (i > 0).astype(F32)
    return [p * keep for p in vals]


def _gdn_fwd(layer, proj, cw, alog, dtb, nw):
    s = proj.shape[0]
    n = s // GDN_TILE
    sb, sf = _state_block()

    def body(i, v, st):
        prev = _mask_prev(i, v[4:7])
        o, new = _gdn_tile(v[0], v[1], v[2], v[3], *prev, *v[7:12], st)
        return [o, jnp.concatenate(st, axis=0)[None]], [], new

    return _seq_call(
        f"gdn_fwd{layer}", body, steps=n, ins=_gdn_ins(proj, cw, alog, dtb, nw),
        outs=[_row_out(s, GROUP_WIDTH, BF16, GDN_TILE), ((n, N_HEADS * HEAD_DIM, HEAD_DIM), F32, sb, sf)],
        carries=STATE_CARRIES)


def _gdn_bwd(layer, proj, cw, alog, dtb, nw, states, dmix):
    s = proj.shape[0]
    n = s // GDN_TILE
    sb, sf = _state_block()

    def body(i, v, car):
        dst, dprev = car[:N_HEADS], car[N_HEADS:]
        prev = _mask_prev(i, v[4:7])
        st = _split_states(v[12])

        def f(cw_, alog_, dtb_, nw_, pq, pk, pv, cq, ck, cv, bz, ab, *st_):
            return _gdn_tile(cw_, alog_, dtb_, nw_, pq, pk, pv, cq, ck, cv, bz, ab, list(st_))

        _, vjp = jax.vjp(f, v[0], v[1], v[2], v[3], *prev, *v[7:12], *st)
        g = vjp((v[13], list(dst)))
        dcur = [g[7] + dprev[0], g[8] + dprev[1], g[9] + dprev[2], g[10]]
        return ([jnp.concatenate(dcur, axis=1), g[11]], list(g[0:4]), list(g[12:]) + list(g[4:7]))

    return _seq_call(
        f"gdn_bwd{layer}", body, steps=n, reverse=True,
        ins=_gdn_ins(proj, cw, alog, dtb, nw) + [(states, sb, sf), _rows(dmix, GDN_TILE, col=1, width=GROUP_WIDTH)],
        outs=[_row_out(s, 4 * GROUP_WIDTH, BF16, GDN_TILE), _row_out(s, LANES, BF16, GDN_TILE)],
        accs=[(cw.shape, F32), (alog.shape, F32), (dtb.shape, F32), (nw.shape, F32)],
        carries=STATE_CARRIES + [((GDN_TILE, GROUP_WIDTH), F32)] * 3)


def _gmlp_fwd(layer, proj, ln_w, ln_b, ws, bs_cols):
    s = proj.shape[0]

    def body(i, v, c):
        return [_gmlp_tile(*v)], [], []

    return _seq_call(
        f"gmlp_fwd{layer}", body, steps=s // MIX_CHUNK,
        ins=[_whole(ln_w), _whole(ln_b), _whole(ws), _whole(bs_cols),
             _seg(proj, "c_u", MIX_CHUNK), _seg(proj, "c_v", MIX_CHUNK)],
        outs=[_row_out(s, GROUP_WIDTH, BF16, MIX_CHUNK)])[0]


def _gmlp_bwd(layer, proj, ln_w, ln_b, ws, bs_cols, dmix):
    s = proj.shape[0]

    def body(i, v, c):
        _, vjp = jax.vjp(_gmlp_tile, *v[:6])
        g = vjp(v[6])
        return [jnp.concatenate(g[4:6], axis=1)], list(g[0:4]), []

    return _seq_call(
        f"gmlp_bwd{layer}", body, steps=s // MIX_CHUNK,
        ins=[_whole(ln_w), _whole(ln_b), _whole(ws), _whole(bs_cols),
             _seg(proj, "c_u", MIX_CHUNK), _seg(proj, "c_v", MIX_CHUNK),
             _rows(dmix, MIX_CHUNK, col=2, width=GROUP_WIDTH)],
        outs=[_row_out(s, 2 * GROUP_WIDTH, BF16, MIX_CHUNK)],
        accs=[(ln_w.shape, F32), (ln_b.shape, F32), (ws.shape, F32), (bs_cols.shape, F32)])


def _conformer_ins(proj, dw_w, dw_b, ln_w, ln_b):
    return ([_whole(dw_w), _whole(dw_b), _whole(ln_w), _whole(ln_b)]
            + [_seg(proj, k, CONV_TILE, shift=-1) for k in ("d_a", "d_gate")]
            + [_seg(proj, k, CONV_TILE) for k in ("d_a", "d_gate")])


def _conformer_fwd(layer, proj, dw_w, dw_b, ln_w, ln_b):
    s = proj.shape[0]

    def body(i, v, c):
        prev = _mask_prev(i, v[4:6])
        return [_conformer_tile(v[0], v[1], v[2], v[3], *prev, v[6], v[7])], [], []

    return _seq_call(
        f"conformer_fwd{layer}", body, steps=s // CONV_TILE, ins=_conformer_ins(proj, dw_w, dw_b, ln_w, ln_b),
        outs=[_row_out(s, GROUP_WIDTH, BF16, CONV_TILE)])[0]


def _conformer_bwd(layer, proj, dw_w, dw_b, ln_w, ln_b, dmix):
    s = proj.shape[0]

    def body(i, v, dprev):
        prev = _mask_prev(i, v[4:6])
        _, vjp = jax.vjp(_conformer_tile, v[0], v[1], v[2], v[3], *prev, v[6], v[7])
        g = vjp(v[8])
        return [jnp.concatenate([g[6] + dprev[0], g[7] + dprev[1]], axis=1)], list(g[0:4]), [g[4], g[5]]

    return _seq_call(
        f"conformer_bwd{layer}", body, steps=s // CONV_TILE, reverse=True,
        ins=_conformer_ins(proj, dw_w, dw_b, ln_w, ln_b) + [_rows(dmix, CONV_TILE, col=3, width=GROUP_WIDTH)],
        outs=[_row_out(s, 2 * GROUP_WIDTH, BF16, CONV_TILE)],
        accs=[(dw_w.shape, F32), (dw_b.shape, F32), (ln_w.shape, F32), (ln_b.shape, F32)],
        carries=[((CONV_TILE, GROUP_WIDTH), F32)] * 2)


SMALL = ["lower_bounds", "norm_mix_pre", "norm_mix_post", "norm_ff_pre", "norm_ff_post", "hgrn_norm_w",
         "gdn_conv_w", "gdn_a_log", "gdn_dt_bias", "gdn_norm_w", "gmlp_ln_w", "gmlp_ln_b", "gmlp_w_s",
         "gmlp_b_s", "conv_dw_w", "conv_dw_b", "conv_ln_w", "conv_ln_b"]


def _gate_row(v):
    return jnp.pad(v.reshape(1, N_HEADS), ((0, 0), (N_HEADS, LANES - 2 * N_HEADS)))


def _cut_rows(layer, rows, cols):
    return ((DEPTH, N_CHIPS, rows, cols), lambda tm, tn: (None, None, tm, tn),
            lambda i, j, tm, tn: (layer, i // (rows // tm), i % (rows // tm), j))


def _cut_cols(layer, rows, cols):
    return ((DEPTH, N_CHIPS, rows, cols), lambda tm, tn: (None, None, tm, tn),
            lambda i, j, tm, tn: (layer, j // (cols // tn), i, j % (cols // tn)))


def _relu2(acc):
    r = jnp.maximum(acc, 0.0)
    return acc, r * r


def _relu2_bwd(acc, u):
    return (2.0 * jnp.maximum(u, 0.0) * acc,)


def _local_step(x, target, sp, w_in, w_out, w_ff1, w_ff2):
    row = lambda v: v.reshape(1, -1)
    saved = []
    for l in range(DEPTH):
        par = dict(
            lbp=sp["lower_bounds"], hn=row(sp["hgrn_norm_w"][l]), cw=sp["gdn_conv_w"][l],
            alog=_gate_row(sp["gdn_a_log"][l]), dtb=_gate_row(sp["gdn_dt_bias"][l]), gn=row(sp["gdn_norm_w"][l]),
            glw=row(sp["gmlp_ln_w"][l]), glb=row(sp["gmlp_ln_b"][l]), ws=sp["gmlp_w_s"][l],
            bsc=jnp.pad(sp["gmlp_b_s"][l].T, ((0, 0), (0, LANES - N_HEADS))),
            dww=sp["conv_dw_w"][l], dwb=row(sp["conv_dw_b"][l]), clw=row(sp["conv_ln_w"][l]),
            clb=row(sp["conv_ln_b"][l]), n1=row(sp["norm_mix_pre"][l]), n2=row(sp["norm_mix_post"][l]),
            n3=row(sp["norm_ff_pre"][l]), n4=row(sp["norm_ff_post"][l]))
        h = _rms_fwd(f"norm_mix_pre{l}", x, par["n1"])
        proj = _matmul(f"in_proj{l}", h, w_in[l], "nn", [F32])[0]
        o_a, st_a = _hgrn_fwd(l, proj, par["lbp"], par["hn"])
        o_b, st_b = _gdn_fwd(l, proj, par["cw"], par["alog"], par["dtb"], par["gn"])
        o_c = _gmlp_fwd(l, proj, par["glw"], par["glb"], par["ws"], par["bsc"])
        o_d = _conformer_fwd(l, proj, par["dww"], par["dwb"], par["clw"], par["clb"])
        mix = jnp.concatenate([o_a, o_b, o_c, o_d], axis=1)
        y = _matmul(f"out_proj{l}", mix, w_out[l], "nn", [F32])[0]
        x1 = _resid_rms_fwd(f"norm_mix_post{l}", x, y, par["n2"])
        h2 = _rms_fwd(f"norm_ff_pre{l}", x1, par["n3"])
        u, act = _matmul(f"ff1_{l}", h2, w_ff1[l], "nn", [F32, BF16], epilogue=_relu2)
        y2 = _matmul(f"ff2_{l}", act, w_ff2[l], "nn", [F32])[0]
        x2 = _resid_rms_fwd(f"norm_ff_post{l}", x1, y2, par["n4"])
        saved.append(dict(par=par, x=x, h=h, proj=proj, st_a=st_a, st_b=st_b, mix=mix, y=y, x1=x1, h2=h2,
                          u=u, act=act, y2=y2))
        x = x2

    dx, loss_acc = _loss_head("loss_head", x, target)
    loss_part = loss_acc[0, 0]

    gs = {k: [None] * DEPTH for k in SMALL if k != "lower_bounds"}
    g_lb = jnp.zeros((DEPTH, GROUP_WIDTH), F32)
    g_in = [None] * DEPTH
    g_out = g_ff1 = g_ff2 = None
    for l in reversed(range(DEPTH)):
        sv = saved[l]
        par = sv["par"]
        dy2, dn4 = _resid_rms_bwd(f"norm_ff_post_bwd{l}", sv["y2"], par["n4"], dx)
        du = _matmul(f"ff2_dx{l}", dy2, w_ff2[l], "nt", [BF16], epilogue=_relu2_bwd, extras=(sv["u"],))[0]
        g_ff2 = _matmul(f"ff2_dw{l}", sv["act"], dy2, "tn", [F32], into=g_ff2,
                        place=_cut_rows(l, D_FF // N_CHIPS, D_MODEL))[0]
        g_ff1 = _matmul(f"ff1_dw{l}", sv["h2"], du, "tn", [F32], into=g_ff1,
                        place=_cut_cols(l, D_MODEL, D_FF // N_CHIPS))[0]
        dh2 = _matmul(f"ff1_dx{l}", du, w_ff1[l], "nt", [F32])[0]
        dx1, dn3 = _rms_bwd(f"norm_ff_pre_bwd{l}", sv["x1"], par["n3"], dh2, dx)
        dy, dn2 = _resid_rms_bwd(f"norm_mix_post_bwd{l}", sv["y"], par["n2"], dx1)
        dmix = _matmul(f"out_proj_dx{l}", dy, w_out[l], "nt", [F32])[0]
        g_out = _matmul(f"out_proj_dw{l}", sv["mix"], dy, "tn", [F32], into=g_out, max_tm=GROUP_WIDTH,
                        place=_cut_rows(l, GROUP_WIDTH, D_MODEL))[0]
        proj = sv["proj"]
        dp_a, dlb, dhn = _hgrn_bwd(l, proj, par["lbp"], par["hn"], sv["st_a"], dmix)
        dp_b, dp_ab, dcw, dalog, ddtb, dgn = _gdn_bwd(
            l, proj, par["cw"], par["alog"], par["dtb"], par["gn"], sv["st_b"], dmix)
        dp_c, dglw, dglb, dws, dbsc = _gmlp_bwd(l, proj, par["glw"], par["glb"], par["ws"], par["bsc"], dmix)
        dp_d, ddww, ddwb, dclw, dclb = _conformer_bwd(
            l, proj, par["dww"], par["dwb"], par["clw"], par["clb"], dmix)
        dproj = jnp.concatenate([dp_a, dp_b, dp_c, dp_d, dp_ab], axis=1)
        g_in[l] = _matmul(f"in_proj_dw{l}", sv["h"], dproj, "tn", [F32])[0]
        dh = _matmul(f"in_proj_dx{l}", dproj, w_in[l], "nt", [F32])[0]
        dx, dn1 = _rms_bwd(f"norm_mix_pre_bwd{l}", sv["x"], par["n1"], dh, dx1)
        g_lb = g_lb + dlb
        for k, v in dict(
                norm_mix_pre=dn1[0], norm_mix_post=dn2[0], norm_ff_pre=dn3[0], norm_ff_post=dn4[0],
                hgrn_norm_w=dhn[0], gdn_conv_w=dcw, gdn_a_log=dalog[0, N_HEADS:2 * N_HEADS],
                gdn_dt_bias=ddtb[0, N_HEADS:2 * N_HEADS], gdn_norm_w=dgn[0], gmlp_ln_w=dglw[0],
                gmlp_ln_b=dglb[0], gmlp_w_s=dws, gmlp_b_s=dbsc[:, :N_HEADS].T, conv_dw_w=ddww,
                conv_dw_b=ddwb[0], conv_ln_w=dclw[0], conv_ln_b=dclb[0]).items():
            gs[k][l] = v
    small_grads = {k: jnp.stack(v) for k, v in gs.items()}
    small_grads["lower_bounds"] = g_lb
    return loss_part, dx, small_grads, g_in, g_out, g_ff1, g_ff2


HBM = pl.BlockSpec(memory_space=pl.ANY)


def _place():
    return lax.axis_index("x"), lax.axis_index("y"), lax.axis_index("c")


def _other_chips(x, y):
    chips = [(1 - x, y), (x, 1 - y), (1 - x, 1 - y)]
    return [(px, py, 2 * px + py) for px, py in chips]


def _gather_weights(name, blk):
    def body(blk_ref, out_ref, ici_send, ici_recv, d2d_send, d2d_recv):
        x, y, c = _place()
        mine = 2 * x + y

        def ici(j, chip_of_slab, to):
            return pltpu.make_async_remote_copy(
                src_ref=blk_ref.at[c], dst_ref=out_ref.at[chip_of_slab, c], send_sem=ici_send.at[j],
                recv_sem=ici_recv.at[j], device_id=to, device_id_type=MESH)

        def d2d(j, chip_of_slab, layer):
            return pltpu.make_async_remote_copy(
                src_ref=out_ref.at[chip_of_slab, layer], dst_ref=out_ref.at[chip_of_slab, layer],
                send_sem=d2d_send.at[j], recv_sem=d2d_recv.at[j], device_id=(x, y, 1 - c), device_id_type=MESH)

        peers = _other_chips(x, y)
        sends = [ici(j, mine, (px, py, c)) for j, (px, py, _) in enumerate(peers)]
        for cp in sends:
            cp.start()
        passed = []
        for j, (px, py, k) in enumerate(peers):
            ici(j, k, (px, py, c)).wait_recv()
            fwd = d2d(j, k, c)
            fwd.start()
            passed.append(fwd)
        for j, (px, py, k) in enumerate(peers):
            d2d(j, k, 1 - c).wait_recv()
        for cp in sends + passed:
            cp.wait_send()

    return pl.pallas_call(
        body, name=name, out_shape=jax.ShapeDtypeStruct((N_CHIPS,) + blk.shape, blk.dtype),
        in_specs=[HBM], out_specs=HBM, scratch_shapes=[pltpu.SemaphoreType.DMA((3,))] * 4,
    )(blk)


def _pair_exchange(name, g, by_layer):
    def body(g_ref, got_ref, send_sem, recv_sem):
        x, y, c = _place()
        cp = pltpu.make_async_remote_copy(
            src_ref=g_ref.at[1 - c] if by_layer else g_ref, dst_ref=got_ref, send_sem=send_sem,
            recv_sem=recv_sem, device_id=(x, y, 1 - c), device_id_type=MESH)
        cp.start()
        cp.wait()

    return pl.pallas_call(
        body, name=name, out_shape=jax.ShapeDtypeStruct(g.shape[1:] if by_layer else g.shape, g.dtype),
        in_specs=[HBM], out_specs=HBM, scratch_shapes=[pltpu.SemaphoreType.DMA, pltpu.SemaphoreType.DMA],
    )(g)


def _scatter_partials(name, p):
    def body(p_ref, out_ref, send_sems, recv_sems):
        x, y, c = _place()

        def cp(j, to):
            return pltpu.make_async_remote_copy(
                src_ref=p_ref.at[j], dst_ref=out_ref.at[j], send_sem=send_sems.at[j],
                recv_sem=recv_sems.at[j], device_id=to, device_id_type=MESH)

        copies = [cp(j, (px, py, c)) for j, (px, py, _) in enumerate(_other_chips(x, y))]
        for s in copies:
            s.start()
        for s in copies:
            s.wait_recv()
        for s in copies:
            s.wait_send()

    return pl.pallas_call(
        body, name=name, out_shape=jax.ShapeDtypeStruct(p.shape, p.dtype), in_specs=[HBM], out_specs=HBM,
        scratch_shapes=[pltpu.SemaphoreType.DMA((3,)), pltpu.SemaphoreType.DMA((3,))],
    )(p)


def _gather_small(name, pack):
    def body(p_ref, out_ref, send_sems, recv_sems, local_sem):
        x, y, c = _place()
        me = 4 * x + 2 * y + c
        local = pltpu.make_async_copy(p_ref, out_ref.at[me], local_sem)
        local.start()
        flips = [(fx, fy, fc) for fx in (0, 1) for fy in (0, 1) for fc in (0, 1)][1:]
        peers = [((1 - x) if fx else x, (1 - y) if fy else y, (1 - c) if fc else c) for fx, fy, fc in flips]

        def cp(j, slot, to):
            return pltpu.make_async_remote_copy(
                src_ref=p_ref, dst_ref=out_ref.at[slot], send_sem=send_sems.at[j], recv_sem=recv_sems.at[j],
                device_id=to, device_id_type=MESH)

        sends = [cp(j, me, to) for j, to in enumerate(peers)]
        for s in sends:
            s.start()
        for j, (px, py, pc) in enumerate(peers):
            cp(j, 4 * px + 2 * py + pc, (px, py, pc)).wait_recv()
        for s in sends:
            s.wait_send()
        local.wait()

    return pl.pallas_call(
        body, name=name, out_shape=jax.ShapeDtypeStruct((N_DEV,) + pack.shape, pack.dtype), in_specs=[HBM],
        out_specs=HBM,
        scratch_shapes=[pltpu.SemaphoreType.DMA((7,)), pltpu.SemaphoreType.DMA((7,)), pltpu.SemaphoreType.DMA],
    )(pack)


SLAB_ROWS = 256


I_CORE, I_CHIP, I_PEER = 0, 1, 2


def _peer_chip(pre, j):
    return jnp.where(j == 0, pre[I_PEER][0], jnp.where(j == 1, pre[I_PEER + 1][0], pre[I_PEER + 2][0]))


def _pair_sum(name, g, got, where):
    _, _, r, c = g.shape
    t = _pick_tile(r, (SLAB_ROWS, 128, 64, 8))
    per = r // t

    def body(i, v, car, pre):
        return [v[0] + v[1]], [], []

    return _seq_call(
        name, body, steps=3 * per, prefetch=where,
        ins=[(g, (None, None, t, c), lambda i, pre: (pre[I_CORE][0], _peer_chip(pre, i // per), i % per, 0)),
             (got, (None, t, c), lambda i, pre: (_peer_chip(pre, i // per), i % per, 0))],
        outs=[((3, r, c), BF16, (None, t, c), lambda i, pre: (i // per, i % per, 0))])[0]


def _chip_sum(name, g, got, rcv, where):
    _, _, r, c = g.shape
    t = _pick_tile(r, (SLAB_ROWS, 128, 64, 8))

    def body(i, v, car, pre):
        acc = v[0] + v[1]
        for part in v[2:]:
            acc = acc + part.astype(F32)
        return [acc], [], []

    return _seq_call(
        name, body, steps=r // t, prefetch=where,
        ins=[(g, (None, None, t, c), lambda i, pre: (pre[I_CORE][0], pre[I_CHIP][0], i, 0)),
             (got, (None, t, c), lambda i, pre: (pre[I_CHIP][0], i, 0))]
        + [(rcv, (None, t, c), (lambda i, pre, _j=j: (_j, i, 0))) for j in range(3)],
        outs=[((r, c), F32, (t, c), lambda i, pre: (i, 0))])[0]


def _adamw_math(w, g, m, v):
    m = ADAM_B1 * m + (1.0 - ADAM_B1) * g
    v = ADAM_B2 * v + (1.0 - ADAM_B2) * (g * g)
    m_hat = m / (1.0 - ADAM_B1 ** ADAM_STEP)
    v_hat = v / (1.0 - ADAM_B2 ** ADAM_STEP)
    delta = -ADAM_LR * (m_hat / (jnp.sqrt(v_hat) + ADAM_EPS) + ADAM_WD * w)
    return delta, m, v


def _adamw(name, w, g, m, v):
    n, r, c = w.shape
    t = _pick_tile(r, (SLAB_ROWS, 128, 64, 8))
    per = r // t

    def body(i, vals, car):
        return list(_adamw_math(*vals)), [], []

    blk = lambda a: (a, (None, t, c), lambda i: (i // per, i % per, 0))
    out = ((n, r, c), F32, (None, t, c), lambda i: (i // per, i % per, 0))
    return _seq_call(name, body, steps=n * per, ins=[blk(w), blk(g), blk(m), blk(v)], outs=[out] * 3)


def _adamw_cut(name, w, mine, sib, m, v, where):
    n, r, c = w.shape
    t = _pick_tile(r, (SLAB_ROWS, 128, 64, 8))
    per = r // t

    def body(i, vals, car, pre):
        g = jnp.where(i // per == pre[I_CORE][0], vals[1], vals[2])
        return [g] + list(_adamw_math(vals[0], g, vals[3], vals[4])), [], []

    by_layer = lambda a: (a, (None, t, c), lambda i, pre: (i // per, i % per, 0))
    flat = lambda a: (a, (t, c), lambda i, pre: (i % per, 0))
    out = ((n, r, c), F32, (None, t, c), lambda i, pre: (i // per, i % per, 0))
    return _seq_call(name, body, steps=n * per, prefetch=where,
                     ins=[by_layer(w), flat(mine), flat(sib), by_layer(m), by_layer(v)], outs=[out] * 4)


def _ordered_sum(name, packs):
    n, r, c = packs.shape

    def body(i, v, car):
        acc = v[0][0]
        for k in range(1, n):
            acc = acc + v[0][k]
        return [acc], [], []

    return _seq_call(name, body, steps=1, ins=[_whole(packs)], outs=[((r, c), F32, (r, c), lambda i: (0, 0))])[0]


def _pack(arrays):
    flat = []
    for a in arrays:
        a = a.reshape(-1).astype(F32)
        pad = (-a.shape[0]) % LANES
        flat.append(jnp.pad(a, (0, pad)) if pad else a)
    v = jnp.concatenate(flat)
    pad = (-v.shape[0]) % (64 * LANES)
    if pad:
        v = jnp.pad(v, (0, pad))
    return v.reshape(-1, LANES)


def _unpack(pack, shapes):
    v = pack.reshape(-1)
    out, off = [], 0
    for s in shapes:
        n = math.prod(s)
        out.append(v[off:off + n].reshape(s))
        off += n + ((-n) % LANES)
    return out


def _reorder_in(full):
    g0 = 8 * GROUP_WIDTH
    pad = jnp.zeros(full.shape[:-1] + (LANES - 2 * N_HEADS,), full.dtype)
    return jnp.concatenate([full[..., :g0], full[..., g0 + 2 * N_HEADS:], full[..., g0:g0 + 2 * N_HEADS], pad], axis=-1)


def _restore_in(padded):
    g0 = 8 * GROUP_WIDTH
    wide = 12 * GROUP_WIDTH
    return jnp.concatenate([padded[..., :g0], padded[..., wide:wide + 2 * N_HEADS], padded[..., g0:wide]], axis=-1)


def _reduce_big(tag, g, where):
    got = _pair_exchange(f"pair_exchange_{tag}", g, True)
    pair = _pair_sum(f"pair_sum_{tag}", g, got, where)
    rcv = _scatter_partials(f"scatter_partials_{tag}", pair)
    mine = _chip_sum(f"chip_sum_{tag}", g, got, rcv, where)
    return mine, _pair_exchange(f"pair_send_{tag}", mine, False)


def kernel(x, lower_bounds, norm_mix_pre, norm_mix_post, norm_ff_pre, norm_ff_post, w_in, w_out, hgrn_norm_w, gdn_conv_w, gdn_a_log, gdn_dt_bias, gdn_norm_w, gmlp_ln_w, gmlp_ln_b, gmlp_w_s, gmlp_b_s, conv_dw_w, conv_dw_b, conv_ln_w, conv_ln_b, w_ff1, w_ff2, loss_target, m_lower_bounds, m_norm_mix_pre, m_norm_mix_post, m_norm_ff_pre, m_norm_ff_post, m_w_in, m_w_out, m_hgrn_norm_w, m_gdn_conv_w, m_gdn_a_log, m_gdn_dt_bias, m_gdn_norm_w, m_gmlp_ln_w, m_gmlp_ln_b, m_gmlp_w_s, m_gmlp_b_s, m_conv_dw_w, m_conv_dw_b, m_conv_ln_w, m_conv_ln_b, m_w_ff1, m_w_ff2, v_lower_bounds, v_norm_mix_pre, v_norm_mix_post, v_norm_ff_pre, v_norm_ff_post, v_w_in, v_w_out, v_hgrn_norm_w, v_gdn_conv_w, v_gdn_a_log, v_gdn_dt_bias, v_gdn_norm_w, v_gmlp_ln_w, v_gmlp_ln_b, v_gmlp_w_s, v_gmlp_b_s, v_conv_dw_w, v_conv_dw_b, v_conv_ln_w, v_conv_ln_b, v_w_ff1, v_w_ff2):
    args = dict(locals())
    chip = 2 * lax.axis_index("x") + lax.axis_index("y")
    where = tuple(jnp.asarray(v, jnp.int32).reshape(1)
                  for v in (lax.axis_index("c"), chip, chip ^ 2, chip ^ 1, chip ^ 3))

    def gathered(name, w):
        blk = w.astype(BF16)
        return lax.dynamic_update_slice(_gather_weights(name, blk), blk[None], (chip, 0, 0, 0))

    win_all = gathered("gather_w_in", w_in)
    wout_all = gathered("gather_w_out", w_out)
    ff1_all = gathered("gather_w_ff1", w_ff1)
    ff2_all = gathered("gather_w_ff2", w_ff2)
    cut_shapes = [gdn_conv_w.shape, conv_dw_w.shape]
    cuts = _gather_small("gather_cut_small", _pack([gdn_conv_w, conv_dw_w]))
    cut_parts = [_unpack(cuts[2 * k], cut_shapes) for k in range(N_CHIPS)]
    sp = {k: args[k] for k in SMALL}
    sp["gdn_conv_w"] = jnp.concatenate([p[0] for p in cut_parts], axis=-1)
    sp["conv_dw_w"] = jnp.concatenate([p[1] for p in cut_parts], axis=-1)

    w_in_l, w_out_l, w_ff1_l, w_ff2_l = [], [], [], []
    for l in range(DEPTH):
        full = jnp.concatenate([win_all[k, l] for k in range(N_CHIPS)], axis=-1)
        w_in_l.append(_reorder_in(full))
        w_out_l.append(wout_all[:, l].reshape(D_MODEL, D_MODEL))
        w_ff1_l.append(jnp.concatenate([ff1_all[k, l] for k in range(N_CHIPS)], axis=-1))
        w_ff2_l.append(ff2_all[:, l].reshape(D_FF, D_MODEL))

    loss_part, grad_x, small_g, g_in, g_out, g_ff1, g_ff2 = _local_step(
        x[0], loss_target[0], sp, w_in_l, w_out_l, w_ff1_l, w_ff2_l)

    gi = jnp.stack([jnp.stack(jnp.split(_restore_in(g_in[l]), N_CHIPS, axis=-1)) for l in range(DEPTH)])
    big_g = dict(w_in=_reduce_big("w_in", gi, where), w_out=_reduce_big("w_out", g_out, where),
                 w_ff1=_reduce_big("w_ff1", g_ff1, where), w_ff2=_reduce_big("w_ff2", g_ff2, where))

    names = SMALL + ["loss"]
    small_g["loss"] = loss_part.reshape(1)
    shapes = [small_g[k].shape for k in names]
    total = _ordered_sum("sum_small", _gather_small("gather_small", _pack([small_g[k] for k in names])))
    summed = dict(zip(names, _unpack(total, shapes)))
    loss = summed.pop("loss")[0]
    for k, width in (("gdn_conv_w", gdn_conv_w.shape[-1]), ("conv_dw_w", conv_dw_w.shape[-1])):
        summed[k] = lax.dynamic_slice_in_dim(summed[k], chip * width, width, axis=-1)

    grads, deltas, new_m, new_v = {}, {}, {}, {}
    for k in ("w_in", "w_out", "w_ff1", "w_ff2"):
        grads[k], deltas[k], new_m[k], new_v[k] = _adamw_cut(
            f"adamw_{k}", args[k], *big_g[k], args["m_" + k], args["v_" + k], where)
    local_shapes = [args[k].shape for k in SMALL]
    packs = [_pack([src[k] for k in SMALL]) for src in (
        {k: args[k] for k in SMALL}, summed, {k: args["m_" + k] for k in SMALL}, {k: args["v_" + k] for k in SMALL})]
    d_s, m_s, v_s = _adamw("adamw_small", *[p[None] for p in packs])
    for k, d, mm, vv in zip(SMALL, _unpack(d_s[0], local_shapes), _unpack(m_s[0], local_shapes),
                            _unpack(v_s[0], local_shapes)):
        grads[k], deltas[k], new_m[k], new_v[k] = summed[k], d, mm, vv

    order = ["lower_bounds", "norm_mix_pre", "norm_mix_post", "norm_ff_pre", "norm_ff_post", "w_in", "w_out",
             "hgrn_norm_w", "gdn_conv_w", "gdn_a_log", "gdn_dt_bias", "gdn_norm_w", "gmlp_ln_w", "gmlp_ln_b",
             "gmlp_w_s", "gmlp_b_s", "conv_dw_w", "conv_dw_b", "conv_ln_w", "conv_ln_b", "w_ff1", "w_ff2"]
    return (loss, grad_x[None], *[grads[k] for k in order], *[deltas[k] for k in order],
            *[new_m[k] for k in order], *[new_v[k] for k in order])
```

```python
import functools
import math

import jax
import jax.numpy as jnp
from jax import lax
from jax.experimental import pallas as pl
from jax.experimental.pallas import tpu as pltpu

F32 = jnp.float32
BF16 = jnp.bfloat16

DEPTH = 2
D_MODEL = 2048
GROUP_WIDTH = 512
HEAD_DIM = 128
N_HEADS = 4
CHUNK = 64
SHORT_CONV = 4
MIX_CHUNK = 128
CONV_WIDTH = 31
D_FF = 4 * D_MODEL
D_IN_PROJ = 12 * GROUP_WIDTH + 2 * N_HEADS
EPS = 1e-6
NEG_BIG = -1e30
TINY = 1e-30
ADAM_LR = 0.001
ADAM_B1 = 0.9
ADAM_B2 = 0.999
ADAM_EPS = 1e-08
ADAM_WD = 0.01
ADAM_STEP = 10

LANES = 128
P_IN = 12 * GROUP_WIDTH + LANES
SUB_BLOCK = 16
HGRN_TILE = 128
GDN_TILE = 128
CONV_TILE = 128
CONV_HALO = 32
VMEM_LIMIT = 56 * 1024 * 1024
N_CHIPS = 4
N_DEV = 8
MESH = pl.DeviceIdType.MESH


_DIMS = {
    "nn": (((1,), (0,)), ((), ())),
    "nt": (((1,), (1,)), ((), ())),
    "tn": (((0,), (0,)), ((), ())),
}


def _split2(a):
    hi = a.astype(BF16)
    return hi, (a - hi.astype(F32)).astype(BF16)


def _raw_mm(a, b, mode, exact):
    dot = lambda p, q: lax.dot_general(p, q, _DIMS[mode], preferred_element_type=F32)
    if not exact:
        return dot(a.astype(BF16), b.astype(BF16))
    a_hi, a_lo = _split2(a)
    b_hi, b_lo = _split2(b)
    return dot(a_hi, b_hi) + (dot(a_hi, b_lo) + dot(a_lo, b_hi))


@functools.partial(jax.custom_vjp, nondiff_argnums=(2, 3))
def _mm(a, b, mode, exact):
    return _raw_mm(a, b, mode, exact)


def _mm_fwd(a, b, mode, exact):
    return _raw_mm(a, b, mode, exact), (a, b)


def _mm_bwd(mode, exact, res, g):
    a, b = res
    if mode == "nn":
        return _raw_mm(g, b, "nt", exact), _raw_mm(a, g, "tn", exact)
    if mode == "nt":
        return _raw_mm(g, b, "nn", exact), _raw_mm(g, a, "tn", exact)
    return _raw_mm(b, g, "nt", exact), _raw_mm(a, g, "nn", exact)


_mm.defvjp(_mm_fwd, _mm_bwd)


def _sig(x):
    return jax.nn.sigmoid(x)


def _silu(x):
    return x * jax.nn.sigmoid(x)


def _gelu(x):
    return 0.5 * x * (1.0 + lax.erf(x * (1.0 / math.sqrt(2.0))))


def _rms(x, w):
    return x * lax.rsqrt(jnp.mean(x * x, axis=-1, keepdims=True) + EPS) * w


def _ln(x, w, b):
    mu = jnp.mean(x, axis=-1, keepdims=True)
    xc = x - mu
    var = jnp.mean(xc * xc, axis=-1, keepdims=True)
    return xc * lax.rsqrt(var + EPS) * w + b


def _iota(shape, dim):
    return lax.broadcasted_iota(jnp.int32, shape, dim)


def _tri_mm(x, mode):
    n = x.shape[0]
    tri = (_iota((n, n), 0) >= _iota((n, n), 1)).astype(BF16)
    x1 = x.astype(BF16)
    r1 = x - x1.astype(F32)
    x2 = r1.astype(BF16)
    x3 = (r1 - x2.astype(F32)).astype(BF16)
    dot = lambda q: lax.dot_general(tri, q, _DIMS[mode], preferred_element_type=F32)
    return dot(x1) + (dot(x2) + dot(x3))


@jax.custom_vjp
def _cumsum_rows(x):
    return _tri_mm(x, "nn")


_cumsum_rows.defvjp(lambda x: (_tri_mm(x, "nn"), None), lambda _, g: (_tri_mm(g, "tn"),))


def _hgrn_head(q, k, v, b, st):
    n = q.shape[0]
    ii = _iota((n, 1), 0)
    zpad = jnp.zeros((SUB_BLOCK, HEAD_DIM), F32)
    k_ext = jnp.concatenate([zpad, k], axis=0)
    b_ext = jnp.concatenate([zpad, b], axis=0)
    v_ext = jnp.concatenate([zpad, v], axis=0)
    o = jnp.zeros((n, HEAD_DIM), F32)
    for d in range(SUB_BLOCK):
        ks = k_ext[SUB_BLOCK - d:SUB_BLOCK - d + n]
        bs = b_ext[SUB_BLOCK - d:SUB_BLOCK - d + n]
        vs = v_ext[SUB_BLOCK - d:SUB_BLOCK - d + n]
        e = jnp.exp(jnp.where((ii % SUB_BLOCK) >= d, b - bs, NEG_BIG))
        o = o + jnp.sum(q * ks * e, axis=-1, keepdims=True) * vs
    blocks = [o[0:SUB_BLOCK]]
    for blk in range(1, n // SUB_BLOCK):
        lo = SUB_BLOCK * blk
        r = b[lo - 1:lo]
        a_q = q[lo:lo + SUB_BLOCK] * jnp.exp(b[lo:lo + SUB_BLOCK] - r)
        b_k = jnp.where(ii < lo, k * jnp.exp(jnp.minimum(r - b, 0.0)), 0.0)
        sc = _mm(a_q, b_k, "nt", False)
        blocks.append(o[lo:lo + SUB_BLOCK] + _mm(sc, v, "nn", False))
    o = jnp.concatenate(blocks, axis=0)
    o = o + _mm(q * jnp.exp(b), st, "nt", False)
    b_end = b[n - 1:n]
    st_new = st * jnp.exp(b_end) + _mm(v, k * jnp.exp(b_end - b), "tn", False)
    return o, st_new


def _hgrn_chunk(layer, lbp, nw, aq, af, ai, ag, states):
    rows = [lbp[i:i + 1, :] for i in range(DEPTH)]
    mx = functools.reduce(jnp.maximum, rows)
    es = [jnp.exp(r - mx) for r in rows]
    den = functools.reduce(lambda p, s: p + s, es)
    soft = [e / den for e in es]
    lb = functools.reduce(lambda p, s: p + s, soft[:layer + 1]) - soft[0]
    f = lb + (1.0 - lb) * _sig(af)
    logf = jnp.log(jnp.maximum(f, TINY))
    k = (1.0 - lb) * _sig(-af)
    q = _silu(aq)
    b = _cumsum_rows(logf)
    outs, new_states = [], []
    for h in range(N_HEADS):
        sl = slice(HEAD_DIM * h, HEAD_DIM * (h + 1))
        o, st = _hgrn_head(q[:, sl], k[:, sl], ai[:, sl], b[:, sl], states[h])
        outs.append(_rms(o, nw) * _silu(ag[:, sl]))
        new_states.append(st)
    return jnp.concatenate(outs, axis=1), new_states


def _hgrn_tile(layer, lbp, nw, aq, af, ai, ag, states):
    outs = []
    for ci in range(aq.shape[0] // CHUNK):
        rs = slice(CHUNK * ci, CHUNK * (ci + 1))
        o, states = _hgrn_chunk(layer, lbp, nw, aq[rs], af[rs], ai[rs], ag[rs], states)
        outs.append(o)
    return jnp.concatenate(outs, axis=0), states


def _short_conv(prev, cur, w):
    n = cur.shape[0]
    ext = jnp.concatenate([prev[n - 8:n], cur], axis=0)
    y = jnp.zeros_like(cur)
    for t in range(SHORT_CONV):
        off = 8 - (SHORT_CONV - 1) + t
        y = y + w[t:t + 1, :] * ext[off:off + n]
    return _silu(y)


def _gdn_tile(cw, alog, dtb, nw, pq, pk, pv, cq, ck, cv, bz, ab, states):
    n = cq.shape[0]
    q_all = _short_conv(pq, cq, cw[:, 0:GROUP_WIDTH])
    k_all = _short_conv(pk, ck, cw[:, GROUP_WIDTH:2 * GROUP_WIDTH])
    v_all = _short_conv(pv, cv, cw[:, 2 * GROUP_WIDTH:3 * GROUP_WIDTH])
    beta_all = _sig(ab)
    g_all = -jnp.exp(alog) * jax.nn.softplus(ab + dtb)
    units = [(ci, h) for ci in range(n // CHUNK) for h in range(N_HEADS)]
    gc_all = [_cumsum_rows(g_all[CHUNK * ci:CHUNK * (ci + 1)]) for ci in range(n // CHUNK)]
    gc_t = [g.T for g in gc_all]
    ii = _iota((CHUNK, CHUNK), 0)
    jj = _iota((CHUNK, CHUNK), 1)
    eye = (ii == jj).astype(F32)

    def cut(a, ci, h):
        return a[CHUNK * ci:CHUNK * (ci + 1), HEAD_DIM * h:HEAD_DIM * (h + 1)]

    q = [cut(q_all, ci, h) for ci, h in units]
    k = [cut(k_all, ci, h) for ci, h in units]
    v = [cut(v_all, ci, h) for ci, h in units]
    q = [t * lax.rsqrt(jnp.sum(t * t, axis=-1, keepdims=True) + EPS) * (HEAD_DIM ** -0.5) for t in q]
    k = [t * lax.rsqrt(jnp.sum(t * t, axis=-1, keepdims=True) + EPS) for t in k]
    beta = [beta_all[CHUNK * ci:CHUNK * (ci + 1), h:h + 1] for ci, h in units]
    gc = [gc_all[ci][:, N_HEADS + h:N_HEADS + h + 1] for ci, h in units]
    gcr = [gc_t[ci][N_HEADS + h:N_HEADS + h + 1, :] for ci, h in units]
    gamma = [jnp.exp(jnp.where(ii >= jj, a - b, NEG_BIG)) for a, b in zip(gc, gcr)]
    kb = [a * b for a, b in zip(k, beta)]
    m = [jnp.where(ii > jj, _mm(a, b, "nt", False) * g, 0.0) for a, b, g in zip(kb, k, gamma)]
    inv = [eye - t for t in m]
    p = m
    for _ in range(max(1, int(math.ceil(math.log2(CHUNK))) - 1)):
        p = [_mm(t, t, "nn", True) for t in p]
        inv = [a + _mm(a, t, "nn", True) for a, t in zip(inv, p)]
    eg = [jnp.exp(t) for t in gc]
    u = [_mm(a, b * c, "nn", True) for a, b, c in zip(inv, v, beta)]
    w = [_mm(a, b * c, "nn", True) for a, b, c in zip(inv, kb, eg)]
    qk = [_mm(a, b, "nt", False) * g for a, b, g in zip(q, k, gamma)]
    qd = [a * b for a, b in zip(q, eg)]
    g_end = [t[CHUNK - 1:CHUNK] for t in gc]
    kd = [a * jnp.exp(e - g) for a, e, g in zip(k, g_end, gc)]
    states = list(states)
    outs = {}
    for i, (ci, h) in enumerate(units):
        st = states[h]
        v_new = u[i] - _mm(w[i], st, "nt", False)
        outs[ci, h] = _mm(qd[i], st, "nt", False) + _mm(qk[i], v_new, "nn", False)
        states[h] = st * jnp.exp(g_end[i]) + _mm(v_new, kd[i], "tn", False)
    rows = []
    for ci in range(n // CHUNK):
        rows.append(jnp.concatenate(
            [_rms(outs[ci, h], nw) * _silu(cut(bz, ci, h)) for h in range(N_HEADS)], axis=1))
    return jnp.concatenate(rows, axis=0), states


def _gmlp_tile(ln_w, ln_b, ws, bs_cols, cu, cv):
    u = _gelu(cu)
    v = _ln(_gelu(cv), ln_w, ln_b)
    n = cu.shape[0]
    tril = _iota((n, n), 0) >= _iota((n, n), 1)
    outs = []
    for h in range(N_HEADS):
        sl = slice(HEAD_DIM * h, HEAD_DIM * (h + 1))
        wc = jnp.where(tril, ws[h], 0.0)
        outs.append(_mm(wc, v[:, sl], "nn", False) + bs_cols[:, h:h + 1])
    return u * jnp.concatenate(outs, axis=1)


def _conformer_tile(dw_w, dw_b, ln_w, ln_b, pa, pg, ca, cg):
    n = ca.shape[0]
    yp = pa[n - CONV_HALO:n] * _sig(pg[n - CONV_HALO:n])
    ext = jnp.concatenate([yp, ca * _sig(cg)], axis=0)
    acc = jnp.zeros_like(ca)
    for t in range(CONV_WIDTH):
        off = CONV_HALO - (CONV_WIDTH - 1) + t
        acc = acc + dw_w[t:t + 1, :] * ext[off:off + n]
    return _silu(_ln(acc + dw_b, ln_w, ln_b))


def _seq_call(name, body, *, steps, ins, outs, accs=(), carries=(), reverse=False, prefetch=None):
    n_in, n_out, n_acc, n_car = len(ins), len(outs), len(accs), len(carries)
    n_pre = 0 if prefetch is None else len(prefetch)

    def logical(g):
        return (steps - 1 - g) if reverse else g

    def kern(*refs):
        pre = refs[:n_pre]
        refs = refs[n_pre:]
        in_refs = refs[:n_in]
        out_refs = refs[n_in:n_in + n_out]
        acc_refs = refs[n_in + n_out:n_in + n_out + n_acc]
        car_refs = refs[n_in + n_out + n_acc:]
        g = pl.program_id(0)

        @pl.when(g == 0)
        def _():
            for r in list(acc_refs) + list(car_refs):
                r[...] = jnp.zeros(r.shape, r.dtype)

        o, a, c = body(logical(g), [r[...] for r in in_refs], [r[...] for r in car_refs], *((pre,) if n_pre else ()))
        for r, v in zip(out_refs, o, strict=True):
            r[...] = v.astype(r.dtype)
        for r, v in zip(acc_refs, a, strict=True):
            r[...] += v
        for r, v in zip(car_refs, c, strict=True):
            r[...] = v

    def spec(block, fn):
        return pl.BlockSpec(block, lambda g, *pre: fn(logical(g), *((pre,) if n_pre else ())))

    in_specs = [spec(bs, fn) for (_, bs, fn) in ins]
    out_specs = [spec(bs, fn) for (_, _, bs, fn) in outs]
    out_specs += [pl.BlockSpec(shape, lambda g, *pre, _n=len(shape): (0,) * _n) for (shape, _) in accs]
    out_shape = [jax.ShapeDtypeStruct(s, d) for (s, d, _, _) in outs]
    out_shape += [jax.ShapeDtypeStruct(s, d) for (s, d) in accs]
    grid_spec = pltpu.PrefetchScalarGridSpec(
        num_scalar_prefetch=n_pre, grid=(steps,), in_specs=in_specs, out_specs=out_specs,
        scratch_shapes=[pltpu.VMEM(s, d) for (s, d) in carries])
    args = ([] if prefetch is None else list(prefetch)) + [a for (a, _, _) in ins]
    return pl.pallas_call(
        kern, name=name, grid_spec=grid_spec, out_shape=out_shape,
        compiler_params=pltpu.CompilerParams(dimension_semantics=("arbitrary",), vmem_limit_bytes=VMEM_LIMIT),
    )(*args)


def _whole(a):
    nd = a.ndim
    return (a, a.shape, lambda i, *pre: (0,) * nd)


def _rows(a, tile, col=0, width=None, shift=0):
    width = a.shape[1] if width is None else width
    if shift:
        return (a, (tile, width), lambda i, *pre: (jnp.maximum(i + shift, 0), col))
    return (a, (tile, width), lambda i, *pre: (i, col))


def _row_out(n_rows, width, dtype, tile):
    return ((n_rows, width), dtype, (tile, width), lambda i, *pre: (i, 0))


def _pick_tile(n, prefs):
    for t in prefs:
        if n % t == 0:
            return t
    return n


def _matmul(name, a, b, mode, out_dtypes, epilogue=None, extras=(), place=None, into=None, max_tm=1024):
    if mode == "nn":
        (m, k), n = a.shape, b.shape[1]
    elif mode == "nt":
        (m, k), n = a.shape, b.shape[0]
    else:
        (k, m), n = a.shape, b.shape[1]
    tm = _pick_tile(m, tuple(t for t in (1024, 512, 256, 128) if t <= max_tm))
    tn = _pick_tile(n, (1024, 896, 512, 256, 128))
    tk = _pick_tile(k, (2048, 896, 512, 256, 128))
    nk = k // tk
    n_ex = len(extras)
    n_out = len(out_dtypes)

    n_into = 0 if into is None else 1

    def kern(*refs):
        a_ref, b_ref = refs[0], refs[1]
        ex_refs = refs[2:2 + n_ex]
        out_refs = refs[2 + n_ex + n_into:2 + n_ex + n_into + n_out]
        acc_ref = refs[2 + n_ex + n_into + n_out]
        kk = pl.program_id(2)

        @pl.when(kk == 0)
        def _():
            acc_ref[...] = jnp.zeros(acc_ref.shape, F32)

        acc_ref[...] += lax.dot_general(a_ref[...], b_ref[...], _DIMS[mode], preferred_element_type=F32)

        @pl.when(kk == nk - 1)
        def _():
            acc = acc_ref[...]
            vals = (acc,) if epilogue is None else epilogue(acc, *[r[...] for r in ex_refs])
            for r, v in zip(out_refs, vals, strict=True):
                r[...] = v.astype(r.dtype)

    if mode == "tn":
        a_spec = pl.BlockSpec((tk, tm), lambda i, j, kk: (kk, i))
    else:
        a_spec = pl.BlockSpec((tm, tk), lambda i, j, kk: (i, kk))
    if mode == "nt":
        b_spec = pl.BlockSpec((tn, tk), lambda i, j, kk: (j, kk))
    else:
        b_spec = pl.BlockSpec((tk, tn), lambda i, j, kk: (kk, j))
    tile = pl.BlockSpec((tm, tn), lambda i, j, kk: (i, j))
    out_specs = [tile] * n_out
    out_shape = [jax.ShapeDtypeStruct((m, n), d) for d in out_dtypes]
    if place is not None:
        shape, block_fn, index_fn = place
        out_specs = [pl.BlockSpec(block_fn(tm, tn), lambda i, j, kk: index_fn(i, j, tm, tn))]
        out_shape = [jax.ShapeDtypeStruct(shape, out_dtypes[0])]
    return pl.pallas_call(
        kern, name=name, grid=(m // tm, n // tn, nk),
        in_specs=[a_spec, b_spec] + [tile] * n_ex + [HBM] * n_into,
        out_specs=out_specs, out_shape=out_shape,
        input_output_aliases=({2 + n_ex: 0} if n_into else {}),
        scratch_shapes=[pltpu.VMEM((tm, tn), F32)],
        compiler_params=pltpu.CompilerParams(
            dimension_semantics=("parallel", "parallel", "arbitrary"), vmem_limit_bytes=VMEM_LIMIT),
    )(a, b, *extras, *([into] if n_into else []))


ROW_TILE = 256


def _rms_fwd(name, x, w):
    s, d = x.shape
    t = _pick_tile(s, (ROW_TILE,))

    def body(i, v, c):
        return [_rms(v[0], v[1])], [], []

    return _seq_call(name, body, steps=s // t, ins=[_rows(x, t), _whole(w)], outs=[_row_out(s, d, BF16, t)])[0]


def _resid_rms_fwd(name, x, y, w):
    s, d = x.shape
    t = _pick_tile(s, (ROW_TILE,))

    def body(i, v, c):
        return [v[0] + _rms(v[1], v[2])], [], []

    return _seq_call(name, body, steps=s // t, ins=[_rows(x, t), _rows(y, t), _whole(w)],
                     outs=[_row_out(s, d, F32, t)])[0]


def _rms_bwd(name, x, w, dh, dres):
    s, d = x.shape
    t = _pick_tile(s, (ROW_TILE,))

    def body(i, v, c):
        _, vjp = jax.vjp(_rms, v[0], v[1])
        dx, dw = vjp(v[2])
        return [dx + v[3]], [dw], []

    return _seq_call(name, body, steps=s // t, ins=[_rows(x, t), _whole(w), _rows(dh, t), _rows(dres, t)],
                     outs=[_row_out(s, d, F32, t)], accs=[((1, d), F32)])


def _resid_rms_bwd(name, y, w, dxo):
    s, d = y.shape
    t = _pick_tile(s, (ROW_TILE,))

    def body(i, v, c):
        _, vjp = jax.vjp(_rms, v[0], v[1])
        dy, dw = vjp(v[2])
        return [dy], [dw], []

    return _seq_call(name, body, steps=s // t, ins=[_rows(y, t), _whole(w), _rows(dxo, t)],
                     outs=[_row_out(s, d, BF16, t)], accs=[((1, d), F32)])


def _loss_head(name, y, target):
    s, d = y.shape
    t = _pick_tile(s, (ROW_TILE,))

    def body(i, v, c):
        err = v[0] - v[1]
        part = 0.5 * jnp.sum(jnp.mean(err * err, axis=-1, keepdims=True))
        return [err * (1.0 / d)], [jnp.full((1, LANES), part, F32)], []

    return _seq_call(name, body, steps=s // t, ins=[_rows(y, t), _rows(target, t)],
                     outs=[_row_out(s, d, F32, t)], accs=[((1, LANES), F32)])


SEG = {n: i for i, n in enumerate(
    ["a_q", "a_f", "a_i", "a_g", "b_q", "b_k", "b_v", "b_z", "c_u", "c_v", "d_a", "d_gate"])}
AB_COL = 12 * GROUP_WIDTH // LANES


def _seg(proj, name, tile, shift=0):
    return _rows(proj, tile, col=SEG[name], width=GROUP_WIDTH, shift=shift)


def _state_block():
    return (1, N_HEADS * HEAD_DIM, HEAD_DIM), lambda i, *pre: (i, 0, 0)


def _split_states(blk):
    return [blk[0, HEAD_DIM * h:HEAD_DIM * (h + 1), :] for h in range(N_HEADS)]


STATE_CARRIES = [((HEAD_DIM, HEAD_DIM), F32)] * N_HEADS


def _hgrn_fwd(layer, proj, lbp, nw):
    s = proj.shape[0]
    n = s // HGRN_TILE
    sb, sf = _state_block()

    def body(i, v, st):
        o, new = _hgrn_tile(layer, v[0], v[1], v[2], v[3], v[4], v[5], st)
        return [o, jnp.concatenate(st, axis=0)[None]], [], new

    return _seq_call(
        f"hgrn_fwd{layer}", body, steps=n,
        ins=[_whole(lbp), _whole(nw)] + [_seg(proj, k, HGRN_TILE) for k in ("a_q", "a_f", "a_i", "a_g")],
        outs=[_row_out(s, GROUP_WIDTH, BF16, HGRN_TILE), ((n, N_HEADS * HEAD_DIM, HEAD_DIM), F32, sb, sf)],
        carries=STATE_CARRIES)


def _hgrn_bwd(layer, proj, lbp, nw, states, dmix):
    s = proj.shape[0]
    n = s // HGRN_TILE
    sb, sf = _state_block()

    def body(i, v, dst):
        st = _split_states(v[6])

        def f(lbp_, nw_, aq, af, ai, ag, *st_):
            return _hgrn_tile(layer, lbp_, nw_, aq, af, ai, ag, list(st_))

        _, vjp = jax.vjp(f, v[0], v[1], v[2], v[3], v[4], v[5], *st)
        g = vjp((v[7], list(dst)))
        return [jnp.concatenate(g[2:6], axis=1)], [g[0], g[1]], list(g[6:])

    return _seq_call(
        f"hgrn_bwd{layer}", body, steps=n, reverse=True,
        ins=[_whole(lbp), _whole(nw)] + [_seg(proj, k, HGRN_TILE) for k in ("a_q", "a_f", "a_i", "a_g")]
        + [(states, sb, sf), _rows(dmix, HGRN_TILE, col=0, width=GROUP_WIDTH)],
        outs=[_row_out(s, 4 * GROUP_WIDTH, BF16, HGRN_TILE)],
        accs=[(lbp.shape, F32), (nw.shape, F32)], carries=STATE_CARRIES)


def _gdn_ins(proj, cw, alog, dtb, nw):
    return ([_whole(cw), _whole(alog), _whole(dtb), _whole(nw)]
            + [_seg(proj, k, GDN_TILE, shift=-1) for k in ("b_q", "b_k", "b_v")]
            + [_seg(proj, k, GDN_TILE) for k in ("b_q", "b_k", "b_v", "b_z")]
            + [_rows(proj, GDN_TILE, col=AB_COL, width=LANES)])


def _mask_prev(i, vals):
    keep = (i > 0).astype(F32)
    return [p * keep for p in vals]


def _gdn_fwd(layer, proj, cw, alog, dtb, nw):
    s = proj.shape[0]
    n = s // GDN_TILE
    sb, sf = _state_block()

    def body(i, v, st):
        prev = _mask_prev(i, v[4:7])
        o, new = _gdn_tile(v[0], v[1], v[2], v[3], *prev, *v[7:12], st)
        return [o, jnp.concatenate(st, axis=0)[None]], [], new

    return _seq_call(
        f"gdn_fwd{layer}", body, steps=n, ins=_gdn_ins(proj, cw, alog, dtb, nw),
        outs=[_row_out(s, GROUP_WIDTH, BF16, GDN_TILE), ((n, N_HEADS * HEAD_DIM, HEAD_DIM), F32, sb, sf)],
        carries=STATE_CARRIES)


def _gdn_bwd(layer, proj, cw, alog, dtb, nw, states, dmix):
    s = proj.shape[0]
    n = s // GDN_TILE
    sb, sf = _state_block()

    def body(i, v, car):
        dst, dprev = car[:N_HEADS], car[N_HEADS:]
        prev = _mask_prev(i, v[4:7])
        st = _split_states(v[12])

        def f(cw_, alog_, dtb_, nw_, pq, pk, pv, cq, ck, cv, bz, ab, *st_):
            return _gdn_tile(cw_, alog_, dtb_, nw_, pq, pk, pv, cq, ck, cv, bz, ab, list(st_))

        _, vjp = jax.vjp(f, v[0], v[1], v[2], v[3], *prev, *v[7:12], *st)
        g = vjp((v[13], list(dst)))
        dcur = [g[7] + dprev[0], g[8] + dprev[1], g[9] + dprev[2], g[10]]
        return ([jnp.concatenate(dcur, axis=1), g[11]], list(g[0:4]), list(g[12:]) + list(g[4:7]))

    return _seq_call(
        f"gdn_bwd{layer}", body, steps=n, reverse=True,
        ins=_gdn_ins(proj, cw, alog, dtb, nw) + [(states, sb, sf), _rows(dmix, GDN_TILE, col=1, width=GROUP_WIDTH)],
        outs=[_row_out(s, 4 * GROUP_WIDTH, BF16, GDN_TILE), _row_out(s, LANES, BF16, GDN_TILE)],
        accs=[(cw.shape, F32), (alog.shape, F32), (dtb.shape, F32), (nw.shape, F32)],
        carries=STATE_CARRIES + [((GDN_TILE, GROUP_WIDTH), F32)] * 3)


def _gmlp_fwd(layer, proj, ln_w, ln_b, ws, bs_cols):
    s = proj.shape[0]

    def body(i, v, c):
        return [_gmlp_tile(*v)], [], []

    return _seq_call(
        f"gmlp_fwd{layer}", body, steps=s // MIX_CHUNK,
        ins=[_whole(ln_w), _whole(ln_b), _whole(ws), _whole(bs_cols),
             _seg(proj, "c_u", MIX_CHUNK), _seg(proj, "c_v", MIX_CHUNK)],
        outs=[_row_out(s, GROUP_WIDTH, BF16, MIX_CHUNK)])[0]


def _gmlp_bwd(layer, proj, ln_w, ln_b, ws, bs_cols, dmix):
    s = proj.shape[0]

    def body(i, v, c):
        _, vjp = jax.vjp(_gmlp_tile, *v[:6])
        g = vjp(v[6])
        return [jnp.concatenate(g[4:6], axis=1)], list(g[0:4]), []

    return _seq_call(
        f"gmlp_bwd{layer}", body, steps=s // MIX_CHUNK,
        ins=[_whole(ln_w), _whole(ln_b), _whole(ws), _whole(bs_cols),
             _seg(proj, "c_u", MIX_CHUNK), _seg(proj, "c_v", MIX_CHUNK),
             _rows(dmix, MIX_CHUNK, col=2, width=GROUP_WIDTH)],
        outs=[_row_out(s, 2 * GROUP_WIDTH, BF16, MIX_CHUNK)],
        accs=[(ln_w.shape, F32), (ln_b.shape, F32), (ws.shape, F32), (bs_cols.shape, F32)])


def _conformer_ins(proj, dw_w, dw_b, ln_w, ln_b):
    return ([_whole(dw_w), _whole(dw_b), _whole(ln_w), _whole(ln_b)]
            + [_seg(proj, k, CONV_TILE, shift=-1) for k in ("d_a", "d_gate")]
            + [_seg(proj, k, CONV_TILE) for k in ("d_a", "d_gate")])


def _conformer_fwd(layer, proj, dw_w, dw_b, ln_w, ln_b):
    s = proj.shape[0]

    def body(i, v, c):
        prev = _mask_prev(i, v[4:6])
        return [_conformer_tile(v[0], v[1], v[2], v[3], *prev, v[6], v[7])], [], []

    return _seq_call(
        f"conformer_fwd{layer}", body, steps=s // CONV_TILE, ins=_conformer_ins(proj, dw_w, dw_b, ln_w, ln_b),
        outs=[_row_out(s, GROUP_WIDTH, BF16, CONV_TILE)])[0]


def _conformer_bwd(layer, proj, dw_w, dw_b, ln_w, ln_b, dmix):
    s = proj.shape[0]

    def body(i, v, dprev):
        prev = _mask_prev(i, v[4:6])
        _, vjp = jax.vjp(_conformer_tile, v[0], v[1], v[2], v[3], *prev, v[6], v[7])
        g = vjp(v[8])
        return [jnp.concatenate([g[6] + dprev[0], g[7] + dprev[1]], axis=1)], list(g[0:4]), [g[4], g[5]]

    return _seq_call(
        f"conformer_bwd{layer}", body, steps=s // CONV_TILE, reverse=True,
        ins=_conformer_ins(proj, dw_w, dw_b, ln_w, ln_b) + [_rows(dmix, CONV_TILE, col=3, width=GROUP_WIDTH)],
        outs=[_row_out(s, 2 * GROUP_WIDTH, BF16, CONV_TILE)],
        accs=[(dw_w.shape, F32), (dw_b.shape, F32), (ln_w.shape, F32), (ln_b.shape, F32)],
        carries=[((CONV_TILE, GROUP_WIDTH), F32)] * 2)


SMALL = ["lower_bounds", "norm_mix_pre", "norm_mix_post", "norm_ff_pre", "norm_ff_post", "hgrn_norm_w",
         "gdn_conv_w", "gdn_a_log", "gdn_dt_bias", "gdn_norm_w", "gmlp_ln_w", "gmlp_ln_b", "gmlp_w_s",
         "gmlp_b_s", "conv_dw_w", "conv_dw_b", "conv_ln_w", "conv_ln_b"]


def _gate_row(v):
    return jnp.pad(v.reshape(1, N_HEADS), ((0, 0), (N_HEADS, LANES - 2 * N_HEADS)))


def _cut_rows(layer, rows, cols):
    return ((DEPTH, N_CHIPS, rows, cols), lambda tm, tn: (None, None, tm, tn),
            lambda i, j, tm, tn: (layer, i // (rows // tm), i % (rows // tm), j))


def _cut_cols(layer, rows, cols):
    return ((DEPTH, N_CHIPS, rows, cols), lambda tm, tn: (None, None, tm, tn),
            lambda i, j, tm, tn: (layer, j // (cols // tn), i, j % (cols // tn)))


def _relu2(acc):
    r = jnp.maximum(acc, 0.0)
    return acc, r * r


def _relu2_bwd(acc, u):
    return (2.0 * jnp.maximum(u, 0.0) * acc,)


def _local_step(x, target, sp, wts):
    row = lambda v: v.reshape(1, -1)
    saved = []
    w_in, w_out, w_ff1, w_ff2 = [], [], [], []
    for l in range(DEPTH):
        par = dict(
            lbp=sp["lower_bounds"], hn=row(sp["hgrn_norm_w"][l]), cw=sp["gdn_conv_w"][l],
            alog=_gate_row(sp["gdn_a_log"][l]), dtb=_gate_row(sp["gdn_dt_bias"][l]), gn=row(sp["gdn_norm_w"][l]),
            glw=row(sp["gmlp_ln_w"][l]), glb=row(sp["gmlp_ln_b"][l]), ws=sp["gmlp_w_s"][l],
            bsc=jnp.pad(sp["gmlp_b_s"][l].T, ((0, 0), (0, LANES - N_HEADS))),
            dww=sp["conv_dw_w"][l], dwb=row(sp["conv_dw_b"][l]), clw=row(sp["conv_ln_w"][l]),
            clb=row(sp["conv_ln_b"][l]), n1=row(sp["norm_mix_pre"][l]), n2=row(sp["norm_mix_post"][l]),
            n3=row(sp["norm_ff_pre"][l]), n4=row(sp["norm_ff_post"][l]))
        h = _rms_fwd(f"norm_mix_pre{l}", x, par["n1"])
        w_in.append(wts.get("w_in", l))
        proj = _matmul(f"in_proj{l}", h, w_in[l], "nn", [F32])[0]
        wts.mark(f"proj{l}", proj)
        o_a, st_a = _hgrn_fwd(l, proj, par["lbp"], par["hn"])
        o_b, st_b = _gdn_fwd(l, proj, par["cw"], par["alog"], par["dtb"], par["gn"])
        wts.mark(f"gdn{l}", o_b)
        o_c = _gmlp_fwd(l, proj, par["glw"], par["glb"], par["ws"], par["bsc"])
        o_d = _conformer_fwd(l, proj, par["dww"], par["dwb"], par["clw"], par["clb"])
        wts.mark(f"conformer{l}", o_d)
        mix = jnp.concatenate([o_a, o_b, o_c, o_d], axis=1)
        w_out.append(wts.get("w_out", l))
        y = _matmul(f"out_proj{l}", mix, w_out[l], "nn", [F32])[0]
        wts.mark(f"y{l}", y)
        x1 = _resid_rms_fwd(f"norm_mix_post{l}", x, y, par["n2"])
        h2 = _rms_fwd(f"norm_ff_pre{l}", x1, par["n3"])
        w_ff1.append(wts.get("w_ff1", l))
        u, act = _matmul(f"ff1_{l}", h2, w_ff1[l], "nn", [F32, BF16], epilogue=_relu2)
        wts.mark(f"u{l}", u)
        w_ff2.append(wts.get("w_ff2", l))
        y2 = _matmul(f"ff2_{l}", act, w_ff2[l], "nn", [F32])[0]
        x2 = _resid_rms_fwd(f"norm_ff_post{l}", x1, y2, par["n4"])
        saved.append(dict(par=par, x=x, h=h, proj=proj, st_a=st_a, st_b=st_b, mix=mix, y=y, x1=x1, h2=h2,
                          u=u, act=act, y2=y2))
        x = x2

    dx, loss_acc = _loss_head("loss_head", x, target)
    loss_part = loss_acc[0, 0]

    gs = {k: [None] * DEPTH for k in SMALL if k != "lower_bounds"}
    g_lb = jnp.zeros((DEPTH, GROUP_WIDTH), F32)
    g_in = [None] * DEPTH
    g_out = g_ff1 = g_ff2 = None
    for l in reversed(range(DEPTH)):
        sv = saved[l]
        par = sv["par"]
        dy2, dn4 = _resid_rms_bwd(f"norm_ff_post_bwd{l}", sv["y2"], par["n4"], dx)
        du = _matmul(f"ff2_dx{l}", dy2, w_ff2[l], "nt", [BF16], epilogue=_relu2_bwd, extras=(sv["u"],))[0]
        g_ff2 = _matmul(f"ff2_dw{l}", sv["act"], dy2, "tn", [F32], into=g_ff2,
                        place=_cut_rows(l, D_FF // N_CHIPS, D_MODEL))[0]
        g_ff1 = _matmul(f"ff1_dw{l}", sv["h2"], du, "tn", [F32], into=g_ff1,
                        place=_cut_cols(l, D_MODEL, D_FF // N_CHIPS))[0]
        dh2 = _matmul(f"ff1_dx{l}", du, w_ff1[l], "nt", [F32])[0]
        dx1, dn3 = _rms_bwd(f"norm_ff_pre_bwd{l}", sv["x1"], par["n3"], dh2, dx)
        dy, dn2 = _resid_rms_bwd(f"norm_mix_post_bwd{l}", sv["y"], par["n2"], dx1)
        dmix = _matmul(f"out_proj_dx{l}", dy, w_out[l], "nt", [F32])[0]
        g_out = _matmul(f"out_proj_dw{l}", sv["mix"], dy, "tn", [F32], into=g_out, max_tm=GROUP_WIDTH,
                        place=_cut_rows(l, GROUP_WIDTH, D_MODEL))[0]
        proj = sv["proj"]
        dp_a, dlb, dhn = _hgrn_bwd(l, proj, par["lbp"], par["hn"], sv["st_a"], dmix)
        dp_b, dp_ab, dcw, dalog, ddtb, dgn = _gdn_bwd(
            l, proj, par["cw"], par["alog"], par["dtb"], par["gn"], sv["st_b"], dmix)
        dp_c, dglw, dglb, dws, dbsc = _gmlp_bwd(l, proj, par["glw"], par["glb"], par["ws"], par["bsc"], dmix)
        dp_d, ddww, ddwb, dclw, dclb = _conformer_bwd(
            l, proj, par["dww"], par["dwb"], par["clw"], par["clb"], dmix)
        dproj = jnp.concatenate([dp_a, dp_b, dp_c, dp_d, dp_ab], axis=1)
        g_in[l] = _matmul(f"in_proj_dw{l}", sv["h"], dproj, "tn", [F32])[0]
        dh = _matmul(f"in_proj_dx{l}", dproj, w_in[l], "nt", [F32])[0]
        dx, dn1 = _rms_bwd(f"norm_mix_pre_bwd{l}", sv["x"], par["n1"], dh, dx1)
        g_lb = g_lb + dlb
        for k, v in dict(
                norm_mix_pre=dn1[0], norm_mix_post=dn2[0], norm_ff_pre=dn3[0], norm_ff_post=dn4[0],
                hgrn_norm_w=dhn[0], gdn_conv_w=dcw, gdn_a_log=dalog[0, N_HEADS:2 * N_HEADS],
                gdn_dt_bias=ddtb[0, N_HEADS:2 * N_HEADS], gdn_norm_w=dgn[0], gmlp_ln_w=dglw[0],
                gmlp_ln_b=dglb[0], gmlp_w_s=dws, gmlp_b_s=dbsc[:, :N_HEADS].T, conv_dw_w=ddww,
                conv_dw_b=ddwb[0], conv_ln_w=dclw[0], conv_ln_b=dclb[0]).items():
            gs[k][l] = v
    small_grads = {k: jnp.stack(v) for k, v in gs.items()}
    small_grads["lower_bounds"] = g_lb
    return loss_part, dx, small_grads, g_in, g_out, g_ff1, g_ff2


HBM = pl.BlockSpec(memory_space=pl.ANY)


def _place():
    return lax.axis_index("x"), lax.axis_index("y"), lax.axis_index("c")


def _other_chips(x, y):
    chips = [(1 - x, y), (x, 1 - y), (1 - x, 1 - y)]
    return [(px, py, 2 * px + py) for px, py in chips]


def _gather_weights(name, blk):
    def body(blk_ref, out_ref, ici_send, ici_recv, d2d_send, d2d_recv):
        x, y, c = _place()
        mine = 2 * x + y

        def ici(j, chip_of_slab, to):
            return pltpu.make_async_remote_copy(
                src_ref=blk_ref.at[c], dst_ref=out_ref.at[chip_of_slab, c], send_sem=ici_send.at[j],
                recv_sem=ici_recv.at[j], device_id=to, device_id_type=MESH)

        def d2d(j, chip_of_slab, layer):
            return pltpu.make_async_remote_copy(
                src_ref=out_ref.at[chip_of_slab, layer], dst_ref=out_ref.at[chip_of_slab, layer],
                send_sem=d2d_send.at[j], recv_sem=d2d_recv.at[j], device_id=(x, y, 1 - c), device_id_type=MESH)

        peers = _other_chips(x, y)
        sends = [ici(j, mine, (px, py, c)) for j, (px, py, _) in enumerate(peers)]
        for cp in sends:
            cp.start()
        passed = []
        for j, (px, py, k) in enumerate(peers):
            ici(j, k, (px, py, c)).wait_recv()
            fwd = d2d(j, k, c)
            fwd.start()
            passed.append(fwd)
        for j, (px, py, k) in enumerate(peers):
            d2d(j, k, 1 - c).wait_recv()
        for cp in sends + passed:
            cp.wait_send()

    return pl.pallas_call(
        body, name=name, out_shape=jax.ShapeDtypeStruct((N_CHIPS,) + blk.shape, blk.dtype),
        in_specs=[HBM], out_specs=HBM, scratch_shapes=[pltpu.SemaphoreType.DMA((3,))] * 4,
    )(blk)


SEM = pl.BlockSpec(memory_space=pltpu.SEMAPHORE)
IN_HBM = pl.BlockSpec(memory_space=pltpu.HBM)
EFFECT = pltpu.SideEffectType.DATAFLOW_SIDE_EFFECTING


def _copies_start(name, plan, bufs):
    nb = len(bufs)

    def body(*refs):
        token = refs[-1]
        for started, _ in plan(refs[:nb], refs[nb], refs[nb + 1]):
            started.start()
        token[...] = jnp.zeros(token.shape, token.dtype)

    out = pl.pallas_call(
        body, name=name,
        out_shape=(pltpu.SemaphoreType.DMA((3,)), pltpu.SemaphoreType.DMA((3,)))
        + tuple(pltpu.HBM(b.shape, b.dtype) for b in bufs) + (jax.ShapeDtypeStruct((8, LANES), F32),),
        in_specs=(IN_HBM,) * nb,
        out_specs=(SEM, SEM) + (IN_HBM,) * nb + (pl.BlockSpec(memory_space=pltpu.VMEM),),
        input_output_aliases={i: 2 + i for i in range(nb)},
        compiler_params=pltpu.CompilerParams(has_side_effects=EFFECT),
    )(*[pltpu.with_memory_space_constraint(b, pltpu.HBM) for b in bufs])
    return out[0], out[1], list(out[2:2 + nb]), out[-1]


def _copies_wait(name, plan, send_sems, recv_sems, bufs, after):
    nb = len(bufs)

    def body(*refs):
        for started, arriving in plan(refs[:nb], refs[nb], refs[nb + 1]):
            started.wait_send()
            arriving.wait_recv()

    out = pl.pallas_call(
        body, name=name, out_shape=tuple(pltpu.HBM(b.shape, b.dtype) for b in bufs),
        in_specs=(IN_HBM,) * nb + (SEM, SEM, HBM), out_specs=(IN_HBM,) * nb,
        input_output_aliases={i: i for i in range(nb)},
        compiler_params=pltpu.CompilerParams(has_side_effects=EFFECT),
    )(*bufs, send_sems, recv_sems, after)
    return list(out)


def _ici_plan(bufs, send_sems, recv_sems):
    blk, land = bufs
    x, y, c = _place()
    mine = 2 * x + y
    plan = []
    for j, (px, py, k) in enumerate(_other_chips(x, y)):
        def copy(dst, j=j, to=(px, py, c)):
            return pltpu.make_async_remote_copy(
                src_ref=blk.at[c], dst_ref=dst, send_sem=send_sems.at[j], recv_sem=recv_sems.at[j],
                device_id=to, device_id_type=MESH)
        plan.append((copy(land.at[mine, c]), copy(land.at[k, c])))
    return plan


def _d2d_plan(bufs, send_sems, recv_sems):
    (land,) = bufs
    x, y, c = _place()
    plan = []
    for j, (_, _, k) in enumerate(_other_chips(x, y)):
        def copy(layer, j=j, k=k):
            return pltpu.make_async_remote_copy(
                src_ref=land.at[k, c], dst_ref=land.at[k, layer], send_sem=send_sems.at[j],
                recv_sem=recv_sems.at[j], device_id=(x, y, 1 - c), device_id_type=MESH)
        plan.append((copy(c), copy(1 - c)))
    return plan


class _GatheredWeights:
    STAGES = {"w_out": ("proj0", "conformer0"), "w_ff1": ("gdn0", "y0"), "w_ff2": ("u0", None)}

    def __init__(self, chip, shards):
        self.chip = chip
        self.blk, self.ici, self.d2d, self.full, self.mats = {}, {}, {}, {}, {}
        tokens = []
        for k in ("w_in", "w_out", "w_ff1", "w_ff2"):
            self.blk[k] = shards[k].astype(BF16)
            land = lax.empty((N_CHIPS,) + self.blk[k].shape, BF16)
            send, recv, bufs, token = _copies_start(f"gather_{k}_ici", _ici_plan, [self.blk[k], land])
            self.ici[k] = (send, recv, bufs)
            tokens.append(token)
        self.token = functools.reduce(lambda a, b: a + b, tokens)
        self._hand_over("w_in", self.token)
        self._finish("w_in", self.token)

    def _hand_over(self, k, after):
        send, recv, bufs = self.ici.pop(k)
        _, land = _copies_wait(f"gather_{k}_ici_done", _ici_plan, send, recv, bufs, after)
        send, recv, bufs, _ = _copies_start(f"gather_{k}_d2d", _d2d_plan, [land])
        self.d2d[k] = (send, recv, bufs)

    def _finish(self, k, after):
        send, recv, bufs = self.d2d.pop(k)
        (land,) = _copies_wait(f"gather_{k}_d2d_done", _d2d_plan, send, recv, bufs, after)
        self.full[k] = lax.dynamic_update_slice(land, self.blk[k][None], (self.chip, 0, 0, 0))

    def mark(self, tag, value):
        for k, (first, second) in self.STAGES.items():
            if tag == first:
                self._hand_over(k, value)
                if second is None:
                    self._finish(k, value)
            elif tag == second:
                self._finish(k, value)

    def get(self, kind, l):
        if (kind, l) not in self.mats:
            a = self.full[kind]
            if kind == "w_in":
                m = _reorder_in(jnp.concatenate([a[k, l] for k in range(N_CHIPS)], axis=-1))
            elif kind == "w_ff1":
                m = jnp.concatenate([a[k, l] for k in range(N_CHIPS)], axis=-1)
            else:
                m = a[:, l].reshape(-1, D_MODEL)
            self.mats[kind, l] = m
        return self.mats[kind, l]


def _pair_exchange(name, g, by_layer):
    def body(g_ref, got_ref, send_sem, recv_sem):
        x, y, c = _place()
        cp = pltpu.make_async_remote_copy(
            src_ref=g_ref.at[1 - c] if by_layer else g_ref, dst_ref=got_ref, send_sem=send_sem,
            recv_sem=recv_sem, device_id=(x, y, 1 - c), device_id_type=MESH)
        cp.start()
        cp.wait()

    return pl.pallas_call(
        body, name=name, out_shape=jax.ShapeDtypeStruct(g.shape[1:] if by_layer else g.shape, g.dtype),
        in_specs=[HBM], out_specs=HBM, scratch_shapes=[pltpu.SemaphoreType.DMA, pltpu.SemaphoreType.DMA],
    )(g)


def _scatter_partials(name, p):
    def body(p_ref, out_ref, send_sems, recv_sems):
        x, y, c = _place()

        def cp(j, to):
            return pltpu.make_async_remote_copy(
                src_ref=p_ref.at[j], dst_ref=out_ref.at[j], send_sem=send_sems.at[j],
                recv_sem=recv_sems.at[j], device_id=to, device_id_type=MESH)

        copies = [cp(j, (px, py, c)) for j, (px, py, _) in enumerate(_other_chips(x, y))]
        for s in copies:
            s.start()
        for s in copies:
            s.wait_recv()
        for s in copies:
            s.wait_send()

    return pl.pallas_call(
        body, name=name, out_shape=jax.ShapeDtypeStruct(p.shape, p.dtype), in_specs=[HBM], out_specs=HBM,
        scratch_shapes=[pltpu.SemaphoreType.DMA((3,)), pltpu.SemaphoreType.DMA((3,))],
    )(p)


def _gather_small(name, pack):
    def body(p_ref, out_ref, send_sems, recv_sems, local_sem):
        x, y, c = _place()
        me = 4 * x + 2 * y + c
        local = pltpu.make_async_copy(p_ref, out_ref.at[me], local_sem)
        local.start()
        flips = [(fx, fy, fc) for fx in (0, 1) for fy in (0, 1) for fc in (0, 1)][1:]
        peers = [((1 - x) if fx else x, (1 - y) if fy else y, (1 - c) if fc else c) for fx, fy, fc in flips]

        def cp(j, slot, to):
            return pltpu.make_async_remote_copy(
                src_ref=p_ref, dst_ref=out_ref.at[slot], send_sem=send_sems.at[j], recv_sem=recv_sems.at[j],
                device_id=to, device_id_type=MESH)

        sends = [cp(j, me, to) for j, to in enumerate(peers)]
        for s in sends:
            s.start()
        for j, (px, py, pc) in enumerate(peers):
            cp(j, 4 * px + 2 * py + pc, (px, py, pc)).wait_recv()
        for s in sends:
            s.wait_send()
        local.wait()

    return pl.pallas_call(
        body, name=name, out_shape=jax.ShapeDtypeStruct((N_DEV,) + pack.shape, pack.dtype), in_specs=[HBM],
        out_specs=HBM,
        scratch_shapes=[pltpu.SemaphoreType.DMA((7,)), pltpu.SemaphoreType.DMA((7,)), pltpu.SemaphoreType.DMA],
    )(pack)


SLAB_ROWS = 256


I_CORE, I_CHIP, I_PEER = 0, 1, 2


def _peer_chip(pre, j):
    return jnp.where(j == 0, pre[I_PEER][0], jnp.where(j == 1, pre[I_PEER + 1][0], pre[I_PEER + 2][0]))


def _pair_sum(name, g, got, where):
    _, _, r, c = g.shape
    t = _pick_tile(r, (SLAB_ROWS, 128, 64, 8))
    per = r // t

    def body(i, v, car, pre):
        return [v[0] + v[1]], [], []

    return _seq_call(
        name, body, steps=3 * per, prefetch=where,
        ins=[(g, (None, None, t, c), lambda i, pre: (pre[I_CORE][0], _peer_chip(pre, i // per), i % per, 0)),
             (got, (None, t, c), lambda i, pre: (_peer_chip(pre, i // per), i % per, 0))],
        outs=[((3, r, c), BF16, (None, t, c), lambda i, pre: (i // per, i % per, 0))])[0]


def _chip_sum(name, g, got, rcv, where):
    _, _, r, c = g.shape
    t = _pick_tile(r, (SLAB_ROWS, 128, 64, 8))

    def body(i, v, car, pre):
        acc = v[0] + v[1]
        for part in v[2:]:
            acc = acc + part.astype(F32)
        return [acc], [], []

    return _seq_call(
        name, body, steps=r // t, prefetch=where,
        ins=[(g, (None, None, t, c), lambda i, pre: (pre[I_CORE][0], pre[I_CHIP][0], i, 0)),
             (got, (None, t, c), lambda i, pre: (pre[I_CHIP][0], i, 0))]
        + [(rcv, (None, t, c), (lambda i, pre, _j=j: (_j, i, 0))) for j in range(3)],
        outs=[((r, c), F32, (t, c), lambda i, pre: (i, 0))])[0]


def _adamw_math(w, g, m, v):
    m = ADAM_B1 * m + (1.0 - ADAM_B1) * g
    v = ADAM_B2 * v + (1.0 - ADAM_B2) * (g * g)
    m_hat = m / (1.0 - ADAM_B1 ** ADAM_STEP)
    v_hat = v / (1.0 - ADAM_B2 ** ADAM_STEP)
    delta = -ADAM_LR * (m_hat / (jnp.sqrt(v_hat) + ADAM_EPS) + ADAM_WD * w)
    return delta, m, v


def _adamw(name, w, g, m, v):
    n, r, c = w.shape
    t = _pick_tile(r, (SLAB_ROWS, 128, 64, 8))
    per = r // t

    def body(i, vals, car):
        return list(_adamw_math(*vals)), [], []

    blk = lambda a: (a, (None, t, c), lambda i: (i // per, i % per, 0))
    out = ((n, r, c), F32, (None, t, c), lambda i: (i // per, i % per, 0))
    return _seq_call(name, body, steps=n * per, ins=[blk(w), blk(g), blk(m), blk(v)], outs=[out] * 3)


def _adamw_cut(name, w, mine, sib, m, v, where):
    n, r, c = w.shape
    t = _pick_tile(r, (SLAB_ROWS, 128, 64, 8))
    per = r // t

    def body(i, vals, car, pre):
        g = jnp.where(i // per == pre[I_CORE][0], vals[1], vals[2])
        return [g] + list(_adamw_math(vals[0], g, vals[3], vals[4])), [], []

    by_layer = lambda a: (a, (None, t, c), lambda i, pre: (i // per, i % per, 0))
    flat = lambda a: (a, (t, c), lambda i, pre: (i % per, 0))
    out = ((n, r, c), F32, (None, t, c), lambda i, pre: (i // per, i % per, 0))
    return _seq_call(name, body, steps=n * per, prefetch=where,
                     ins=[by_layer(w), flat(mine), flat(sib), by_layer(m), by_layer(v)], outs=[out] * 4)


def _ordered_sum(name, packs):
    n, r, c = packs.shape

    def body(i, v, car):
        acc = v[0][0]
        for k in range(1, n):
            acc = acc + v[0][k]
        return [acc], [], []

    return _seq_call(name, body, steps=1, ins=[_whole(packs)], outs=[((r, c), F32, (r, c), lambda i: (0, 0))])[0]


def _pack(arrays):
    flat = []
    for a in arrays:
        a = a.reshape(-1).astype(F32)
        pad = (-a.shape[0]) % LANES
        flat.append(jnp.pad(a, (0, pad)) if pad else a)
    v = jnp.concatenate(flat)
    pad = (-v.shape[0]) % (64 * LANES)
    if pad:
        v = jnp.pad(v, (0, pad))
    return v.reshape(-1, LANES)


def _unpack(pack, shapes):
    v = pack.reshape(-1)
    out, off = [], 0
    for s in shapes:
        n = math.prod(s)
        out.append(v[off:off + n].reshape(s))
        off += n + ((-n) % LANES)
    return out


def _reorder_in(full):
    g0 = 8 * GROUP_WIDTH
    pad = jnp.zeros(full.shape[:-1] + (LANES - 2 * N_HEADS,), full.dtype)
    return jnp.concatenate([full[..., :g0], full[..., g0 + 2 * N_HEADS:], full[..., g0:g0 + 2 * N_HEADS], pad], axis=-1)


def _restore_in(padded):
    g0 = 8 * GROUP_WIDTH
    wide = 12 * GROUP_WIDTH
    return jnp.concatenate([padded[..., :g0], padded[..., wide:wide + 2 * N_HEADS], padded[..., g0:wide]], axis=-1)


def _reduce_big(tag, g, where):
    got = _pair_exchange(f"pair_exchange_{tag}", g, True)
    pair = _pair_sum(f"pair_sum_{tag}", g, got, where)
    rcv = _scatter_partials(f"scatter_partials_{tag}", pair)
    mine = _chip_sum(f"chip_sum_{tag}", g, got, rcv, where)
    return mine, _pair_exchange(f"pair_send_{tag}", mine, False)


def kernel(x, lower_bounds, norm_mix_pre, norm_mix_post, norm_ff_pre, norm_ff_post, w_in, w_out, hgrn_norm_w, gdn_conv_w, gdn_a_log, gdn_dt_bias, gdn_norm_w, gmlp_ln_w, gmlp_ln_b, gmlp_w_s, gmlp_b_s, conv_dw_w, conv_dw_b, conv_ln_w, conv_ln_b, w_ff1, w_ff2, loss_target, m_lower_bounds, m_norm_mix_pre, m_norm_mix_post, m_norm_ff_pre, m_norm_ff_post, m_w_in, m_w_out, m_hgrn_norm_w, m_gdn_conv_w, m_gdn_a_log, m_gdn_dt_bias, m_gdn_norm_w, m_gmlp_ln_w, m_gmlp_ln_b, m_gmlp_w_s, m_gmlp_b_s, m_conv_dw_w, m_conv_dw_b, m_conv_ln_w, m_conv_ln_b, m_w_ff1, m_w_ff2, v_lower_bounds, v_norm_mix_pre, v_norm_mix_post, v_norm_ff_pre, v_norm_ff_post, v_w_in, v_w_out, v_hgrn_norm_w, v_gdn_conv_w, v_gdn_a_log, v_gdn_dt_bias, v_gdn_norm_w, v_gmlp_ln_w, v_gmlp_ln_b, v_gmlp_w_s, v_gmlp_b_s, v_conv_dw_w, v_conv_dw_b, v_conv_ln_w, v_conv_ln_b, v_w_ff1, v_w_ff2):
    args = dict(locals())
    chip = 2 * lax.axis_index("x") + lax.axis_index("y")
    where = tuple(jnp.asarray(v, jnp.int32).reshape(1)
                  for v in (lax.axis_index("c"), chip, chip ^ 2, chip ^ 1, chip ^ 3))

    wts = _GatheredWeights(chip, dict(w_in=w_in, w_out=w_out, w_ff1=w_ff1, w_ff2=w_ff2))
    cut_shapes = [gdn_conv_w.shape, conv_dw_w.shape]
    cuts = _gather_small("gather_cut_small", _pack([gdn_conv_w, conv_dw_w]))
    cut_parts = [_unpack(cuts[2 * k], cut_shapes) for k in range(N_CHIPS)]
    sp = {k: args[k] for k in SMALL}
    sp["gdn_conv_w"] = jnp.concatenate([p[0] for p in cut_parts], axis=-1)
    sp["conv_dw_w"] = jnp.concatenate([p[1] for p in cut_parts], axis=-1)

    loss_part, grad_x, small_g, g_in, g_out, g_ff1, g_ff2 = _local_step(x[0], loss_target[0], sp, wts)

    gi = jnp.stack([jnp.stack(jnp.split(_restore_in(g_in[l]), N_CHIPS, axis=-1)) for l in range(DEPTH)])
    big_g = dict(w_in=_reduce_big("w_in", gi, where), w_out=_reduce_big("w_out", g_out, where),
                 w_ff1=_reduce_big("w_ff1", g_ff1, where), w_ff2=_reduce_big("w_ff2", g_ff2, where))

    names = SMALL + ["loss"]
    small_g["loss"] = loss_part.reshape(1)
    shapes = [small_g[k].shape for k in names]
    total = _ordered_sum("sum_small", _gather_small("gather_small", _pack([small_g[k] for k in names])))
    summed = dict(zip(names, _unpack(total, shapes)))
    loss = summed.pop("loss")[0]
    for k, width in (("gdn_conv_w", gdn_conv_w.shape[-1]), ("conv_dw_w", conv_dw_w.shape[-1])):
        summed[k] = lax.dynamic_slice_in_dim(summed[k], chip * width, width, axis=-1)

    grads, deltas, new_m, new_v = {}, {}, {}, {}
    for k in ("w_in", "w_out", "w_ff1", "w_ff2"):
        grads[k], deltas[k], new_m[k], new_v[k] = _adamw_cut(
            f"adamw_{k}", args[k], *big_g[k], args["m_" + k], args["v_" + k], where)
    local_shapes = [args[k].shape for k in SMALL]
    packs = [_pack([src[k] for k in SMALL]) for src in (
        {k: args[k] for k in SMALL}, summed, {k: args["m_" + k] for k in SMALL}, {k: args["v_" + k] for k in SMALL})]
    d_s, m_s, v_s = _adamw("adamw_small", *[p[None] for p in packs])
    for k, d, mm, vv in zip(SMALL, _unpack(d_s[0], local_shapes), _unpack(m_s[0], local_shapes),
                            _unpack(v_s[0], local_shapes)):
        grads[k], deltas[k], new_m[k], new_v[k] = summed[k], d, mm, vv

    order = ["lower_bounds", "norm_mix_pre", "norm_mix_post", "norm_ff_pre", "norm_ff_post", "w_in", "w_out",
             "hgrn_norm_w", "gdn_conv_w", "gdn_a_log", "gdn_dt_bias", "gdn_norm_w", "gmlp_ln_w", "gmlp_ln_b",
             "gmlp_w_s", "gmlp_b_s", "conv_dw_w", "conv_dw_b", "conv_ln_w", "conv_ln_b", "w_ff1", "w_ff2"]
    return (loss, grad_x[None], *[grads[k] for k in order], *[deltas[k] for k in order],
            *[new_m[k] for k in order], *[new_v[k] for k in order])
```

```python
import functools
import math

import jax
import jax.numpy as jnp
from jax import lax
from jax.experimental import pallas as pl
from jax.experimental.pallas import tpu as pltpu

F32 = jnp.float32
BF16 = jnp.bfloat16

DEPTH = 2
D_MODEL = 2048
GROUP_WIDTH = 512
HEAD_DIM = 128
N_HEADS = 4
CHUNK = 64
SHORT_CONV = 4
MIX_CHUNK = 128
CONV_WIDTH = 31
D_FF = 4 * D_MODEL
D_IN_PROJ = 12 * GROUP_WIDTH + 2 * N_HEADS
EPS = 1e-6
NEG_BIG = -1e30
TINY = 1e-30
ADAM_LR = 0.001
ADAM_B1 = 0.9
ADAM_B2 = 0.999
ADAM_EPS = 1e-08
ADAM_WD = 0.01
ADAM_STEP = 10

LANES = 128
P_IN = 12 * GROUP_WIDTH + LANES
SUB_BLOCK = 16
HGRN_TILE = 128
GDN_TILE = 128
CONV_TILE = 128
CONV_HALO = 32
VMEM_LIMIT = 56 * 1024 * 1024
N_CHIPS = 4
N_DEV = 8
MESH = pl.DeviceIdType.MESH


_DIMS = {
    "nn": (((1,), (0,)), ((), ())),
    "nt": (((1,), (1,)), ((), ())),
    "tn": (((0,), (0,)), ((), ())),
}


def _split2(a):
    hi = a.astype(BF16)
    return hi, (a - hi.astype(F32)).astype(BF16)


def _raw_mm(a, b, mode, exact):
    dot = lambda p, q: lax.dot_general(p, q, _DIMS[mode], preferred_element_type=F32)
    if not exact:
        return dot(a.astype(BF16), b.astype(BF16))
    a_hi, a_lo = _split2(a)
    b_hi, b_lo = _split2(b)
    return dot(a_hi, b_hi) + (dot(a_hi, b_lo) + dot(a_lo, b_hi))


@functools.partial(jax.custom_vjp, nondiff_argnums=(2, 3))
def _mm(a, b, mode, exact):
    return _raw_mm(a, b, mode, exact)


def _mm_fwd(a, b, mode, exact):
    return _raw_mm(a, b, mode, exact), (a, b)


def _mm_bwd(mode, exact, res, g):
    a, b = res
    if mode == "nn":
        return _raw_mm(g, b, "nt", exact), _raw_mm(a, g, "tn", exact)
    if mode == "nt":
        return _raw_mm(g, b, "nn", exact), _raw_mm(g, a, "tn", exact)
    return _raw_mm(b, g, "nt", exact), _raw_mm(a, g, "nn", exact)


_mm.defvjp(_mm_fwd, _mm_bwd)


def _sig(x):
    return jax.nn.sigmoid(x)


def _silu(x):
    return x * jax.nn.sigmoid(x)


def _gelu(x):
    return 0.5 * x * (1.0 + lax.erf(x * (1.0 / math.sqrt(2.0))))


def _rms(x, w):
    return x * lax.rsqrt(jnp.mean(x * x, axis=-1, keepdims=True) + EPS) * w


def _ln(x, w, b):
    mu = jnp.mean(x, axis=-1, keepdims=True)
    xc = x - mu
    var = jnp.mean(xc * xc, axis=-1, keepdims=True)
    return xc * lax.rsqrt(var + EPS) * w + b


def _iota(shape, dim):
    return lax.broadcasted_iota(jnp.int32, shape, dim)


def _tri_mm(x, mode):
    n = x.shape[0]
    tri = (_iota((n, n), 0) >= _iota((n, n), 1)).astype(BF16)
    x1 = x.astype(BF16)
    r1 = x - x1.astype(F32)
    x2 = r1.astype(BF16)
    x3 = (r1 - x2.astype(F32)).astype(BF16)
    dot = lambda q: lax.dot_general(tri, q, _DIMS[mode], preferred_element_type=F32)
    return dot(x1) + (dot(x2) + dot(x3))


@jax.custom_vjp
def _cumsum_rows(x):
    return _tri_mm(x, "nn")


_cumsum_rows.defvjp(lambda x: (_tri_mm(x, "nn"), None), lambda _, g: (_tri_mm(g, "tn"),))


def _hgrn_head(q, k, v, b, st):
    n = q.shape[0]
    ii = _iota((n, 1), 0)
    zpad = jnp.zeros((SUB_BLOCK, HEAD_DIM), F32)
    k_ext = jnp.concatenate([zpad, k], axis=0)
    b_ext = jnp.concatenate([zpad, b], axis=0)
    v_ext = jnp.concatenate([zpad, v], axis=0)
    o = jnp.zeros((n, HEAD_DIM), F32)
    for d in range(SUB_BLOCK):
        ks = k_ext[SUB_BLOCK - d:SUB_BLOCK - d + n]
        bs = b_ext[SUB_BLOCK - d:SUB_BLOCK - d + n]
        vs = v_ext[SUB_BLOCK - d:SUB_BLOCK - d + n]
        e = jnp.exp(jnp.where((ii % SUB_BLOCK) >= d, b - bs, NEG_BIG))
        o = o + jnp.sum(q * ks * e, axis=-1, keepdims=True) * vs
    blocks = [o[0:SUB_BLOCK]]
    for blk in range(1, n // SUB_BLOCK):
        lo = SUB_BLOCK * blk
        r = b[lo - 1:lo]
        a_q = q[lo:lo + SUB_BLOCK] * jnp.exp(b[lo:lo + SUB_BLOCK] - r)
        b_k = jnp.where(ii < lo, k * jnp.exp(jnp.minimum(r - b, 0.0)), 0.0)
        sc = _mm(a_q, b_k, "nt", False)
        blocks.append(o[lo:lo + SUB_BLOCK] + _mm(sc, v, "nn", False))
    o = jnp.concatenate(blocks, axis=0)
    o = o + _mm(q * jnp.exp(b), st, "nt", False)
    b_end = b[n - 1:n]
    st_new = st * jnp.exp(b_end) + _mm(v, k * jnp.exp(b_end - b), "tn", False)
    return o, st_new


def _hgrn_chunk(layer, lbp, nw, aq, af, ai, ag, states):
    rows = [lbp[i:i + 1, :] for i in range(DEPTH)]
    mx = functools.reduce(jnp.maximum, rows)
    es = [jnp.exp(r - mx) for r in rows]
    den = functools.reduce(lambda p, s: p + s, es)
    soft = [e / den for e in es]
    lb = functools.reduce(lambda p, s: p + s, soft[:layer + 1]) - soft[0]
    f = lb + (1.0 - lb) * _sig(af)
    logf = jnp.log(jnp.maximum(f, TINY))
    k = (1.0 - lb) * _sig(-af)
    q = _silu(aq)
    b = _cumsum_rows(logf)
    outs, new_states = [], []
    for h in range(N_HEADS):
        sl = slice(HEAD_DIM * h, HEAD_DIM * (h + 1))
        o, st = _hgrn_head(q[:, sl], k[:, sl], ai[:, sl], b[:, sl], states[h])
        outs.append(_rms(o, nw) * _silu(ag[:, sl]))
        new_states.append(st)
    return jnp.concatenate(outs, axis=1), new_states


def _hgrn_tile(layer, lbp, nw, aq, af, ai, ag, states):
    outs = []
    for ci in range(aq.shape[0] // CHUNK):
        rs = slice(CHUNK * ci, CHUNK * (ci + 1))
        o, states = _hgrn_chunk(layer, lbp, nw, aq[rs], af[rs], ai[rs], ag[rs], states)
        outs.append(o)
    return jnp.concatenate(outs, axis=0), states


def _short_conv(prev, cur, w):
    n = cur.shape[0]
    ext = jnp.concatenate([prev[n - 8:n], cur], axis=0)
    y = jnp.zeros_like(cur)
    for t in range(SHORT_CONV):
        off = 8 - (SHORT_CONV - 1) + t
        y = y + w[t:t + 1, :] * ext[off:off + n]
    return _silu(y)


def _gdn_tile(cw, alog, dtb, nw, pq, pk, pv, cq, ck, cv, bz, ab, states):
    n = cq.shape[0]
    q_all = _short_conv(pq, cq, cw[:, 0:GROUP_WIDTH])
    k_all = _short_conv(pk, ck, cw[:, GROUP_WIDTH:2 * GROUP_WIDTH])
    v_all = _short_conv(pv, cv, cw[:, 2 * GROUP_WIDTH:3 * GROUP_WIDTH])
    beta_all = _sig(ab)
    g_all = -jnp.exp(alog) * jax.nn.softplus(ab + dtb)
    units = [(ci, h) for ci in range(n // CHUNK) for h in range(N_HEADS)]
    gc_all = [_cumsum_rows(g_all[CHUNK * ci:CHUNK * (ci + 1)]) for ci in range(n // CHUNK)]
    gc_t = [g.T for g in gc_all]
    ii = _iota((CHUNK, CHUNK), 0)
    jj = _iota((CHUNK, CHUNK), 1)
    eye = (ii == jj).astype(F32)

    def cut(a, ci, h):
        return a[CHUNK * ci:CHUNK * (ci + 1), HEAD_DIM * h:HEAD_DIM * (h + 1)]

    q = [cut(q_all, ci, h) for ci, h in units]
    k = [cut(k_all, ci, h) for ci, h in units]
    v = [cut(v_all, ci, h) for ci, h in units]
    q = [t * lax.rsqrt(jnp.sum(t * t, axis=-1, keepdims=True) + EPS) * (HEAD_DIM ** -0.5) for t in q]
    k = [t * lax.rsqrt(jnp.sum(t * t, axis=-1, keepdims=True) + EPS) for t in k]
    beta = [beta_all[CHUNK * ci:CHUNK * (ci + 1), h:h + 1] for ci, h in units]
    gc = [gc_all[ci][:, N_HEADS + h:N_HEADS + h + 1] for ci, h in units]
    gcr = [gc_t[ci][N_HEADS + h:N_HEADS + h + 1, :] for ci, h in units]
    gamma = [jnp.exp(jnp.where(ii >= jj, a - b, NEG_BIG)) for a, b in zip(gc, gcr)]
    kb = [a * b for a, b in zip(k, beta)]
    m = [jnp.where(ii > jj, _mm(a, b, "nt", False) * g, 0.0) for a, b, g in zip(kb, k, gamma)]
    inv = [eye - t for t in m]
    p = m
    for _ in range(max(1, int(math.ceil(math.log2(CHUNK))) - 1)):
        p = [_mm(t, t, "nn", True) for t in p]
        inv = [a + _mm(a, t, "nn", True) for a, t in zip(inv, p)]
    eg = [jnp.exp(t) for t in gc]
    u = [_mm(a, b * c, "nn", True) for a, b, c in zip(inv, v, beta)]
    w = [_mm(a, b * c, "nn", True) for a, b, c in zip(inv, kb, eg)]
    qk = [_mm(a, b, "nt", False) * g for a, b, g in zip(q, k, gamma)]
    qd = [a * b for a, b in zip(q, eg)]
    g_end = [t[CHUNK - 1:CHUNK] for t in gc]
    kd = [a * jnp.exp(e - g) for a, e, g in zip(k, g_end, gc)]
    states = list(states)
    outs = {}
    for i, (ci, h) in enumerate(units):
        st = states[h]
        v_new = u[i] - _mm(w[i], st, "nt", False)
        outs[ci, h] = _mm(qd[i], st, "nt", False) + _mm(qk[i], v_new, "nn", False)
        states[h] = st * jnp.exp(g_end[i]) + _mm(v_new, kd[i], "tn", False)
    rows = []
    for ci in range(n // CHUNK):
        rows.append(jnp.concatenate(
            [_rms(outs[ci, h], nw) * _silu(cut(bz, ci, h)) for h in range(N_HEADS)], axis=1))
    return jnp.concatenate(rows, axis=0), states


def _gmlp_tile(ln_w, ln_b, ws, bs_cols, cu, cv):
    u = _gelu(cu)
    v = _ln(_gelu(cv), ln_w, ln_b)
    n = cu.shape[0]
    tril = _iota((n, n), 0) >= _iota((n, n), 1)
    outs = []
    for h in range(N_HEADS):
        sl = slice(HEAD_DIM * h, HEAD_DIM * (h + 1))
        wc = jnp.where(tril, ws[h], 0.0)
        outs.append(_mm(wc, v[:, sl], "nn", False) + bs_cols[:, h:h + 1])
    return u * jnp.concatenate(outs, axis=1)


def _conformer_tile(dw_w, dw_b, ln_w, ln_b, pa, pg, ca, cg):
    n = ca.shape[0]
    yp = pa[n - CONV_HALO:n] * _sig(pg[n - CONV_HALO:n])
    ext = jnp.concatenate([yp, ca * _sig(cg)], axis=0)
    acc = jnp.zeros_like(ca)
    for t in range(CONV_WIDTH):
        off = CONV_HALO - (CONV_WIDTH - 1) + t
        acc = acc + dw_w[t:t + 1, :] * ext[off:off + n]
    return _silu(_ln(acc + dw_b, ln_w, ln_b))


def _seq_call(name, body, *, steps, ins, outs, accs=(), carries=(), reverse=False, prefetch=None):
    n_in, n_out, n_acc, n_car = len(ins), len(outs), len(accs), len(carries)
    n_pre = 0 if prefetch is None else len(prefetch)

    def logical(g):
        return (steps - 1 - g) if reverse else g

    def kern(*refs):
        pre = refs[:n_pre]
        refs = refs[n_pre:]
        in_refs = refs[:n_in]
        out_refs = refs[n_in:n_in + n_out]
        acc_refs = refs[n_in + n_out:n_in + n_out + n_acc]
        car_refs = refs[n_in + n_out + n_acc:]
        g = pl.program_id(0)

        @pl.when(g == 0)
        def _():
            for r in list(acc_refs) + list(car_refs):
                r[...] = jnp.zeros(r.shape, r.dtype)

        o, a, c = body(logical(g), [r[...] for r in in_refs], [r[...] for r in car_refs], *((pre,) if n_pre else ()))
        for r, v in zip(out_refs, o, strict=True):
            r[...] = v.astype(r.dtype)
        for r, v in zip(acc_refs, a, strict=True):
            r[...] += v
        for r, v in zip(car_refs, c, strict=True):
            r[...] = v

    def spec(block, fn):
        return pl.BlockSpec(block, lambda g, *pre: fn(logical(g), *((pre,) if n_pre else ())))

    in_specs = [spec(bs, fn) for (_, bs, fn) in ins]
    out_specs = [spec(bs, fn) for (_, _, bs, fn) in outs]
    out_specs += [pl.BlockSpec(shape, lambda g, *pre, _n=len(shape): (0,) * _n) for (shape, _) in accs]
    out_shape = [jax.ShapeDtypeStruct(s, d) for (s, d, _, _) in outs]
    out_shape += [jax.ShapeDtypeStruct(s, d) for (s, d) in accs]
    grid_spec = pltpu.PrefetchScalarGridSpec(
        num_scalar_prefetch=n_pre, grid=(steps,), in_specs=in_specs, out_specs=out_specs,
        scratch_shapes=[pltpu.VMEM(s, d) for (s, d) in carries])
    args = ([] if prefetch is None else list(prefetch)) + [a for (a, _, _) in ins]
    return pl.pallas_call(
        kern, name=name, grid_spec=grid_spec, out_shape=out_shape,
        compiler_params=pltpu.CompilerParams(dimension_semantics=("arbitrary",), vmem_limit_bytes=VMEM_LIMIT),
    )(*args)


def _whole(a):
    nd = a.ndim
    return (a, a.shape, lambda i, *pre: (0,) * nd)


def _rows(a, tile, col=0, width=None, shift=0):
    width = a.shape[1] if width is None else width
    if shift:
        return (a, (tile, width), lambda i, *pre: (jnp.maximum(i + shift, 0), col))
    return (a, (tile, width), lambda i, *pre: (i, col))


def _row_out(n_rows, width, dtype, tile):
    return ((n_rows, width), dtype, (tile, width), lambda i, *pre: (i, 0))


def _pick_tile(n, prefs):
    for t in prefs:
        if n % t == 0:
            return t
    return n


def _matmul(name, a, b, mode, out_dtypes, epilogue=None, extras=(), place=None, into=None, max_tm=1024):
    if mode == "nn":
        (m, k), n = a.shape, b.shape[1]
    elif mode == "nt":
        (m, k), n = a.shape, b.shape[0]
    else:
        (k, m), n = a.shape, b.shape[1]
    tm = _pick_tile(m, tuple(t for t in (1024, 512, 256, 128) if t <= max_tm))
    tn = _pick_tile(n, (1024, 896, 512, 256, 128))
    tk = _pick_tile(k, (2048, 896, 512, 256, 128))
    nk = k // tk
    n_ex = len(extras)
    n_out = len(out_dtypes)

    n_into = 0 if into is None else 1

    def kern(*refs):
        a_ref, b_ref = refs[0], refs[1]
        ex_refs = refs[2:2 + n_ex]
        out_refs = refs[2 + n_ex + n_into:2 + n_ex + n_into + n_out]
        acc_ref = refs[2 + n_ex + n_into + n_out]
        kk = pl.program_id(2)

        @pl.when(kk == 0)
        def _():
            acc_ref[...] = jnp.zeros(acc_ref.shape, F32)

        acc_ref[...] += lax.dot_general(a_ref[...], b_ref[...], _DIMS[mode], preferred_element_type=F32)

        @pl.when(kk == nk - 1)
        def _():
            acc = acc_ref[...]
            vals = (acc,) if epilogue is None else epilogue(acc, *[r[...] for r in ex_refs])
            for r, v in zip(out_refs, vals, strict=True):
                r[...] = v.astype(r.dtype)

    if mode == "tn":
        a_spec = pl.BlockSpec((tk, tm), lambda i, j, kk: (kk, i))
    else:
        a_spec = pl.BlockSpec((tm, tk), lambda i, j, kk: (i, kk))
    if mode == "nt":
        b_spec = pl.BlockSpec((tn, tk), lambda i, j, kk: (j, kk))
    else:
        b_spec = pl.BlockSpec((tk, tn), lambda i, j, kk: (kk, j))
    tile = pl.BlockSpec((tm, tn), lambda i, j, kk: (i, j))
    out_specs = [tile] * n_out
    out_shape = [jax.ShapeDtypeStruct((m, n), d) for d in out_dtypes]
    if place is not None:
        shape, block_fn, index_fn = place
        out_specs = [pl.BlockSpec(block_fn(tm, tn), lambda i, j, kk: index_fn(i, j, tm, tn))]
        out_shape = [jax.ShapeDtypeStruct(shape, out_dtypes[0])]
    return pl.pallas_call(
        kern, name=name, grid=(m // tm, n // tn, nk),
        in_specs=[a_spec, b_spec] + [tile] * n_ex + [HBM] * n_into,
        out_specs=out_specs, out_shape=out_shape,
        input_output_aliases=({2 + n_ex: 0} if n_into else {}),
        scratch_shapes=[pltpu.VMEM((tm, tn), F32)],
        compiler_params=pltpu.CompilerParams(
            dimension_semantics=("parallel", "parallel", "arbitrary"), vmem_limit_bytes=VMEM_LIMIT),
    )(a, b, *extras, *([into] if n_into else []))


ROW_TILE = 256


def _rms_fwd(name, x, w):
    s, d = x.shape
    t = _pick_tile(s, (ROW_TILE,))

    def body(i, v, c):
        return [_rms(v[0], v[1])], [], []

    return _seq_call(name, body, steps=s // t, ins=[_rows(x, t), _whole(w)], outs=[_row_out(s, d, BF16, t)])[0]


def _resid_rms_fwd(name, x, y, w):
    s, d = x.shape
    t = _pick_tile(s, (ROW_TILE,))

    def body(i, v, c):
        return [v[0] + _rms(v[1], v[2])], [], []

    return _seq_call(name, body, steps=s // t, ins=[_rows(x, t), _rows(y, t), _whole(w)],
                     outs=[_row_out(s, d, F32, t)])[0]


def _rms_bwd(name, x, w, dh, dres):
    s, d = x.shape
    t = _pick_tile(s, (ROW_TILE,))

    def body(i, v, c):
        _, vjp = jax.vjp(_rms, v[0], v[1])
        dx, dw = vjp(v[2])
        return [dx + v[3]], [dw], []

    return _seq_call(name, body, steps=s // t, ins=[_rows(x, t), _whole(w), _rows(dh, t), _rows(dres, t)],
                     outs=[_row_out(s, d, F32, t)], accs=[((1, d), F32)])


def _resid_rms_bwd(name, y, w, dxo):
    s, d = y.shape
    t = _pick_tile(s, (ROW_TILE,))

    def body(i, v, c):
        _, vjp = jax.vjp(_rms, v[0], v[1])
        dy, dw = vjp(v[2])
        return [dy], [dw], []

    return _seq_call(name, body, steps=s // t, ins=[_rows(y, t), _whole(w), _rows(dxo, t)],
                     outs=[_row_out(s, d, BF16, t)], accs=[((1, d), F32)])


def _loss_head(name, y, target):
    s, d = y.shape
    t = _pick_tile(s, (ROW_TILE,))

    def body(i, v, c):
        err = v[0] - v[1]
        part = 0.5 * jnp.sum(jnp.mean(err * err, axis=-1, keepdims=True))
        return [err * (1.0 / d)], [jnp.full((1, LANES), part, F32)], []

    return _seq_call(name, body, steps=s // t, ins=[_rows(y, t), _rows(target, t)],
                     outs=[_row_out(s, d, F32, t)], accs=[((1, LANES), F32)])


SEG = {n: i for i, n in enumerate(
    ["a_q", "a_f", "a_i", "a_g", "b_q", "b_k", "b_v", "b_z", "c_u", "c_v", "d_a", "d_gate"])}
AB_COL = 12 * GROUP_WIDTH // LANES


def _seg(proj, name, tile, shift=0):
    return _rows(proj, tile, col=SEG[name], width=GROUP_WIDTH, shift=shift)


def _state_block():
    return (1, N_HEADS * HEAD_DIM, HEAD_DIM), lambda i, *pre: (i, 0, 0)


def _split_states(blk):
    return [blk[0, HEAD_DIM * h:HEAD_DIM * (h + 1), :] for h in range(N_HEADS)]


STATE_CARRIES = [((HEAD_DIM, HEAD_DIM), F32)] * N_HEADS


def _hgrn_fwd(layer, proj, lbp, nw):
    s = proj.shape[0]
    n = s // HGRN_TILE
    sb, sf = _state_block()

    def body(i, v, st):
        o, new = _hgrn_tile(layer, v[0], v[1], v[2], v[3], v[4], v[5], st)
        return [o, jnp.concatenate(st, axis=0)[None]], [], new

    return _seq_call(
        f"hgrn_fwd{layer}", body, steps=n,
        ins=[_whole(lbp), _whole(nw)] + [_seg(proj, k, HGRN_TILE) for k in ("a_q", "a_f", "a_i", "a_g")],
        outs=[_row_out(s, GROUP_WIDTH, BF16, HGRN_TILE), ((n, N_HEADS * HEAD_DIM, HEAD_DIM), F32, sb, sf)],
        carries=STATE_CARRIES)


def _hgrn_bwd(layer, proj, lbp, nw, states, dmix):
    s = proj.shape[0]
    n = s // HGRN_TILE
    sb, sf = _state_block()

    def body(i, v, dst):
        st = _split_states(v[6])

        def f(lbp_, nw_, aq, af, ai, ag, *st_):
            return _hgrn_tile(layer, lbp_, nw_, aq, af, ai, ag, list(st_))

        _, vjp = jax.vjp(f, v[0], v[1], v[2], v[3], v[4], v[5], *st)
        g = vjp((v[7], list(dst)))
        return [jnp.concatenate(g[2:6], axis=1)], [g[0], g[1]], list(g[6:])

    return _seq_call(
        f"hgrn_bwd{layer}", body, steps=n, reverse=True,
        ins=[_whole(lbp), _whole(nw)] + [_seg(proj, k, HGRN_TILE) for k in ("a_q", "a_f", "a_i", "a_g")]
        + [(states, sb, sf), _rows(dmix, HGRN_TILE, col=0, width=GROUP_WIDTH)],
        outs=[_row_out(s, 4 * GROUP_WIDTH, BF16, HGRN_TILE)],
        accs=[(lbp.shape, F32), (nw.shape, F32)], carries=STATE_CARRIES)


def _gdn_ins(proj, cw, alog, dtb, nw):
    return ([_whole(cw), _whole(alog), _whole(dtb), _whole(nw)]
            + [_seg(proj, k, GDN_TILE, shift=-1) for k in ("b_q", "b_k", "b_v")]
            + [_seg(proj, k, GDN_TILE) for k in ("b_q", "b_k", "b_v", "b_z")]
            + [_rows(proj, GDN_TILE, col=AB_COL, width=LANES)])


def _mask_prev(i, vals):
    keep = (i > 0).astype(F32)
    return [p * keep for p in vals]


def _gdn_fwd(layer, proj, cw, alog, dtb, nw):
    s = proj.shape[0]
    n = s // GDN_TILE
    sb, sf = _state_block()

    def body(i, v, st):
        prev = _mask_prev(i, v[4:7])
        o, new = _gdn_tile(v[0], v[1], v[2], v[3], *prev, *v[7:12], st)
        return [o, jnp.concatenate(st, axis=0)[None]], [], new

    return _seq_call(
        f"gdn_fwd{layer}", body, steps=n, ins=_gdn_ins(proj, cw, alog, dtb, nw),
        outs=[_row_out(s, GROUP_WIDTH, BF16, GDN_TILE), ((n, N_HEADS * HEAD_DIM, HEAD_DIM), F32, sb, sf)],
        carries=STATE_CARRIES)


def _gdn_bwd(layer, proj, cw, alog, dtb, nw, states, dmix):
    s = proj.shape[0]
    n = s // GDN_TILE
    sb, sf = _state_block()

    def body(i, v, car):
        dst, dprev = car[:N_HEADS], car[N_HEADS:]
        prev = _mask_prev(i, v[4:7])
        st = _split_states(v[12])

        def f(cw_, alog_, dtb_, nw_, pq, pk, pv, cq, ck, cv, bz, ab, *st_):
            return _gdn_tile(cw_, alog_, dtb_, nw_, pq, pk, pv, cq, ck, cv, bz, ab, list(st_))

        _, vjp = jax.vjp(f, v[0], v[1], v[2], v[3], *prev, *v[7:12], *st)
        g = vjp((v[13], list(dst)))
        dcur = [g[7] + dprev[0], g[8] + dprev[1], g[9] + dprev[2], g[10]]
        return ([jnp.concatenate(dcur, axis=1), g[11]], list(g[0:4]), list(g[12:]) + list(g[4:7]))

    return _seq_call(
        f"gdn_bwd{layer}", body, steps=n, reverse=True,
        ins=_gdn_ins(proj, cw, alog, dtb, nw) + [(states, sb, sf), _rows(dmix, GDN_TILE, col=1, width=GROUP_WIDTH)],
        outs=[_row_out(s, 4 * GROUP_WIDTH, BF16, GDN_TILE), _row_out(s, LANES, BF16, GDN_TILE)],
        accs=[(cw.shape, F32), (alog.shape, F32), (dtb.shape, F32), (nw.shape, F32)],
        carries=STATE_CARRIES + [((GDN_TILE, GROUP_WIDTH), F32)] * 3)


def _gmlp_fwd(layer, proj, ln_w, ln_b, ws, bs_cols):
    s = proj.shape[0]

    def body(i, v, c):
        return [_gmlp_tile(*v)], [], []

    return _seq_call(
        f"gmlp_fwd{layer}", body, steps=s // MIX_CHUNK,
        ins=[_whole(ln_w), _whole(ln_b), _whole(ws), _whole(bs_cols),
             _seg(proj, "c_u", MIX_CHUNK), _seg(proj, "c_v", MIX_CHUNK)],
        outs=[_row_out(s, GROUP_WIDTH, BF16, MIX_CHUNK)])[0]


def _gmlp_bwd(layer, proj, ln_w, ln_b, ws, bs_cols, dmix):
    s = proj.shape[0]

    def body(i, v, c):
        _, vjp = jax.vjp(_gmlp_tile, *v[:6])
        g = vjp(v[6])
        return [jnp.concatenate(g[4:6], axis=1)], list(g[0:4]), []

    return _seq_call(
        f"gmlp_bwd{layer}", body, steps=s // MIX_CHUNK,
        ins=[_whole(ln_w), _whole(ln_b), _whole(ws), _whole(bs_cols),
             _seg(proj, "c_u", MIX_CHUNK), _seg(proj, "c_v", MIX_CHUNK),
             _rows(dmix, MIX_CHUNK, col=2, width=GROUP_WIDTH)],
        outs=[_row_out(s, 2 * GROUP_WIDTH, BF16, MIX_CHUNK)],
        accs=[(ln_w.shape, F32), (ln_b.shape, F32), (ws.shape, F32), (bs_cols.shape, F32)])


def _conformer_ins(proj, dw_w, dw_b, ln_w, ln_b):
    return ([_whole(dw_w), _whole(dw_b), _whole(ln_w), _whole(ln_b)]
            + [_seg(proj, k, CONV_TILE, shift=-1) for k in ("d_a", "d_gate")]
            + [_seg(proj, k, CONV_TILE) for k in ("d_a", "d_gate")])


def _conformer_fwd(layer, proj, dw_w, dw_b, ln_w, ln_b):
    s = proj.shape[0]

    def body(i, v, c):
        prev = _mask_prev(i, v[4:6])
        return [_conformer_tile(v[0], v[1], v[2], v[3], *prev, v[6], v[7])], [], []

    return _seq_call(
        f"conformer_fwd{layer}", body, steps=s // CONV_TILE, ins=_conformer_ins(proj, dw_w, dw_b, ln_w, ln_b),
        outs=[_row_out(s, GROUP_WIDTH, BF16, CONV_TILE)])[0]


def _conformer_bwd(layer, proj, dw_w, dw_b, ln_w, ln_b, dmix):
    s = proj.shape[0]

    def body(i, v, dprev):
        prev = _mask_prev(i, v[4:6])
        _, vjp = jax.vjp(_conformer_tile, v[0], v[1], v[2], v[3], *prev, v[6], v[7])
        g = vjp(v[8])
        return [jnp.concatenate([g[6] + dprev[0], g[7] + dprev[1]], axis=1)], list(g[0:4]), [g[4], g[5]]

    return _seq_call(
        f"conformer_bwd{layer}", body, steps=s // CONV_TILE, reverse=True,
        ins=_conformer_ins(proj, dw_w, dw_b, ln_w, ln_b) + [_rows(dmix, CONV_TILE, col=3, width=GROUP_WIDTH)],
        outs=[_row_out(s, 2 * GROUP_WIDTH, BF16, CONV_TILE)],
        accs=[(dw_w.shape, F32), (dw_b.shape, F32), (ln_w.shape, F32), (ln_b.shape, F32)],
        carries=[((CONV_TILE, GROUP_WIDTH), F32)] * 2)


SMALL = ["lower_bounds", "norm_mix_pre", "norm_mix_post", "norm_ff_pre", "norm_ff_post", "hgrn_norm_w",
         "gdn_conv_w", "gdn_a_log", "gdn_dt_bias", "gdn_norm_w", "gmlp_ln_w", "gmlp_ln_b", "gmlp_w_s",
         "gmlp_b_s", "conv_dw_w", "conv_dw_b", "conv_ln_w", "conv_ln_b"]


def _gate_row(v):
    return jnp.pad(v.reshape(1, N_HEADS), ((0, 0), (N_HEADS, LANES - 2 * N_HEADS)))


def _cut_rows(layer, rows, cols):
    return ((DEPTH, N_CHIPS, rows, cols), lambda tm, tn: (None, None, tm, tn),
            lambda i, j, tm, tn: (layer, i // (rows // tm), i % (rows // tm), j))


def _cut_cols(layer, rows, cols):
    return ((DEPTH, N_CHIPS, rows, cols), lambda tm, tn: (None, None, tm, tn),
            lambda i, j, tm, tn: (layer, j // (cols // tn), i, j % (cols // tn)))


def _relu2(acc):
    r = jnp.maximum(acc, 0.0)
    return acc, r * r


def _relu2_bwd(acc, u):
    return (2.0 * jnp.maximum(u, 0.0) * acc,)


def _local_step(x, target, sp, wts):
    row = lambda v: v.reshape(1, -1)
    saved = []
    w_in, w_out, w_ff1, w_ff2 = [], [], [], []
    for l in range(DEPTH):
        par = dict(
            lbp=sp["lower_bounds"], hn=row(sp["hgrn_norm_w"][l]), cw=sp["gdn_conv_w"][l],
            alog=_gate_row(sp["gdn_a_log"][l]), dtb=_gate_row(sp["gdn_dt_bias"][l]), gn=row(sp["gdn_norm_w"][l]),
            glw=row(sp["gmlp_ln_w"][l]), glb=row(sp["gmlp_ln_b"][l]), ws=sp["gmlp_w_s"][l],
            bsc=jnp.pad(sp["gmlp_b_s"][l].T, ((0, 0), (0, LANES - N_HEADS))),
            dww=sp["conv_dw_w"][l], dwb=row(sp["conv_dw_b"][l]), clw=row(sp["conv_ln_w"][l]),
            clb=row(sp["conv_ln_b"][l]), n1=row(sp["norm_mix_pre"][l]), n2=row(sp["norm_mix_post"][l]),
            n3=row(sp["norm_ff_pre"][l]), n4=row(sp["norm_ff_post"][l]))
        h = _rms_fwd(f"norm_mix_pre{l}", x, par["n1"])
        w_in.append(wts.get("w_in", l))
        proj = _matmul(f"in_proj{l}", h, w_in[l], "nn", [F32])[0]
        wts.mark(f"proj{l}", proj)
        o_a, st_a = _hgrn_fwd(l, proj, par["lbp"], par["hn"])
        o_b, st_b = _gdn_fwd(l, proj, par["cw"], par["alog"], par["dtb"], par["gn"])
        wts.mark(f"gdn{l}", o_b)
        o_c = _gmlp_fwd(l, proj, par["glw"], par["glb"], par["ws"], par["bsc"])
        o_d = _conformer_fwd(l, proj, par["dww"], par["dwb"], par["clw"], par["clb"])
        wts.mark(f"conformer{l}", o_d)
        mix = jnp.concatenate([o_a, o_b, o_c, o_d], axis=1)
        w_out.append(wts.get("w_out", l))
        y = _matmul(f"out_proj{l}", mix, w_out[l], "nn", [F32])[0]
        wts.mark(f"y{l}", y)
        x1 = _resid_rms_fwd(f"norm_mix_post{l}", x, y, par["n2"])
        h2 = _rms_fwd(f"norm_ff_pre{l}", x1, par["n3"])
        w_ff1.append(wts.get("w_ff1", l))
        u, act = _matmul(f"ff1_{l}", h2, w_ff1[l], "nn", [F32, BF16], epilogue=_relu2)
        wts.mark(f"u{l}", u)
        w_ff2.append(wts.get("w_ff2", l))
        y2 = _matmul(f"ff2_{l}", act, w_ff2[l], "nn", [F32])[0]
        x2 = _resid_rms_fwd(f"norm_ff_post{l}", x1, y2, par["n4"])
        saved.append(dict(par=par, x=x, h=h, proj=proj, st_a=st_a, st_b=st_b, mix=mix, y=y, x1=x1, h2=h2,
                          u=u, act=act, y2=y2))
        x = x2

    dx, loss_acc = _loss_head("loss_head", x, target)
    loss_part = loss_acc[0, 0]

    gs = {k: [None] * DEPTH for k in SMALL if k != "lower_bounds"}
    g_lb = jnp.zeros((DEPTH, GROUP_WIDTH), F32)
    g_in = [None] * DEPTH
    g_out = g_ff1 = g_ff2 = None
    for l in reversed(range(DEPTH)):
        sv = saved[l]
        par = sv["par"]
        dy2, dn4 = _resid_rms_bwd(f"norm_ff_post_bwd{l}", sv["y2"], par["n4"], dx)
        du = _matmul(f"ff2_dx{l}", dy2, w_ff2[l], "nt", [BF16], epilogue=_relu2_bwd, extras=(sv["u"],))[0]
        g_ff2 = _matmul(f"ff2_dw{l}", sv["act"], dy2, "tn", [F32], into=g_ff2,
                        place=_cut_rows(l, D_FF // N_CHIPS, D_MODEL))[0]
        g_ff1 = _matmul(f"ff1_dw{l}", sv["h2"], du, "tn", [F32], into=g_ff1,
                        place=_cut_cols(l, D_MODEL, D_FF // N_CHIPS))[0]
        dh2 = _matmul(f"ff1_dx{l}", du, w_ff1[l], "nt", [F32])[0]
        dx1, dn3 = _rms_bwd(f"norm_ff_pre_bwd{l}", sv["x1"], par["n3"], dh2, dx)
        dy, dn2 = _resid_rms_bwd(f"norm_mix_post_bwd{l}", sv["y"], par["n2"], dx1)
        dmix = _matmul(f"out_proj_dx{l}", dy, w_out[l], "nt", [F32])[0]
        g_out = _matmul(f"out_proj_dw{l}", sv["mix"], dy, "tn", [F32], into=g_out, max_tm=GROUP_WIDTH,
                        place=_cut_rows(l, GROUP_WIDTH, D_MODEL))[0]
        proj = sv["proj"]
        dp_a, dlb, dhn = _hgrn_bwd(l, proj, par["lbp"], par["hn"], sv["st_a"], dmix)
        dp_b, dp_ab, dcw, dalog, ddtb, dgn = _gdn_bwd(
            l, proj, par["cw"], par["alog"], par["dtb"], par["gn"], sv["st_b"], dmix)
        dp_c, dglw, dglb, dws, dbsc = _gmlp_bwd(l, proj, par["glw"], par["glb"], par["ws"], par["bsc"], dmix)
        dp_d, ddww, ddwb, dclw, dclb = _conformer_bwd(
            l, proj, par["dww"], par["dwb"], par["clw"], par["clb"], dmix)
        dproj = jnp.concatenate([dp_a, dp_b, dp_c, dp_d, dp_ab], axis=1)
        g_in[l] = _matmul(f"in_proj_dw{l}", sv["h"], dproj, "tn", [F32])[0]
        dh = _matmul(f"in_proj_dx{l}", dproj, w_in[l], "nt", [F32])[0]
        dx, dn1 = _rms_bwd(f"norm_mix_pre_bwd{l}", sv["x"], par["n1"], dh, dx1)
        g_lb = g_lb + dlb
        for k, v in dict(
                norm_mix_pre=dn1[0], norm_mix_post=dn2[0], norm_ff_pre=dn3[0], norm_ff_post=dn4[0],
                hgrn_norm_w=dhn[0], gdn_conv_w=dcw, gdn_a_log=dalog[0, N_HEADS:2 * N_HEADS],
                gdn_dt_bias=ddtb[0, N_HEADS:2 * N_HEADS], gdn_norm_w=dgn[0], gmlp_ln_w=dglw[0],
                gmlp_ln_b=dglb[0], gmlp_w_s=dws, gmlp_b_s=dbsc[:, :N_HEADS].T, conv_dw_w=ddww,
                conv_dw_b=ddwb[0], conv_ln_w=dclw[0], conv_ln_b=dclb[0]).items():
            gs[k][l] = v
    small_grads = {k: jnp.stack(v) for k, v in gs.items()}
    small_grads["lower_bounds"] = g_lb
    return loss_part, dx, small_grads, g_in, g_out, g_ff1, g_ff2


HBM = pl.BlockSpec(memory_space=pl.ANY)


def _place():
    return lax.axis_index("x"), lax.axis_index("y"), lax.axis_index("c")


def _other_chips(x, y):
    chips = [(1 - x, y), (x, 1 - y), (1 - x, 1 - y)]
    return [(px, py, 2 * px + py) for px, py in chips]


def _gather_weights(name, blk):
    def body(blk_ref, out_ref, ici_send, ici_recv, d2d_send, d2d_recv):
        x, y, c = _place()
        mine = 2 * x + y

        def ici(j, chip_of_slab, to):
            return pltpu.make_async_remote_copy(
                src_ref=blk_ref.at[c], dst_ref=out_ref.at[chip_of_slab, c], send_sem=ici_send.at[j],
                recv_sem=ici_recv.at[j], device_id=to, device_id_type=MESH)

        def d2d(j, chip_of_slab, layer):
            return pltpu.make_async_remote_copy(
                src_ref=out_ref.at[chip_of_slab, layer], dst_ref=out_ref.at[chip_of_slab, layer],
                send_sem=d2d_send.at[j], recv_sem=d2d_recv.at[j], device_id=(x, y, 1 - c), device_id_type=MESH)

        peers = _other_chips(x, y)
        sends = [ici(j, mine, (px, py, c)) for j, (px, py, _) in enumerate(peers)]
        for cp in sends:
            cp.start()
        passed = []
        for j, (px, py, k) in enumerate(peers):
            ici(j, k, (px, py, c)).wait_recv()
            fwd = d2d(j, k, c)
            fwd.start()
            passed.append(fwd)
        for j, (px, py, k) in enumerate(peers):
            d2d(j, k, 1 - c).wait_recv()
        for cp in sends + passed:
            cp.wait_send()

    return pl.pallas_call(
        body, name=name, out_shape=jax.ShapeDtypeStruct((N_CHIPS,) + blk.shape, blk.dtype),
        in_specs=[HBM], out_specs=HBM, scratch_shapes=[pltpu.SemaphoreType.DMA((3,))] * 4,
    )(blk)


SEM = pl.BlockSpec(memory_space=pltpu.SEMAPHORE)
IN_HBM = pl.BlockSpec(memory_space=pltpu.HBM)
EFFECT = pltpu.SideEffectType.DATAFLOW_SIDE_EFFECTING


def _copies_start(name, plan, bufs, after):
    nb = len(bufs)

    def body(*refs):
        token = refs[-1]
        for started, _ in plan(refs[:nb], refs[nb + 1], refs[nb + 2]):
            started.start()
        token[...] = jnp.zeros(token.shape, token.dtype)

    out = pl.pallas_call(
        body, name=name,
        out_shape=(pltpu.SemaphoreType.DMA((3,)), pltpu.SemaphoreType.DMA((3,)))
        + tuple(pltpu.HBM(b.shape, b.dtype) for b in bufs) + (jax.ShapeDtypeStruct((8, LANES), F32),),
        in_specs=(IN_HBM,) * nb + (HBM,),
        out_specs=(SEM, SEM) + (IN_HBM,) * nb + (pl.BlockSpec(memory_space=pltpu.VMEM),),
        input_output_aliases={i: 2 + i for i in range(nb)},
        compiler_params=pltpu.CompilerParams(has_side_effects=EFFECT),
    )(*[pltpu.with_memory_space_constraint(b, pltpu.HBM) for b in bufs], after)
    return out[0], out[1], list(out[2:2 + nb]), out[-1]


def _copies_wait(name, plan, send_sems, recv_sems, bufs, after):
    nb = len(bufs)

    def body(*refs):
        for started, arriving in plan(refs[:nb], refs[nb], refs[nb + 1]):
            started.wait_send()
            arriving.wait_recv()

    out = pl.pallas_call(
        body, name=name, out_shape=tuple(pltpu.HBM(b.shape, b.dtype) for b in bufs),
        in_specs=(IN_HBM,) * nb + (SEM, SEM, HBM), out_specs=(IN_HBM,) * nb,
        input_output_aliases={i: i for i in range(nb)},
        compiler_params=pltpu.CompilerParams(has_side_effects=EFFECT),
    )(*bufs, send_sems, recv_sems, after)
    return list(out)


def _ici_plan(bufs, send_sems, recv_sems):
    blk, land = bufs
    x, y, c = _place()
    mine = 2 * x + y
    plan = []
    for j, (px, py, k) in enumerate(_other_chips(x, y)):
        def copy(dst, j=j, to=(px, py, c)):
            return pltpu.make_async_remote_copy(
                src_ref=blk.at[c], dst_ref=dst, send_sem=send_sems.at[j], recv_sem=recv_sems.at[j],
                device_id=to, device_id_type=MESH)
        plan.append((copy(land.at[mine, c]), copy(land.at[k, c])))
    return plan


def _d2d_plan(bufs, send_sems, recv_sems):
    (land,) = bufs
    x, y, c = _place()
    plan = []
    for j, (_, _, k) in enumerate(_other_chips(x, y)):
        def copy(layer, j=j, k=k):
            return pltpu.make_async_remote_copy(
                src_ref=land.at[k, c], dst_ref=land.at[k, layer], send_sem=send_sems.at[j],
                recv_sem=recv_sems.at[j], device_id=(x, y, 1 - c), device_id_type=MESH)
        plan.append((copy(c), copy(1 - c)))
    return plan


class _GatheredWeights:
    STAGES = {"w_out": ("proj0", "conformer0"), "w_ff1": ("gdn0", "y0"), "w_ff2": ("u0", None)}

    def __init__(self, chip, shards):
        self.chip = chip
        self.blk, self.ici, self.d2d, self.full, self.mats = {}, {}, {}, {}, {}
        token = jnp.zeros((8, LANES), F32)
        for k in ("w_in", "w_out", "w_ff1", "w_ff2"):
            self.blk[k] = shards[k].astype(BF16)
            land = lax.empty((N_CHIPS,) + self.blk[k].shape, BF16)
            send, recv, bufs, token = _copies_start(f"gather_{k}_ici", _ici_plan, [self.blk[k], land], token)
            self.ici[k] = (send, recv, bufs)
        self._hand_over("w_in", token)
        self._finish("w_in", token)

    def _hand_over(self, k, after):
        send, recv, bufs = self.ici.pop(k)
        _, land = _copies_wait(f"gather_{k}_ici_done", _ici_plan, send, recv, bufs, after)
        send, recv, bufs, _ = _copies_start(f"gather_{k}_d2d", _d2d_plan, [land], after)
        self.d2d[k] = (send, recv, bufs)

    def _finish(self, k, after):
        send, recv, bufs = self.d2d.pop(k)
        (land,) = _copies_wait(f"gather_{k}_d2d_done", _d2d_plan, send, recv, bufs, after)
        self.full[k] = lax.dynamic_update_slice(land, self.blk[k][None], (self.chip, 0, 0, 0))

    def mark(self, tag, value):
        for k, (first, second) in self.STAGES.items():
            if tag == first:
                self._hand_over(k, value)
                if second is None:
                    self._finish(k, value)
            elif tag == second:
                self._finish(k, value)

    def get(self, kind, l):
        if (kind, l) not in self.mats:
            a = self.full[kind]
            if kind == "w_in":
                m = _reorder_in(jnp.concatenate([a[k, l] for k in range(N_CHIPS)], axis=-1))
            elif kind == "w_ff1":
                m = jnp.concatenate([a[k, l] for k in range(N_CHIPS)], axis=-1)
            else:
                m = a[:, l].reshape(-1, D_MODEL)
            self.mats[kind, l] = m
        return self.mats[kind, l]


def _pair_exchange(name, g, by_layer):
    def body(g_ref, got_ref, send_sem, recv_sem):
        x, y, c = _place()
        cp = pltpu.make_async_remote_copy(
            src_ref=g_ref.at[1 - c] if by_layer else g_ref, dst_ref=got_ref, send_sem=send_sem,
            recv_sem=recv_sem, device_id=(x, y, 1 - c), device_id_type=MESH)
        cp.start()
        cp.wait()

    return pl.pallas_call(
        body, name=name, out_shape=jax.ShapeDtypeStruct(g.shape[1:] if by_layer else g.shape, g.dtype),
        in_specs=[HBM], out_specs=HBM, scratch_shapes=[pltpu.SemaphoreType.DMA, pltpu.SemaphoreType.DMA],
    )(g)


def _scatter_partials(name, p):
    def body(p_ref, out_ref, send_sems, recv_sems):
        x, y, c = _place()

        def cp(j, to):
            return pltpu.make_async_remote_copy(
                src_ref=p_ref.at[j], dst_ref=out_ref.at[j], send_sem=send_sems.at[j],
                recv_sem=recv_sems.at[j], device_id=to, device_id_type=MESH)

        copies = [cp(j, (px, py, c)) for j, (px, py, _) in enumerate(_other_chips(x, y))]
        for s in copies:
            s.start()
        for s in copies:
            s.wait_recv()
        for s in copies:
            s.wait_send()

    return pl.pallas_call(
        body, name=name, out_shape=jax.ShapeDtypeStruct(p.shape, p.dtype), in_specs=[HBM], out_specs=HBM,
        scratch_shapes=[pltpu.SemaphoreType.DMA((3,)), pltpu.SemaphoreType.DMA((3,))],
    )(p)


def _gather_small(name, pack):
    def body(p_ref, out_ref, send_sems, recv_sems, local_sem):
        x, y, c = _place()
        me = 4 * x + 2 * y + c
        local = pltpu.make_async_copy(p_ref, out_ref.at[me], local_sem)
        local.start()
        flips = [(fx, fy, fc) for fx in (0, 1) for fy in (0, 1) for fc in (0, 1)][1:]
        peers = [((1 - x) if fx else x, (1 - y) if fy else y, (1 - c) if fc else c) for fx, fy, fc in flips]

        def cp(j, slot, to):
            return pltpu.make_async_remote_copy(
                src_ref=p_ref, dst_ref=out_ref.at[slot], send_sem=send_sems.at[j], recv_sem=recv_sems.at[j],
                device_id=to, device_id_type=MESH)

        sends = [cp(j, me, to) for j, to in enumerate(peers)]
        for s in sends:
            s.start()
        for j, (px, py, pc) in enumerate(peers):
            cp(j, 4 * px + 2 * py + pc, (px, py, pc)).wait_recv()
        for s in sends:
            s.wait_send()
        local.wait()

    return pl.pallas_call(
        body, name=name, out_shape=jax.ShapeDtypeStruct((N_DEV,) + pack.shape, pack.dtype), in_specs=[HBM],
        out_specs=HBM,
        scratch_shapes=[pltpu.SemaphoreType.DMA((7,)), pltpu.SemaphoreType.DMA((7,)), pltpu.SemaphoreType.DMA],
    )(pack)


SLAB_ROWS = 256


I_CORE, I_CHIP, I_PEER = 0, 1, 2


def _peer_chip(pre, j):
    return jnp.where(j == 0, pre[I_PEER][0], jnp.where(j == 1, pre[I_PEER + 1][0], pre[I_PEER + 2][0]))


def _pair_sum(name, g, got, where):
    _, _, r, c = g.shape
    t = _pick_tile(r, (SLAB_ROWS, 128, 64, 8))
    per = r // t

    def body(i, v, car, pre):
        return [v[0] + v[1]], [], []

    return _seq_call(
        name, body, steps=3 * per, prefetch=where,
        ins=[(g, (None, None, t, c), lambda i, pre: (pre[I_CORE][0], _peer_chip(pre, i // per), i % per, 0)),
             (got, (None, t, c), lambda i, pre: (_peer_chip(pre, i // per), i % per, 0))],
        outs=[((3, r, c), BF16, (None, t, c), lambda i, pre: (i // per, i % per, 0))])[0]


def _chip_sum(name, g, got, rcv, where):
    _, _, r, c = g.shape
    t = _pick_tile(r, (SLAB_ROWS, 128, 64, 8))

    def body(i, v, car, pre):
        acc = v[0] + v[1]
        for part in v[2:]:
            acc = acc + part.astype(F32)
        return [acc], [], []

    return _seq_call(
        name, body, steps=r // t, prefetch=where,
        ins=[(g, (None, None, t, c), lambda i, pre: (pre[I_CORE][0], pre[I_CHIP][0], i, 0)),
             (got, (None, t, c), lambda i, pre: (pre[I_CHIP][0], i, 0))]
        + [(rcv, (None, t, c), (lambda i, pre, _j=j: (_j, i, 0))) for j in range(3)],
        outs=[((r, c), F32, (t, c), lambda i, pre: (i, 0))])[0]


def _adamw_math(w, g, m, v):
    m = ADAM_B1 * m + (1.0 - ADAM_B1) * g
    v = ADAM_B2 * v + (1.0 - ADAM_B2) * (g * g)
    m_hat = m / (1.0 - ADAM_B1 ** ADAM_STEP)
    v_hat = v / (1.0 - ADAM_B2 ** ADAM_STEP)
    delta = -ADAM_LR * (m_hat / (jnp.sqrt(v_hat) + ADAM_EPS) + ADAM_WD * w)
    return delta, m, v


def _adamw(name, w, g, m, v):
    n, r, c = w.shape
    t = _pick_tile(r, (SLAB_ROWS, 128, 64, 8))
    per = r // t

    def body(i, vals, car):
        return list(_adamw_math(*vals)), [], []

    blk = lambda a: (a, (None, t, c), lambda i: (i // per, i % per, 0))
    out = ((n, r, c), F32, (None, t, c), lambda i: (i // per, i % per, 0))
    return _seq_call(name, body, steps=n * per, ins=[blk(w), blk(g), blk(m), blk(v)], outs=[out] * 3)


def _adamw_cut(name, w, mine, sib, m, v, where):
    n, r, c = w.shape
    t = _pick_tile(r, (SLAB_ROWS, 128, 64, 8))
    per = r // t

    def body(i, vals, car, pre):
        g = jnp.where(i // per == pre[I_CORE][0], vals[1], vals[2])
        return [g] + list(_adamw_math(vals[0], g, vals[3], vals[4])), [], []

    by_layer = lambda a: (a, (None, t, c), lambda i, pre: (i // per, i % per, 0))
    flat = lambda a: (a, (t, c), lambda i, pre: (i % per, 0))
    out = ((n, r, c), F32, (None, t, c), lambda i, pre: (i // per, i % per, 0))
    return _seq_call(name, body, steps=n * per, prefetch=where,
                     ins=[by_layer(w), flat(mine), flat(sib), by_layer(m), by_layer(v)], outs=[out] * 4)


def _ordered_sum(name, packs):
    n, r, c = packs.shape

    def body(i, v, car):
        acc = v[0][0]
        for k in range(1, n):
            acc = acc + v[0][k]
        return [acc], [], []

    return _seq_call(name, body, steps=1, ins=[_whole(packs)], outs=[((r, c), F32, (r, c), lambda i: (0, 0))])[0]


def _pack(arrays):
    flat = []
    for a in arrays:
        a = a.reshape(-1).astype(F32)
        pad = (-a.shape[0]) % LANES
        flat.append(jnp.pad(a, (0, pad)) if pad else a)
    v = jnp.concatenate(flat)
    pad = (-v.shape[0]) % (64 * LANES)
    if pad:
        v = jnp.pad(v, (0, pad))
    return v.reshape(-1, LANES)


def _unpack(pack, shapes):
    v = pack.reshape(-1)
    out, off = [], 0
    for s in shapes:
        n = math.prod(s)
        out.append(v[off:off + n].reshape(s))
        off += n + ((-n) % LANES)
    return out


def _reorder_in(full):
    g0 = 8 * GROUP_WIDTH
    pad = jnp.zeros(full.shape[:-1] + (LANES - 2 * N_HEADS,), full.dtype)
    return jnp.concatenate([full[..., :g0], full[..., g0 + 2 * N_HEADS:], full[..., g0:g0 + 2 * N_HEADS], pad], axis=-1)


def _restore_in(padded):
    g0 = 8 * GROUP_WIDTH
    wide = 12 * GROUP_WIDTH
    return jnp.concatenate([padded[..., :g0], padded[..., wide:wide + 2 * N_HEADS], padded[..., g0:wide]], axis=-1)


def _reduce_big(tag, g, where):
    got = _pair_exchange(f"pair_exchange_{tag}", g, True)
    pair = _pair_sum(f"pair_sum_{tag}", g, got, where)
    rcv = _scatter_partials(f"scatter_partials_{tag}", pair)
    mine = _chip_sum(f"chip_sum_{tag}", g, got, rcv, where)
    return mine, _pair_exchange(f"pair_send_{tag}", mine, False)


def kernel(x, lower_bounds, norm_mix_pre, norm_mix_post, norm_ff_pre, norm_ff_post, w_in, w_out, hgrn_norm_w, gdn_conv_w, gdn_a_log, gdn_dt_bias, gdn_norm_w, gmlp_ln_w, gmlp_ln_b, gmlp_w_s, gmlp_b_s, conv_dw_w, conv_dw_b, conv_ln_w, conv_ln_b, w_ff1, w_ff2, loss_target, m_lower_bounds, m_norm_mix_pre, m_norm_mix_post, m_norm_ff_pre, m_norm_ff_post, m_w_in, m_w_out, m_hgrn_norm_w, m_gdn_conv_w, m_gdn_a_log, m_gdn_dt_bias, m_gdn_norm_w, m_gmlp_ln_w, m_gmlp_ln_b, m_gmlp_w_s, m_gmlp_b_s, m_conv_dw_w, m_conv_dw_b, m_conv_ln_w, m_conv_ln_b, m_w_ff1, m_w_ff2, v_lower_bounds, v_norm_mix_pre, v_norm_mix_post, v_norm_ff_pre, v_norm_ff_post, v_w_in, v_w_out, v_hgrn_norm_w, v_gdn_conv_w, v_gdn_a_log, v_gdn_dt_bias, v_gdn_norm_w, v_gmlp_ln_w, v_gmlp_ln_b, v_gmlp_w_s, v_gmlp_b_s, v_conv_dw_w, v_conv_dw_b, v_conv_ln_w, v_conv_ln_b, v_w_ff1, v_w_ff2):
    args = dict(locals())
    chip = 2 * lax.axis_index("x") + lax.axis_index("y")
    where = tuple(jnp.asarray(v, jnp.int32).reshape(1)
                  for v in (lax.axis_index("c"), chip, chip ^ 2, chip ^ 1, chip ^ 3))

    wts = _GatheredWeights(chip, dict(w_in=w_in, w_out=w_out, w_ff1=w_ff1, w_ff2=w_ff2))
    cut_shapes = [gdn_conv_w.shape, conv_dw_w.shape]
    cuts = _gather_small("gather_cut_small", _pack([gdn_conv_w, conv_dw_w]))
    cut_parts = [_unpack(cuts[2 * k], cut_shapes) for k in range(N_CHIPS)]
    sp = {k: args[k] for k in SMALL}
    sp["gdn_conv_w"] = jnp.concatenate([p[0] for p in cut_parts], axis=-1)
    sp["conv_dw_w"] = jnp.concatenate([p[1] for p in cut_parts], axis=-1)

    loss_part, grad_x, small_g, g_in, g_out, g_ff1, g_ff2 = _local_step(x[0], loss_target[0], sp, wts)

    gi = jnp.stack([jnp.stack(jnp.split(_restore_in(g_in[l]), N_CHIPS, axis=-1)) for l in range(DEPTH)])
    big_g = dict(w_in=_reduce_big("w_in", gi, where), w_out=_reduce_big("w_out", g_out, where),
                 w_ff1=_reduce_big("w_ff1", g_ff1, where), w_ff2=_reduce_big("w_ff2", g_ff2, where))

    names = SMALL + ["loss"]
    small_g["loss"] = loss_part.reshape(1)
    shapes = [small_g[k].shape for k in names]
    total = _ordered_sum("sum_small", _gather_small("gather_small", _pack([small_g[k] for k in names])))
    summed = dict(zip(names, _unpack(total, shapes)))
    loss = summed.pop("loss")[0]
    for k, width in (("gdn_conv_w", gdn_conv_w.shape[-1]), ("conv_dw_w", conv_dw_w.shape[-1])):
        summed[k] = lax.dynamic_slice_in_dim(summed[k], chip * width, width, axis=-1)

    grads, deltas, new_m, new_v = {}, {}, {}, {}
    for k in ("w_in", "w_out", "w_ff1", "w_ff2"):
        grads[k], deltas[k], new_m[k], new_v[k] = _adamw_cut(
            f"adamw_{k}", args[k], *big_g[k], args["m_" + k], args["v_" + k], where)
    local_shapes = [args[k].shape for k in SMALL]
    packs = [_pack([src[k] for k in SMALL]) for src in (
        {k: args[k] for k in SMALL}, summed, {k: args["m_" + k] for k in SMALL}, {k: args["v_" + k] for k in SMALL})]
    d_s, m_s, v_s = _adamw("adamw_small", *[p[None] for p in packs])
    for k, d, mm, vv in zip(SMALL, _unpack(d_s[0], local_shapes), _unpack(m_s[0], local_shapes),
                            _unpack(v_s[0], local_shapes)):
        grads[k], deltas[k], new_m[k], new_v[k] = summed[k], d, mm, vv

    order = ["lower_bounds", "norm_mix_pre", "norm_mix_post", "norm_ff_pre", "norm_ff_post", "w_in", "w_out",
             "hgrn_norm_w", "gdn_conv_w", "gdn_a_log", "gdn_dt_bias", "gdn_norm_w", "gmlp_ln_w", "gmlp_ln_b",
             "gmlp_w_s", "gmlp_b_s", "conv_dw_w", "conv_dw_b", "conv_ln_w", "conv_ln_b", "w_ff1", "w_ff2"]
    return (loss, grad_x[None], *[grads[k] for k in order], *[deltas[k] for k in order],
            *[new_m[k] for k in order], *[new_v[k] for k in order])
```

```python
import functools
import math

import jax
import jax.numpy as jnp
from jax import lax
from jax.experimental import pallas as pl
from jax.experimental.pallas import tpu as pltpu

F32 = jnp.float32
BF16 = jnp.bfloat16

DEPTH = 2
D_MODEL = 2048
GROUP_WIDTH = 512
HEAD_DIM = 128
N_HEADS = 4
CHUNK = 64
SHORT_CONV = 4
MIX_CHUNK = 128
CONV_WIDTH = 31
D_FF = 4 * D_MODEL
D_IN_PROJ = 12 * GROUP_WIDTH + 2 * N_HEADS
EPS = 1e-6
NEG_BIG = -1e30
TINY = 1e-30
ADAM_LR = 0.001
ADAM_B1 = 0.9
ADAM_B2 = 0.999
ADAM_EPS = 1e-08
ADAM_WD = 0.01
ADAM_STEP = 10

LANES = 128
P_IN = 12 * GROUP_WIDTH + LANES
SUB_BLOCK = 16
HGRN_TILE = 128
GDN_TILE = 128
CONV_TILE = 128
CONV_HALO = 32
VMEM_LIMIT = 56 * 1024 * 1024
N_CHIPS = 4
N_DEV = 8
MESH = pl.DeviceIdType.MESH


_DIMS = {
    "nn": (((1,), (0,)), ((), ())),
    "nt": (((1,), (1,)), ((), ())),
    "tn": (((0,), (0,)), ((), ())),
}


def _split2(a):
    hi = a.astype(BF16)
    return hi, (a - hi.astype(F32)).astype(BF16)


def _raw_mm(a, b, mode, exact):
    dot = lambda p, q: lax.dot_general(p, q, _DIMS[mode], preferred_element_type=F32)
    if not exact:
        return dot(a.astype(BF16), b.astype(BF16))
    a_hi, a_lo = _split2(a)
    b_hi, b_lo = _split2(b)
    return dot(a_hi, b_hi) + (dot(a_hi, b_lo) + dot(a_lo, b_hi))


@functools.partial(jax.custom_vjp, nondiff_argnums=(2, 3))
def _mm(a, b, mode, exact):
    return _raw_mm(a, b, mode, exact)


def _mm_fwd(a, b, mode, exact):
    return _raw_mm(a, b, mode, exact), (a, b)


def _mm_bwd(mode, exact, res, g):
    a, b = res
    if mode == "nn":
        return _raw_mm(g, b, "nt", exact), _raw_mm(a, g, "tn", exact)
    if mode == "nt":
        return _raw_mm(g, b, "nn", exact), _raw_mm(g, a, "tn", exact)
    return _raw_mm(b, g, "nt", exact), _raw_mm(a, g, "nn", exact)


_mm.defvjp(_mm_fwd, _mm_bwd)


def _sig(x):
    return jax.nn.sigmoid(x)


def _silu(x):
    return x * jax.nn.sigmoid(x)


def _gelu(x):
    return 0.5 * x * (1.0 + lax.erf(x * (1.0 / math.sqrt(2.0))))


def _rms(x, w):
    return x * lax.rsqrt(jnp.mean(x * x, axis=-1, keepdims=True) + EPS) * w


def _ln(x, w, b):
    mu = jnp.mean(x, axis=-1, keepdims=True)
    xc = x - mu
    var = jnp.mean(xc * xc, axis=-1, keepdims=True)
    return xc * lax.rsqrt(var + EPS) * w + b


def _iota(shape, dim):
    return lax.broadcasted_iota(jnp.int32, shape, dim)


def _tri_mm(x, mode):
    n = x.shape[0]
    tri = (_iota((n, n), 0) >= _iota((n, n), 1)).astype(BF16)
    x1 = x.astype(BF16)
    r1 = x - x1.astype(F32)
    x2 = r1.astype(BF16)
    x3 = (r1 - x2.astype(F32)).astype(BF16)
    dot = lambda q: lax.dot_general(tri, q, _DIMS[mode], preferred_element_type=F32)
    return dot(x1) + (dot(x2) + dot(x3))


@jax.custom_vjp
def _cumsum_rows(x):
    return _tri_mm(x, "nn")


_cumsum_rows.defvjp(lambda x: (_tri_mm(x, "nn"), None), lambda _, g: (_tri_mm(g, "tn"),))


def _hgrn_head(q, k, v, b, st):
    n = q.shape[0]
    ii = _iota((n, 1), 0)
    zpad = jnp.zeros((SUB_BLOCK, HEAD_DIM), F32)
    k_ext = jnp.concatenate([zpad, k], axis=0)
    b_ext = jnp.concatenate([zpad, b], axis=0)
    v_ext = jnp.concatenate([zpad, v], axis=0)
    o = jnp.zeros((n, HEAD_DIM), F32)
    for d in range(SUB_BLOCK):
        ks = k_ext[SUB_BLOCK - d:SUB_BLOCK - d + n]
        bs = b_ext[SUB_BLOCK - d:SUB_BLOCK - d + n]
        vs = v_ext[SUB_BLOCK - d:SUB_BLOCK - d + n]
        e = jnp.exp(jnp.where((ii % SUB_BLOCK) >= d, b - bs, NEG_BIG))
        o = o + jnp.sum(q * ks * e, axis=-1, keepdims=True) * vs
    blocks = [o[0:SUB_BLOCK]]
    for blk in range(1, n // SUB_BLOCK):
        lo = SUB_BLOCK * blk
        r = b[lo - 1:lo]
        a_q = q[lo:lo + SUB_BLOCK] * jnp.exp(b[lo:lo + SUB_BLOCK] - r)
        b_k = jnp.where(ii < lo, k * jnp.exp(jnp.minimum(r - b, 0.0)), 0.0)
        sc = _mm(a_q, b_k, "nt", False)
        blocks.append(o[lo:lo + SUB_BLOCK] + _mm(sc, v, "nn", False))
    o = jnp.concatenate(blocks, axis=0)
    o = o + _mm(q * jnp.exp(b), st, "nt", False)
    b_end = b[n - 1:n]
    st_new = st * jnp.exp(b_end) + _mm(v, k * jnp.exp(b_end - b), "tn", False)
    return o, st_new


def _hgrn_chunk(layer, lbp, nw, aq, af, ai, ag, states):
    rows = [lbp[i:i + 1, :] for i in range(DEPTH)]
    mx = functools.reduce(jnp.maximum, rows)
    es = [jnp.exp(r - mx) for r in rows]
    den = functools.reduce(lambda p, s: p + s, es)
    soft = [e / den for e in es]
    lb = functools.reduce(lambda p, s: p + s, soft[:layer + 1]) - soft[0]
    f = lb + (1.0 - lb) * _sig(af)
    logf = jnp.log(jnp.maximum(f, TINY))
    k = (1.0 - lb) * _sig(-af)
    q = _silu(aq)
    b = _cumsum_rows(logf)
    outs, new_states = [], []
    for h in range(N_HEADS):
        sl = slice(HEAD_DIM * h, HEAD_DIM * (h + 1))
        o, st = _hgrn_head(q[:, sl], k[:, sl], ai[:, sl], b[:, sl], states[h])
        outs.append(_rms(o, nw) * _silu(ag[:, sl]))
        new_states.append(st)
    return jnp.concatenate(outs, axis=1), new_states


def _hgrn_tile(layer, lbp, nw, aq, af, ai, ag, states):
    outs = []
    for ci in range(aq.shape[0] // CHUNK):
        rs = slice(CHUNK * ci, CHUNK * (ci + 1))
        o, states = _hgrn_chunk(layer, lbp, nw, aq[rs], af[rs], ai[rs], ag[rs], states)
        outs.append(o)
    return jnp.concatenate(outs, axis=0), states


def _short_conv(prev, cur, w):
    n = cur.shape[0]
    ext = jnp.concatenate([prev[n - 8:n], cur], axis=0)
    y = jnp.zeros_like(cur)
    for t in range(SHORT_CONV):
        off = 8 - (SHORT_CONV - 1) + t
        y = y + w[t:t + 1, :] * ext[off:off + n]
    return _silu(y)


def _gdn_tile(cw, alog, dtb, nw, pq, pk, pv, cq, ck, cv, bz, ab, states):
    n = cq.shape[0]
    q_all = _short_conv(pq, cq, cw[:, 0:GROUP_WIDTH])
    k_all = _short_conv(pk, ck, cw[:, GROUP_WIDTH:2 * GROUP_WIDTH])
    v_all = _short_conv(pv, cv, cw[:, 2 * GROUP_WIDTH:3 * GROUP_WIDTH])
    beta_all = _sig(ab)
    g_all = -jnp.exp(alog) * jax.nn.softplus(ab + dtb)
    units = [(ci, h) for ci in range(n // CHUNK) for h in range(N_HEADS)]
    gc_all = [_cumsum_rows(g_all[CHUNK * ci:CHUNK * (ci + 1)]) for ci in range(n // CHUNK)]
    gc_t = [g.T for g in gc_all]
    ii = _iota((CHUNK, CHUNK), 0)
    jj = _iota((CHUNK, CHUNK), 1)
    eye = (ii == jj).astype(F32)

    def cut(a, ci, h):
        return a[CHUNK * ci:CHUNK * (ci + 1), HEAD_DIM * h:HEAD_DIM * (h + 1)]

    q = [cut(q_all, ci, h) for ci, h in units]
    k = [cut(k_all, ci, h) for ci, h in units]
    v = [cut(v_all, ci, h) for ci, h in units]
    q = [t * lax.rsqrt(jnp.sum(t * t, axis=-1, keepdims=True) + EPS) * (HEAD_DIM ** -0.5) for t in q]
    k = [t * lax.rsqrt(jnp.sum(t * t, axis=-1, keepdims=True) + EPS) for t in k]
    beta = [beta_all[CHUNK * ci:CHUNK * (ci + 1), h:h + 1] for ci, h in units]
    gc = [gc_all[ci][:, N_HEADS + h:N_HEADS + h + 1] for ci, h in units]
    gcr = [gc_t[ci][N_HEADS + h:N_HEADS + h + 1, :] for ci, h in units]
    gamma = [jnp.exp(jnp.where(ii >= jj, a - b, NEG_BIG)) for a, b in zip(gc, gcr)]
    kb = [a * b for a, b in zip(k, beta)]
    m = [jnp.where(ii > jj, _mm(a, b, "nt", False) * g, 0.0) for a, b, g in zip(kb, k, gamma)]
    inv = [eye - t for t in m]
    p = m
    for _ in range(max(1, int(math.ceil(math.log2(CHUNK))) - 1)):
        p = [_mm(t, t, "nn", True) for t in p]
        inv = [a + _mm(a, t, "nn", True) for a, t in zip(inv, p)]
    eg = [jnp.exp(t) for t in gc]
    u = [_mm(a, b * c, "nn", True) for a, b, c in zip(inv, v, beta)]
    w = [_mm(a, b * c, "nn", True) for a, b, c in zip(inv, kb, eg)]
    qk = [_mm(a, b, "nt", False) * g for a, b, g in zip(q, k, gamma)]
    qd = [a * b for a, b in zip(q, eg)]
    g_end = [t[CHUNK - 1:CHUNK] for t in gc]
    kd = [a * jnp.exp(e - g) for a, e, g in zip(k, g_end, gc)]
    states = list(states)
    outs = {}
    for i, (ci, h) in enumerate(units):
        st = states[h]
        v_new = u[i] - _mm(w[i], st, "nt", False)
        outs[ci, h] = _mm(qd[i], st, "nt", False) + _mm(qk[i], v_new, "nn", False)
        states[h] = st * jnp.exp(g_end[i]) + _mm(v_new, kd[i], "tn", False)
    rows = []
    for ci in range(n // CHUNK):
        rows.append(jnp.concatenate(
            [_rms(outs[ci, h], nw) * _silu(cut(bz, ci, h)) for h in range(N_HEADS)], axis=1))
    return jnp.concatenate(rows, axis=0), states


def _gmlp_tile(ln_w, ln_b, ws, bs_cols, cu, cv):
    u = _gelu(cu)
    v = _ln(_gelu(cv), ln_w, ln_b)
    n = cu.shape[0]
    tril = _iota((n, n), 0) >= _iota((n, n), 1)
    outs = []
    for h in range(N_HEADS):
        sl = slice(HEAD_DIM * h, HEAD_DIM * (h + 1))
        wc = jnp.where(tril, ws[h], 0.0)
        outs.append(_mm(wc, v[:, sl], "nn", False) + bs_cols[:, h:h + 1])
    return u * jnp.concatenate(outs, axis=1)


def _conformer_tile(dw_w, dw_b, ln_w, ln_b, pa, pg, ca, cg):
    n = ca.shape[0]
    yp = pa[n - CONV_HALO:n] * _sig(pg[n - CONV_HALO:n])
    ext = jnp.concatenate([yp, ca * _sig(cg)], axis=0)
    acc = jnp.zeros_like(ca)
    for t in range(CONV_WIDTH):
        off = CONV_HALO - (CONV_WIDTH - 1) + t
        acc = acc + dw_w[t:t + 1, :] * ext[off:off + n]
    return _silu(_ln(acc + dw_b, ln_w, ln_b))


def _seq_call(name, body, *, steps, ins, outs, accs=(), carries=(), reverse=False, prefetch=None, into=()):
    n_in, n_out, n_acc, n_car = len(ins), len(outs), len(accs), len(carries)
    n_into = len(into)
    n_pre = 0 if prefetch is None else len(prefetch)

    def logical(g):
        return (steps - 1 - g) if reverse else g

    def kern(*refs):
        pre = refs[:n_pre]
        refs = refs[n_pre:]
        in_refs = refs[:n_in]
        refs = refs[n_in + n_into:]
        out_refs = refs[:n_out]
        acc_refs = refs[n_out:n_out + n_acc]
        car_refs = refs[n_out + n_acc:]
        g = pl.program_id(0)

        @pl.when(g == 0)
        def _():
            for r in list(acc_refs) + list(car_refs):
                r[...] = jnp.zeros(r.shape, r.dtype)

        o, a, c = body(logical(g), [r[...] for r in in_refs], [r[...] for r in car_refs], *((pre,) if n_pre else ()))
        for r, v in zip(out_refs, o, strict=True):
            r[...] = v.astype(r.dtype)
        for r, v in zip(acc_refs, a, strict=True):
            r[...] += v
        for r, v in zip(car_refs, c, strict=True):
            r[...] = v

    def spec(block, fn):
        return pl.BlockSpec(block, lambda g, *pre: fn(logical(g), *((pre,) if n_pre else ())))

    in_specs = [spec(bs, fn) for (_, bs, fn) in ins] + [HBM] * n_into
    out_specs = [spec(bs, fn) for (_, _, bs, fn) in outs]
    out_specs += [pl.BlockSpec(shape, lambda g, *pre, _n=len(shape): (0,) * _n) for (shape, _) in accs]
    out_shape = [jax.ShapeDtypeStruct(s, d) for (s, d, _, _) in outs]
    out_shape += [jax.ShapeDtypeStruct(s, d) for (s, d) in accs]
    grid_spec = pltpu.PrefetchScalarGridSpec(
        num_scalar_prefetch=n_pre, grid=(steps,), in_specs=in_specs, out_specs=out_specs,
        scratch_shapes=[pltpu.VMEM(s, d) for (s, d) in carries])
    args = ([] if prefetch is None else list(prefetch)) + [a for (a, _, _) in ins] + list(into)
    return pl.pallas_call(
        kern, name=name, grid_spec=grid_spec, out_shape=out_shape,
        input_output_aliases={n_pre + n_in + i: i for i in range(n_into)},
        compiler_params=pltpu.CompilerParams(dimension_semantics=("arbitrary",), vmem_limit_bytes=VMEM_LIMIT),
    )(*args)


def _whole(a):
    nd = a.ndim
    return (a, a.shape, lambda i, *pre: (0,) * nd)


def _rows(a, tile, col=0, width=None, shift=0):
    width = a.shape[1] if width is None else width
    if shift:
        return (a, (tile, width), lambda i, *pre: (jnp.maximum(i + shift, 0), col))
    return (a, (tile, width), lambda i, *pre: (i, col))


def _row_out(n_rows, width, dtype, tile):
    return ((n_rows, width), dtype, (tile, width), lambda i, *pre: (i, 0))


def _pick_tile(n, prefs):
    for t in prefs:
        if n % t == 0:
            return t
    return n


def _matmul(name, a, b, mode, out_dtypes, epilogue=None, extras=(), place=None, max_tm=1024):
    if mode == "nn":
        (m, k), n = a.shape, b.shape[1]
    elif mode == "nt":
        (m, k), n = a.shape, b.shape[0]
    else:
        (k, m), n = a.shape, b.shape[1]
    tm = _pick_tile(m, tuple(t for t in (1024, 512, 256, 128) if t <= max_tm))
    tn = _pick_tile(n, (1024, 896, 512, 256, 128))
    tk = _pick_tile(k, (2048, 896, 512, 256, 128))
    nk = k // tk
    n_ex = len(extras)
    n_out = len(out_dtypes)

    def kern(*refs):
        a_ref, b_ref = refs[0], refs[1]
        ex_refs = refs[2:2 + n_ex]
        out_refs = refs[2 + n_ex:2 + n_ex + n_out]
        acc_ref = refs[2 + n_ex + n_out]
        kk = pl.program_id(2)

        @pl.when(kk == 0)
        def _():
            acc_ref[...] = jnp.zeros(acc_ref.shape, F32)

        acc_ref[...] += lax.dot_general(a_ref[...], b_ref[...], _DIMS[mode], preferred_element_type=F32)

        @pl.when(kk == nk - 1)
        def _():
            acc = acc_ref[...]
            vals = (acc,) if epilogue is None else epilogue(acc, *[r[...] for r in ex_refs])
            for r, v in zip(out_refs, vals, strict=True):
                r[...] = v.astype(r.dtype)

    if mode == "tn":
        a_spec = pl.BlockSpec((tk, tm), lambda i, j, kk: (kk, i))
    else:
        a_spec = pl.BlockSpec((tm, tk), lambda i, j, kk: (i, kk))
    if mode == "nt":
        b_spec = pl.BlockSpec((tn, tk), lambda i, j, kk: (j, kk))
    else:
        b_spec = pl.BlockSpec((tk, tn), lambda i, j, kk: (kk, j))
    tile = pl.BlockSpec((tm, tn), lambda i, j, kk: (i, j))
    out_specs = [tile] * n_out
    out_shape = [jax.ShapeDtypeStruct((m, n), d) for d in out_dtypes]
    if place is not None:
        shape, block_fn, index_fn = place
        out_specs = [pl.BlockSpec(block_fn(tm, tn), lambda i, j, kk: index_fn(i, j, tm, tn))]
        out_shape = [jax.ShapeDtypeStruct(shape, out_dtypes[0])]
    return pl.pallas_call(
        kern, name=name, grid=(m // tm, n // tn, nk),
        in_specs=[a_spec, b_spec] + [tile] * n_ex,
        out_specs=out_specs, out_shape=out_shape,
        scratch_shapes=[pltpu.VMEM((tm, tn), F32)],
        compiler_params=pltpu.CompilerParams(
            dimension_semantics=("parallel", "parallel", "arbitrary"), vmem_limit_bytes=VMEM_LIMIT),
    )(a, b, *extras)


ROW_TILE = 256


def _rms_fwd(name, x, w):
    s, d = x.shape
    t = _pick_tile(s, (ROW_TILE,))

    def body(i, v, c):
        return [_rms(v[0], v[1])], [], []

    return _seq_call(name, body, steps=s // t, ins=[_rows(x, t), _whole(w)], outs=[_row_out(s, d, BF16, t)])[0]


def _resid_rms_fwd(name, x, y, w):
    s, d = x.shape
    t = _pick_tile(s, (ROW_TILE,))

    def body(i, v, c):
        return [v[0] + _rms(v[1], v[2])], [], []

    return _seq_call(name, body, steps=s // t, ins=[_rows(x, t), _rows(y, t), _whole(w)],
                     outs=[_row_out(s, d, F32, t)])[0]


def _rms_bwd(name, x, w, dh, dres):
    s, d = x.shape
    t = _pick_tile(s, (ROW_TILE,))

    def body(i, v, c):
        _, vjp = jax.vjp(_rms, v[0], v[1])
        dx, dw = vjp(v[2])
        return [dx + v[3]], [dw], []

    return _seq_call(name, body, steps=s // t, ins=[_rows(x, t), _whole(w), _rows(dh, t), _rows(dres, t)],
                     outs=[_row_out(s, d, F32, t)], accs=[((1, d), F32)])


def _resid_rms_bwd(name, y, w, dxo):
    s, d = y.shape
    t = _pick_tile(s, (ROW_TILE,))

    def body(i, v, c):
        _, vjp = jax.vjp(_rms, v[0], v[1])
        dy, dw = vjp(v[2])
        return [dy], [dw], []

    return _seq_call(name, body, steps=s // t, ins=[_rows(y, t), _whole(w), _rows(dxo, t)],
                     outs=[_row_out(s, d, BF16, t)], accs=[((1, d), F32)])


def _loss_head(name, y, target):
    s, d = y.shape
    t = _pick_tile(s, (ROW_TILE,))

    def body(i, v, c):
        err = v[0] - v[1]
        part = 0.5 * jnp.sum(jnp.mean(err * err, axis=-1, keepdims=True))
        return [err * (1.0 / d)], [jnp.full((1, LANES), part, F32)], []

    return _seq_call(name, body, steps=s // t, ins=[_rows(y, t), _rows(target, t)],
                     outs=[_row_out(s, d, F32, t)], accs=[((1, LANES), F32)])


SEG = {n: i for i, n in enumerate(
    ["a_q", "a_f", "a_i", "a_g", "b_q", "b_k", "b_v", "b_z", "c_u", "c_v", "d_a", "d_gate"])}
AB_COL = 12 * GROUP_WIDTH // LANES


def _seg(proj, name, tile, shift=0):
    return _rows(proj, tile, col=SEG[name], width=GROUP_WIDTH, shift=shift)


def _state_block():
    return (1, N_HEADS * HEAD_DIM, HEAD_DIM), lambda i, *pre: (i, 0, 0)


def _split_states(blk):
    return [blk[0, HEAD_DIM * h:HEAD_DIM * (h + 1), :] for h in range(N_HEADS)]


STATE_CARRIES = [((HEAD_DIM, HEAD_DIM), F32)] * N_HEADS


def _hgrn_fwd(layer, proj, lbp, nw):
    s = proj.shape[0]
    n = s // HGRN_TILE
    sb, sf = _state_block()

    def body(i, v, st):
        o, new = _hgrn_tile(layer, v[0], v[1], v[2], v[3], v[4], v[5], st)
        return [o, jnp.concatenate(st, axis=0)[None]], [], new

    return _seq_call(
        f"hgrn_fwd{layer}", body, steps=n,
        ins=[_whole(lbp), _whole(nw)] + [_seg(proj, k, HGRN_TILE) for k in ("a_q", "a_f", "a_i", "a_g")],
        outs=[_row_out(s, GROUP_WIDTH, BF16, HGRN_TILE), ((n, N_HEADS * HEAD_DIM, HEAD_DIM), F32, sb, sf)],
        carries=STATE_CARRIES)


def _hgrn_bwd(layer, proj, lbp, nw, states, dmix):
    s = proj.shape[0]
    n = s // HGRN_TILE
    sb, sf = _state_block()

    def body(i, v, dst):
        st = _split_states(v[6])

        def f(lbp_, nw_, aq, af, ai, ag, *st_):
            return _hgrn_tile(layer, lbp_, nw_, aq, af, ai, ag, list(st_))

        _, vjp = jax.vjp(f, v[0], v[1], v[2], v[3], v[4], v[5], *st)
        g = vjp((v[7], list(dst)))
        return [jnp.concatenate(g[2:6], axis=1)], [g[0], g[1]], list(g[6:])

    return _seq_call(
        f"hgrn_bwd{layer}", body, steps=n, reverse=True,
        ins=[_whole(lbp), _whole(nw)] + [_seg(proj, k, HGRN_TILE) for k in ("a_q", "a_f", "a_i", "a_g")]
        + [(states, sb, sf), _rows(dmix, HGRN_TILE, col=0, width=GROUP_WIDTH)],
        outs=[_row_out(s, 4 * GROUP_WIDTH, BF16, HGRN_TILE)],
        accs=[(lbp.shape, F32), (nw.shape, F32)], carries=STATE_CARRIES)


def _gdn_ins(proj, cw, alog, dtb, nw):
    return ([_whole(cw), _whole(alog), _whole(dtb), _whole(nw)]
            + [_seg(proj, k, GDN_TILE, shift=-1) for k in ("b_q", "b_k", "b_v")]
            + [_seg(proj, k, GDN_TILE) for k in ("b_q", "b_k", "b_v", "b_z")]
            + [_rows(proj, GDN_TILE, col=AB_COL, width=LANES)])


def _mask_prev(i, vals):
    keep = (i > 0).astype(F32)
    return [p * keep for p in vals]


def _gdn_fwd(layer, proj, cw, alog, dtb, nw):
    s = proj.shape[0]
    n = s // GDN_TILE
    sb, sf = _state_block()

    def body(i, v, st):
        prev = _mask_prev(i, v[4:7])
        o, new = _gdn_tile(v[0], v[1], v[2], v[3], *prev, *v[7:12], st)
        return [o, jnp.concatenate(st, axis=0)[None]], [], new

    return _seq_call(
        f"gdn_fwd{layer}", body, steps=n, ins=_gdn_ins(proj, cw, alog, dtb, nw),
        outs=[_row_out(s, GROUP_WIDTH, BF16, GDN_TILE), ((n, N_HEADS * HEAD_DIM, HEAD_DIM), F32, sb, sf)],
        carries=STATE_CARRIES)


def _gdn_bwd(layer, proj, cw, alog, dtb, nw, states, dmix):
    s = proj.shape[0]
    n = s // GDN_TILE
    sb, sf = _state_block()

    def body(i, v, car):
        dst, dprev = car[:N_HEADS], car[N_HEADS:]
        prev = _mask_prev(i, v[4:7])
        st = _split_states(v[12])

        def f(cw_, alog_, dtb_, nw_, pq, pk, pv, cq, ck, cv, bz, ab, *st_):
            return _gdn_tile(cw_, alog_, dtb_, nw_, pq, pk, pv, cq, ck, cv, bz, ab, list(st_))

        _, vjp = jax.vjp(f, v[0], v[1], v[2], v[3], *prev, *v[7:12], *st)
        g = vjp((v[13], list(dst)))
        dcur = [g[7] + dprev[0], g[8] + dprev[1], g[9] + dprev[2], g[10]]
        return ([jnp.concatenate(dcur, axis=1), g[11]], list(g[0:4]), list(g[12:]) + list(g[4:7]))

    return _seq_call(
        f"gdn_bwd{layer}", body, steps=n, reverse=True,
        ins=_gdn_ins(proj, cw, alog, dtb, nw) + [(states, sb, sf), _rows(dmix, GDN_TILE, col=1, width=GROUP_WIDTH)],
        outs=[_row_out(s, 4 * GROUP_WIDTH, BF16, GDN_TILE), _row_out(s, LANES, BF16, GDN_TILE)],
        accs=[(cw.shape, F32), (alog.shape, F32), (dtb.shape, F32), (nw.shape, F32)],
        carries=STATE_CARRIES + [((GDN_TILE, GROUP_WIDTH), F32)] * 3)


def _gmlp_fwd(layer, proj, ln_w, ln_b, ws, bs_cols):
    s = proj.shape[0]

    def body(i, v, c):
        return [_gmlp_tile(*v)], [], []

    return _seq_call(
        f"gmlp_fwd{layer}", body, steps=s // MIX_CHUNK,
        ins=[_whole(ln_w), _whole(ln_b), _whole(ws), _whole(bs_cols),
             _seg(proj, "c_u", MIX_CHUNK), _seg(proj, "c_v", MIX_CHUNK)],
        outs=[_row_out(s, GROUP_WIDTH, BF16, MIX_CHUNK)])[0]


def _gmlp_bwd(layer, proj, ln_w, ln_b, ws, bs_cols, dmix):
    s = proj.shape[0]

    def body(i, v, c):
        _, vjp = jax.vjp(_gmlp_tile, *v[:6])
        g = vjp(v[6])
        return [jnp.concatenate(g[4:6], axis=1)], list(g[0:4]), []

    return _seq_call(
        f"gmlp_bwd{layer}", body, steps=s // MIX_CHUNK,
        ins=[_whole(ln_w), _whole(ln_b), _whole(ws), _whole(bs_cols),
             _seg(proj, "c_u", MIX_CHUNK), _seg(proj, "c_v", MIX_CHUNK),
             _rows(dmix, MIX_CHUNK, col=2, width=GROUP_WIDTH)],
        outs=[_row_out(s, 2 * GROUP_WIDTH, BF16, MIX_CHUNK)],
        accs=[(ln_w.shape, F32), (ln_b.shape, F32), (ws.shape, F32), (bs_cols.shape, F32)])


def _conformer_ins(proj, dw_w, dw_b, ln_w, ln_b):
    return ([_whole(dw_w), _whole(dw_b), _whole(ln_w), _whole(ln_b)]
            + [_seg(proj, k, CONV_TILE, shift=-1) for k in ("d_a", "d_gate")]
            + [_seg(proj, k, CONV_TILE) for k in ("d_a", "d_gate")])


def _conformer_fwd(layer, proj, dw_w, dw_b, ln_w, ln_b):
    s = proj.shape[0]

    def body(i, v, c):
        prev = _mask_prev(i, v[4:6])
        return [_conformer_tile(v[0], v[1], v[2], v[3], *prev, v[6], v[7])], [], []

    return _seq_call(
        f"conformer_fwd{layer}", body, steps=s // CONV_TILE, ins=_conformer_ins(proj, dw_w, dw_b, ln_w, ln_b),
        outs=[_row_out(s, GROUP_WIDTH, BF16, CONV_TILE)])[0]


def _conformer_bwd(layer, proj, dw_w, dw_b, ln_w, ln_b, dmix):
    s = proj.shape[0]

    def body(i, v, dprev):
        prev = _mask_prev(i, v[4:6])
        _, vjp = jax.vjp(_conformer_tile, v[0], v[1], v[2], v[3], *prev, v[6], v[7])
        g = vjp(v[8])
        return [jnp.concatenate([g[6] + dprev[0], g[7] + dprev[1]], axis=1)], list(g[0:4]), [g[4], g[5]]

    return _seq_call(
        f"conformer_bwd{layer}", body, steps=s // CONV_TILE, reverse=True,
        ins=_conformer_ins(proj, dw_w, dw_b, ln_w, ln_b) + [_rows(dmix, CONV_TILE, col=3, width=GROUP_WIDTH)],
        outs=[_row_out(s, 2 * GROUP_WIDTH, BF16, CONV_TILE)],
        accs=[(dw_w.shape, F32), (dw_b.shape, F32), (ln_w.shape, F32), (ln_b.shape, F32)],
        carries=[((CONV_TILE, GROUP_WIDTH), F32)] * 2)


SMALL = ["lower_bounds", "norm_mix_pre", "norm_mix_post", "norm_ff_pre", "norm_ff_post", "hgrn_norm_w",
         "gdn_conv_w", "gdn_a_log", "gdn_dt_bias", "gdn_norm_w", "gmlp_ln_w", "gmlp_ln_b", "gmlp_w_s",
         "gmlp_b_s", "conv_dw_w", "conv_dw_b", "conv_ln_w", "conv_ln_b"]


def _gate_row(v):
    return jnp.pad(v.reshape(1, N_HEADS), ((0, 0), (N_HEADS, LANES - 2 * N_HEADS)))


def _cut_rows(rows, cols):
    return ((N_CHIPS, rows, cols), lambda tm, tn: (None, tm, tn),
            lambda i, j, tm, tn: (i // (rows // tm), i % (rows // tm), j))


def _cut_cols(rows, cols):
    return ((N_CHIPS, rows, cols), lambda tm, tn: (None, tm, tn),
            lambda i, j, tm, tn: (j // (cols // tn), i, j % (cols // tn)))


def _relu2(acc):
    r = jnp.maximum(acc, 0.0)
    return acc, r * r


def _relu2_bwd(acc, u):
    return (2.0 * jnp.maximum(u, 0.0) * acc,)


def _local_step(x, target, sp, wts, red):
    row = lambda v: v.reshape(1, -1)
    saved = []
    w_in, w_out, w_ff1, w_ff2 = [], [], [], []
    for l in range(DEPTH):
        par = dict(
            lbp=sp["lower_bounds"], hn=row(sp["hgrn_norm_w"][l]), cw=sp["gdn_conv_w"][l],
            alog=_gate_row(sp["gdn_a_log"][l]), dtb=_gate_row(sp["gdn_dt_bias"][l]), gn=row(sp["gdn_norm_w"][l]),
            glw=row(sp["gmlp_ln_w"][l]), glb=row(sp["gmlp_ln_b"][l]), ws=sp["gmlp_w_s"][l],
            bsc=jnp.pad(sp["gmlp_b_s"][l].T, ((0, 0), (0, LANES - N_HEADS))),
            dww=sp["conv_dw_w"][l], dwb=row(sp["conv_dw_b"][l]), clw=row(sp["conv_ln_w"][l]),
            clb=row(sp["conv_ln_b"][l]), n1=row(sp["norm_mix_pre"][l]), n2=row(sp["norm_mix_post"][l]),
            n3=row(sp["norm_ff_pre"][l]), n4=row(sp["norm_ff_post"][l]))
        h = _rms_fwd(f"norm_mix_pre{l}", x, par["n1"])
        w_in.append(wts.get("w_in", l))
        proj = _matmul(f"in_proj{l}", h, w_in[l], "nn", [F32])[0]
        wts.mark(f"proj{l}", proj)
        o_a, st_a = _hgrn_fwd(l, proj, par["lbp"], par["hn"])
        o_b, st_b = _gdn_fwd(l, proj, par["cw"], par["alog"], par["dtb"], par["gn"])
        wts.mark(f"gdn{l}", o_b)
        o_c = _gmlp_fwd(l, proj, par["glw"], par["glb"], par["ws"], par["bsc"])
        o_d = _conformer_fwd(l, proj, par["dww"], par["dwb"], par["clw"], par["clb"])
        wts.mark(f"conformer{l}", o_d)
        mix = jnp.concatenate([o_a, o_b, o_c, o_d], axis=1)
        w_out.append(wts.get("w_out", l))
        y = _matmul(f"out_proj{l}", mix, w_out[l], "nn", [F32])[0]
        wts.mark(f"y{l}", y)
        x1 = _resid_rms_fwd(f"norm_mix_post{l}", x, y, par["n2"])
        h2 = _rms_fwd(f"norm_ff_pre{l}", x1, par["n3"])
        w_ff1.append(wts.get("w_ff1", l))
        u, act = _matmul(f"ff1_{l}", h2, w_ff1[l], "nn", [F32, BF16], epilogue=_relu2)
        wts.mark(f"u{l}", u)
        w_ff2.append(wts.get("w_ff2", l))
        y2 = _matmul(f"ff2_{l}", act, w_ff2[l], "nn", [F32])[0]
        x2 = _resid_rms_fwd(f"norm_ff_post{l}", x1, y2, par["n4"])
        saved.append(dict(par=par, x=x, h=h, proj=proj, st_a=st_a, st_b=st_b, mix=mix, y=y, x1=x1, h2=h2,
                          u=u, act=act, y2=y2))
        x = x2

    dx, loss_acc = _loss_head("loss_head", x, target)
    loss_part = loss_acc[0, 0]

    gs = {k: [None] * DEPTH for k in SMALL if k != "lower_bounds"}
    g_lb = jnp.zeros((DEPTH, GROUP_WIDTH), F32)
    def behind(row, token):
        return row if token is None else row + token[:1, :1]

    tok = None
    for l in reversed(range(DEPTH)):
        sv = saved[l]
        par = sv["par"]
        dy2, dn4 = _resid_rms_bwd(f"norm_ff_post_bwd{l}", sv["y2"], behind(par["n4"], tok), dx)
        du = _matmul(f"ff2_dx{l}", dy2, w_ff2[l], "nt", [BF16], epilogue=_relu2_bwd, extras=(sv["u"],))[0]
        g_ff2 = _matmul(f"ff2_dw{l}", sv["act"], dy2, "tn", [F32], place=_cut_rows(D_FF // N_CHIPS, D_MODEL))[0]
        tok = red.mark(f"ff2_dw{l}", g_ff2)
        g_ff1 = _matmul(f"ff1_dw{l}", sv["h2"], du, "tn", [F32], place=_cut_cols(D_MODEL, D_FF // N_CHIPS))[0]
        dh2 = _matmul(f"ff1_dx{l}", du, w_ff1[l], "nt", [F32])[0]
        dx1, dn3 = _rms_bwd(f"norm_ff_pre_bwd{l}", sv["x1"], behind(par["n3"], tok), dh2, dx)
        dy, dn2 = _resid_rms_bwd(f"norm_mix_post_bwd{l}", sv["y"], par["n2"], dx1)
        dmix = _matmul(f"out_proj_dx{l}", dy, w_out[l], "nt", [F32])[0]
        g_out = _matmul(f"out_proj_dw{l}", sv["mix"], dy, "tn", [F32], max_tm=GROUP_WIDTH,
                        place=_cut_rows(GROUP_WIDTH, D_MODEL))[0]
        proj = sv["proj"]
        dp_a, dlb, dhn = _hgrn_bwd(l, proj, par["lbp"], par["hn"], sv["st_a"], dmix)
        dp_b, dp_ab, dcw, dalog, ddtb, dgn = _gdn_bwd(
            l, proj, par["cw"], par["alog"], par["dtb"], par["gn"], sv["st_b"], dmix)
        dp_c, dglw, dglb, dws, dbsc = _gmlp_bwd(l, proj, par["glw"], par["glb"], par["ws"], par["bsc"], dmix)
        dp_d, ddww, ddwb, dclw, dclb = _conformer_bwd(
            l, proj, par["dww"], par["dwb"], par["clw"], par["clb"], dmix)
        tok = red.mark(f"conformer_bwd{l}", dp_d)
        dproj = jnp.concatenate([dp_a, dp_b, dp_c, dp_d, dp_ab], axis=1)
        g_in = _matmul(f"in_proj_dw{l}", sv["h"], dproj, "tn", [F32])[0]
        dh = _matmul(f"in_proj_dx{l}", dproj, w_in[l], "nt", [F32])[0]
        red.mark(f"in_proj_dx{l}", dh)
        dx, dn1 = _rms_bwd(f"norm_mix_pre_bwd{l}", sv["x"], behind(par["n1"], tok), dh, dx1)
        tok = red.layer_done(l, dict(w_in=g_in, w_out=g_out, w_ff1=g_ff1, w_ff2=g_ff2), dx)
        g_lb = g_lb + dlb
        for k, v in dict(
                norm_mix_pre=dn1[0], norm_mix_post=dn2[0], norm_ff_pre=dn3[0], norm_ff_post=dn4[0],
                hgrn_norm_w=dhn[0], gdn_conv_w=dcw, gdn_a_log=dalog[0, N_HEADS:2 * N_HEADS],
                gdn_dt_bias=ddtb[0, N_HEADS:2 * N_HEADS], gdn_norm_w=dgn[0], gmlp_ln_w=dglw[0],
                gmlp_ln_b=dglb[0], gmlp_w_s=dws, gmlp_b_s=dbsc[:, :N_HEADS].T, conv_dw_w=ddww,
                conv_dw_b=ddwb[0], conv_ln_w=dclw[0], conv_ln_b=dclb[0]).items():
            gs[k][l] = v
    small_grads = {k: jnp.stack(v) for k, v in gs.items()}
    small_grads["lower_bounds"] = g_lb
    return loss_part, dx, small_grads


HBM = pl.BlockSpec(memory_space=pl.ANY)


def _place():
    return lax.axis_index("x"), lax.axis_index("y"), lax.axis_index("c")


def _other_chips(x, y):
    chips = [(1 - x, y), (x, 1 - y), (1 - x, 1 - y)]
    return [(px, py, 2 * px + py) for px, py in chips]


SEM = pl.BlockSpec(memory_space=pltpu.SEMAPHORE)
IN_HBM = pl.BlockSpec(memory_space=pltpu.HBM)
EFFECT = pltpu.SideEffectType.DATAFLOW_SIDE_EFFECTING


def _copies_start(name, plan, bufs, after, n=3):
    nb = len(bufs)

    def body(*refs):
        token = refs[-1]
        for started, _ in plan(refs[:nb], refs[nb + 1], refs[nb + 2]):
            started.start()
        token[...] = jnp.zeros(token.shape, token.dtype)

    out = pl.pallas_call(
        body, name=name,
        out_shape=(pltpu.SemaphoreType.DMA((n,)), pltpu.SemaphoreType.DMA((n,)))
        + tuple(pltpu.HBM(b.shape, b.dtype) for b in bufs) + (jax.ShapeDtypeStruct((8, LANES), F32),),
        in_specs=(IN_HBM,) * nb + (HBM,),
        out_specs=(SEM, SEM) + (IN_HBM,) * nb + (pl.BlockSpec(memory_space=pltpu.VMEM),),
        input_output_aliases={i: 2 + i for i in range(nb)},
        compiler_params=pltpu.CompilerParams(has_side_effects=EFFECT),
    )(*[pltpu.with_memory_space_constraint(b, pltpu.HBM) for b in bufs], after)
    return out[0], out[1], list(out[2:2 + nb]), out[-1]


def _copies_wait(name, plan, send_sems, recv_sems, bufs, after):
    nb = len(bufs)

    def body(*refs):
        for started, arriving in plan(refs[:nb], refs[nb], refs[nb + 1]):
            started.wait_send()
            arriving.wait_recv()

    out = pl.pallas_call(
        body, name=name, out_shape=tuple(pltpu.HBM(b.shape, b.dtype) for b in bufs),
        in_specs=(IN_HBM,) * nb + (SEM, SEM, HBM), out_specs=(IN_HBM,) * nb,
        input_output_aliases={i: i for i in range(nb)},
        compiler_params=pltpu.CompilerParams(has_side_effects=EFFECT),
    )(*bufs, send_sems, recv_sems, after)
    return list(out)


def _ici_plan(bufs, send_sems, recv_sems):
    blk, land = bufs
    x, y, c = _place()
    mine = 2 * x + y
    plan = []
    for j, (px, py, k) in enumerate(_other_chips(x, y)):
        def copy(dst, j=j, to=(px, py, c)):
            return pltpu.make_async_remote_copy(
                src_ref=blk.at[c], dst_ref=dst, send_sem=send_sems.at[j], recv_sem=recv_sems.at[j],
                device_id=to, device_id_type=MESH)
        plan.append((copy(land.at[mine, c]), copy(land.at[k, c])))
    return plan


def _d2d_plan(bufs, send_sems, recv_sems):
    (land,) = bufs
    x, y, c = _place()
    plan = []
    for j, (_, _, k) in enumerate(_other_chips(x, y)):
        def copy(layer, j=j, k=k):
            return pltpu.make_async_remote_copy(
                src_ref=land.at[k, c], dst_ref=land.at[k, layer], send_sem=send_sems.at[j],
                recv_sem=recv_sems.at[j], device_id=(x, y, 1 - c), device_id_type=MESH)
        plan.append((copy(c), copy(1 - c)))
    return plan


class _GatheredWeights:
    STAGES = {"w_out": ("proj0", "conformer0"), "w_ff1": ("gdn0", "y0"), "w_ff2": ("u0", None)}

    def __init__(self, chip, shards):
        self.chip = chip
        self.blk, self.ici, self.d2d, self.full, self.mats = {}, {}, {}, {}, {}
        token = jnp.zeros((8, LANES), F32)
        for k in ("w_in", "w_out", "w_ff1", "w_ff2"):
            self.blk[k] = shards[k].astype(BF16)
            land = lax.empty((N_CHIPS,) + self.blk[k].shape, BF16)
            send, recv, bufs, token = _copies_start(f"gather_{k}_ici", _ici_plan, [self.blk[k], land], token)
            self.ici[k] = (send, recv, bufs)
        self._hand_over("w_in", token)
        self._finish("w_in", token)

    def _hand_over(self, k, after):
        send, recv, bufs = self.ici.pop(k)
        _, land = _copies_wait(f"gather_{k}_ici_done", _ici_plan, send, recv, bufs, after)
        send, recv, bufs, _ = _copies_start(f"gather_{k}_d2d", _d2d_plan, [land], after)
        self.d2d[k] = (send, recv, bufs)

    def _finish(self, k, after):
        send, recv, bufs = self.d2d.pop(k)
        (land,) = _copies_wait(f"gather_{k}_d2d_done", _d2d_plan, send, recv, bufs, after)
        self.full[k] = lax.dynamic_update_slice(land, self.blk[k][None], (self.chip, 0, 0, 0))

    def mark(self, tag, value):
        for k, (first, second) in self.STAGES.items():
            if tag == first:
                self._hand_over(k, value)
                if second is None:
                    self._finish(k, value)
            elif tag == second:
                self._finish(k, value)

    def get(self, kind, l):
        if (kind, l) not in self.mats:
            a = self.full[kind]
            if kind == "w_in":
                m = _reorder_in(jnp.concatenate([a[k, l] for k in range(N_CHIPS)], axis=-1))
            elif kind == "w_ff1":
                m = jnp.concatenate([a[k, l] for k in range(N_CHIPS)], axis=-1)
            else:
                m = a[:, l].reshape(-1, D_MODEL)
            self.mats[kind, l] = m
        return self.mats[kind, l]


def _exchange_plan(bufs, send_sems, recv_sems):
    g, got = bufs
    x, y, c = _place()
    half = got.shape[1]
    cp = pltpu.make_async_remote_copy(
        src_ref=g.at[pl.ds(0, N_CHIPS), pl.ds((1 - c) * half, half)], dst_ref=got, send_sem=send_sems.at[0],
        recv_sem=recv_sems.at[0], device_id=(x, y, 1 - c), device_id_type=MESH)
    return [(cp, cp)]


def _scatter_plan(bufs, send_sems, recv_sems):
    p, rcv = bufs
    x, y, c = _place()
    plan = []
    for j, (px, py, _) in enumerate(_other_chips(x, y)):
        cp = pltpu.make_async_remote_copy(
            src_ref=p.at[j], dst_ref=rcv.at[j], send_sem=send_sems.at[j], recv_sem=recv_sems.at[j],
            device_id=(px, py, c), device_id_type=MESH)
        plan.append((cp, cp))
    return plan


def _share_plan(bufs, send_sems, recv_sems):
    mine, sib = bufs
    x, y, c = _place()
    cp = pltpu.make_async_remote_copy(
        src_ref=mine, dst_ref=sib, send_sem=send_sems.at[0], recv_sem=recv_sems.at[0],
        device_id=(x, y, 1 - c), device_id_type=MESH)
    return [(cp, cp)]


def _gather_small(name, pack):
    def body(p_ref, out_ref, send_sems, recv_sems, local_sem):
        x, y, c = _place()
        me = 4 * x + 2 * y + c
        local = pltpu.make_async_copy(p_ref, out_ref.at[me], local_sem)
        local.start()
        flips = [(fx, fy, fc) for fx in (0, 1) for fy in (0, 1) for fc in (0, 1)][1:]
        peers = [((1 - x) if fx else x, (1 - y) if fy else y, (1 - c) if fc else c) for fx, fy, fc in flips]

        def cp(j, slot, to):
            return pltpu.make_async_remote_copy(
                src_ref=p_ref, dst_ref=out_ref.at[slot], send_sem=send_sems.at[j], recv_sem=recv_sems.at[j],
                device_id=to, device_id_type=MESH)

        sends = [cp(j, me, to) for j, to in enumerate(peers)]
        for s in sends:
            s.start()
        for j, (px, py, pc) in enumerate(peers):
            cp(j, 4 * px + 2 * py + pc, (px, py, pc)).wait_recv()
        for s in sends:
            s.wait_send()
        local.wait()

    return pl.pallas_call(
        body, name=name, out_shape=jax.ShapeDtypeStruct((N_DEV,) + pack.shape, pack.dtype), in_specs=[HBM],
        out_specs=HBM,
        scratch_shapes=[pltpu.SemaphoreType.DMA((7,)), pltpu.SemaphoreType.DMA((7,)), pltpu.SemaphoreType.DMA],
    )(pack)


SLAB_ROWS = 256


I_CORE, I_CHIP, I_PEER = 0, 1, 2


def _peer_chip(pre, j):
    return jnp.where(j == 0, pre[I_PEER][0], jnp.where(j == 1, pre[I_PEER + 1][0], pre[I_PEER + 2][0]))


def _pair_sum(name, g, got, where):
    _, half, c = got.shape
    t = _pick_tile(half, (SLAB_ROWS, 128, 64, 8))
    per = half // t

    def body(i, v, car, pre):
        return [v[0] + v[1]], [], []

    return _seq_call(
        name, body, steps=3 * per, prefetch=where,
        ins=[(g, (None, t, c), lambda i, pre: (_peer_chip(pre, i // per), pre[I_CORE][0] * per + i % per, 0)),
             (got, (None, t, c), lambda i, pre: (_peer_chip(pre, i // per), i % per, 0))],
        outs=[((3, half, c), BF16, (None, t, c), lambda i, pre: (i // per, i % per, 0))])[0]


def _chip_sum(name, g, got, rcv, where):
    _, half, c = got.shape
    t = _pick_tile(half, (SLAB_ROWS, 128, 64, 8))
    per = half // t

    def body(i, v, car, pre):
        acc = v[0] + v[1]
        for part in v[2:]:
            acc = acc + part.astype(F32)
        return [acc], [], []

    return _seq_call(
        name, body, steps=per, prefetch=where,
        ins=[(g, (None, t, c), lambda i, pre: (pre[I_CHIP][0], pre[I_CORE][0] * per + i, 0)),
             (got, (None, t, c), lambda i, pre: (pre[I_CHIP][0], i, 0))]
        + [(rcv, (None, t, c), (lambda i, pre, _j=j: (_j, i, 0))) for j in range(3)],
        outs=[((half, c), F32, (t, c), lambda i, pre: (i, 0))])[0]


def _adamw_math(w, g, m, v):
    m = ADAM_B1 * m + (1.0 - ADAM_B1) * g
    v = ADAM_B2 * v + (1.0 - ADAM_B2) * (g * g)
    m_hat = m / (1.0 - ADAM_B1 ** ADAM_STEP)
    v_hat = v / (1.0 - ADAM_B2 ** ADAM_STEP)
    delta = -ADAM_LR * (m_hat / (jnp.sqrt(v_hat) + ADAM_EPS) + ADAM_WD * w)
    return delta, m, v


def _adamw(name, w, g, m, v):
    n, r, c = w.shape
    t = _pick_tile(r, (SLAB_ROWS, 128, 64, 8))
    per = r // t

    def body(i, vals, car):
        return list(_adamw_math(*vals)), [], []

    blk = lambda a: (a, (None, t, c), lambda i: (i // per, i % per, 0))
    out = ((n, r, c), F32, (None, t, c), lambda i: (i // per, i % per, 0))
    return _seq_call(name, body, steps=n * per, ins=[blk(w), blk(g), blk(m), blk(v)], outs=[out] * 3)


def _adamw_layer(name, layer, w, mine, sib, m, v, where, into):
    n, r, c = w.shape
    half = r // 2
    t = _pick_tile(half, (SLAB_ROWS, 128, 64, 8))
    per = half // t

    def body(i, vals, car, pre):
        g = jnp.where(i // per == pre[I_CORE][0], vals[1], vals[2])
        return [g] + list(_adamw_math(vals[0], g, vals[3], vals[4])), [], []

    of_layer = lambda a: (a, (None, t, c), lambda i, pre: (layer, i, 0))
    halves = lambda a: (a, (t, c), lambda i, pre: (i % per, 0))
    out = ((n, r, c), F32, (None, t, c), lambda i, pre: (layer, i, 0))
    return _seq_call(name, body, steps=2 * per, prefetch=where, into=into,
                     ins=[of_layer(w), halves(mine), halves(sib), of_layer(m), of_layer(v)], outs=[out] * 4)


def _ordered_sum(name, packs):
    n, r, c = packs.shape

    def body(i, v, car):
        acc = v[0][0]
        for k in range(1, n):
            acc = acc + v[0][k]
        return [acc], [], []

    return _seq_call(name, body, steps=1, ins=[_whole(packs)], outs=[((r, c), F32, (r, c), lambda i: (0, 0))])[0]


def _pack(arrays):
    flat = []
    for a in arrays:
        a = a.reshape(-1).astype(F32)
        pad = (-a.shape[0]) % LANES
        flat.append(jnp.pad(a, (0, pad)) if pad else a)
    v = jnp.concatenate(flat)
    pad = (-v.shape[0]) % (64 * LANES)
    if pad:
        v = jnp.pad(v, (0, pad))
    return v.reshape(-1, LANES)


def _unpack(pack, shapes):
    v = pack.reshape(-1)
    out, off = [], 0
    for s in shapes:
        n = math.prod(s)
        out.append(v[off:off + n].reshape(s))
        off += n + ((-n) % LANES)
    return out


def _reorder_in(full):
    g0 = 8 * GROUP_WIDTH
    pad = jnp.zeros(full.shape[:-1] + (LANES - 2 * N_HEADS,), full.dtype)
    return jnp.concatenate([full[..., :g0], full[..., g0 + 2 * N_HEADS:], full[..., g0:g0 + 2 * N_HEADS], pad], axis=-1)


def _restore_in(padded):
    g0 = 8 * GROUP_WIDTH
    wide = 12 * GROUP_WIDTH
    return jnp.concatenate([padded[..., :g0], padded[..., wide:wide + 2 * N_HEADS], padded[..., g0:wide]], axis=-1)


BIG = ("w_in", "w_out", "w_ff1", "w_ff2")


class _GradReducer:
    STEPS = {"ff2_dw0": "_add_pairs", "conformer_bwd0": "_add_chips", "in_proj_dx0": "_finish"}

    def __init__(self, where, state):
        self.where, self.state = where, state
        self.out = {k: () for k in BIG}
        self.layer, self.flight = None, {}

    def layer_done(self, l, grads, after):
        self.layer = l
        tokens = []
        for k in BIG:
            g = grads[k]
            if k == "w_in":
                g = jnp.stack(jnp.split(_restore_in(g), N_CHIPS, axis=-1))
            got = lax.empty((N_CHIPS, g.shape[1] // 2, g.shape[2]), F32)
            send, recv, bufs, token = _copies_start(f"swap_halves_{k}{l}", _exchange_plan, [g, got], after, n=1)
            self.flight[k] = (send, recv, bufs)
            tokens.append(token)
        if l == 0:
            for step in ("_add_pairs", "_add_chips", "_finish"):
                getattr(self, step)(after)
        return functools.reduce(lambda a, b: a + b, tokens)

    def mark(self, tag, value):
        if self.layer == 1 and tag in self.STEPS:
            return getattr(self, self.STEPS[tag])(value)
        return None

    def _add_pairs(self, after):
        l = self.layer
        tokens = []
        for k in BIG:
            send, recv, bufs = self.flight[k]
            g, got = _copies_wait(f"swap_halves_{k}{l}_done", _exchange_plan, send, recv, bufs, after)
            pair = _pair_sum(f"pair_sum_{k}{l}", g, got, self.where)
            send, recv, bufs, token = _copies_start(
                f"scatter_{k}{l}", _scatter_plan, [pair, lax.empty(pair.shape, pair.dtype)], after)
            self.flight[k] = (send, recv, bufs, g, got)
            tokens.append(token)
        return functools.reduce(lambda a, b: a + b, tokens)

    def _add_chips(self, after):
        l = self.layer
        tokens = []
        for k in BIG:
            send, recv, bufs, g, got = self.flight[k]
            _, rcv = _copies_wait(f"scatter_{k}{l}_done", _scatter_plan, send, recv, bufs, after)
            mine = _chip_sum(f"chip_sum_{k}{l}", g, got, rcv, self.where)
            send, recv, bufs, token = _copies_start(
                f"share_{k}{l}", _share_plan, [mine, lax.empty(mine.shape, mine.dtype)], after, n=1)
            self.flight[k] = (send, recv, bufs)
            tokens.append(token)
        return functools.reduce(lambda a, b: a + b, tokens)

    def _finish(self, after):
        l = self.layer
        for k in BIG:
            send, recv, bufs = self.flight.pop(k)
            mine, sib = _copies_wait(f"share_{k}{l}_done", _share_plan, send, recv, bufs, after)
            self.out[k] = _adamw_layer(f"adamw_{k}{l}", l, self.state[k][0], mine, sib, self.state[k][1],
                                       self.state[k][2], self.where, self.out[k])


def kernel(x, lower_bounds, norm_mix_pre, norm_mix_post, norm_ff_pre, norm_ff_post, w_in, w_out, hgrn_norm_w, gdn_conv_w, gdn_a_log, gdn_dt_bias, gdn_norm_w, gmlp_ln_w, gmlp_ln_b, gmlp_w_s, gmlp_b_s, conv_dw_w, conv_dw_b, conv_ln_w, conv_ln_b, w_ff1, w_ff2, loss_target, m_lower_bounds, m_norm_mix_pre, m_norm_mix_post, m_norm_ff_pre, m_norm_ff_post, m_w_in, m_w_out, m_hgrn_norm_w, m_gdn_conv_w, m_gdn_a_log, m_gdn_dt_bias, m_gdn_norm_w, m_gmlp_ln_w, m_gmlp_ln_b, m_gmlp_w_s, m_gmlp_b_s, m_conv_dw_w, m_conv_dw_b, m_conv_ln_w, m_conv_ln_b, m_w_ff1, m_w_ff2, v_lower_bounds, v_norm_mix_pre, v_norm_mix_post, v_norm_ff_pre, v_norm_ff_post, v_w_in, v_w_out, v_hgrn_norm_w, v_gdn_conv_w, v_gdn_a_log, v_gdn_dt_bias, v_gdn_norm_w, v_gmlp_ln_w, v_gmlp_ln_b, v_gmlp_w_s, v_gmlp_b_s, v_conv_dw_w, v_conv_dw_b, v_conv_ln_w, v_conv_ln_b, v_w_ff1, v_w_ff2):
    args = dict(locals())
    chip = 2 * lax.axis_index("x") + lax.axis_index("y")
    where = tuple(jnp.asarray(v, jnp.int32).reshape(1)
                  for v in (lax.axis_index("c"), chip, chip ^ 2, chip ^ 1, chip ^ 3))

    wts = _GatheredWeights(chip, dict(w_in=w_in, w_out=w_out, w_ff1=w_ff1, w_ff2=w_ff2))
    cut_shapes = [gdn_conv_w.shape, conv_dw_w.shape]
    cuts = _gather_small("gather_cut_small", _pack([gdn_conv_w, conv_dw_w]))
    cut_parts = [_unpack(cuts[2 * k], cut_shapes) for k in range(N_CHIPS)]
    sp = {k: args[k] for k in SMALL}
    sp["gdn_conv_w"] = jnp.concatenate([p[0] for p in cut_parts], axis=-1)
    sp["conv_dw_w"] = jnp.concatenate([p[1] for p in cut_parts], axis=-1)

    red = _GradReducer(where, {k: (args[k], args["m_" + k], args["v_" + k]) for k in BIG})
    loss_part, grad_x, small_g = _local_step(x[0], loss_target[0], sp, wts, red)

    names = SMALL + ["loss"]
    small_g["loss"] = loss_part.reshape(1)
    shapes = [small_g[k].shape for k in names]
    total = _ordered_sum("sum_small", _gather_small("gather_small", _pack([small_g[k] for k in names])))
    summed = dict(zip(names, _unpack(total, shapes)))
    loss = summed.pop("loss")[0]
    for k, width in (("gdn_conv_w", gdn_conv_w.shape[-1]), ("conv_dw_w", conv_dw_w.shape[-1])):
        summed[k] = lax.dynamic_slice_in_dim(summed[k], chip * width, width, axis=-1)

    grads, deltas, new_m, new_v = {}, {}, {}, {}
    for k in BIG:
        grads[k], deltas[k], new_m[k], new_v[k] = red.out[k]
    local_shapes = [args[k].shape for k in SMALL]
    packs = [_pack([src[k] for k in SMALL]) for src in (
        {k: args[k] for k in SMALL}, summed, {k: args["m_" + k] for k in SMALL}, {k: args["v_" + k] for k in SMALL})]
    d_s, m_s, v_s = _adamw("adamw_small", *[p[None] for p in packs])
    for k, d, mm, vv in zip(SMALL, _unpack(d_s[0], local_shapes), _unpack(m_s[0], local_shapes),
                            _unpack(v_s[0], local_shapes)):
        grads[k], deltas[k], new_m[k], new_v[k] = summed[k], d, mm, vv

    order = ["lower_bounds", "norm_mix_pre", "norm_mix_post", "norm_ff_pre", "norm_ff_post", "w_in", "w_out",
             "hgrn_norm_w", "gdn_conv_w", "gdn_a_log", "gdn_dt_bias", "gdn_norm_w", "gmlp_ln_w", "gmlp_ln_b",
             "gmlp_w_s", "gmlp_b_s", "conv_dw_w", "conv_dw_b", "conv_ln_w", "conv_ln_b", "w_ff1", "w_ff2"]
    return (loss, grad_x[None], *[grads[k] for k in order], *[deltas[k] for k in order],
            *[new_m[k] for k in order], *[new_v[k] for k in order])
```

```python
import functools
import math

import jax
import jax.numpy as jnp
from jax import lax
from jax.experimental import pallas as pl
from jax.experimental.pallas import tpu as pltpu

F32 = jnp.float32
BF16 = jnp.bfloat16

DEPTH = 2
D_MODEL = 2048
GROUP_WIDTH = 512
HEAD_DIM = 128
N_HEADS = 4
CHUNK = 64
SHORT_CONV = 4
MIX_CHUNK = 128
CONV_WIDTH = 31
D_FF = 4 * D_MODEL
D_IN_PROJ = 12 * GROUP_WIDTH + 2 * N_HEADS
EPS = 1e-6
NEG_BIG = -1e30
TINY = 1e-30
ADAM_LR = 0.001
ADAM_B1 = 0.9
ADAM_B2 = 0.999
ADAM_EPS = 1e-08
ADAM_WD = 0.01
ADAM_STEP = 10

LANES = 128
P_IN = 12 * GROUP_WIDTH + LANES
SUB_BLOCK = 16
HGRN_TILE = 128
GDN_TILE = 128
CONV_TILE = 128
CONV_HALO = 32
VMEM_LIMIT = 56 * 1024 * 1024
N_CHIPS = 4
N_DEV = 8
MESH = pl.DeviceIdType.MESH


_DIMS = {
    "nn": (((1,), (0,)), ((), ())),
    "nt": (((1,), (1,)), ((), ())),
    "tn": (((0,), (0,)), ((), ())),
}


def _split2(a):
    hi = a.astype(BF16)
    return hi, (a - hi.astype(F32)).astype(BF16)


def _raw_mm(a, b, mode, exact):
    dot = lambda p, q: lax.dot_general(p, q, _DIMS[mode], preferred_element_type=F32)
    if not exact:
        return dot(a.astype(BF16), b.astype(BF16))
    a_hi, a_lo = _split2(a)
    b_hi, b_lo = _split2(b)
    return dot(a_hi, b_hi) + (dot(a_hi, b_lo) + dot(a_lo, b_hi))


@functools.partial(jax.custom_vjp, nondiff_argnums=(2, 3))
def _mm(a, b, mode, exact):
    return _raw_mm(a, b, mode, exact)


def _mm_fwd(a, b, mode, exact):
    return _raw_mm(a, b, mode, exact), (a, b)


def _mm_bwd(mode, exact, res, g):
    a, b = res
    if mode == "nn":
        return _raw_mm(g, b, "nt", exact), _raw_mm(a, g, "tn", exact)
    if mode == "nt":
        return _raw_mm(g, b, "nn", exact), _raw_mm(g, a, "tn", exact)
    return _raw_mm(b, g, "nt", exact), _raw_mm(a, g, "nn", exact)


_mm.defvjp(_mm_fwd, _mm_bwd)


def _sig(x):
    return jax.nn.sigmoid(x)


def _silu(x):
    return x * jax.nn.sigmoid(x)


def _gelu(x):
    return 0.5 * x * (1.0 + lax.erf(x * (1.0 / math.sqrt(2.0))))


def _rms(x, w):
    return x * lax.rsqrt(jnp.mean(x * x, axis=-1, keepdims=True) + EPS) * w


def _ln(x, w, b):
    mu = jnp.mean(x, axis=-1, keepdims=True)
    xc = x - mu
    var = jnp.mean(xc * xc, axis=-1, keepdims=True)
    return xc * lax.rsqrt(var + EPS) * w + b


def _iota(shape, dim):
    return lax.broadcasted_iota(jnp.int32, shape, dim)


def _tri_mm(x, mode):
    n = x.shape[0]
    tri = (_iota((n, n), 0) >= _iota((n, n), 1)).astype(BF16)
    x1 = x.astype(BF16)
    r1 = x - x1.astype(F32)
    x2 = r1.astype(BF16)
    x3 = (r1 - x2.astype(F32)).astype(BF16)
    dot = lambda q: lax.dot_general(tri, q, _DIMS[mode], preferred_element_type=F32)
    return dot(x1) + (dot(x2) + dot(x3))


@jax.custom_vjp
def _cumsum_rows(x):
    return _tri_mm(x, "nn")


_cumsum_rows.defvjp(lambda x: (_tri_mm(x, "nn"), None), lambda _, g: (_tri_mm(g, "tn"),))


def _hgrn_head(q, k, v, b, st):
    n = q.shape[0]
    ii = _iota((n, 1), 0)
    zpad = jnp.zeros((SUB_BLOCK, HEAD_DIM), F32)
    k_ext = jnp.concatenate([zpad, k], axis=0)
    b_ext = jnp.concatenate([zpad, b], axis=0)
    v_ext = jnp.concatenate([zpad, v], axis=0)
    o = jnp.zeros((n, HEAD_DIM), F32)
    for d in range(SUB_BLOCK):
        ks = k_ext[SUB_BLOCK - d:SUB_BLOCK - d + n]
        bs = b_ext[SUB_BLOCK - d:SUB_BLOCK - d + n]
        vs = v_ext[SUB_BLOCK - d:SUB_BLOCK - d + n]
        e = jnp.exp(jnp.where((ii % SUB_BLOCK) >= d, b - bs, NEG_BIG))
        o = o + jnp.sum(q * ks * e, axis=-1, keepdims=True) * vs
    blocks = [o[0:SUB_BLOCK]]
    for blk in range(1, n // SUB_BLOCK):
        lo = SUB_BLOCK * blk
        r = b[lo - 1:lo]
        a_q = q[lo:lo + SUB_BLOCK] * jnp.exp(b[lo:lo + SUB_BLOCK] - r)
        b_k = jnp.where(ii < lo, k * jnp.exp(jnp.minimum(r - b, 0.0)), 0.0)
        sc = _mm(a_q, b_k, "nt", False)
        blocks.append(o[lo:lo + SUB_BLOCK] + _mm(sc, v, "nn", False))
    o = jnp.concatenate(blocks, axis=0)
    o = o + _mm(q * jnp.exp(b), st, "nt", False)
    b_end = b[n - 1:n]
    st_new = st * jnp.exp(b_end) + _mm(v, k * jnp.exp(b_end - b), "tn", False)
    return o, st_new


def _hgrn_chunk(layer, lbp, nw, aq, af, ai, ag, states):
    rows = [lbp[i:i + 1, :] for i in range(DEPTH)]
    mx = functools.reduce(jnp.maximum, rows)
    es = [jnp.exp(r - mx) for r in rows]
    den = functools.reduce(lambda p, s: p + s, es)
    soft = [e / den for e in es]
    lb = functools.reduce(lambda p, s: p + s, soft[:layer + 1]) - soft[0]
    f = lb + (1.0 - lb) * _sig(af)
    logf = jnp.log(jnp.maximum(f, TINY))
    k = (1.0 - lb) * _sig(-af)
    q = _silu(aq)
    b = _cumsum_rows(logf)
    outs, new_states = [], []
    for h in range(N_HEADS):
        sl = slice(HEAD_DIM * h, HEAD_DIM * (h + 1))
        o, st = _hgrn_head(q[:, sl], k[:, sl], ai[:, sl], b[:, sl], states[h])
        outs.append(_rms(o, nw) * _silu(ag[:, sl]))
        new_states.append(st)
    return jnp.concatenate(outs, axis=1), new_states


def _hgrn_tile(layer, lbp, nw, aq, af, ai, ag, states):
    outs = []
    for ci in range(aq.shape[0] // CHUNK):
        rs = slice(CHUNK * ci, CHUNK * (ci + 1))
        o, states = _hgrn_chunk(layer, lbp, nw, aq[rs], af[rs], ai[rs], ag[rs], states)
        outs.append(o)
    return jnp.concatenate(outs, axis=0), states


def _short_conv(prev, cur, w):
    n = cur.shape[0]
    ext = jnp.concatenate([prev[n - 8:n], cur], axis=0)
    y = jnp.zeros_like(cur)
    for t in range(SHORT_CONV):
        off = 8 - (SHORT_CONV - 1) + t
        y = y + w[t:t + 1, :] * ext[off:off + n]
    return _silu(y)


def _gdn_tile(cw, alog, dtb, nw, pq, pk, pv, cq, ck, cv, bz, ab, states):
    n = cq.shape[0]
    q_all = _short_conv(pq, cq, cw[:, 0:GROUP_WIDTH])
    k_all = _short_conv(pk, ck, cw[:, GROUP_WIDTH:2 * GROUP_WIDTH])
    v_all = _short_conv(pv, cv, cw[:, 2 * GROUP_WIDTH:3 * GROUP_WIDTH])
    beta_all = _sig(ab)
    g_all = -jnp.exp(alog) * jax.nn.softplus(ab + dtb)
    units = [(ci, h) for ci in range(n // CHUNK) for h in range(N_HEADS)]
    gc_all = [_cumsum_rows(g_all[CHUNK * ci:CHUNK * (ci + 1)]) for ci in range(n // CHUNK)]
    gc_t = [g.T for g in gc_all]
    ii = _iota((CHUNK, CHUNK), 0)
    jj = _iota((CHUNK, CHUNK), 1)
    eye = (ii == jj).astype(F32)

    def cut(a, ci, h):
        return a[CHUNK * ci:CHUNK * (ci + 1), HEAD_DIM * h:HEAD_DIM * (h + 1)]

    q = [cut(q_all, ci, h) for ci, h in units]
    k = [cut(k_all, ci, h) for ci, h in units]
    v = [cut(v_all, ci, h) for ci, h in units]
    q = [t * lax.rsqrt(jnp.sum(t * t, axis=-1, keepdims=True) + EPS) * (HEAD_DIM ** -0.5) for t in q]
    k = [t * lax.rsqrt(jnp.sum(t * t, axis=-1, keepdims=True) + EPS) for t in k]
    beta = [beta_all[CHUNK * ci:CHUNK * (ci + 1), h:h + 1] for ci, h in units]
    gc = [gc_all[ci][:, N_HEADS + h:N_HEADS + h + 1] for ci, h in units]
    gcr = [gc_t[ci][N_HEADS + h:N_HEADS + h + 1, :] for ci, h in units]
    gamma = [jnp.exp(jnp.where(ii >= jj, a - b, NEG_BIG)) for a, b in zip(gc, gcr)]
    kb = [a * b for a, b in zip(k, beta)]
    m = [jnp.where(ii > jj, _mm(a, b, "nt", False) * g, 0.0) for a, b, g in zip(kb, k, gamma)]
    inv = [eye - t for t in m]
    p = m
    for _ in range(max(1, int(math.ceil(math.log2(CHUNK))) - 1)):
        p = [_mm(t, t, "nn", True) for t in p]
        inv = [a + _mm(a, t, "nn", True) for a, t in zip(inv, p)]
    eg = [jnp.exp(t) for t in gc]
    u = [_mm(a, b * c, "nn", True) for a, b, c in zip(inv, v, beta)]
    w = [_mm(a, b * c, "nn", True) for a, b, c in zip(inv, kb, eg)]
    qk = [_mm(a, b, "nt", False) * g for a, b, g in zip(q, k, gamma)]
    qd = [a * b for a, b in zip(q, eg)]
    g_end = [t[CHUNK - 1:CHUNK] for t in gc]
    kd = [a * jnp.exp(e - g) for a, e, g in zip(k, g_end, gc)]
    states = list(states)
    outs = {}
    for i, (ci, h) in enumerate(units):
        st = states[h]
        v_new = u[i] - _mm(w[i], st, "nt", False)
        outs[ci, h] = _mm(qd[i], st, "nt", False) + _mm(qk[i], v_new, "nn", False)
        states[h] = st * jnp.exp(g_end[i]) + _mm(v_new, kd[i], "tn", False)
    rows = []
    for ci in range(n // CHUNK):
        rows.append(jnp.concatenate(
            [_rms(outs[ci, h], nw) * _silu(cut(bz, ci, h)) for h in range(N_HEADS)], axis=1))
    return jnp.concatenate(rows, axis=0), states


def _gmlp_tile(ln_w, ln_b, ws, bs_cols, cu, cv):
    u = _gelu(cu)
    v = _ln(_gelu(cv), ln_w, ln_b)
    n = cu.shape[0]
    tril = _iota((n, n), 0) >= _iota((n, n), 1)
    outs = []
    for h in range(N_HEADS):
        sl = slice(HEAD_DIM * h, HEAD_DIM * (h + 1))
        wc = jnp.where(tril, ws[h], 0.0)
        outs.append(_mm(wc, v[:, sl], "nn", False) + bs_cols[:, h:h + 1])
    return u * jnp.concatenate(outs, axis=1)


def _conformer_tile(dw_w, dw_b, ln_w, ln_b, pa, pg, ca, cg):
    n = ca.shape[0]
    yp = pa[n - CONV_HALO:n] * _sig(pg[n - CONV_HALO:n])
    ext = jnp.concatenate([yp, ca * _sig(cg)], axis=0)
    acc = jnp.zeros_like(ca)
    for t in range(CONV_WIDTH):
        off = CONV_HALO - (CONV_WIDTH - 1) + t
        acc = acc + dw_w[t:t + 1, :] * ext[off:off + n]
    return _silu(_ln(acc + dw_b, ln_w, ln_b))


def _seq_call(name, body, *, steps, ins, outs, accs=(), carries=(), reverse=False, prefetch=None, into=()):
    n_in, n_out, n_acc, n_car = len(ins), len(outs), len(accs), len(carries)
    n_into = len(into)
    n_pre = 0 if prefetch is None else len(prefetch)

    def logical(g):
        return (steps - 1 - g) if reverse else g

    def kern(*refs):
        pre = refs[:n_pre]
        refs = refs[n_pre:]
        in_refs = refs[:n_in]
        refs = refs[n_in + n_into:]
        out_refs = refs[:n_out]
        acc_refs = refs[n_out:n_out + n_acc]
        car_refs = refs[n_out + n_acc:]
        g = pl.program_id(0)

        @pl.when(g == 0)
        def _():
            for r in list(acc_refs) + list(car_refs):
                r[...] = jnp.zeros(r.shape, r.dtype)

        o, a, c = body(logical(g), [r[...] for r in in_refs], [r[...] for r in car_refs], *((pre,) if n_pre else ()))
        for r, v in zip(out_refs, o, strict=True):
            r[...] = v.astype(r.dtype)
        for r, v in zip(acc_refs, a, strict=True):
            r[...] += v
        for r, v in zip(car_refs, c, strict=True):
            r[...] = v

    def spec(block, fn):
        return pl.BlockSpec(block, lambda g, *pre: fn(logical(g), *((pre,) if n_pre else ())))

    in_specs = [spec(bs, fn) for (_, bs, fn) in ins] + [HBM] * n_into
    out_specs = [spec(bs, fn) for (_, _, bs, fn) in outs]
    out_specs += [pl.BlockSpec(shape, lambda g, *pre, _n=len(shape): (0,) * _n) for (shape, _) in accs]
    out_shape = [jax.ShapeDtypeStruct(s, d) for (s, d, _, _) in outs]
    out_shape += [jax.ShapeDtypeStruct(s, d) for (s, d) in accs]
    grid_spec = pltpu.PrefetchScalarGridSpec(
        num_scalar_prefetch=n_pre, grid=(steps,), in_specs=in_specs, out_specs=out_specs,
        scratch_shapes=[pltpu.VMEM(s, d) for (s, d) in carries])
    args = ([] if prefetch is None else list(prefetch)) + [a for (a, _, _) in ins] + list(into)
    return pl.pallas_call(
        kern, name=name, grid_spec=grid_spec, out_shape=out_shape,
        input_output_aliases={n_pre + n_in + i: i for i in range(n_into)},
        compiler_params=pltpu.CompilerParams(dimension_semantics=("arbitrary",), vmem_limit_bytes=VMEM_LIMIT),
    )(*args)


def _whole(a):
    nd = a.ndim
    return (a, a.shape, lambda i, *pre: (0,) * nd)


def _rows(a, tile, col=0, width=None, shift=0):
    width = a.shape[1] if width is None else width
    if shift:
        return (a, (tile, width), lambda i, *pre: (jnp.maximum(i + shift, 0), col))
    return (a, (tile, width), lambda i, *pre: (i, col))


def _row_out(n_rows, width, dtype, tile):
    return ((n_rows, width), dtype, (tile, width), lambda i, *pre: (i, 0))


def _pick_tile(n, prefs):
    for t in prefs:
        if n % t == 0:
            return t
    return n


def _matmul(name, a, b, mode, out_dtypes, epilogue=None, extras=(), place=None, max_tm=1024):
    if mode == "nn":
        (m, k), n = a.shape, b.shape[1]
    elif mode == "nt":
        (m, k), n = a.shape, b.shape[0]
    else:
        (k, m), n = a.shape, b.shape[1]
    tm = _pick_tile(m, tuple(t for t in (1024, 512, 256, 128) if t <= max_tm))
    tn = _pick_tile(n, (1024, 896, 512, 256, 128))
    tk = _pick_tile(k, (2048, 896, 512, 256, 128))
    nk = k // tk
    n_ex = len(extras)
    n_out = len(out_dtypes)

    def kern(*refs):
        a_ref, b_ref = refs[0], refs[1]
        ex_refs = refs[2:2 + n_ex]
        out_refs = refs[2 + n_ex:2 + n_ex + n_out]
        acc_ref = refs[2 + n_ex + n_out]
        kk = pl.program_id(2)

        @pl.when(kk == 0)
        def _():
            acc_ref[...] = jnp.zeros(acc_ref.shape, F32)

        acc_ref[...] += lax.dot_general(a_ref[...], b_ref[...], _DIMS[mode], preferred_element_type=F32)

        @pl.when(kk == nk - 1)
        def _():
            acc = acc_ref[...]
            vals = (acc,) if epilogue is None else epilogue(acc, *[r[...] for r in ex_refs])
            for r, v in zip(out_refs, vals, strict=True):
                r[...] = v.astype(r.dtype)

    if mode == "tn":
        a_spec = pl.BlockSpec((tk, tm), lambda i, j, kk: (kk, i))
    else:
        a_spec = pl.BlockSpec((tm, tk), lambda i, j, kk: (i, kk))
    if mode == "nt":
        b_spec = pl.BlockSpec((tn, tk), lambda i, j, kk: (j, kk))
    else:
        b_spec = pl.BlockSpec((tk, tn), lambda i, j, kk: (kk, j))
    tile = pl.BlockSpec((tm, tn), lambda i, j, kk: (i, j))
    out_specs = [tile] * n_out
    out_shape = [jax.ShapeDtypeStruct((m, n), d) for d in out_dtypes]
    if place is not None:
        shape, block_fn, index_fn = place
        out_specs = [pl.BlockSpec(block_fn(tm, tn), lambda i, j, kk: index_fn(i, j, tm, tn))]
        out_shape = [jax.ShapeDtypeStruct(shape, out_dtypes[0])]
    return pl.pallas_call(
        kern, name=name, grid=(m // tm, n // tn, nk),
        in_specs=[a_spec, b_spec] + [tile] * n_ex,
        out_specs=out_specs, out_shape=out_shape,
        scratch_shapes=[pltpu.VMEM((tm, tn), F32)],
        compiler_params=pltpu.CompilerParams(
            dimension_semantics=("parallel", "parallel", "arbitrary"), vmem_limit_bytes=VMEM_LIMIT),
    )(a, b, *extras)


ROW_TILE = 256


def _rms_fwd(name, x, w):
    s, d = x.shape
    t = _pick_tile(s, (ROW_TILE,))

    def body(i, v, c):
        return [_rms(v[0], v[1])], [], []

    return _seq_call(name, body, steps=s // t, ins=[_rows(x, t), _whole(w)], outs=[_row_out(s, d, BF16, t)])[0]


def _resid_rms_fwd(name, x, y, w):
    s, d = x.shape
    t = _pick_tile(s, (ROW_TILE,))

    def body(i, v, c):
        return [v[0] + _rms(v[1], v[2])], [], []

    return _seq_call(name, body, steps=s // t, ins=[_rows(x, t), _rows(y, t), _whole(w)],
                     outs=[_row_out(s, d, F32, t)])[0]


def _rms_bwd(name, x, w, dh, dres):
    s, d = x.shape
    t = _pick_tile(s, (ROW_TILE,))

    def body(i, v, c):
        _, vjp = jax.vjp(_rms, v[0], v[1])
        dx, dw = vjp(v[2])
        return [dx + v[3]], [dw], []

    return _seq_call(name, body, steps=s // t, ins=[_rows(x, t), _whole(w), _rows(dh, t), _rows(dres, t)],
                     outs=[_row_out(s, d, F32, t)], accs=[((1, d), F32)])


def _resid_rms_bwd(name, y, w, dxo):
    s, d = y.shape
    t = _pick_tile(s, (ROW_TILE,))

    def body(i, v, c):
        _, vjp = jax.vjp(_rms, v[0], v[1])
        dy, dw = vjp(v[2])
        return [dy], [dw], []

    return _seq_call(name, body, steps=s // t, ins=[_rows(y, t), _whole(w), _rows(dxo, t)],
                     outs=[_row_out(s, d, BF16, t)], accs=[((1, d), F32)])


def _loss_head(name, y, target):
    s, d = y.shape
    t = _pick_tile(s, (ROW_TILE,))

    def body(i, v, c):
        err = v[0] - v[1]
        part = 0.5 * jnp.sum(jnp.mean(err * err, axis=-1, keepdims=True))
        return [err * (1.0 / d)], [jnp.full((1, LANES), part, F32)], []

    return _seq_call(name, body, steps=s // t, ins=[_rows(y, t), _rows(target, t)],
                     outs=[_row_out(s, d, F32, t)], accs=[((1, LANES), F32)])


SEG = {n: i for i, n in enumerate(
    ["a_q", "a_f", "a_i", "a_g", "b_q", "b_k", "b_v", "b_z", "c_u", "c_v", "d_a", "d_gate"])}
AB_COL = 12 * GROUP_WIDTH // LANES


def _seg(proj, name, tile, shift=0):
    return _rows(proj, tile, col=SEG[name], width=GROUP_WIDTH, shift=shift)


def _state_block():
    return (1, N_HEADS * HEAD_DIM, HEAD_DIM), lambda i, *pre: (i, 0, 0)


def _split_states(blk):
    return [blk[0, HEAD_DIM * h:HEAD_DIM * (h + 1), :] for h in range(N_HEADS)]


STATE_CARRIES = [((HEAD_DIM, HEAD_DIM), F32)] * N_HEADS


def _hgrn_fwd(layer, proj, lbp, nw):
    s = proj.shape[0]
    n = s // HGRN_TILE
    sb, sf = _state_block()

    def body(i, v, st):
        o, new = _hgrn_tile(layer, v[0], v[1], v[2], v[3], v[4], v[5], st)
        return [o, jnp.concatenate(st, axis=0)[None]], [], new

    return _seq_call(
        f"hgrn_fwd{layer}", body, steps=n,
        ins=[_whole(lbp), _whole(nw)] + [_seg(proj, k, HGRN_TILE) for k in ("a_q", "a_f", "a_i", "a_g")],
        outs=[_row_out(s, GROUP_WIDTH, BF16, HGRN_TILE), ((n, N_HEADS * HEAD_DIM, HEAD_DIM), F32, sb, sf)],
        carries=STATE_CARRIES)


def _hgrn_bwd(layer, proj, lbp, nw, states, dmix):
    s = proj.shape[0]
    n = s // HGRN_TILE
    sb, sf = _state_block()

    def body(i, v, dst):
        st = _split_states(v[6])

        def f(lbp_, nw_, aq, af, ai, ag, *st_):
            return _hgrn_tile(layer, lbp_, nw_, aq, af, ai, ag, list(st_))

        _, vjp = jax.vjp(f, v[0], v[1], v[2], v[3], v[4], v[5], *st)
        g = vjp((v[7], list(dst)))
        return [jnp.concatenate(g[2:6], axis=1)], [g[0], g[1]], list(g[6:])

    return _seq_call(
        f"hgrn_bwd{layer}", body, steps=n, reverse=True,
        ins=[_whole(lbp), _whole(nw)] + [_seg(proj, k, HGRN_TILE) for k in ("a_q", "a_f", "a_i", "a_g")]
        + [(states, sb, sf), _rows(dmix, HGRN_TILE, col=0, width=GROUP_WIDTH)],
        outs=[_row_out(s, 4 * GROUP_WIDTH, BF16, HGRN_TILE)],
        accs=[(lbp.shape, F32), (nw.shape, F32)], carries=STATE_CARRIES)


def _gdn_ins(proj, cw, alog, dtb, nw):
    return ([_whole(cw), _whole(alog), _whole(dtb), _whole(nw)]
            + [_seg(proj, k, GDN_TILE, shift=-1) for k in ("b_q", "b_k", "b_v")]
            + [_seg(proj, k, GDN_TILE) for k in ("b_q", "b_k", "b_v", "b_z")]
            + [_rows(proj, GDN_TILE, col=AB_COL, width=LANES)])


def _mask_prev(i, vals):
    keep = (i > 0).astype(F32)
    return [p * keep for p in vals]


def _gdn_fwd(layer, proj, cw, alog, dtb, nw):
    s = proj.shape[0]
    n = s // GDN_TILE
    sb, sf = _state_block()

    def body(i, v, st):
        prev = _mask_prev(i, v[4:7])
        o, new = _gdn_tile(v[0], v[1], v[2], v[3], *prev, *v[7:12], st)
        return [o, jnp.concatenate(st, axis=0)[None]], [], new

    return _seq_call(
        f"gdn_fwd{layer}", body, steps=n, ins=_gdn_ins(proj, cw, alog, dtb, nw),
        outs=[_row_out(s, GROUP_WIDTH, BF16, GDN_TILE), ((n, N_HEADS * HEAD_DIM, HEAD_DIM), F32, sb, sf)],
        carries=STATE_CARRIES)


def _gdn_bwd(layer, proj, cw, alog, dtb, nw, states, dmix):
    s = proj.shape[0]
    n = s // GDN_TILE
    sb, sf = _state_block()

    def body(i, v, car):
        dst, dprev = car[:N_HEADS], car[N_HEADS:]
        prev = _mask_prev(i, v[4:7])
        st = _split_states(v[12])

        def f(cw_, alog_, dtb_, nw_, pq, pk, pv, cq, ck, cv, bz, ab, *st_):
            return _gdn_tile(cw_, alog_, dtb_, nw_, pq, pk, pv, cq, ck, cv, bz, ab, list(st_))

        _, vjp = jax.vjp(f, v[0], v[1], v[2], v[3], *prev, *v[7:12], *st)
        g = vjp((v[13], list(dst)))
        dcur = [g[7] + dprev[0], g[8] + dprev[1], g[9] + dprev[2], g[10]]
        return ([jnp.concatenate(dcur, axis=1), g[11]], list(g[0:4]), list(g[12:]) + list(g[4:7]))

    return _seq_call(
        f"gdn_bwd{layer}", body, steps=n, reverse=True,
        ins=_gdn_ins(proj, cw, alog, dtb, nw) + [(states, sb, sf), _rows(dmix, GDN_TILE, col=1, width=GROUP_WIDTH)],
        outs=[_row_out(s, 4 * GROUP_WIDTH, BF16, GDN_TILE), _row_out(s, LANES, BF16, GDN_TILE)],
        accs=[(cw.shape, F32), (alog.shape, F32), (dtb.shape, F32), (nw.shape, F32)],
        carries=STATE_CARRIES + [((GDN_TILE, GROUP_WIDTH), F32)] * 3)


def _gmlp_fwd(layer, proj, ln_w, ln_b, ws, bs_cols):
    s = proj.shape[0]

    def body(i, v, c):
        return [_gmlp_tile(*v)], [], []

    return _seq_call(
        f"gmlp_fwd{layer}", body, steps=s // MIX_CHUNK,
        ins=[_whole(ln_w), _whole(ln_b), _whole(ws), _whole(bs_cols),
             _seg(proj, "c_u", MIX_CHUNK), _seg(proj, "c_v", MIX_CHUNK)],
        outs=[_row_out(s, GROUP_WIDTH, BF16, MIX_CHUNK)])[0]


def _gmlp_bwd(layer, proj, ln_w, ln_b, ws, bs_cols, dmix):
    s = proj.shape[0]

    def body(i, v, c):
        _, vjp = jax.vjp(_gmlp_tile, *v[:6])
        g = vjp(v[6])
        return [jnp.concatenate(g[4:6], axis=1)], list(g[0:4]), []

    return _seq_call(
        f"gmlp_bwd{layer}", body, steps=s // MIX_CHUNK,
        ins=[_whole(ln_w), _whole(ln_b), _whole(ws), _whole(bs_cols),
             _seg(proj, "c_u", MIX_CHUNK), _seg(proj, "c_v", MIX_CHUNK),
             _rows(dmix, MIX_CHUNK, col=2, width=GROUP_WIDTH)],
        outs=[_row_out(s, 2 * GROUP_WIDTH, BF16, MIX_CHUNK)],
        accs=[(ln_w.shape, F32), (ln_b.shape, F32), (ws.shape, F32), (bs_cols.shape, F32)])


def _conformer_ins(proj, dw_w, dw_b, ln_w, ln_b):
    return ([_whole(dw_w), _whole(dw_b), _whole(ln_w), _whole(ln_b)]
            + [_seg(proj, k, CONV_TILE, shift=-1) for k in ("d_a", "d_gate")]
            + [_seg(proj, k, CONV_TILE) for k in ("d_a", "d_gate")])


def _conformer_fwd(layer, proj, dw_w, dw_b, ln_w, ln_b):
    s = proj.shape[0]

    def body(i, v, c):
        prev = _mask_prev(i, v[4:6])
        return [_conformer_tile(v[0], v[1], v[2], v[3], *prev, v[6], v[7])], [], []

    return _seq_call(
        f"conformer_fwd{layer}", body, steps=s // CONV_TILE, ins=_conformer_ins(proj, dw_w, dw_b, ln_w, ln_b),
        outs=[_row_out(s, GROUP_WIDTH, BF16, CONV_TILE)])[0]


def _conformer_bwd(layer, proj, dw_w, dw_b, ln_w, ln_b, dmix):
    s = proj.shape[0]

    def body(i, v, dprev):
        prev = _mask_prev(i, v[4:6])
        _, vjp = jax.vjp(_conformer_tile, v[0], v[1], v[2], v[3], *prev, v[6], v[7])
        g = vjp(v[8])
        return [jnp.concatenate([g[6] + dprev[0], g[7] + dprev[1]], axis=1)], list(g[0:4]), [g[4], g[5]]

    return _seq_call(
        f"conformer_bwd{layer}", body, steps=s // CONV_TILE, reverse=True,
        ins=_conformer_ins(proj, dw_w, dw_b, ln_w, ln_b) + [_rows(dmix, CONV_TILE, col=3, width=GROUP_WIDTH)],
        outs=[_row_out(s, 2 * GROUP_WIDTH, BF16, CONV_TILE)],
        accs=[(dw_w.shape, F32), (dw_b.shape, F32), (ln_w.shape, F32), (ln_b.shape, F32)],
        carries=[((CONV_TILE, GROUP_WIDTH), F32)] * 2)


SMALL = ["lower_bounds", "norm_mix_pre", "norm_mix_post", "norm_ff_pre", "norm_ff_post", "hgrn_norm_w",
         "gdn_conv_w", "gdn_a_log", "gdn_dt_bias", "gdn_norm_w", "gmlp_ln_w", "gmlp_ln_b", "gmlp_w_s",
         "gmlp_b_s", "conv_dw_w", "conv_dw_b", "conv_ln_w", "conv_ln_b"]


def _gate_row(v):
    return jnp.pad(v.reshape(1, N_HEADS), ((0, 0), (N_HEADS, LANES - 2 * N_HEADS)))


def _cut_rows(rows, cols):
    return ((N_CHIPS, rows, cols), lambda tm, tn: (None, tm, tn),
            lambda i, j, tm, tn: (i // (rows // tm), i % (rows // tm), j))


def _cut_cols(rows, cols):
    return ((N_CHIPS, rows, cols), lambda tm, tn: (None, tm, tn),
            lambda i, j, tm, tn: (j // (cols // tn), i, j % (cols // tn)))


def _relu2(acc):
    r = jnp.maximum(acc, 0.0)
    return acc, r * r


def _relu2_bwd(acc, u):
    return (2.0 * jnp.maximum(u, 0.0) * acc,)


def _local_step(x, target, sp, wts, red):
    row = lambda v: v.reshape(1, -1)
    saved = []
    w_in, w_out, w_ff1, w_ff2 = [], [], [], []
    for l in range(DEPTH):
        par = dict(
            lbp=sp["lower_bounds"], hn=row(sp["hgrn_norm_w"][l]), cw=sp["gdn_conv_w"][l],
            alog=_gate_row(sp["gdn_a_log"][l]), dtb=_gate_row(sp["gdn_dt_bias"][l]), gn=row(sp["gdn_norm_w"][l]),
            glw=row(sp["gmlp_ln_w"][l]), glb=row(sp["gmlp_ln_b"][l]), ws=sp["gmlp_w_s"][l],
            bsc=jnp.pad(sp["gmlp_b_s"][l].T, ((0, 0), (0, LANES - N_HEADS))),
            dww=sp["conv_dw_w"][l], dwb=row(sp["conv_dw_b"][l]), clw=row(sp["conv_ln_w"][l]),
            clb=row(sp["conv_ln_b"][l]), n1=row(sp["norm_mix_pre"][l]), n2=row(sp["norm_mix_post"][l]),
            n3=row(sp["norm_ff_pre"][l]), n4=row(sp["norm_ff_post"][l]))
        h = _rms_fwd(f"norm_mix_pre{l}", x, par["n1"])
        w_in.append(wts.get("w_in", l))
        proj = _matmul(f"in_proj{l}", h, w_in[l], "nn", [F32])[0]
        wts.mark(f"proj{l}", proj)
        o_a, st_a = _hgrn_fwd(l, proj, par["lbp"], par["hn"])
        o_b, st_b = _gdn_fwd(l, proj, par["cw"], par["alog"], par["dtb"], par["gn"])
        wts.mark(f"gdn{l}", o_b)
        o_c = _gmlp_fwd(l, proj, par["glw"], par["glb"], par["ws"], par["bsc"])
        o_d = _conformer_fwd(l, proj, par["dww"], par["dwb"], par["clw"], par["clb"])
        wts.mark(f"conformer{l}", o_d)
        mix = jnp.concatenate([o_a, o_b, o_c, o_d], axis=1)
        w_out.append(wts.get("w_out", l))
        y = _matmul(f"out_proj{l}", mix, w_out[l], "nn", [F32])[0]
        wts.mark(f"y{l}", y)
        x1 = _resid_rms_fwd(f"norm_mix_post{l}", x, y, par["n2"])
        h2 = _rms_fwd(f"norm_ff_pre{l}", x1, par["n3"])
        w_ff1.append(wts.get("w_ff1", l))
        u, act = _matmul(f"ff1_{l}", h2, w_ff1[l], "nn", [F32, BF16], epilogue=_relu2)
        wts.mark(f"u{l}", u)
        w_ff2.append(wts.get("w_ff2", l))
        y2 = _matmul(f"ff2_{l}", act, w_ff2[l], "nn", [F32])[0]
        x2 = _resid_rms_fwd(f"norm_ff_post{l}", x1, y2, par["n4"])
        saved.append(dict(par=par, x=x, h=h, proj=proj, st_a=st_a, st_b=st_b, mix=mix, y=y, x1=x1, h2=h2,
                          u=u, act=act, y2=y2))
        x = x2

    dx, loss_acc = _loss_head("loss_head", x, target)
    loss_part = loss_acc[0, 0]

    gs = {k: [None] * DEPTH for k in SMALL if k != "lower_bounds"}
    g_lb = jnp.zeros((DEPTH, GROUP_WIDTH), F32)
    def behind(row, token):
        return row if token is None else row + token[:1, :1]

    def join(a, b):
        return b if a is None else a if b is None else a + b

    tok = None
    for l in reversed(range(DEPTH)):
        sv = saved[l]
        par = sv["par"]
        dy2, dn4 = _resid_rms_bwd(f"norm_ff_post_bwd{l}", sv["y2"], behind(par["n4"], tok), dx)
        du = _matmul(f"ff2_dx{l}", dy2, w_ff2[l], "nt", [BF16], epilogue=_relu2_bwd, extras=(sv["u"],))[0]
        g_ff2 = _matmul(f"ff2_dw{l}", sv["act"], dy2, "tn", [F32], place=_cut_rows(D_FF // N_CHIPS, D_MODEL))[0]
        tok = red.mark(f"ff2_dw{l}", du, g_ff2)
        g_ff1 = _matmul(f"ff1_dw{l}", sv["h2"], du, "tn", [F32], place=_cut_cols(D_MODEL, D_FF // N_CHIPS))[0]
        tok = join(tok, red.mark(f"ff1_dw{l}", du, g_ff1))
        dh2 = _matmul(f"ff1_dx{l}", du, w_ff1[l], "nt", [F32])[0]
        dx1, dn3 = _rms_bwd(f"norm_ff_pre_bwd{l}", sv["x1"], behind(par["n3"], tok), dh2, dx)
        dy, dn2 = _resid_rms_bwd(f"norm_mix_post_bwd{l}", sv["y"], par["n2"], dx1)
        dmix = _matmul(f"out_proj_dx{l}", dy, w_out[l], "nt", [F32])[0]
        tok = red.mark(f"out_proj_dx{l}", dmix)
        g_out = _matmul(f"out_proj_dw{l}", sv["mix"], dy, "tn", [F32], max_tm=GROUP_WIDTH,
                        place=_cut_rows(GROUP_WIDTH, D_MODEL))[0]
        proj = sv["proj"]
        dp_a, dlb, dhn = _hgrn_bwd(l, proj, par["lbp"], behind(par["hn"], tok), sv["st_a"], dmix)
        dp_b, dp_ab, dcw, dalog, ddtb, dgn = _gdn_bwd(
            l, proj, par["cw"], par["alog"], par["dtb"], par["gn"], sv["st_b"], dmix)
        dp_c, dglw, dglb, dws, dbsc = _gmlp_bwd(l, proj, par["glw"], par["glb"], par["ws"], par["bsc"], dmix)
        dp_d, ddww, ddwb, dclw, dclb = _conformer_bwd(
            l, proj, par["dww"], par["dwb"], par["clw"], par["clb"], dmix)
        tok = red.mark(f"conformer_bwd{l}", dp_d)
        dproj = jnp.concatenate([dp_a, dp_b, dp_c, dp_d, dp_ab], axis=1)
        g_in = _matmul(f"in_proj_dw{l}", sv["h"], dproj, "tn", [F32])[0]
        dh = _matmul(f"in_proj_dx{l}", dproj, w_in[l], "nt", [F32])[0]
        red.mark(f"in_proj_dx{l}", dh)
        dx, dn1 = _rms_bwd(f"norm_mix_pre_bwd{l}", sv["x"], behind(par["n1"], tok), dh, dx1)
        tok = red.layer_done(l, dict(w_in=g_in, w_out=g_out, w_ff1=g_ff1, w_ff2=g_ff2), dx)
        g_lb = g_lb + dlb
        for k, v in dict(
                norm_mix_pre=dn1[0], norm_mix_post=dn2[0], norm_ff_pre=dn3[0], norm_ff_post=dn4[0],
                hgrn_norm_w=dhn[0], gdn_conv_w=dcw, gdn_a_log=dalog[0, N_HEADS:2 * N_HEADS],
                gdn_dt_bias=ddtb[0, N_HEADS:2 * N_HEADS], gdn_norm_w=dgn[0], gmlp_ln_w=dglw[0],
                gmlp_ln_b=dglb[0], gmlp_w_s=dws, gmlp_b_s=dbsc[:, :N_HEADS].T, conv_dw_w=ddww,
                conv_dw_b=ddwb[0], conv_ln_w=dclw[0], conv_ln_b=dclb[0]).items():
            gs[k][l] = v
    small_grads = {k: jnp.stack(v) for k, v in gs.items()}
    small_grads["lower_bounds"] = g_lb
    return loss_part, dx, small_grads


HBM = pl.BlockSpec(memory_space=pl.ANY)


def _place():
    return lax.axis_index("x"), lax.axis_index("y"), lax.axis_index("c")


def _other_chips(x, y):
    chips = [(1 - x, y), (x, 1 - y), (1 - x, 1 - y)]
    return [(px, py, 2 * px + py) for px, py in chips]


SEM = pl.BlockSpec(memory_space=pltpu.SEMAPHORE)
IN_HBM = pl.BlockSpec(memory_space=pltpu.HBM)
EFFECT = pltpu.SideEffectType.DATAFLOW_SIDE_EFFECTING


def _copies_start(name, plan, bufs, after, n=3):
    nb = len(bufs)

    def body(*refs):
        token = refs[-1]
        for started, _ in plan(refs[:nb], refs[nb + 1], refs[nb + 2]):
            started.start()
        token[...] = jnp.zeros(token.shape, token.dtype)

    out = pl.pallas_call(
        body, name=name,
        out_shape=(pltpu.SemaphoreType.DMA((n,)), pltpu.SemaphoreType.DMA((n,)))
        + tuple(pltpu.HBM(b.shape, b.dtype) for b in bufs) + (jax.ShapeDtypeStruct((8, LANES), F32),),
        in_specs=(IN_HBM,) * nb + (HBM,),
        out_specs=(SEM, SEM) + (IN_HBM,) * nb + (pl.BlockSpec(memory_space=pltpu.VMEM),),
        input_output_aliases={i: 2 + i for i in range(nb)},
        compiler_params=pltpu.CompilerParams(has_side_effects=EFFECT),
    )(*[pltpu.with_memory_space_constraint(b, pltpu.HBM) for b in bufs], after)
    return out[0], out[1], list(out[2:2 + nb]), out[-1]


def _copies_wait(name, plan, send_sems, recv_sems, bufs, after):
    nb = len(bufs)

    def body(*refs):
        for started, arriving in plan(refs[:nb], refs[nb], refs[nb + 1]):
            started.wait_send()
            arriving.wait_recv()

    out = pl.pallas_call(
        body, name=name, out_shape=tuple(pltpu.HBM(b.shape, b.dtype) for b in bufs),
        in_specs=(IN_HBM,) * nb + (SEM, SEM, HBM), out_specs=(IN_HBM,) * nb,
        input_output_aliases={i: i for i in range(nb)},
        compiler_params=pltpu.CompilerParams(has_side_effects=EFFECT),
    )(*bufs, send_sems, recv_sems, after)
    return list(out)


def _ici_plan(bufs, send_sems, recv_sems):
    blk, land = bufs
    x, y, c = _place()
    mine = 2 * x + y
    plan = []
    for j, (px, py, k) in enumerate(_other_chips(x, y)):
        def copy(dst, j=j, to=(px, py, c)):
            return pltpu.make_async_remote_copy(
                src_ref=blk.at[c], dst_ref=dst, send_sem=send_sems.at[j], recv_sem=recv_sems.at[j],
                device_id=to, device_id_type=MESH)
        plan.append((copy(land.at[mine, c]), copy(land.at[k, c])))
    return plan


def _d2d_plan(bufs, send_sems, recv_sems):
    (land,) = bufs
    x, y, c = _place()
    plan = []
    for j, (_, _, k) in enumerate(_other_chips(x, y)):
        def copy(layer, j=j, k=k):
            return pltpu.make_async_remote_copy(
                src_ref=land.at[k, c], dst_ref=land.at[k, layer], send_sem=send_sems.at[j],
                recv_sem=recv_sems.at[j], device_id=(x, y, 1 - c), device_id_type=MESH)
        plan.append((copy(c), copy(1 - c)))
    return plan


class _GatheredWeights:
    STAGES = {"w_out": ("proj0", "conformer0"), "w_ff1": ("gdn0", "y0"), "w_ff2": ("u0", None)}

    def __init__(self, chip, shards):
        self.chip = chip
        self.blk, self.ici, self.d2d, self.full, self.mats = {}, {}, {}, {}, {}
        token = jnp.zeros((8, LANES), F32)
        for k in ("w_in", "w_out", "w_ff1", "w_ff2"):
            self.blk[k] = shards[k].astype(BF16)
            land = lax.empty((N_CHIPS,) + self.blk[k].shape, BF16)
            send, recv, bufs, token = _copies_start(f"gather_{k}_ici", _ici_plan, [self.blk[k], land], token)
            self.ici[k] = (send, recv, bufs)
        self._hand_over("w_in", token)
        self._finish("w_in", token)

    def _hand_over(self, k, after):
        send, recv, bufs = self.ici.pop(k)
        _, land = _copies_wait(f"gather_{k}_ici_done", _ici_plan, send, recv, bufs, after)
        send, recv, bufs, _ = _copies_start(f"gather_{k}_d2d", _d2d_plan, [land], after)
        self.d2d[k] = (send, recv, bufs)

    def _finish(self, k, after):
        send, recv, bufs = self.d2d.pop(k)
        (land,) = _copies_wait(f"gather_{k}_d2d_done", _d2d_plan, send, recv, bufs, after)
        self.full[k] = lax.dynamic_update_slice(land, self.blk[k][None], (self.chip, 0, 0, 0))

    def mark(self, tag, value):
        for k, (first, second) in self.STAGES.items():
            if tag == first:
                self._hand_over(k, value)
                if second is None:
                    self._finish(k, value)
            elif tag == second:
                self._finish(k, value)

    def get(self, kind, l):
        if (kind, l) not in self.mats:
            a = self.full[kind]
            if kind == "w_in":
                m = _reorder_in(jnp.concatenate([a[k, l] for k in range(N_CHIPS)], axis=-1))
            elif kind == "w_ff1":
                m = jnp.concatenate([a[k, l] for k in range(N_CHIPS)], axis=-1)
            else:
                m = a[:, l].reshape(-1, D_MODEL)
            self.mats[kind, l] = m
        return self.mats[kind, l]


def _exchange_plan(bufs, send_sems, recv_sems):
    g, got = bufs
    x, y, c = _place()
    half = got.shape[1]
    cp = pltpu.make_async_remote_copy(
        src_ref=g.at[pl.ds(0, N_CHIPS), pl.ds((1 - c) * half, half)], dst_ref=got, send_sem=send_sems.at[0],
        recv_sem=recv_sems.at[0], device_id=(x, y, 1 - c), device_id_type=MESH)
    return [(cp, cp)]


def _scatter_plan(bufs, send_sems, recv_sems):
    p, rcv = bufs
    x, y, c = _place()
    plan = []
    for j, (px, py, _) in enumerate(_other_chips(x, y)):
        cp = pltpu.make_async_remote_copy(
            src_ref=p.at[j], dst_ref=rcv.at[j], send_sem=send_sems.at[j], recv_sem=recv_sems.at[j],
            device_id=(px, py, c), device_id_type=MESH)
        plan.append((cp, cp))
    return plan


def _share_plan(bufs, send_sems, recv_sems):
    mine, sib = bufs
    x, y, c = _place()
    cp = pltpu.make_async_remote_copy(
        src_ref=mine, dst_ref=sib, send_sem=send_sems.at[0], recv_sem=recv_sems.at[0],
        device_id=(x, y, 1 - c), device_id_type=MESH)
    return [(cp, cp)]


def _gather_small(name, pack):
    def body(p_ref, out_ref, send_sems, recv_sems, local_sem):
        x, y, c = _place()
        me = 4 * x + 2 * y + c
        local = pltpu.make_async_copy(p_ref, out_ref.at[me], local_sem)
        local.start()
        flips = [(fx, fy, fc) for fx in (0, 1) for fy in (0, 1) for fc in (0, 1)][1:]
        peers = [((1 - x) if fx else x, (1 - y) if fy else y, (1 - c) if fc else c) for fx, fy, fc in flips]

        def cp(j, slot, to):
            return pltpu.make_async_remote_copy(
                src_ref=p_ref, dst_ref=out_ref.at[slot], send_sem=send_sems.at[j], recv_sem=recv_sems.at[j],
                device_id=to, device_id_type=MESH)

        sends = [cp(j, me, to) for j, to in enumerate(peers)]
        for s in sends:
            s.start()
        for j, (px, py, pc) in enumerate(peers):
            cp(j, 4 * px + 2 * py + pc, (px, py, pc)).wait_recv()
        for s in sends:
            s.wait_send()
        local.wait()

    return pl.pallas_call(
        body, name=name, out_shape=jax.ShapeDtypeStruct((N_DEV,) + pack.shape, pack.dtype), in_specs=[HBM],
        out_specs=HBM,
        scratch_shapes=[pltpu.SemaphoreType.DMA((7,)), pltpu.SemaphoreType.DMA((7,)), pltpu.SemaphoreType.DMA],
    )(pack)


SLAB_ROWS = 256


I_CORE, I_CHIP, I_PEER = 0, 1, 2


def _peer_chip(pre, j):
    return jnp.where(j == 0, pre[I_PEER][0], jnp.where(j == 1, pre[I_PEER + 1][0], pre[I_PEER + 2][0]))


def _pair_sum(name, g, got, where):
    _, half, c = got.shape
    t = _pick_tile(half, (SLAB_ROWS, 128, 64, 8))
    per = half // t

    def body(i, v, car, pre):
        return [v[0] + v[1]], [], []

    return _seq_call(
        name, body, steps=3 * per, prefetch=where,
        ins=[(g, (None, t, c), lambda i, pre: (_peer_chip(pre, i // per), pre[I_CORE][0] * per + i % per, 0)),
             (got, (None, t, c), lambda i, pre: (_peer_chip(pre, i // per), i % per, 0))],
        outs=[((3, half, c), BF16, (None, t, c), lambda i, pre: (i // per, i % per, 0))])[0]


def _chip_sum(name, g, got, rcv, where):
    _, half, c = got.shape
    t = _pick_tile(half, (SLAB_ROWS, 128, 64, 8))
    per = half // t

    def body(i, v, car, pre):
        acc = v[0] + v[1]
        for part in v[2:]:
            acc = acc + part.astype(F32)
        return [acc], [], []

    return _seq_call(
        name, body, steps=per, prefetch=where,
        ins=[(g, (None, t, c), lambda i, pre: (pre[I_CHIP][0], pre[I_CORE][0] * per + i, 0)),
             (got, (None, t, c), lambda i, pre: (pre[I_CHIP][0], i, 0))]
        + [(rcv, (None, t, c), (lambda i, pre, _j=j: (_j, i, 0))) for j in range(3)],
        outs=[((half, c), F32, (t, c), lambda i, pre: (i, 0))])[0]


def _adamw_math(w, g, m, v):
    m = ADAM_B1 * m + (1.0 - ADAM_B1) * g
    v = ADAM_B2 * v + (1.0 - ADAM_B2) * (g * g)
    m_hat = m / (1.0 - ADAM_B1 ** ADAM_STEP)
    v_hat = v / (1.0 - ADAM_B2 ** ADAM_STEP)
    delta = -ADAM_LR * (m_hat / (jnp.sqrt(v_hat) + ADAM_EPS) + ADAM_WD * w)
    return delta, m, v


def _adamw(name, w, g, m, v):
    n, r, c = w.shape
    t = _pick_tile(r, (SLAB_ROWS, 128, 64, 8))
    per = r // t

    def body(i, vals, car):
        return list(_adamw_math(*vals)), [], []

    blk = lambda a: (a, (None, t, c), lambda i: (i // per, i % per, 0))
    out = ((n, r, c), F32, (None, t, c), lambda i: (i // per, i % per, 0))
    return _seq_call(name, body, steps=n * per, ins=[blk(w), blk(g), blk(m), blk(v)], outs=[out] * 3)


def _adamw_layer(name, layer, w, mine, sib, m, v, where, into):
    n, r, c = w.shape
    half = r // 2
    t = _pick_tile(half, (SLAB_ROWS, 128, 64, 8))
    per = half // t

    def body(i, vals, car, pre):
        g = jnp.where(i // per == pre[I_CORE][0], vals[1], vals[2])
        return [g] + list(_adamw_math(vals[0], g, vals[3], vals[4])), [], []

    of_layer = lambda a: (a, (None, t, c), lambda i, pre: (layer, i, 0))
    halves = lambda a: (a, (t, c), lambda i, pre: (i % per, 0))
    out = ((n, r, c), F32, (None, t, c), lambda i, pre: (layer, i, 0))
    return _seq_call(name, body, steps=2 * per, prefetch=where, into=into,
                     ins=[of_layer(w), halves(mine), halves(sib), of_layer(m), of_layer(v)], outs=[out] * 4)


def _ordered_sum(name, packs):
    n, r, c = packs.shape

    def body(i, v, car):
        acc = v[0][0]
        for k in range(1, n):
            acc = acc + v[0][k]
        return [acc], [], []

    return _seq_call(name, body, steps=1, ins=[_whole(packs)], outs=[((r, c), F32, (r, c), lambda i: (0, 0))])[0]


def _pack(arrays):
    flat = []
    for a in arrays:
        a = a.reshape(-1).astype(F32)
        pad = (-a.shape[0]) % LANES
        flat.append(jnp.pad(a, (0, pad)) if pad else a)
    v = jnp.concatenate(flat)
    pad = (-v.shape[0]) % (64 * LANES)
    if pad:
        v = jnp.pad(v, (0, pad))
    return v.reshape(-1, LANES)


def _unpack(pack, shapes):
    v = pack.reshape(-1)
    out, off = [], 0
    for s in shapes:
        n = math.prod(s)
        out.append(v[off:off + n].reshape(s))
        off += n + ((-n) % LANES)
    return out


def _reorder_in(full):
    g0 = 8 * GROUP_WIDTH
    pad = jnp.zeros(full.shape[:-1] + (LANES - 2 * N_HEADS,), full.dtype)
    return jnp.concatenate([full[..., :g0], full[..., g0 + 2 * N_HEADS:], full[..., g0:g0 + 2 * N_HEADS], pad], axis=-1)


def _restore_in(padded):
    g0 = 8 * GROUP_WIDTH
    wide = 12 * GROUP_WIDTH
    return jnp.concatenate([padded[..., :g0], padded[..., wide:wide + 2 * N_HEADS], padded[..., g0:wide]], axis=-1)


BIG = ("w_in", "w_out", "w_ff1", "w_ff2")


class _GradReducer:
    EARLY = ("w_ff2", "w_ff1")

    def __init__(self, where, state):
        self.where, self.state = where, state
        self.out = {k: () for k in BIG}
        self.flight = {}

    @staticmethod
    def _join(tokens):
        tokens = [t for t in tokens if t is not None]
        return functools.reduce(lambda a, b: a + b, tokens) if tokens else None

    def mark(self, tag, value, grad=None):
        if tag == "ff2_dw0":
            token = self._add_pairs(1, BIG, value)
            return self._join([token, self._swap(0, "w_ff2", grad, token)])
        if tag == "ff1_dw0":
            return self._swap(0, "w_ff1", grad, value)
        if tag == "out_proj_dx0":
            return self._add_pairs(0, self.EARLY, value)
        if tag == "conformer_bwd0":
            return self._add_chips(1, BIG, value)
        if tag == "in_proj_dx0":
            self._finish(1, BIG, value)
        return None

    def layer_done(self, l, grads, after):
        rest = [k for k in BIG if (l, k) not in self.flight]
        token = self._join([self._swap(l, k, grads[k], after) for k in rest])
        if l == 0:
            self._add_pairs(0, rest, after)
            self._add_chips(0, BIG, after)
            self._finish(0, BIG, after)
        return token

    def _swap(self, l, k, g, after):
        if k == "w_in":
            g = jnp.stack(jnp.split(_restore_in(g), N_CHIPS, axis=-1))
        got = lax.empty((N_CHIPS, g.shape[1] // 2, g.shape[2]), F32)
        send, recv, bufs, token = _copies_start(f"swap_halves_{k}{l}", _exchange_plan, [g, got], after, n=1)
        self.flight[l, k] = (send, recv, bufs)
        return token

    def _add_pairs(self, l, ks, after):
        tokens = []
        for k in ks:
            send, recv, bufs = self.flight[l, k]
            g, got = _copies_wait(f"swap_halves_{k}{l}_done", _exchange_plan, send, recv, bufs, after)
            pair = _pair_sum(f"pair_sum_{k}{l}", g, got, self.where)
            send, recv, bufs, token = _copies_start(
                f"scatter_{k}{l}", _scatter_plan, [pair, lax.empty(pair.shape, pair.dtype)], after)
            self.flight[l, k] = (send, recv, bufs, g, got)
            tokens.append(token)
        return self._join(tokens)

    def _add_chips(self, l, ks, after):
        tokens = []
        for k in ks:
            send, recv, bufs, g, got = self.flight[l, k]
            _, rcv = _copies_wait(f"scatter_{k}{l}_done", _scatter_plan, send, recv, bufs, after)
            mine = _chip_sum(f"chip_sum_{k}{l}", g, got, rcv, self.where)
            send, recv, bufs, token = _copies_start(
                f"share_{k}{l}", _share_plan, [mine, lax.empty(mine.shape, mine.dtype)], after, n=1)
            self.flight[l, k] = (send, recv, bufs)
            tokens.append(token)
        return self._join(tokens)

    def _finish(self, l, ks, after):
        for k in ks:
            send, recv, bufs = self.flight[l, k]
            mine, sib = _copies_wait(f"share_{k}{l}_done", _share_plan, send, recv, bufs, after)
            self.out[k] = _adamw_layer(f"adamw_{k}{l}", l, self.state[k][0], mine, sib, self.state[k][1],
                                       self.state[k][2], self.where, self.out[k])


def kernel(x, lower_bounds, norm_mix_pre, norm_mix_post, norm_ff_pre, norm_ff_post, w_in, w_out, hgrn_norm_w, gdn_conv_w, gdn_a_log, gdn_dt_bias, gdn_norm_w, gmlp_ln_w, gmlp_ln_b, gmlp_w_s, gmlp_b_s, conv_dw_w, conv_dw_b, conv_ln_w, conv_ln_b, w_ff1, w_ff2, loss_target, m_lower_bounds, m_norm_mix_pre, m_norm_mix_post, m_norm_ff_pre, m_norm_ff_post, m_w_in, m_w_out, m_hgrn_norm_w, m_gdn_conv_w, m_gdn_a_log, m_gdn_dt_bias, m_gdn_norm_w, m_gmlp_ln_w, m_gmlp_ln_b, m_gmlp_w_s, m_gmlp_b_s, m_conv_dw_w, m_conv_dw_b, m_conv_ln_w, m_conv_ln_b, m_w_ff1, m_w_ff2, v_lower_bounds, v_norm_mix_pre, v_norm_mix_post, v_norm_ff_pre, v_norm_ff_post, v_w_in, v_w_out, v_hgrn_norm_w, v_gdn_conv_w, v_gdn_a_log, v_gdn_dt_bias, v_gdn_norm_w, v_gmlp_ln_w, v_gmlp_ln_b, v_gmlp_w_s, v_gmlp_b_s, v_conv_dw_w, v_conv_dw_b, v_conv_ln_w, v_conv_ln_b, v_w_ff1, v_w_ff2):
    args = dict(locals())
    chip = 2 * lax.axis_index("x") + lax.axis_index("y")
    where = tuple(jnp.asarray(v, jnp.int32).reshape(1)
                  for v in (lax.axis_index("c"), chip, chip ^ 2, chip ^ 1, chip ^ 3))

    wts = _GatheredWeights(chip, dict(w_in=w_in, w_out=w_out, w_ff1=w_ff1, w_ff2=w_ff2))
    cut_shapes = [gdn_conv_w.shape, conv_dw_w.shape]
    cuts = _gather_small("gather_cut_small", _pack([gdn_conv_w, conv_dw_w]))
    cut_parts = [_unpack(cuts[2 * k], cut_shapes) for k in range(N_CHIPS)]
    sp = {k: args[k] for k in SMALL}
    sp["gdn_conv_w"] = jnp.concatenate([p[0] for p in cut_parts], axis=-1)
    sp["conv_dw_w"] = jnp.concatenate([p[1] for p in cut_parts], axis=-1)

    red = _GradReducer(where, {k: (args[k], args["m_" + k], args["v_" + k]) for k in BIG})
    loss_part, grad_x, small_g = _local_step(x[0], loss_target[0], sp, wts, red)

    names = SMALL + ["loss"]
    small_g["loss"] = loss_part.reshape(1)
    shapes = [small_g[k].shape for k in names]
    total = _ordered_sum("sum_small", _gather_small("gather_small", _pack([small_g[k] for k in names])))
    summed = dict(zip(names, _unpack(total, shapes)))
    loss = summed.pop("loss")[0]
    for k, width in (("gdn_conv_w", gdn_conv_w.shape[-1]), ("conv_dw_w", conv_dw_w.shape[-1])):
        summed[k] = lax.dynamic_slice_in_dim(summed[k], chip * width, width, axis=-1)

    grads, deltas, new_m, new_v = {}, {}, {}, {}
    for k in BIG:
        grads[k], deltas[k], new_m[k], new_v[k] = red.out[k]
    local_shapes = [args[k].shape for k in SMALL]
    packs = [_pack([src[k] for k in SMALL]) for src in (
        {k: args[k] for k in SMALL}, summed, {k: args["m_" + k] for k in SMALL}, {k: args["v_" + k] for k in SMALL})]
    d_s, m_s, v_s = _adamw("adamw_small", *[p[None] for p in packs])
    for k, d, mm, vv in zip(SMALL, _unpack(d_s[0], local_shapes), _unpack(m_s[0], local_shapes),
                            _unpack(v_s[0], local_shapes)):
        grads[k], deltas[k], new_m[k], new_v[k] = summed[k], d, mm, vv

    order = ["lower_bounds", "norm_mix_pre", "norm_mix_post", "norm_ff_pre", "norm_ff_post", "w_in", "w_out",
             "hgrn_norm_w", "gdn_conv_w", "gdn_a_log", "gdn_dt_bias", "gdn_norm_w", "gmlp_ln_w", "gmlp_ln_b",
             "gmlp_w_s", "gmlp_b_s", "conv_dw_w", "conv_dw_b", "conv_ln_w", "conv_ln_b", "w_ff1", "w_ff2"]
    return (loss, grad_x[None], *[grads[k] for k in order], *[deltas[k] for k in order],
            *[new_m[k] for k in order], *[new_v[k] for k in order])
```

```python
import functools
import math

import jax
import jax.numpy as jnp
from jax import lax
from jax.experimental import pallas as pl
from jax.experimental.pallas import tpu as pltpu

F32 = jnp.float32
BF16 = jnp.bfloat16

DEPTH = 2
D_MODEL = 2048
GROUP_WIDTH = 512
HEAD_DIM = 128
N_HEADS = 4
CHUNK = 64
SHORT_CONV = 4
MIX_CHUNK = 128
CONV_WIDTH = 31
D_FF = 4 * D_MODEL
D_IN_PROJ = 12 * GROUP_WIDTH + 2 * N_HEADS
EPS = 1e-6
NEG_BIG = -1e30
TINY = 1e-30
ADAM_LR = 0.001
ADAM_B1 = 0.9
ADAM_B2 = 0.999
ADAM_EPS = 1e-08
ADAM_WD = 0.01
ADAM_STEP = 10

LANES = 128
P_IN = 12 * GROUP_WIDTH + LANES
SUB_BLOCK = 16
HGRN_TILE = 128
GDN_TILE = 128
CONV_TILE = 128
CONV_HALO = 32
VMEM_LIMIT = 56 * 1024 * 1024
N_CHIPS = 4
N_DEV = 8
MESH = pl.DeviceIdType.MESH


_DIMS = {
    "nn": (((1,), (0,)), ((), ())),
    "nt": (((1,), (1,)), ((), ())),
    "tn": (((0,), (0,)), ((), ())),
}


def _split2(a):
    hi = a.astype(BF16)
    return hi, (a - hi.astype(F32)).astype(BF16)


def _raw_mm(a, b, mode, exact):
    dot = lambda p, q: lax.dot_general(p, q, _DIMS[mode], preferred_element_type=F32)
    if not exact:
        return dot(a.astype(BF16), b.astype(BF16))
    a_hi, a_lo = _split2(a)
    b_hi, b_lo = _split2(b)
    return dot(a_hi, b_hi) + (dot(a_hi, b_lo) + dot(a_lo, b_hi))


@functools.partial(jax.custom_vjp, nondiff_argnums=(2, 3))
def _mm(a, b, mode, exact):
    return _raw_mm(a, b, mode, exact)


def _mm_fwd(a, b, mode, exact):
    return _raw_mm(a, b, mode, exact), (a, b)


def _mm_bwd(mode, exact, res, g):
    a, b = res
    if mode == "nn":
        return _raw_mm(g, b, "nt", exact), _raw_mm(a, g, "tn", exact)
    if mode == "nt":
        return _raw_mm(g, b, "nn", exact), _raw_mm(g, a, "tn", exact)
    return _raw_mm(b, g, "nt", exact), _raw_mm(a, g, "nn", exact)


_mm.defvjp(_mm_fwd, _mm_bwd)


def _sig(x):
    return jax.nn.sigmoid(x)


def _silu(x):
    return x * jax.nn.sigmoid(x)


def _gelu(x):
    return 0.5 * x * (1.0 + lax.erf(x * (1.0 / math.sqrt(2.0))))


def _rms(x, w):
    return x * lax.rsqrt(jnp.mean(x * x, axis=-1, keepdims=True) + EPS) * w


def _ln(x, w, b):
    mu = jnp.mean(x, axis=-1, keepdims=True)
    xc = x - mu
    var = jnp.mean(xc * xc, axis=-1, keepdims=True)
    return xc * lax.rsqrt(var + EPS) * w + b


def _iota(shape, dim):
    return lax.broadcasted_iota(jnp.int32, shape, dim)


def _tri_mm(x, mode):
    n = x.shape[0]
    tri = (_iota((n, n), 0) >= _iota((n, n), 1)).astype(BF16)
    x1 = x.astype(BF16)
    r1 = x - x1.astype(F32)
    x2 = r1.astype(BF16)
    x3 = (r1 - x2.astype(F32)).astype(BF16)
    dot = lambda q: lax.dot_general(tri, q, _DIMS[mode], preferred_element_type=F32)
    return dot(x1) + (dot(x2) + dot(x3))


@jax.custom_vjp
def _cumsum_rows(x):
    return _tri_mm(x, "nn")


_cumsum_rows.defvjp(lambda x: (_tri_mm(x, "nn"), None), lambda _, g: (_tri_mm(g, "tn"),))


def _hgrn_head(q, k, v, b, st):
    n = q.shape[0]
    ii = _iota((n, 1), 0)
    zpad = jnp.zeros((SUB_BLOCK, HEAD_DIM), F32)
    k_ext = jnp.concatenate([zpad, k], axis=0)
    b_ext = jnp.concatenate([zpad, b], axis=0)
    v_ext = jnp.concatenate([zpad, v], axis=0)
    o = jnp.zeros((n, HEAD_DIM), F32)
    for d in range(SUB_BLOCK):
        ks = k_ext[SUB_BLOCK - d:SUB_BLOCK - d + n]
        bs = b_ext[SUB_BLOCK - d:SUB_BLOCK - d + n]
        vs = v_ext[SUB_BLOCK - d:SUB_BLOCK - d + n]
        e = jnp.exp(jnp.where((ii % SUB_BLOCK) >= d, b - bs, NEG_BIG))
        o = o + jnp.sum(q * ks * e, axis=-1, keepdims=True) * vs
    blocks = [o[0:SUB_BLOCK]]
    for blk in range(1, n // SUB_BLOCK):
        lo = SUB_BLOCK * blk
        r = b[lo - 1:lo]
        a_q = q[lo:lo + SUB_BLOCK] * jnp.exp(b[lo:lo + SUB_BLOCK] - r)
        b_k = jnp.where(ii < lo, k * jnp.exp(jnp.minimum(r - b, 0.0)), 0.0)
        sc = _mm(a_q, b_k, "nt", False)
        blocks.append(o[lo:lo + SUB_BLOCK] + _mm(sc, v, "nn", False))
    o = jnp.concatenate(blocks, axis=0)
    o = o + _mm(q * jnp.exp(b), st, "nt", False)
    b_end = b[n - 1:n]
    st_new = st * jnp.exp(b_end) + _mm(v, k * jnp.exp(b_end - b), "tn", False)
    return o, st_new


def _hgrn_chunk(layer, lbp, nw, aq, af, ai, ag, states):
    rows = [lbp[i:i + 1, :] for i in range(DEPTH)]
    mx = functools.reduce(jnp.maximum, rows)
    es = [jnp.exp(r - mx) for r in rows]
    den = functools.reduce(lambda p, s: p + s, es)
    soft = [e / den for e in es]
    lb = functools.reduce(lambda p, s: p + s, soft[:layer + 1]) - soft[0]
    f = lb + (1.0 - lb) * _sig(af)
    logf = jnp.log(jnp.maximum(f, TINY))
    k = (1.0 - lb) * _sig(-af)
    q = _silu(aq)
    b = _cumsum_rows(logf)
    outs, new_states = [], []
    for h in range(N_HEADS):
        sl = slice(HEAD_DIM * h, HEAD_DIM * (h + 1))
        o, st = _hgrn_head(q[:, sl], k[:, sl], ai[:, sl], b[:, sl], states[h])
        outs.append(_rms(o, nw) * _silu(ag[:, sl]))
        new_states.append(st)
    return jnp.concatenate(outs, axis=1), new_states


def _hgrn_tile(layer, lbp, nw, aq, af, ai, ag, states):
    outs = []
    for ci in range(aq.shape[0] // CHUNK):
        rs = slice(CHUNK * ci, CHUNK * (ci + 1))
        o, states = _hgrn_chunk(layer, lbp, nw, aq[rs], af[rs], ai[rs], ag[rs], states)
        outs.append(o)
    return jnp.concatenate(outs, axis=0), states


def _short_conv(prev, cur, w):
    n = cur.shape[0]
    ext = jnp.concatenate([prev[n - 8:n], cur], axis=0)
    y = jnp.zeros_like(cur)
    for t in range(SHORT_CONV):
        off = 8 - (SHORT_CONV - 1) + t
        y = y + w[t:t + 1, :] * ext[off:off + n]
    return _silu(y)


def _gdn_tile(cw, alog, dtb, nw, pq, pk, pv, cq, ck, cv, bz, ab, states):
    n = cq.shape[0]
    q_all = _short_conv(pq, cq, cw[:, 0:GROUP_WIDTH])
    k_all = _short_conv(pk, ck, cw[:, GROUP_WIDTH:2 * GROUP_WIDTH])
    v_all = _short_conv(pv, cv, cw[:, 2 * GROUP_WIDTH:3 * GROUP_WIDTH])
    beta_all = _sig(ab)
    g_all = -jnp.exp(alog) * jax.nn.softplus(ab + dtb)
    units = [(ci, h) for ci in range(n // CHUNK) for h in range(N_HEADS)]
    gc_all = [_cumsum_rows(g_all[CHUNK * ci:CHUNK * (ci + 1)]) for ci in range(n // CHUNK)]
    gc_t = [g.T for g in gc_all]
    ii = _iota((CHUNK, CHUNK), 0)
    jj = _iota((CHUNK, CHUNK), 1)
    eye = (ii == jj).astype(F32)

    def cut(a, ci, h):
        return a[CHUNK * ci:CHUNK * (ci + 1), HEAD_DIM * h:HEAD_DIM * (h + 1)]

    q = [cut(q_all, ci, h) for ci, h in units]
    k = [cut(k_all, ci, h) for ci, h in units]
    v = [cut(v_all, ci, h) for ci, h in units]
    q = [t * lax.rsqrt(jnp.sum(t * t, axis=-1, keepdims=True) + EPS) * (HEAD_DIM ** -0.5) for t in q]
    k = [t * lax.rsqrt(jnp.sum(t * t, axis=-1, keepdims=True) + EPS) for t in k]
    beta = [beta_all[CHUNK * ci:CHUNK * (ci + 1), h:h + 1] for ci, h in units]
    gc = [gc_all[ci][:, N_HEADS + h:N_HEADS + h + 1] for ci, h in units]
    gcr = [gc_t[ci][N_HEADS + h:N_HEADS + h + 1, :] for ci, h in units]
    gamma = [jnp.exp(jnp.where(ii >= jj, a - b, NEG_BIG)) for a, b in zip(gc, gcr)]
    kb = [a * b for a, b in zip(k, beta)]
    m = [jnp.where(ii > jj, _mm(a, b, "nt", False) * g, 0.0) for a, b, g in zip(kb, k, gamma)]
    inv = [eye - t for t in m]
    p = m
    for _ in range(max(1, int(math.ceil(math.log2(CHUNK))) - 1)):
        p = [_mm(t, t, "nn", True) for t in p]
        inv = [a + _mm(a, t, "nn", True) for a, t in zip(inv, p)]
    eg = [jnp.exp(t) for t in gc]
    u = [_mm(a, b * c, "nn", True) for a, b, c in zip(inv, v, beta)]
    w = [_mm(a, b * c, "nn", True) for a, b, c in zip(inv, kb, eg)]
    qk = [_mm(a, b, "nt", False) * g for a, b, g in zip(q, k, gamma)]
    qd = [a * b for a, b in zip(q, eg)]
    g_end = [t[CHUNK - 1:CHUNK] for t in gc]
    kd = [a * jnp.exp(e - g) for a, e, g in zip(k, g_end, gc)]
    states = list(states)
    outs = {}
    for i, (ci, h) in enumerate(units):
        st = states[h]
        v_new = u[i] - _mm(w[i], st, "nt", False)
        outs[ci, h] = _mm(qd[i], st, "nt", False) + _mm(qk[i], v_new, "nn", False)
        states[h] = st * jnp.exp(g_end[i]) + _mm(v_new, kd[i], "tn", False)
    rows = []
    for ci in range(n // CHUNK):
        rows.append(jnp.concatenate(
            [_rms(outs[ci, h], nw) * _silu(cut(bz, ci, h)) for h in range(N_HEADS)], axis=1))
    return jnp.concatenate(rows, axis=0), states


def _gmlp_tile(ln_w, ln_b, ws, bs_cols, cu, cv):
    u = _gelu(cu)
    v = _ln(_gelu(cv), ln_w, ln_b)
    n = cu.shape[0]
    tril = _iota((n, n), 0) >= _iota((n, n), 1)
    outs = []
    for h in range(N_HEADS):
        sl = slice(HEAD_DIM * h, HEAD_DIM * (h + 1))
        wc = jnp.where(tril, ws[h], 0.0)
        outs.append(_mm(wc, v[:, sl], "nn", False) + bs_cols[:, h:h + 1])
    return u * jnp.concatenate(outs, axis=1)


def _conformer_tile(dw_w, dw_b, ln_w, ln_b, pa, pg, ca, cg):
    n = ca.shape[0]
    yp = pa[n - CONV_HALO:n] * _sig(pg[n - CONV_HALO:n])
    ext = jnp.concatenate([yp, ca * _sig(cg)], axis=0)
    acc = jnp.zeros_like(ca)
    for t in range(CONV_WIDTH):
        off = CONV_HALO - (CONV_WIDTH - 1) + t
        acc = acc + dw_w[t:t + 1, :] * ext[off:off + n]
    return _silu(_ln(acc + dw_b, ln_w, ln_b))


def _seq_call(name, body, *, steps, ins, outs, accs=(), carries=(), reverse=False, prefetch=None, into=()):
    n_in, n_out, n_acc, n_car = len(ins), len(outs), len(accs), len(carries)
    n_into = len(into)
    n_pre = 0 if prefetch is None else len(prefetch)

    def logical(g):
        return (steps - 1 - g) if reverse else g

    def kern(*refs):
        pre = refs[:n_pre]
        refs = refs[n_pre:]
        in_refs = refs[:n_in]
        refs = refs[n_in + n_into:]
        out_refs = refs[:n_out]
        acc_refs = refs[n_out:n_out + n_acc]
        car_refs = refs[n_out + n_acc:]
        g = pl.program_id(0)

        @pl.when(g == 0)
        def _():
            for r in list(acc_refs) + list(car_refs):
                r[...] = jnp.zeros(r.shape, r.dtype)

        o, a, c = body(logical(g), [r[...] for r in in_refs], [r[...] for r in car_refs], *((pre,) if n_pre else ()))
        for r, v in zip(out_refs, o, strict=True):
            r[...] = v.astype(r.dtype)
        for r, v in zip(acc_refs, a, strict=True):
            r[...] += v
        for r, v in zip(car_refs, c, strict=True):
            r[...] = v

    def spec(block, fn):
        return pl.BlockSpec(block, lambda g, *pre: fn(logical(g), *((pre,) if n_pre else ())))

    in_specs = [spec(bs, fn) for (_, bs, fn) in ins] + [HBM] * n_into
    out_specs = [spec(bs, fn) for (_, _, bs, fn) in outs]
    out_specs += [pl.BlockSpec(shape, lambda g, *pre, _n=len(shape): (0,) * _n) for (shape, _) in accs]
    out_shape = [jax.ShapeDtypeStruct(s, d) for (s, d, _, _) in outs]
    out_shape += [jax.ShapeDtypeStruct(s, d) for (s, d) in accs]
    grid_spec = pltpu.PrefetchScalarGridSpec(
        num_scalar_prefetch=n_pre, grid=(steps,), in_specs=in_specs, out_specs=out_specs,
        scratch_shapes=[pltpu.VMEM(s, d) for (s, d) in carries])
    args = ([] if prefetch is None else list(prefetch)) + [a for (a, _, _) in ins] + list(into)
    return pl.pallas_call(
        kern, name=name, grid_spec=grid_spec, out_shape=out_shape,
        input_output_aliases={n_pre + n_in + i: i for i in range(n_into)},
        compiler_params=pltpu.CompilerParams(dimension_semantics=("arbitrary",), vmem_limit_bytes=VMEM_LIMIT),
    )(*args)


def _whole(a):
    nd = a.ndim
    return (a, a.shape, lambda i, *pre: (0,) * nd)


def _rows(a, tile, col=0, width=None, shift=0):
    width = a.shape[1] if width is None else width
    if shift:
        return (a, (tile, width), lambda i, *pre: (jnp.maximum(i + shift, 0), col))
    return (a, (tile, width), lambda i, *pre: (i, col))


def _row_out(n_rows, width, dtype, tile):
    return ((n_rows, width), dtype, (tile, width), lambda i, *pre: (i, 0))


def _pick_tile(n, prefs):
    for t in prefs:
        if n % t == 0:
            return t
    return n


def _matmul(name, a, b, mode, out_dtypes, epilogue=None, extras=(), place=None, max_tm=1024):
    if mode == "nn":
        (m, k), n = a.shape, b.shape[1]
    elif mode == "nt":
        (m, k), n = a.shape, b.shape[0]
    else:
        (k, m), n = a.shape, b.shape[1]
    tm = _pick_tile(m, tuple(t for t in (1024, 512, 256, 128) if t <= max_tm))
    tn = _pick_tile(n, (1024, 896, 512, 256, 128))
    tk = _pick_tile(k, (2048, 896, 512, 256, 128))
    nk = k // tk
    n_ex = len(extras)
    n_out = len(out_dtypes)

    def kern(*refs):
        a_ref, b_ref = refs[0], refs[1]
        ex_refs = refs[2:2 + n_ex]
        out_refs = refs[2 + n_ex:2 + n_ex + n_out]
        acc_ref = refs[2 + n_ex + n_out]
        kk = pl.program_id(2)

        @pl.when(kk == 0)
        def _():
            acc_ref[...] = jnp.zeros(acc_ref.shape, F32)

        acc_ref[...] += lax.dot_general(a_ref[...], b_ref[...], _DIMS[mode], preferred_element_type=F32)

        @pl.when(kk == nk - 1)
        def _():
            acc = acc_ref[...]
            vals = (acc,) if epilogue is None else epilogue(acc, *[r[...] for r in ex_refs])
            for r, v in zip(out_refs, vals, strict=True):
                r[...] = v.astype(r.dtype)

    if mode == "tn":
        a_spec = pl.BlockSpec((tk, tm), lambda i, j, kk: (kk, i))
    else:
        a_spec = pl.BlockSpec((tm, tk), lambda i, j, kk: (i, kk))
    if mode == "nt":
        b_spec = pl.BlockSpec((tn, tk), lambda i, j, kk: (j, kk))
    else:
        b_spec = pl.BlockSpec((tk, tn), lambda i, j, kk: (kk, j))
    tile = pl.BlockSpec((tm, tn), lambda i, j, kk: (i, j))
    out_specs = [tile] * n_out
    out_shape = [jax.ShapeDtypeStruct((m, n), d) for d in out_dtypes]
    if place is not None:
        shape, block_fn, index_fn = place
        out_specs = [pl.BlockSpec(block_fn(tm, tn), lambda i, j, kk: index_fn(i, j, tm, tn))]
        out_shape = [jax.ShapeDtypeStruct(shape, out_dtypes[0])]
    return pl.pallas_call(
        kern, name=name, grid=(m // tm, n // tn, nk),
        in_specs=[a_spec, b_spec] + [tile] * n_ex,
        out_specs=out_specs, out_shape=out_shape,
        scratch_shapes=[pltpu.VMEM((tm, tn), F32)],
        compiler_params=pltpu.CompilerParams(
            dimension_semantics=("parallel", "parallel", "arbitrary"), vmem_limit_bytes=VMEM_LIMIT),
    )(a, b, *extras)


ROW_TILE = 256


def _rms_fwd(name, x, w):
    s, d = x.shape
    t = _pick_tile(s, (ROW_TILE,))

    def body(i, v, c):
        return [_rms(v[0], v[1])], [], []

    return _seq_call(name, body, steps=s // t, ins=[_rows(x, t), _whole(w)], outs=[_row_out(s, d, BF16, t)])[0]


def _resid_rms_fwd(name, x, y, w):
    s, d = x.shape
    t = _pick_tile(s, (ROW_TILE,))

    def body(i, v, c):
        return [v[0] + _rms(v[1], v[2])], [], []

    return _seq_call(name, body, steps=s // t, ins=[_rows(x, t), _rows(y, t), _whole(w)],
                     outs=[_row_out(s, d, F32, t)])[0]


def _rms_bwd(name, x, w, dh, dres):
    s, d = x.shape
    t = _pick_tile(s, (ROW_TILE,))

    def body(i, v, c):
        _, vjp = jax.vjp(_rms, v[0], v[1])
        dx, dw = vjp(v[2])
        return [dx + v[3]], [dw], []

    return _seq_call(name, body, steps=s // t, ins=[_rows(x, t), _whole(w), _rows(dh, t), _rows(dres, t)],
                     outs=[_row_out(s, d, F32, t)], accs=[((1, d), F32)])


def _resid_rms_bwd(name, y, w, dxo):
    s, d = y.shape
    t = _pick_tile(s, (ROW_TILE,))

    def body(i, v, c):
        _, vjp = jax.vjp(_rms, v[0], v[1])
        dy, dw = vjp(v[2])
        return [dy], [dw], []

    return _seq_call(name, body, steps=s // t, ins=[_rows(y, t), _whole(w), _rows(dxo, t)],
                     outs=[_row_out(s, d, BF16, t)], accs=[((1, d), F32)])


def _loss_head(name, y, target):
    s, d = y.shape
    t = _pick_tile(s, (ROW_TILE,))

    def body(i, v, c):
        err = v[0] - v[1]
        part = 0.5 * jnp.sum(jnp.mean(err * err, axis=-1, keepdims=True))
        return [err * (1.0 / d)], [jnp.full((1, LANES), part, F32)], []

    return _seq_call(name, body, steps=s // t, ins=[_rows(y, t), _rows(target, t)],
                     outs=[_row_out(s, d, F32, t)], accs=[((1, LANES), F32)])


SEG = {n: i for i, n in enumerate(
    ["a_q", "a_f", "a_i", "a_g", "b_q", "b_k", "b_v", "b_z", "c_u", "c_v", "d_a", "d_gate"])}
AB_COL = 12 * GROUP_WIDTH // LANES


def _seg(proj, name, tile, shift=0):
    return _rows(proj, tile, col=SEG[name], width=GROUP_WIDTH, shift=shift)


def _state_block():
    return (1, N_HEADS * HEAD_DIM, HEAD_DIM), lambda i, *pre: (i, 0, 0)


def _split_states(blk):
    return [blk[0, HEAD_DIM * h:HEAD_DIM * (h + 1), :] for h in range(N_HEADS)]


STATE_CARRIES = [((HEAD_DIM, HEAD_DIM), F32)] * N_HEADS


def _hgrn_fwd(layer, proj, lbp, nw):
    s = proj.shape[0]
    n = s // HGRN_TILE
    sb, sf = _state_block()

    def body(i, v, st):
        o, new = _hgrn_tile(layer, v[0], v[1], v[2], v[3], v[4], v[5], st)
        return [o, jnp.concatenate(st, axis=0)[None]], [], new

    return _seq_call(
        f"hgrn_fwd{layer}", body, steps=n,
        ins=[_whole(lbp), _whole(nw)] + [_seg(proj, k, HGRN_TILE) for k in ("a_q", "a_f", "a_i", "a_g")],
        outs=[_row_out(s, GROUP_WIDTH, BF16, HGRN_TILE), ((n, N_HEADS * HEAD_DIM, HEAD_DIM), F32, sb, sf)],
        carries=STATE_CARRIES)


def _hgrn_bwd(layer, proj, lbp, nw, states, dmix):
    s = proj.shape[0]
    n = s // HGRN_TILE
    sb, sf = _state_block()

    def body(i, v, dst):
        st = _split_states(v[6])

        def f(lbp_, nw_, aq, af, ai, ag, *st_):
            return _hgrn_tile(layer, lbp_, nw_, aq, af, ai, ag, list(st_))

        _, vjp = jax.vjp(f, v[0], v[1], v[2], v[3], v[4], v[5], *st)
        g = vjp((v[7], list(dst)))
        return [jnp.concatenate(g[2:6], axis=1)], [g[0], g[1]], list(g[6:])

    return _seq_call(
        f"hgrn_bwd{layer}", body, steps=n, reverse=True,
        ins=[_whole(lbp), _whole(nw)] + [_seg(proj, k, HGRN_TILE) for k in ("a_q", "a_f", "a_i", "a_g")]
        + [(states, sb, sf), _rows(dmix, HGRN_TILE, col=0, width=GROUP_WIDTH)],
        outs=[_row_out(s, 4 * GROUP_WIDTH, BF16, HGRN_TILE)],
        accs=[(lbp.shape, F32), (nw.shape, F32)], carries=STATE_CARRIES)


def _gdn_ins(proj, cw, alog, dtb, nw):
    return ([_whole(cw), _whole(alog), _whole(dtb), _whole(nw)]
            + [_seg(proj, k, GDN_TILE, shift=-1) for k in ("b_q", "b_k", "b_v")]
            + [_seg(proj, k, GDN_TILE) for k in ("b_q", "b_k", "b_v", "b_z")]
            + [_rows(proj, GDN_TILE, col=AB_COL, width=LANES)])


def _mask_prev(i, vals):
    keep = (i > 0).astype(F32)
    return [p * keep for p in vals]


def _gdn_fwd(layer, proj, cw, alog, dtb, nw):
    s = proj.shape[0]
    n = s // GDN_TILE
    sb, sf = _state_block()

    def body(i, v, st):
        prev = _mask_prev(i, v[4:7])
        o, new = _gdn_tile(v[0], v[1], v[2], v[3], *prev, *v[7:12], st)
        return [o, jnp.concatenate(st, axis=0)[None]], [], new

    return _seq_call(
        f"gdn_fwd{layer}", body, steps=n, ins=_gdn_ins(proj, cw, alog, dtb, nw),
        outs=[_row_out(s, GROUP_WIDTH, BF16, GDN_TILE), ((n, N_HEADS * HEAD_DIM, HEAD_DIM), F32, sb, sf)],
        carries=STATE_CARRIES)


def _gdn_bwd(layer, proj, cw, alog, dtb, nw, states, dmix):
    s = proj.shape[0]
    n = s // GDN_TILE
    sb, sf = _state_block()

    def body(i, v, car):
        dst, dprev = car[:N_HEADS], car[N_HEADS:]
        prev = _mask_prev(i, v[4:7])
        st = _split_states(v[12])

        def f(cw_, alog_, dtb_, nw_, pq, pk, pv, cq, ck, cv, bz, ab, *st_):
            return _gdn_tile(cw_, alog_, dtb_, nw_, pq, pk, pv, cq, ck, cv, bz, ab, list(st_))

        _, vjp = jax.vjp(f, v[0], v[1], v[2], v[3], *prev, *v[7:12], *st)
        g = vjp((v[13], list(dst)))
        dcur = [g[7] + dprev[0], g[8] + dprev[1], g[9] + dprev[2], g[10]]
        return ([jnp.concatenate(dcur, axis=1), g[11]], list(g[0:4]), list(g[12:]) + list(g[4:7]))

    return _seq_call(
        f"gdn_bwd{layer}", body, steps=n, reverse=True,
        ins=_gdn_ins(proj, cw, alog, dtb, nw) + [(states, sb, sf), _rows(dmix, GDN_TILE, col=1, width=GROUP_WIDTH)],
        outs=[_row_out(s, 4 * GROUP_WIDTH, BF16, GDN_TILE), _row_out(s, LANES, BF16, GDN_TILE)],
        accs=[(cw.shape, F32), (alog.shape, F32), (dtb.shape, F32), (nw.shape, F32)],
        carries=STATE_CARRIES + [((GDN_TILE, GROUP_WIDTH), F32)] * 3)


def _gmlp_fwd(layer, proj, ln_w, ln_b, ws, bs_cols):
    s = proj.shape[0]

    def body(i, v, c):
        return [_gmlp_tile(*v)], [], []

    return _seq_call(
        f"gmlp_fwd{layer}", body, steps=s // MIX_CHUNK,
        ins=[_whole(ln_w), _whole(ln_b), _whole(ws), _whole(bs_cols),
             _seg(proj, "c_u", MIX_CHUNK), _seg(proj, "c_v", MIX_CHUNK)],
        outs=[_row_out(s, GROUP_WIDTH, BF16, MIX_CHUNK)])[0]


def _gmlp_bwd(layer, proj, ln_w, ln_b, ws, bs_cols, dmix):
    s = proj.shape[0]

    def body(i, v, c):
        _, vjp = jax.vjp(_gmlp_tile, *v[:6])
        g = vjp(v[6])
        return [jnp.concatenate(g[4:6], axis=1)], list(g[0:4]), []

    return _seq_call(
        f"gmlp_bwd{layer}", body, steps=s // MIX_CHUNK,
        ins=[_whole(ln_w), _whole(ln_b), _whole(ws), _whole(bs_cols),
             _seg(proj, "c_u", MIX_CHUNK), _seg(proj, "c_v", MIX_CHUNK),
             _rows(dmix, MIX_CHUNK, col=2, width=GROUP_WIDTH)],
        outs=[_row_out(s, 2 * GROUP_WIDTH, BF16, MIX_CHUNK)],
        accs=[(ln_w.shape, F32), (ln_b.shape, F32), (ws.shape, F32), (bs_cols.shape, F32)])


def _conformer_ins(proj, dw_w, dw_b, ln_w, ln_b):
    return ([_whole(dw_w), _whole(dw_b), _whole(ln_w), _whole(ln_b)]
            + [_seg(proj, k, CONV_TILE, shift=-1) for k in ("d_a", "d_gate")]
            + [_seg(proj, k, CONV_TILE) for k in ("d_a", "d_gate")])


def _conformer_fwd(layer, proj, dw_w, dw_b, ln_w, ln_b):
    s = proj.shape[0]

    def body(i, v, c):
        prev = _mask_prev(i, v[4:6])
        return [_conformer_tile(v[0], v[1], v[2], v[3], *prev, v[6], v[7])], [], []

    return _seq_call(
        f"conformer_fwd{layer}", body, steps=s // CONV_TILE, ins=_conformer_ins(proj, dw_w, dw_b, ln_w, ln_b),
        outs=[_row_out(s, GROUP_WIDTH, BF16, CONV_TILE)])[0]


def _conformer_bwd(layer, proj, dw_w, dw_b, ln_w, ln_b, dmix):
    s = proj.shape[0]

    def body(i, v, dprev):
        prev = _mask_prev(i, v[4:6])
        _, vjp = jax.vjp(_conformer_tile, v[0], v[1], v[2], v[3], *prev, v[6], v[7])
        g = vjp(v[8])
        return [jnp.concatenate([g[6] + dprev[0], g[7] + dprev[1]], axis=1)], list(g[0:4]), [g[4], g[5]]

    return _seq_call(
        f"conformer_bwd{layer}", body, steps=s // CONV_TILE, reverse=True,
        ins=_conformer_ins(proj, dw_w, dw_b, ln_w, ln_b) + [_rows(dmix, CONV_TILE, col=3, width=GROUP_WIDTH)],
        outs=[_row_out(s, 2 * GROUP_WIDTH, BF16, CONV_TILE)],
        accs=[(dw_w.shape, F32), (dw_b.shape, F32), (ln_w.shape, F32), (ln_b.shape, F32)],
        carries=[((CONV_TILE, GROUP_WIDTH), F32)] * 2)


SMALL = ["lower_bounds", "norm_mix_pre", "norm_mix_post", "norm_ff_pre", "norm_ff_post", "hgrn_norm_w",
         "gdn_conv_w", "gdn_a_log", "gdn_dt_bias", "gdn_norm_w", "gmlp_ln_w", "gmlp_ln_b", "gmlp_w_s",
         "gmlp_b_s", "conv_dw_w", "conv_dw_b", "conv_ln_w", "conv_ln_b"]


def _gate_row(v):
    return jnp.pad(v.reshape(1, N_HEADS), ((0, 0), (N_HEADS, LANES - 2 * N_HEADS)))


def _cut_rows(rows, cols):
    return ((N_CHIPS, rows, cols), lambda tm, tn: (None, tm, tn),
            lambda i, j, tm, tn: (i // (rows // tm), i % (rows // tm), j))


def _cut_cols(rows, cols):
    return ((N_CHIPS, rows, cols), lambda tm, tn: (None, tm, tn),
            lambda i, j, tm, tn: (j // (cols // tn), i, j % (cols // tn)))


def _relu2(acc):
    r = jnp.maximum(acc, 0.0)
    return acc, r * r


def _relu2_bwd(acc, u):
    return (2.0 * jnp.maximum(u, 0.0) * acc,)


def _local_step(x, target, sp, wts, red):
    row = lambda v: v.reshape(1, -1)
    saved = []
    w_in, w_out, w_ff1, w_ff2 = [], [], [], []
    for l in range(DEPTH):
        par = dict(
            lbp=sp["lower_bounds"], hn=row(sp["hgrn_norm_w"][l]), cw=sp["gdn_conv_w"][l],
            alog=_gate_row(sp["gdn_a_log"][l]), dtb=_gate_row(sp["gdn_dt_bias"][l]), gn=row(sp["gdn_norm_w"][l]),
            glw=row(sp["gmlp_ln_w"][l]), glb=row(sp["gmlp_ln_b"][l]), ws=sp["gmlp_w_s"][l],
            bsc=jnp.pad(sp["gmlp_b_s"][l].T, ((0, 0), (0, LANES - N_HEADS))),
            dww=sp["conv_dw_w"][l], dwb=row(sp["conv_dw_b"][l]), clw=row(sp["conv_ln_w"][l]),
            clb=row(sp["conv_ln_b"][l]), n1=row(sp["norm_mix_pre"][l]), n2=row(sp["norm_mix_post"][l]),
            n3=row(sp["norm_ff_pre"][l]), n4=row(sp["norm_ff_post"][l]))
        h = _rms_fwd(f"norm_mix_pre{l}", x, par["n1"])
        w_in.append(wts.get("w_in", l))
        proj = _matmul(f"in_proj{l}", h, w_in[l], "nn", [F32])[0]
        wts.mark(f"proj{l}", proj)
        o_a, st_a = _hgrn_fwd(l, proj, par["lbp"], par["hn"])
        o_b, st_b = _gdn_fwd(l, proj, par["cw"], par["alog"], par["dtb"], par["gn"])
        wts.mark(f"gdn{l}", o_b)
        o_c = _gmlp_fwd(l, proj, par["glw"], par["glb"], par["ws"], par["bsc"])
        o_d = _conformer_fwd(l, proj, par["dww"], par["dwb"], par["clw"], par["clb"])
        wts.mark(f"conformer{l}", o_d)
        mix = jnp.concatenate([o_a, o_b, o_c, o_d], axis=1)
        w_out.append(wts.get("w_out", l))
        y = _matmul(f"out_proj{l}", mix, w_out[l], "nn", [F32])[0]
        wts.mark(f"y{l}", y)
        x1 = _resid_rms_fwd(f"norm_mix_post{l}", x, y, par["n2"])
        h2 = _rms_fwd(f"norm_ff_pre{l}", x1, par["n3"])
        w_ff1.append(wts.get("w_ff1", l))
        u, act = _matmul(f"ff1_{l}", h2, w_ff1[l], "nn", [F32, BF16], epilogue=_relu2)
        wts.mark(f"u{l}", u)
        w_ff2.append(wts.get("w_ff2", l))
        y2 = _matmul(f"ff2_{l}", act, w_ff2[l], "nn", [F32])[0]
        wts.mark(f"ff2_{l}", y2)
        x2 = _resid_rms_fwd(f"norm_ff_post{l}", x1, y2, par["n4"])
        saved.append(dict(par=par, x=x, h=h, proj=proj, st_a=st_a, st_b=st_b, mix=mix, y=y, x1=x1, h2=h2,
                          u=u, act=act, y2=y2))
        x = x2

    dx, loss_acc = _loss_head("loss_head", x, target)
    loss_part = loss_acc[0, 0]

    gs = {k: [None] * DEPTH for k in SMALL if k != "lower_bounds"}
    g_lb = jnp.zeros((DEPTH, GROUP_WIDTH), F32)
    def behind(row, token):
        return row if token is None else row + token[:1, :1]

    def join(a, b):
        return b if a is None else a if b is None else a + b

    tok = None
    for l in reversed(range(DEPTH)):
        sv = saved[l]
        par = sv["par"]
        dy2, dn4 = _resid_rms_bwd(f"norm_ff_post_bwd{l}", sv["y2"], behind(par["n4"], tok), dx)
        du = _matmul(f"ff2_dx{l}", dy2, w_ff2[l], "nt", [BF16], epilogue=_relu2_bwd, extras=(sv["u"],))[0]
        g_ff2 = _matmul(f"ff2_dw{l}", sv["act"], dy2, "tn", [F32], place=_cut_rows(D_FF // N_CHIPS, D_MODEL))[0]
        tok = red.mark(f"ff2_dw{l}", du, g_ff2)
        g_ff1 = _matmul(f"ff1_dw{l}", sv["h2"], du, "tn", [F32], place=_cut_cols(D_MODEL, D_FF // N_CHIPS))[0]
        tok = join(tok, red.mark(f"ff1_dw{l}", du, g_ff1))
        dh2 = _matmul(f"ff1_dx{l}", du, w_ff1[l], "nt", [F32])[0]
        dx1, dn3 = _rms_bwd(f"norm_ff_pre_bwd{l}", sv["x1"], behind(par["n3"], tok), dh2, dx)
        dy, dn2 = _resid_rms_bwd(f"norm_mix_post_bwd{l}", sv["y"], par["n2"], dx1)
        dmix = _matmul(f"out_proj_dx{l}", dy, w_out[l], "nt", [F32])[0]
        tok = red.mark(f"out_proj_dx{l}", dmix)
        g_out = _matmul(f"out_proj_dw{l}", sv["mix"], dy, "tn", [F32], max_tm=GROUP_WIDTH,
                        place=_cut_rows(GROUP_WIDTH, D_MODEL))[0]
        proj = sv["proj"]
        dp_a, dlb, dhn = _hgrn_bwd(l, proj, par["lbp"], behind(par["hn"], tok), sv["st_a"], dmix)
        dp_b, dp_ab, dcw, dalog, ddtb, dgn = _gdn_bwd(
            l, proj, par["cw"], par["alog"], par["dtb"], par["gn"], sv["st_b"], dmix)
        dp_c, dglw, dglb, dws, dbsc = _gmlp_bwd(l, proj, par["glw"], par["glb"], par["ws"], par["bsc"], dmix)
        dp_d, ddww, ddwb, dclw, dclb = _conformer_bwd(
            l, proj, par["dww"], par["dwb"], par["clw"], par["clb"], dmix)
        tok = red.mark(f"conformer_bwd{l}", dp_d)
        dproj = jnp.concatenate([dp_a, dp_b, dp_c, dp_d, dp_ab], axis=1)
        g_in = _matmul(f"in_proj_dw{l}", sv["h"], dproj, "tn", [F32])[0]
        dh = _matmul(f"in_proj_dx{l}", dproj, w_in[l], "nt", [F32])[0]
        red.mark(f"in_proj_dx{l}", dh)
        dx, dn1 = _rms_bwd(f"norm_mix_pre_bwd{l}", sv["x"], behind(par["n1"], tok), dh, dx1)
        tok = red.layer_done(l, dict(w_in=g_in, w_out=g_out, w_ff1=g_ff1, w_ff2=g_ff2), dx)
        g_lb = g_lb + dlb
        for k, v in dict(
                norm_mix_pre=dn1[0], norm_mix_post=dn2[0], norm_ff_pre=dn3[0], norm_ff_post=dn4[0],
                hgrn_norm_w=dhn[0], gdn_conv_w=dcw, gdn_a_log=dalog[0, N_HEADS:2 * N_HEADS],
                gdn_dt_bias=ddtb[0, N_HEADS:2 * N_HEADS], gdn_norm_w=dgn[0], gmlp_ln_w=dglw[0],
                gmlp_ln_b=dglb[0], gmlp_w_s=dws, gmlp_b_s=dbsc[:, :N_HEADS].T, conv_dw_w=ddww,
                conv_dw_b=ddwb[0], conv_ln_w=dclw[0], conv_ln_b=dclb[0]).items():
            gs[k][l] = v
    small_grads = {k: jnp.stack(v) for k, v in gs.items()}
    small_grads["lower_bounds"] = g_lb
    return loss_part, dx, small_grads


HBM = pl.BlockSpec(memory_space=pl.ANY)


def _place():
    return lax.axis_index("x"), lax.axis_index("y"), lax.axis_index("c")


def _other_chips(x, y):
    chips = [(1 - x, y), (x, 1 - y), (1 - x, 1 - y)]
    return [(px, py, 2 * px + py) for px, py in chips]


SEM = pl.BlockSpec(memory_space=pltpu.SEMAPHORE)
IN_HBM = pl.BlockSpec(memory_space=pltpu.HBM)
EFFECT = pltpu.SideEffectType.DATAFLOW_SIDE_EFFECTING


def _copies_start(name, plan, bufs, after, n=3):
    nb = len(bufs)

    def body(*refs):
        token = refs[-1]
        for started, _ in plan(refs[:nb], refs[nb + 1], refs[nb + 2]):
            started.start()
        token[...] = jnp.zeros(token.shape, token.dtype)

    out = pl.pallas_call(
        body, name=name,
        out_shape=(pltpu.SemaphoreType.DMA((n,)), pltpu.SemaphoreType.DMA((n,)))
        + tuple(pltpu.HBM(b.shape, b.dtype) for b in bufs) + (jax.ShapeDtypeStruct((8, LANES), F32),),
        in_specs=(IN_HBM,) * nb + (HBM,),
        out_specs=(SEM, SEM) + (IN_HBM,) * nb + (pl.BlockSpec(memory_space=pltpu.VMEM),),
        input_output_aliases={i: 2 + i for i in range(nb)},
        compiler_params=pltpu.CompilerParams(has_side_effects=EFFECT),
    )(*[pltpu.with_memory_space_constraint(b, pltpu.HBM) for b in bufs], after)
    return out[0], out[1], list(out[2:2 + nb]), out[-1]


def _copies_wait(name, plan, send_sems, recv_sems, bufs, after):
    nb = len(bufs)

    def body(*refs):
        for started, arriving in plan(refs[:nb], refs[nb], refs[nb + 1]):
            started.wait_send()
            arriving.wait_recv()

    out = pl.pallas_call(
        body, name=name, out_shape=tuple(pltpu.HBM(b.shape, b.dtype) for b in bufs),
        in_specs=(IN_HBM,) * nb + (SEM, SEM, HBM), out_specs=(IN_HBM,) * nb,
        input_output_aliases={i: i for i in range(nb)},
        compiler_params=pltpu.CompilerParams(has_side_effects=EFFECT),
    )(*bufs, send_sems, recv_sems, after)
    return list(out)


def _ici_plan(bufs, send_sems, recv_sems):
    blk, land = bufs
    x, y, c = _place()
    mine = 2 * x + y
    plan = []
    for j, (px, py, k) in enumerate(_other_chips(x, y)):
        def copy(dst, j=j, to=(px, py, c)):
            return pltpu.make_async_remote_copy(
                src_ref=blk.at[c], dst_ref=dst, send_sem=send_sems.at[j], recv_sem=recv_sems.at[j],
                device_id=to, device_id_type=MESH)
        plan.append((copy(land.at[mine, c]), copy(land.at[k, c])))
    return plan


def _d2d_plan(bufs, send_sems, recv_sems):
    (land,) = bufs
    x, y, c = _place()
    plan = []
    for j, (_, _, k) in enumerate(_other_chips(x, y)):
        def copy(layer, j=j, k=k):
            return pltpu.make_async_remote_copy(
                src_ref=land.at[k, c], dst_ref=land.at[k, layer], send_sem=send_sems.at[j],
                recv_sem=recv_sems.at[j], device_id=(x, y, 1 - c), device_id_type=MESH)
        plan.append((copy(c), copy(1 - c)))
    return plan


def _ici_rows_plan(bufs, send_sems, recv_sems):
    blk, land = bufs
    x, y, c = _place()
    mine = 2 * x + y
    half = blk.shape[0] // 2
    rows = pl.ds(c * half, half)
    plan = []
    for j, (px, py, k) in enumerate(_other_chips(x, y)):
        def copy(dst, j=j, to=(px, py, c)):
            return pltpu.make_async_remote_copy(
                src_ref=blk.at[rows], dst_ref=dst, send_sem=send_sems.at[j], recv_sem=recv_sems.at[j],
                device_id=to, device_id_type=MESH)
        plan.append((copy(land.at[mine, rows]), copy(land.at[k, rows])))
    return plan


def _d2d_rows_plan(bufs, send_sems, recv_sems):
    (land,) = bufs
    x, y, c = _place()
    half = land.shape[1] // 2
    plan = []
    for j, (_, _, k) in enumerate(_other_chips(x, y)):
        def copy(part, j=j, k=k):
            return pltpu.make_async_remote_copy(
                src_ref=land.at[k, pl.ds(c * half, half)], dst_ref=land.at[k, pl.ds(part * half, half)],
                send_sem=send_sems.at[j], recv_sem=recv_sems.at[j], device_id=(x, y, 1 - c), device_id_type=MESH)
        plan.append((copy(c), copy(1 - c)))
    return plan


class _GatheredWeights:
    STAGES = {"w_out": ("proj0", "conformer0"), "w_ff1": ("gdn0", "y0"), "w_ff2": ("u0", None),
              "w_in1": ("ff2_0", None)}
    ORDER = ("w_in0", "w_out", "w_ff1", "w_ff2", "w_in1")

    def __init__(self, chip, shards):
        self.chip = chip
        self.blk, self.ici, self.d2d, self.full, self.mats = {}, {}, {}, {}, {}
        token = jnp.zeros((8, LANES), F32)
        for k in self.ORDER:
            by_rows = k.startswith("w_in")
            blk = shards["w_in"][int(k[-1])] if by_rows else shards[k]
            self.blk[k] = blk.astype(BF16)
            land = lax.empty((N_CHIPS,) + self.blk[k].shape, BF16)
            plans = (_ici_rows_plan, _d2d_rows_plan) if by_rows else (_ici_plan, _d2d_plan)
            send, recv, bufs, token = _copies_start(f"gather_{k}_ici", plans[0], [self.blk[k], land], token)
            self.ici[k] = (send, recv, bufs, plans)
        self._hand_over("w_in0", token)
        self._finish("w_in0", token)

    def _hand_over(self, k, after):
        send, recv, bufs, plans = self.ici.pop(k)
        _, land = _copies_wait(f"gather_{k}_ici_done", plans[0], send, recv, bufs, after)
        send, recv, bufs, _ = _copies_start(f"gather_{k}_d2d", plans[1], [land], after)
        self.d2d[k] = (send, recv, bufs, plans)

    def _finish(self, k, after):
        send, recv, bufs, plans = self.d2d.pop(k)
        (land,) = _copies_wait(f"gather_{k}_d2d_done", plans[1], send, recv, bufs, after)
        self.full[k] = lax.dynamic_update_slice(land, self.blk[k][None], (self.chip,) + (0,) * self.blk[k].ndim)

    def mark(self, tag, value):
        for k, (first, second) in self.STAGES.items():
            if tag == first:
                self._hand_over(k, value)
                if second is None:
                    self._finish(k, value)
            elif tag == second:
                self._finish(k, value)

    def get(self, kind, l):
        if (kind, l) not in self.mats:
            a = self.full[f"w_in{l}" if kind == "w_in" else kind]
            if kind == "w_in":
                m = _assemble_in([a[k] for k in range(N_CHIPS)])
            elif kind == "w_ff1":
                m = jnp.concatenate([a[k, l] for k in range(N_CHIPS)], axis=-1)
            else:
                m = a[:, l].reshape(-1, D_MODEL)
            self.mats[kind, l] = m
        return self.mats[kind, l]


def _exchange_plan(bufs, send_sems, recv_sems):
    g, got = bufs
    x, y, c = _place()
    half = got.shape[1]
    cp = pltpu.make_async_remote_copy(
        src_ref=g.at[pl.ds(0, N_CHIPS), pl.ds((1 - c) * half, half)], dst_ref=got, send_sem=send_sems.at[0],
        recv_sem=recv_sems.at[0], device_id=(x, y, 1 - c), device_id_type=MESH)
    return [(cp, cp)]


def _scatter_plan(bufs, send_sems, recv_sems):
    p, rcv = bufs
    x, y, c = _place()
    plan = []
    for j, (px, py, _) in enumerate(_other_chips(x, y)):
        cp = pltpu.make_async_remote_copy(
            src_ref=p.at[j], dst_ref=rcv.at[j], send_sem=send_sems.at[j], recv_sem=recv_sems.at[j],
            device_id=(px, py, c), device_id_type=MESH)
        plan.append((cp, cp))
    return plan


def _share_plan(bufs, send_sems, recv_sems):
    mine, sib = bufs
    x, y, c = _place()
    cp = pltpu.make_async_remote_copy(
        src_ref=mine, dst_ref=sib, send_sem=send_sems.at[0], recv_sem=recv_sems.at[0],
        device_id=(x, y, 1 - c), device_id_type=MESH)
    return [(cp, cp)]


def _gather_small(name, pack):
    def body(p_ref, out_ref, send_sems, recv_sems, local_sem):
        x, y, c = _place()
        me = 4 * x + 2 * y + c
        local = pltpu.make_async_copy(p_ref, out_ref.at[me], local_sem)
        local.start()
        flips = [(fx, fy, fc) for fx in (0, 1) for fy in (0, 1) for fc in (0, 1)][1:]
        peers = [((1 - x) if fx else x, (1 - y) if fy else y, (1 - c) if fc else c) for fx, fy, fc in flips]

        def cp(j, slot, to):
            return pltpu.make_async_remote_copy(
                src_ref=p_ref, dst_ref=out_ref.at[slot], send_sem=send_sems.at[j], recv_sem=recv_sems.at[j],
                device_id=to, device_id_type=MESH)

        sends = [cp(j, me, to) for j, to in enumerate(peers)]
        for s in sends:
            s.start()
        for j, (px, py, pc) in enumerate(peers):
            cp(j, 4 * px + 2 * py + pc, (px, py, pc)).wait_recv()
        for s in sends:
            s.wait_send()
        local.wait()

    return pl.pallas_call(
        body, name=name, out_shape=jax.ShapeDtypeStruct((N_DEV,) + pack.shape, pack.dtype), in_specs=[HBM],
        out_specs=HBM,
        scratch_shapes=[pltpu.SemaphoreType.DMA((7,)), pltpu.SemaphoreType.DMA((7,)), pltpu.SemaphoreType.DMA],
    )(pack)


SLAB_ROWS = 256


I_CORE, I_CHIP, I_PEER = 0, 1, 2


def _peer_chip(pre, j):
    return jnp.where(j == 0, pre[I_PEER][0], jnp.where(j == 1, pre[I_PEER + 1][0], pre[I_PEER + 2][0]))


def _pair_sum(name, g, got, where):
    _, half, c = got.shape
    t = _pick_tile(half, (SLAB_ROWS, 128, 64, 8))
    per = half // t

    def body(i, v, car, pre):
        return [v[0] + v[1]], [], []

    return _seq_call(
        name, body, steps=3 * per, prefetch=where,
        ins=[(g, (None, t, c), lambda i, pre: (_peer_chip(pre, i // per), pre[I_CORE][0] * per + i % per, 0)),
             (got, (None, t, c), lambda i, pre: (_peer_chip(pre, i // per), i % per, 0))],
        outs=[((3, half, c), BF16, (None, t, c), lambda i, pre: (i // per, i % per, 0))])[0]


def _chip_sum(name, g, got, rcv, where):
    _, half, c = got.shape
    t = _pick_tile(half, (SLAB_ROWS, 128, 64, 8))
    per = half // t

    def body(i, v, car, pre):
        acc = v[0] + v[1]
        for part in v[2:]:
            acc = acc + part.astype(F32)
        return [acc], [], []

    return _seq_call(
        name, body, steps=per, prefetch=where,
        ins=[(g, (None, t, c), lambda i, pre: (pre[I_CHIP][0], pre[I_CORE][0] * per + i, 0)),
             (got, (None, t, c), lambda i, pre: (pre[I_CHIP][0], i, 0))]
        + [(rcv, (None, t, c), (lambda i, pre, _j=j: (_j, i, 0))) for j in range(3)],
        outs=[((half, c), F32, (t, c), lambda i, pre: (i, 0))])[0]


def _adamw_math(w, g, m, v):
    m = ADAM_B1 * m + (1.0 - ADAM_B1) * g
    v = ADAM_B2 * v + (1.0 - ADAM_B2) * (g * g)
    m_hat = m / (1.0 - ADAM_B1 ** ADAM_STEP)
    v_hat = v / (1.0 - ADAM_B2 ** ADAM_STEP)
    delta = -ADAM_LR * (m_hat / (jnp.sqrt(v_hat) + ADAM_EPS) + ADAM_WD * w)
    return delta, m, v


def _adamw(name, w, g, m, v):
    n, r, c = w.shape
    t = _pick_tile(r, (SLAB_ROWS, 128, 64, 8))
    per = r // t

    def body(i, vals, car):
        return list(_adamw_math(*vals)), [], []

    blk = lambda a: (a, (None, t, c), lambda i: (i // per, i % per, 0))
    out = ((n, r, c), F32, (None, t, c), lambda i: (i // per, i % per, 0))
    return _seq_call(name, body, steps=n * per, ins=[blk(w), blk(g), blk(m), blk(v)], outs=[out] * 3)


def _adamw_layer(name, layer, w, mine, sib, m, v, where, into):
    n, r, c = w.shape
    half = r // 2
    t = _pick_tile(half, (SLAB_ROWS, 128, 64, 8))
    per = half // t

    def body(i, vals, car, pre):
        g = jnp.where(i // per == pre[I_CORE][0], vals[1], vals[2])
        return [g] + list(_adamw_math(vals[0], g, vals[3], vals[4])), [], []

    of_layer = lambda a: (a, (None, t, c), lambda i, pre: (layer, i, 0))
    halves = lambda a: (a, (t, c), lambda i, pre: (i % per, 0))
    out = ((n, r, c), F32, (None, t, c), lambda i, pre: (layer, i, 0))
    return _seq_call(name, body, steps=2 * per, prefetch=where, into=into,
                     ins=[of_layer(w), halves(mine), halves(sib), of_layer(m), of_layer(v)], outs=[out] * 4)


def _ordered_sum(name, packs):
    n, r, c = packs.shape

    def body(i, v, car):
        acc = v[0][0]
        for k in range(1, n):
            acc = acc + v[0][k]
        return [acc], [], []

    return _seq_call(name, body, steps=1, ins=[_whole(packs)], outs=[((r, c), F32, (r, c), lambda i: (0, 0))])[0]


def _pack(arrays):
    flat = []
    for a in arrays:
        a = a.reshape(-1).astype(F32)
        pad = (-a.shape[0]) % LANES
        flat.append(jnp.pad(a, (0, pad)) if pad else a)
    v = jnp.concatenate(flat)
    pad = (-v.shape[0]) % (64 * LANES)
    if pad:
        v = jnp.pad(v, (0, pad))
    return v.reshape(-1, LANES)


def _unpack(pack, shapes):
    v = pack.reshape(-1)
    out, off = [], 0
    for s in shapes:
        n = math.prod(s)
        out.append(v[off:off + n].reshape(s))
        off += n + ((-n) % LANES)
    return out


GATE_LO = 8 * GROUP_WIDTH
GATE_HI = GATE_LO + 2 * N_HEADS
IN_CUT = D_IN_PROJ // N_CHIPS


def _padded_col(o):
    if o < GATE_LO:
        return o
    return 12 * GROUP_WIDTH + (o - GATE_LO) if o < GATE_HI else o - 2 * N_HEADS


def _runs(lo, hi, breaks):
    cuts = sorted({lo, hi} | {b for b in breaks if lo < b < hi})
    return list(zip(cuts[:-1], cuts[1:]))


def _assemble_in(blocks):
    shard_edges = [IN_CUT * k for k in range(1, N_CHIPS)]
    pieces = []
    for lo, hi in ((0, GATE_LO), (GATE_HI, D_IN_PROJ), (GATE_LO, GATE_HI)):
        for a, b in _runs(lo, hi, shard_edges):
            k = a // IN_CUT
            pieces.append(blocks[k][:, a - IN_CUT * k:b - IN_CUT * k])
    pieces.append(jnp.zeros((blocks[0].shape[0], LANES - 2 * N_HEADS), blocks[0].dtype))
    return jnp.concatenate(pieces, axis=-1)


def _cut_in(padded):
    cuts = []
    for k in range(N_CHIPS):
        runs = _runs(IN_CUT * k, IN_CUT * (k + 1), (GATE_LO, GATE_HI))
        cuts.append(jnp.concatenate([padded[:, _padded_col(a):_padded_col(a) + (b - a)] for a, b in runs], axis=-1))
    return jnp.stack(cuts)


BIG = ("w_in", "w_out", "w_ff1", "w_ff2")


class _GradReducer:
    EARLY = ("w_ff2", "w_ff1")

    def __init__(self, where, state):
        self.where, self.state = where, state
        self.out = {k: () for k in BIG}
        self.flight = {}

    @staticmethod
    def _join(tokens):
        tokens = [t for t in tokens if t is not None]
        return functools.reduce(lambda a, b: a + b, tokens) if tokens else None

    def mark(self, tag, value, grad=None):
        if tag == "ff2_dw0":
            token = self._add_pairs(1, BIG, value)
            return self._join([token, self._swap(0, "w_ff2", grad, token)])
        if tag == "ff1_dw0":
            return self._swap(0, "w_ff1", grad, value)
        if tag == "out_proj_dx0":
            return self._add_pairs(0, self.EARLY, value)
        if tag == "conformer_bwd0":
            return self._add_chips(1, BIG, value)
        if tag == "in_proj_dx0":
            self._finish(1, BIG, value)
        return None

    def layer_done(self, l, grads, after):
        rest = [k for k in BIG if (l, k) not in self.flight]
        token = self._join([self._swap(l, k, grads[k], after) for k in rest])
        if l == 0:
            self._add_pairs(0, rest, after)
            self._add_chips(0, BIG, after)
            self._finish(0, BIG, after)
        return token

    def _swap(self, l, k, g, after):
        if k == "w_in":
            g = _cut_in(g)
        got = lax.empty((N_CHIPS, g.shape[1] // 2, g.shape[2]), F32)
        send, recv, bufs, token = _copies_start(f"swap_halves_{k}{l}", _exchange_plan, [g, got], after, n=1)
        self.flight[l, k] = (send, recv, bufs)
        return token

    def _add_pairs(self, l, ks, after):
        tokens = []
        for k in ks:
            send, recv, bufs = self.flight[l, k]
            g, got = _copies_wait(f"swap_halves_{k}{l}_done", _exchange_plan, send, recv, bufs, after)
            pair = _pair_sum(f"pair_sum_{k}{l}", g, got, self.where)
            send, recv, bufs, token = _copies_start(
                f"scatter_{k}{l}", _scatter_plan, [pair, lax.empty(pair.shape, pair.dtype)], after)
            self.flight[l, k] = (send, recv, bufs, g, got)
            tokens.append(token)
        return self._join(tokens)

    def _add_chips(self, l, ks, after):
        tokens = []
        for k in ks:
            send, recv, bufs, g, got = self.flight[l, k]
            _, rcv = _copies_wait(f"scatter_{k}{l}_done", _scatter_plan, send, recv, bufs, after)
            mine = _chip_sum(f"chip_sum_{k}{l}", g, got, rcv, self.where)
            send, recv, bufs, token = _copies_start(
                f"share_{k}{l}", _share_plan, [mine, lax.empty(mine.shape, mine.dtype)], after, n=1)
            self.flight[l, k] = (send, recv, bufs)
            tokens.append(token)
        return self._join(tokens)

    def _finish(self, l, ks, after):
        for k in ks:
            send, recv, bufs = self.flight[l, k]
            mine, sib = _copies_wait(f"share_{k}{l}_done", _share_plan, send, recv, bufs, after)
            self.out[k] = _adamw_layer(f"adamw_{k}{l}", l, self.state[k][0], mine, sib, self.state[k][1],
                                       self.state[k][2], self.where, self.out[k])


def kernel(x, lower_bounds, norm_mix_pre, norm_mix_post, norm_ff_pre, norm_ff_post, w_in, w_out, hgrn_norm_w, gdn_conv_w, gdn_a_log, gdn_dt_bias, gdn_norm_w, gmlp_ln_w, gmlp_ln_b, gmlp_w_s, gmlp_b_s, conv_dw_w, conv_dw_b, conv_ln_w, conv_ln_b, w_ff1, w_ff2, loss_target, m_lower_bounds, m_norm_mix_pre, m_norm_mix_post, m_norm_ff_pre, m_norm_ff_post, m_w_in, m_w_out, m_hgrn_norm_w, m_gdn_conv_w, m_gdn_a_log, m_gdn_dt_bias, m_gdn_norm_w, m_gmlp_ln_w, m_gmlp_ln_b, m_gmlp_w_s, m_gmlp_b_s, m_conv_dw_w, m_conv_dw_b, m_conv_ln_w, m_conv_ln_b, m_w_ff1, m_w_ff2, v_lower_bounds, v_norm_mix_pre, v_norm_mix_post, v_norm_ff_pre, v_norm_ff_post, v_w_in, v_w_out, v_hgrn_norm_w, v_gdn_conv_w, v_gdn_a_log, v_gdn_dt_bias, v_gdn_norm_w, v_gmlp_ln_w, v_gmlp_ln_b, v_gmlp_w_s, v_gmlp_b_s, v_conv_dw_w, v_conv_dw_b, v_conv_ln_w, v_conv_ln_b, v_w_ff1, v_w_ff2):
    args = dict(locals())
    chip = 2 * lax.axis_index("x") + lax.axis_index("y")
    where = tuple(jnp.asarray(v, jnp.int32).reshape(1)
                  for v in (lax.axis_index("c"), chip, chip ^ 2, chip ^ 1, chip ^ 3))

    wts = _GatheredWeights(chip, dict(w_in=w_in, w_out=w_out, w_ff1=w_ff1, w_ff2=w_ff2))
    cut_shapes = [gdn_conv_w.shape, conv_dw_w.shape]
    cuts = _gather_small("gather_cut_small", _pack([gdn_conv_w, conv_dw_w]))
    cut_parts = [_unpack(cuts[2 * k], cut_shapes) for k in range(N_CHIPS)]
    sp = {k: args[k] for k in SMALL}
    sp["gdn_conv_w"] = jnp.concatenate([p[0] for p in cut_parts], axis=-1)
    sp["conv_dw_w"] = jnp.concatenate([p[1] for p in cut_parts], axis=-1)

    red = _GradReducer(where, {k: (args[k], args["m_" + k], args["v_" + k]) for k in BIG})
    loss_part, grad_x, small_g = _local_step(x[0], loss_target[0], sp, wts, red)

    names = SMALL + ["loss"]
    small_g["loss"] = loss_part.reshape(1)
    shapes = [small_g[k].shape for k in names]
    total = _ordered_sum("sum_small", _gather_small("gather_small", _pack([small_g[k] for k in names])))
    summed = dict(zip(names, _unpack(total, shapes)))
    loss = summed.pop("loss")[0]
    for k, width in (("gdn_conv_w", gdn_conv_w.shape[-1]), ("conv_dw_w", conv_dw_w.shape[-1])):
        summed[k] = lax.dynamic_slice_in_dim(summed[k], chip * width, width, axis=-1)

    grads, deltas, new_m, new_v = {}, {}, {}, {}
    for k in BIG:
        grads[k], deltas[k], new_m[k], new_v[k] = red.out[k]
    local_shapes = [args[k].shape for k in SMALL]
    packs = [_pack([src[k] for k in SMALL]) for src in (
        {k: args[k] for k in SMALL}, summed, {k: args["m_" + k] for k in SMALL}, {k: args["v_" + k] for k in SMALL})]
    d_s, m_s, v_s = _adamw("adamw_small", *[p[None] for p in packs])
    for k, d, mm, vv in zip(SMALL, _unpack(d_s[0], local_shapes), _unpack(m_s[0], local_shapes),
                            _unpack(v_s[0], local_shapes)):
        grads[k], deltas[k], new_m[k], new_v[k] = summed[k], d, mm, vv

    order = ["lower_bounds", "norm_mix_pre", "norm_mix_post", "norm_ff_pre", "norm_ff_post", "w_in", "w_out",
             "hgrn_norm_w", "gdn_conv_w", "gdn_a_log", "gdn_dt_bias", "gdn_norm_w", "gmlp_ln_w", "gmlp_ln_b",
             "gmlp_w_s", "gmlp_b_s", "conv_dw_w", "conv_dw_b", "conv_ln_w", "conv_ln_b", "w_ff1", "w_ff2"]
    return (loss, grad_x[None], *[grads[k] for k in order], *[deltas[k] for k in order],
            *[new_m[k] for k in order], *[new_v[k] for k in order])
```

```python
import functools
import math

import jax
import jax.numpy as jnp
from jax import lax
from jax.experimental import pallas as pl
from jax.experimental.pallas import tpu as pltpu

F32 = jnp.float32
BF16 = jnp.bfloat16

DEPTH = 2
D_MODEL = 2048
GROUP_WIDTH = 512
HEAD_DIM = 128
N_HEADS = 4
CHUNK = 64
SHORT_CONV = 4
MIX_CHUNK = 128
CONV_WIDTH = 31
D_FF = 4 * D_MODEL
D_IN_PROJ = 12 * GROUP_WIDTH + 2 * N_HEADS
EPS = 1e-6
NEG_BIG = -1e30
TINY = 1e-30
ADAM_LR = 0.001
ADAM_B1 = 0.9
ADAM_B2 = 0.999
ADAM_EPS = 1e-08
ADAM_WD = 0.01
ADAM_STEP = 10

LANES = 128
P_IN = 12 * GROUP_WIDTH + LANES
SUB_BLOCK = 16
HGRN_TILE = 128
GDN_TILE = 128
CONV_TILE = 128
CONV_HALO = 32
VMEM_LIMIT = 56 * 1024 * 1024
N_CHIPS = 4
N_DEV = 8
MESH = pl.DeviceIdType.MESH


_DIMS = {
    "nn": (((1,), (0,)), ((), ())),
    "nt": (((1,), (1,)), ((), ())),
    "tn": (((0,), (0,)), ((), ())),
}


def _split2(a):
    hi = a.astype(BF16)
    return hi, (a - hi.astype(F32)).astype(BF16)


def _raw_mm(a, b, mode, exact):
    dot = lambda p, q: lax.dot_general(p, q, _DIMS[mode], preferred_element_type=F32)
    if not exact:
        return dot(a.astype(BF16), b.astype(BF16))
    a_hi, a_lo = _split2(a)
    b_hi, b_lo = _split2(b)
    return dot(a_hi, b_hi) + (dot(a_hi, b_lo) + dot(a_lo, b_hi))


@functools.partial(jax.custom_vjp, nondiff_argnums=(2, 3))
def _mm(a, b, mode, exact):
    return _raw_mm(a, b, mode, exact)


def _mm_fwd(a, b, mode, exact):
    return _raw_mm(a, b, mode, exact), (a, b)


def _mm_bwd(mode, exact, res, g):
    a, b = res
    if mode == "nn":
        return _raw_mm(g, b, "nt", exact), _raw_mm(a, g, "tn", exact)
    if mode == "nt":
        return _raw_mm(g, b, "nn", exact), _raw_mm(g, a, "tn", exact)
    return _raw_mm(b, g, "nt", exact), _raw_mm(a, g, "nn", exact)


_mm.defvjp(_mm_fwd, _mm_bwd)


def _sig(x):
    return jax.nn.sigmoid(x)


def _silu(x):
    return x * jax.nn.sigmoid(x)


def _gelu(x):
    return 0.5 * x * (1.0 + lax.erf(x * (1.0 / math.sqrt(2.0))))


def _rms(x, w):
    return x * lax.rsqrt(jnp.mean(x * x, axis=-1, keepdims=True) + EPS) * w


def _ln(x, w, b):
    mu = jnp.mean(x, axis=-1, keepdims=True)
    xc = x - mu
    var = jnp.mean(xc * xc, axis=-1, keepdims=True)
    return xc * lax.rsqrt(var + EPS) * w + b


def _iota(shape, dim):
    return lax.broadcasted_iota(jnp.int32, shape, dim)


def _tri_mm(x, mode):
    n = x.shape[0]
    tri = (_iota((n, n), 0) >= _iota((n, n), 1)).astype(BF16)
    x1 = x.astype(BF16)
    r1 = x - x1.astype(F32)
    x2 = r1.astype(BF16)
    x3 = (r1 - x2.astype(F32)).astype(BF16)
    dot = lambda q: lax.dot_general(tri, q, _DIMS[mode], preferred_element_type=F32)
    return dot(x1) + (dot(x2) + dot(x3))


@jax.custom_vjp
def _cumsum_rows(x):
    return _tri_mm(x, "nn")


_cumsum_rows.defvjp(lambda x: (_tri_mm(x, "nn"), None), lambda _, g: (_tri_mm(g, "tn"),))


def _hgrn_head(q, k, v, b, st):
    n = q.shape[0]
    ii = _iota((n, 1), 0)
    zpad = jnp.zeros((SUB_BLOCK, HEAD_DIM), F32)
    k_ext = jnp.concatenate([zpad, k], axis=0)
    b_ext = jnp.concatenate([zpad, b], axis=0)
    v_ext = jnp.concatenate([zpad, v], axis=0)
    o = jnp.zeros((n, HEAD_DIM), F32)
    for d in range(SUB_BLOCK):
        ks = k_ext[SUB_BLOCK - d:SUB_BLOCK - d + n]
        bs = b_ext[SUB_BLOCK - d:SUB_BLOCK - d + n]
        vs = v_ext[SUB_BLOCK - d:SUB_BLOCK - d + n]
        e = jnp.exp(jnp.where((ii % SUB_BLOCK) >= d, b - bs, NEG_BIG))
        o = o + jnp.sum(q * ks * e, axis=-1, keepdims=True) * vs
    blocks = [o[0:SUB_BLOCK]]
    for blk in range(1, n // SUB_BLOCK):
        lo = SUB_BLOCK * blk
        r = b[lo - 1:lo]
        a_q = q[lo:lo + SUB_BLOCK] * jnp.exp(b[lo:lo + SUB_BLOCK] - r)
        b_k = jnp.where(ii < lo, k * jnp.exp(jnp.minimum(r - b, 0.0)), 0.0)
        sc = _mm(a_q, b_k, "nt", False)
        blocks.append(o[lo:lo + SUB_BLOCK] + _mm(sc, v, "nn", False))
    o = jnp.concatenate(blocks, axis=0)
    o = o + _mm(q * jnp.exp(b), st, "nt", False)
    b_end = b[n - 1:n]
    st_new = st * jnp.exp(b_end) + _mm(v, k * jnp.exp(b_end - b), "tn", False)
    return o, st_new


def _hgrn_chunk(layer, lbp, nw, aq, af, ai, ag, states):
    rows = [lbp[i:i + 1, :] for i in range(DEPTH)]
    mx = functools.reduce(jnp.maximum, rows)
    es = [jnp.exp(r - mx) for r in rows]
    den = functools.reduce(lambda p, s: p + s, es)
    soft = [e / den for e in es]
    lb = functools.reduce(lambda p, s: p + s, soft[:layer + 1]) - soft[0]
    f = lb + (1.0 - lb) * _sig(af)
    logf = jnp.log(jnp.maximum(f, TINY))
    k = (1.0 - lb) * _sig(-af)
    q = _silu(aq)
    b = _cumsum_rows(logf)
    outs, new_states = [], []
    for h in range(N_HEADS):
        sl = slice(HEAD_DIM * h, HEAD_DIM * (h + 1))
        o, st = _hgrn_head(q[:, sl], k[:, sl], ai[:, sl], b[:, sl], states[h])
        outs.append(_rms(o, nw) * _silu(ag[:, sl]))
        new_states.append(st)
    return jnp.concatenate(outs, axis=1), new_states


def _hgrn_tile(layer, lbp, nw, aq, af, ai, ag, states):
    outs = []
    for ci in range(aq.shape[0] // CHUNK):
        rs = slice(CHUNK * ci, CHUNK * (ci + 1))
        o, states = _hgrn_chunk(layer, lbp, nw, aq[rs], af[rs], ai[rs], ag[rs], states)
        outs.append(o)
    return jnp.concatenate(outs, axis=0), states


def _short_conv(prev, cur, w):
    n = cur.shape[0]
    ext = jnp.concatenate([prev[n - 8:n], cur], axis=0)
    y = jnp.zeros_like(cur)
    for t in range(SHORT_CONV):
        off = 8 - (SHORT_CONV - 1) + t
        y = y + w[t:t + 1, :] * ext[off:off + n]
    return _silu(y)


def _gdn_tile(cw, alog, dtb, nw, pq, pk, pv, cq, ck, cv, bz, ab, states):
    n = cq.shape[0]
    q_all = _short_conv(pq, cq, cw[:, 0:GROUP_WIDTH])
    k_all = _short_conv(pk, ck, cw[:, GROUP_WIDTH:2 * GROUP_WIDTH])
    v_all = _short_conv(pv, cv, cw[:, 2 * GROUP_WIDTH:3 * GROUP_WIDTH])
    beta_all = _sig(ab)
    g_all = -jnp.exp(alog) * jax.nn.softplus(ab + dtb)
    units = [(ci, h) for ci in range(n // CHUNK) for h in range(N_HEADS)]
    gc_all = [_cumsum_rows(g_all[CHUNK * ci:CHUNK * (ci + 1)]) for ci in range(n // CHUNK)]
    gc_t = [g.T for g in gc_all]
    ii = _iota((CHUNK, CHUNK), 0)
    jj = _iota((CHUNK, CHUNK), 1)
    eye = (ii == jj).astype(F32)

    def cut(a, ci, h):
        return a[CHUNK * ci:CHUNK * (ci + 1), HEAD_DIM * h:HEAD_DIM * (h + 1)]

    q = [cut(q_all, ci, h) for ci, h in units]
    k = [cut(k_all, ci, h) for ci, h in units]
    v = [cut(v_all, ci, h) for ci, h in units]
    q = [t * lax.rsqrt(jnp.sum(t * t, axis=-1, keepdims=True) + EPS) * (HEAD_DIM ** -0.5) for t in q]
    k = [t * lax.rsqrt(jnp.sum(t * t, axis=-1, keepdims=True) + EPS) for t in k]
    beta = [beta_all[CHUNK * ci:CHUNK * (ci + 1), h:h + 1] for ci, h in units]
    gc = [gc_all[ci][:, N_HEADS + h:N_HEADS + h + 1] for ci, h in units]
    gcr = [gc_t[ci][N_HEADS + h:N_HEADS + h + 1, :] for ci, h in units]
    gamma = [jnp.exp(jnp.where(ii >= jj, a - b, NEG_BIG)) for a, b in zip(gc, gcr)]
    kb = [a * b for a, b in zip(k, beta)]
    m = [jnp.where(ii > jj, _mm(a, b, "nt", False) * g, 0.0) for a, b, g in zip(kb, k, gamma)]
    inv = [eye - t for t in m]
    p = m
    for _ in range(max(1, int(math.ceil(math.log2(CHUNK))) - 1)):
        p = [_mm(t, t, "nn", True) for t in p]
        inv = [a + _mm(a, t, "nn", True) for a, t in zip(inv, p)]
    eg = [jnp.exp(t) for t in gc]
    u = [_mm(a, b * c, "nn", True) for a, b, c in zip(inv, v, beta)]
    w = [_mm(a, b * c, "nn", True) for a, b, c in zip(inv, kb, eg)]
    qk = [_mm(a, b, "nt", False) * g for a, b, g in zip(q, k, gamma)]
    qd = [a * b for a, b in zip(q, eg)]
    g_end = [t[CHUNK - 1:CHUNK] for t in gc]
    kd = [a * jnp.exp(e - g) for a, e, g in zip(k, g_end, gc)]
    states = list(states)
    outs = {}
    for i, (ci, h) in enumerate(units):
        st = states[h]
        v_new = u[i] - _mm(w[i], st, "nt", False)
        outs[ci, h] = _mm(qd[i], st, "nt", False) + _mm(qk[i], v_new, "nn", False)
        states[h] = st * jnp.exp(g_end[i]) + _mm(v_new, kd[i], "tn", False)
    rows = []
    for ci in range(n // CHUNK):
        rows.append(jnp.concatenate(
            [_rms(outs[ci, h], nw) * _silu(cut(bz, ci, h)) for h in range(N_HEADS)], axis=1))
    return jnp.concatenate(rows, axis=0), states


def _gmlp_tile(ln_w, ln_b, ws, bs_cols, cu, cv):
    u = _gelu(cu)
    v = _ln(_gelu(cv), ln_w, ln_b)
    n = cu.shape[0]
    tril = _iota((n, n), 0) >= _iota((n, n), 1)
    outs = []
    for h in range(N_HEADS):
        sl = slice(HEAD_DIM * h, HEAD_DIM * (h + 1))
        wc = jnp.where(tril, ws[h], 0.0)
        outs.append(_mm(wc, v[:, sl], "nn", False) + bs_cols[:, h:h + 1])
    return u * jnp.concatenate(outs, axis=1)


def _conformer_tile(dw_w, dw_b, ln_w, ln_b, pa, pg, ca, cg):
    n = ca.shape[0]
    yp = pa[n - CONV_HALO:n] * _sig(pg[n - CONV_HALO:n])
    ext = jnp.concatenate([yp, ca * _sig(cg)], axis=0)
    acc = jnp.zeros_like(ca)
    for t in range(CONV_WIDTH):
        off = CONV_HALO - (CONV_WIDTH - 1) + t
        acc = acc + dw_w[t:t + 1, :] * ext[off:off + n]
    return _silu(_ln(acc + dw_b, ln_w, ln_b))


def _seq_call(name, body, *, steps, ins, outs, accs=(), carries=(), reverse=False, prefetch=None, into=()):
    n_in, n_out, n_acc, n_car = len(ins), len(outs), len(accs), len(carries)
    n_into = len(into)
    n_pre = 0 if prefetch is None else len(prefetch)

    def logical(g):
        return (steps - 1 - g) if reverse else g

    def kern(*refs):
        pre = refs[:n_pre]
        refs = refs[n_pre:]
        in_refs = refs[:n_in]
        refs = refs[n_in + n_into:]
        out_refs = refs[:n_out]
        acc_refs = refs[n_out:n_out + n_acc]
        car_refs = refs[n_out + n_acc:]
        g = pl.program_id(0)

        @pl.when(g == 0)
        def _():
            for r in list(acc_refs) + list(car_refs):
                r[...] = jnp.zeros(r.shape, r.dtype)

        o, a, c = body(logical(g), [r[...] for r in in_refs], [r[...] for r in car_refs], *((pre,) if n_pre else ()))
        for r, v in zip(out_refs, o, strict=True):
            r[...] = v.astype(r.dtype)
        for r, v in zip(acc_refs, a, strict=True):
            r[...] += v
        for r, v in zip(car_refs, c, strict=True):
            r[...] = v

    def spec(block, fn):
        return pl.BlockSpec(block, lambda g, *pre: fn(logical(g), *((pre,) if n_pre else ())))

    in_specs = [spec(bs, fn) for (_, bs, fn) in ins] + [HBM] * n_into
    out_specs = [spec(bs, fn) for (_, _, bs, fn) in outs]
    out_specs += [pl.BlockSpec(shape, lambda g, *pre, _n=len(shape): (0,) * _n) for (shape, _) in accs]
    out_shape = [jax.ShapeDtypeStruct(s, d) for (s, d, _, _) in outs]
    out_shape += [jax.ShapeDtypeStruct(s, d) for (s, d) in accs]
    grid_spec = pltpu.PrefetchScalarGridSpec(
        num_scalar_prefetch=n_pre, grid=(steps,), in_specs=in_specs, out_specs=out_specs,
        scratch_shapes=[pltpu.VMEM(s, d) for (s, d) in carries])
    args = ([] if prefetch is None else list(prefetch)) + [a for (a, _, _) in ins] + list(into)
    return pl.pallas_call(
        kern, name=name, grid_spec=grid_spec, out_shape=out_shape,
        input_output_aliases={n_pre + n_in + i: i for i in range(n_into)},
        compiler_params=pltpu.CompilerParams(dimension_semantics=("arbitrary",), vmem_limit_bytes=VMEM_LIMIT),
    )(*args)


def _whole(a):
    nd = a.ndim
    return (a, a.shape, lambda i, *pre: (0,) * nd)


def _rows(a, tile, col=0, width=None, shift=0):
    width = a.shape[1] if width is None else width
    if shift:
        return (a, (tile, width), lambda i, *pre: (jnp.maximum(i + shift, 0), col))
    return (a, (tile, width), lambda i, *pre: (i, col))


def _row_out(n_rows, width, dtype, tile):
    return ((n_rows, width), dtype, (tile, width), lambda i, *pre: (i, 0))


def _pick_tile(n, prefs):
    for t in prefs:
        if n % t == 0:
            return t
    return n


def _matmul(name, a, b, mode, out_dtypes, epilogue=None, extras=(), place=None, max_tm=1024):
    if mode == "nn":
        (m, k), n = a.shape, b.shape[1]
    elif mode == "nt":
        (m, k), n = a.shape, b.shape[0]
    else:
        (k, m), n = a.shape, b.shape[1]
    tm = _pick_tile(m, tuple(t for t in (1024, 512, 256, 128) if t <= max_tm))
    tn = _pick_tile(n, (1024, 896, 512, 256, 128))
    tk = _pick_tile(k, (2048, 896, 512, 256, 128))
    nk = k // tk
    n_ex = len(extras)
    n_out = len(out_dtypes)

    def kern(*refs):
        a_ref, b_ref = refs[0], refs[1]
        ex_refs = refs[2:2 + n_ex]
        out_refs = refs[2 + n_ex:2 + n_ex + n_out]
        acc_ref = refs[2 + n_ex + n_out]
        kk = pl.program_id(2)

        @pl.when(kk == 0)
        def _():
            acc_ref[...] = jnp.zeros(acc_ref.shape, F32)

        acc_ref[...] += lax.dot_general(a_ref[...], b_ref[...], _DIMS[mode], preferred_element_type=F32)

        @pl.when(kk == nk - 1)
        def _():
            acc = acc_ref[...]
            vals = (acc,) if epilogue is None else epilogue(acc, *[r[...] for r in ex_refs])
            for r, v in zip(out_refs, vals, strict=True):
                r[...] = v.astype(r.dtype)

    if mode == "tn":
        a_spec = pl.BlockSpec((tk, tm), lambda i, j, kk: (kk, i))
    else:
        a_spec = pl.BlockSpec((tm, tk), lambda i, j, kk: (i, kk))
    if mode == "nt":
        b_spec = pl.BlockSpec((tn, tk), lambda i, j, kk: (j, kk))
    else:
        b_spec = pl.BlockSpec((tk, tn), lambda i, j, kk: (kk, j))
    tile = pl.BlockSpec((tm, tn), lambda i, j, kk: (i, j))
    out_specs = [tile] * n_out
    out_shape = [jax.ShapeDtypeStruct((m, n), d) for d in out_dtypes]
    if place is not None:
        shape, block_fn, index_fn = place
        out_specs = [pl.BlockSpec(block_fn(tm, tn), lambda i, j, kk: index_fn(i, j, tm, tn))]
        out_shape = [jax.ShapeDtypeStruct(shape, out_dtypes[0])]
    return pl.pallas_call(
        kern, name=name, grid=(m // tm, n // tn, nk),
        in_specs=[a_spec, b_spec] + [tile] * n_ex,
        out_specs=out_specs, out_shape=out_shape,
        scratch_shapes=[pltpu.VMEM((tm, tn), F32)],
        compiler_params=pltpu.CompilerParams(
            dimension_semantics=("parallel", "parallel", "arbitrary"), vmem_limit_bytes=VMEM_LIMIT),
    )(a, b, *extras)


ROW_TILE = 256


def _rms_fwd(name, x, w):
    s, d = x.shape
    t = _pick_tile(s, (ROW_TILE,))

    def body(i, v, c):
        return [_rms(v[0], v[1])], [], []

    return _seq_call(name, body, steps=s // t, ins=[_rows(x, t), _whole(w)], outs=[_row_out(s, d, BF16, t)])[0]


def _resid_rms_fwd(name, x, y, w):
    s, d = x.shape
    t = _pick_tile(s, (ROW_TILE,))

    def body(i, v, c):
        return [v[0] + _rms(v[1], v[2])], [], []

    return _seq_call(name, body, steps=s // t, ins=[_rows(x, t), _rows(y, t), _whole(w)],
                     outs=[_row_out(s, d, F32, t)])[0]


def _rms_bwd(name, x, w, dh, dres):
    s, d = x.shape
    t = _pick_tile(s, (ROW_TILE,))

    def body(i, v, c):
        _, vjp = jax.vjp(_rms, v[0], v[1])
        dx, dw = vjp(v[2])
        return [dx + v[3]], [dw], []

    return _seq_call(name, body, steps=s // t, ins=[_rows(x, t), _whole(w), _rows(dh, t), _rows(dres, t)],
                     outs=[_row_out(s, d, F32, t)], accs=[((1, d), F32)])


def _resid_rms_bwd(name, y, w, dxo):
    s, d = y.shape
    t = _pick_tile(s, (ROW_TILE,))

    def body(i, v, c):
        _, vjp = jax.vjp(_rms, v[0], v[1])
        dy, dw = vjp(v[2])
        return [dy], [dw], []

    return _seq_call(name, body, steps=s // t, ins=[_rows(y, t), _whole(w), _rows(dxo, t)],
                     outs=[_row_out(s, d, BF16, t)], accs=[((1, d), F32)])


def _loss_head(name, y, target):
    s, d = y.shape
    t = _pick_tile(s, (ROW_TILE,))

    def body(i, v, c):
        err = v[0] - v[1]
        part = 0.5 * jnp.sum(jnp.mean(err * err, axis=-1, keepdims=True))
        return [err * (1.0 / d)], [jnp.full((1, LANES), part, F32)], []

    return _seq_call(name, body, steps=s // t, ins=[_rows(y, t), _rows(target, t)],
                     outs=[_row_out(s, d, F32, t)], accs=[((1, LANES), F32)])


SEG = {n: i for i, n in enumerate(
    ["a_q", "a_f", "a_i", "a_g", "b_q", "b_k", "b_v", "b_z", "c_u", "c_v", "d_a", "d_gate"])}
AB_COL = 12 * GROUP_WIDTH // LANES


def _seg(proj, name, tile, shift=0):
    return _rows(proj, tile, col=SEG[name], width=GROUP_WIDTH, shift=shift)


def _state_block():
    return (1, N_HEADS * HEAD_DIM, HEAD_DIM), lambda i, *pre: (i, 0, 0)


def _split_states(blk):
    return [blk[0, HEAD_DIM * h:HEAD_DIM * (h + 1), :] for h in range(N_HEADS)]


STATE_CARRIES = [((HEAD_DIM, HEAD_DIM), F32)] * N_HEADS


def _hgrn_fwd(layer, proj, lbp, nw):
    s = proj.shape[0]
    n = s // HGRN_TILE
    sb, sf = _state_block()

    def body(i, v, st):
        o, new = _hgrn_tile(layer, v[0], v[1], v[2], v[3], v[4], v[5], st)
        return [o, jnp.concatenate(st, axis=0)[None]], [], new

    return _seq_call(
        f"hgrn_fwd{layer}", body, steps=n,
        ins=[_whole(lbp), _whole(nw)] + [_seg(proj, k, HGRN_TILE) for k in ("a_q", "a_f", "a_i", "a_g")],
        outs=[_row_out(s, GROUP_WIDTH, BF16, HGRN_TILE), ((n, N_HEADS * HEAD_DIM, HEAD_DIM), F32, sb, sf)],
        carries=STATE_CARRIES)


def _hgrn_bwd(layer, proj, lbp, nw, states, dmix):
    s = proj.shape[0]
    n = s // HGRN_TILE
    sb, sf = _state_block()

    def body(i, v, dst):
        st = _split_states(v[6])

        def f(lbp_, nw_, aq, af, ai, ag, *st_):
            return _hgrn_tile(layer, lbp_, nw_, aq, af, ai, ag, list(st_))

        _, vjp = jax.vjp(f, v[0], v[1], v[2], v[3], v[4], v[5], *st)
        g = vjp((v[7], list(dst)))
        return [jnp.concatenate(g[2:6], axis=1)], [g[0], g[1]], list(g[6:])

    return _seq_call(
        f"hgrn_bwd{layer}", body, steps=n, reverse=True,
        ins=[_whole(lbp), _whole(nw)] + [_seg(proj, k, HGRN_TILE) for k in ("a_q", "a_f", "a_i", "a_g")]
        + [(states, sb, sf), _rows(dmix, HGRN_TILE, col=0, width=GROUP_WIDTH)],
        outs=[_row_out(s, 4 * GROUP_WIDTH, BF16, HGRN_TILE)],
        accs=[(lbp.shape, F32), (nw.shape, F32)], carries=STATE_CARRIES)


def _gdn_ins(proj, cw, alog, dtb, nw):
    return ([_whole(cw), _whole(alog), _whole(dtb), _whole(nw)]
            + [_seg(proj, k, GDN_TILE, shift=-1) for k in ("b_q", "b_k", "b_v")]
            + [_seg(proj, k, GDN_TILE) for k in ("b_q", "b_k", "b_v", "b_z")]
            + [_rows(proj, GDN_TILE, col=AB_COL, width=LANES)])


def _mask_prev(i, vals):
    keep = (i > 0).astype(F32)
    return [p * keep for p in vals]


def _gdn_fwd(layer, proj, cw, alog, dtb, nw):
    s = proj.shape[0]
    n = s // GDN_TILE
    sb, sf = _state_block()

    def body(i, v, st):
        prev = _mask_prev(i, v[4:7])
        o, new = _gdn_tile(v[0], v[1], v[2], v[3], *prev, *v[7:12], st)
        return [o, jnp.concatenate(st, axis=0)[None]], [], new

    return _seq_call(
        f"gdn_fwd{layer}", body, steps=n, ins=_gdn_ins(proj, cw, alog, dtb, nw),
        outs=[_row_out(s, GROUP_WIDTH, BF16, GDN_TILE), ((n, N_HEADS * HEAD_DIM, HEAD_DIM), F32, sb, sf)],
        carries=STATE_CARRIES)


def _gdn_bwd(layer, proj, cw, alog, dtb, nw, states, dmix):
    s = proj.shape[0]
    n = s // GDN_TILE
    sb, sf = _state_block()

    def body(i, v, car):
        dst, dprev = car[:N_HEADS], car[N_HEADS:]
        prev = _mask_prev(i, v[4:7])
        st = _split_states(v[12])

        def f(cw_, alog_, dtb_, nw_, pq, pk, pv, cq, ck, cv, bz, ab, *st_):
            return _gdn_tile(cw_, alog_, dtb_, nw_, pq, pk, pv, cq, ck, cv, bz, ab, list(st_))

        _, vjp = jax.vjp(f, v[0], v[1], v[2], v[3], *prev, *v[7:12], *st)
        g = vjp((v[13], list(dst)))
        dcur = [g[7] + dprev[0], g[8] + dprev[1], g[9] + dprev[2], g[10]]
        return ([jnp.concatenate(dcur, axis=1), g[11]], list(g[0:4]), list(g[12:]) + list(g[4:7]))

    return _seq_call(
        f"gdn_bwd{layer}", body, steps=n, reverse=True,
        ins=_gdn_ins(proj, cw, alog, dtb, nw) + [(states, sb, sf), _rows(dmix, GDN_TILE, col=1, width=GROUP_WIDTH)],
        outs=[_row_out(s, 4 * GROUP_WIDTH, BF16, GDN_TILE), _row_out(s, LANES, BF16, GDN_TILE)],
        accs=[(cw.shape, F32), (alog.shape, F32), (dtb.shape, F32), (nw.shape, F32)],
        carries=STATE_CARRIES + [((GDN_TILE, GROUP_WIDTH), F32)] * 3)


def _gmlp_fwd(layer, proj, ln_w, ln_b, ws, bs_cols):
    s = proj.shape[0]

    def body(i, v, c):
        return [_gmlp_tile(*v)], [], []

    return _seq_call(
        f"gmlp_fwd{layer}", body, steps=s // MIX_CHUNK,
        ins=[_whole(ln_w), _whole(ln_b), _whole(ws), _whole(bs_cols),
             _seg(proj, "c_u", MIX_CHUNK), _seg(proj, "c_v", MIX_CHUNK)],
        outs=[_row_out(s, GROUP_WIDTH, BF16, MIX_CHUNK)])[0]


def _gmlp_bwd(layer, proj, ln_w, ln_b, ws, bs_cols, dmix):
    s = proj.shape[0]

    def body(i, v, c):
        _, vjp = jax.vjp(_gmlp_tile, *v[:6])
        g = vjp(v[6])
        return [jnp.concatenate(g[4:6], axis=1)], list(g[0:4]), []

    return _seq_call(
        f"gmlp_bwd{layer}", body, steps=s // MIX_CHUNK,
        ins=[_whole(ln_w), _whole(ln_b), _whole(ws), _whole(bs_cols),
             _seg(proj, "c_u", MIX_CHUNK), _seg(proj, "c_v", MIX_CHUNK),
             _rows(dmix, MIX_CHUNK, col=2, width=GROUP_WIDTH)],
        outs=[_row_out(s, 2 * GROUP_WIDTH, BF16, MIX_CHUNK)],
        accs=[(ln_w.shape, F32), (ln_b.shape, F32), (ws.shape, F32), (bs_cols.shape, F32)])


def _conformer_ins(proj, dw_w, dw_b, ln_w, ln_b):
    return ([_whole(dw_w), _whole(dw_b), _whole(ln_w), _whole(ln_b)]
            + [_seg(proj, k, CONV_TILE, shift=-1) for k in ("d_a", "d_gate")]
            + [_seg(proj, k, CONV_TILE) for k in ("d_a", "d_gate")])


def _conformer_fwd(layer, proj, dw_w, dw_b, ln_w, ln_b):
    s = proj.shape[0]

    def body(i, v, c):
        prev = _mask_prev(i, v[4:6])
        return [_conformer_tile(v[0], v[1], v[2], v[3], *prev, v[6], v[7])], [], []

    return _seq_call(
        f"conformer_fwd{layer}", body, steps=s // CONV_TILE, ins=_conformer_ins(proj, dw_w, dw_b, ln_w, ln_b),
        outs=[_row_out(s, GROUP_WIDTH, BF16, CONV_TILE)])[0]


def _conformer_bwd(layer, proj, dw_w, dw_b, ln_w, ln_b, dmix):
    s = proj.shape[0]

    def body(i, v, dprev):
        prev = _mask_prev(i, v[4:6])
        _, vjp = jax.vjp(_conformer_tile, v[0], v[1], v[2], v[3], *prev, v[6], v[7])
        g = vjp(v[8])
        return [jnp.concatenate([g[6] + dprev[0], g[7] + dprev[1]], axis=1)], list(g[0:4]), [g[4], g[5]]

    return _seq_call(
        f"conformer_bwd{layer}", body, steps=s // CONV_TILE, reverse=True,
        ins=_conformer_ins(proj, dw_w, dw_b, ln_w, ln_b) + [_rows(dmix, CONV_TILE, col=3, width=GROUP_WIDTH)],
        outs=[_row_out(s, 2 * GROUP_WIDTH, BF16, CONV_TILE)],
        accs=[(dw_w.shape, F32), (dw_b.shape, F32), (ln_w.shape, F32), (ln_b.shape, F32)],
        carries=[((CONV_TILE, GROUP_WIDTH), F32)] * 2)


SMALL = ["lower_bounds", "norm_mix_pre", "norm_mix_post", "norm_ff_pre", "norm_ff_post", "hgrn_norm_w",
         "gdn_conv_w", "gdn_a_log", "gdn_dt_bias", "gdn_norm_w", "gmlp_ln_w", "gmlp_ln_b", "gmlp_w_s",
         "gmlp_b_s", "conv_dw_w", "conv_dw_b", "conv_ln_w", "conv_ln_b"]


def _gate_row(v):
    return jnp.pad(v.reshape(1, N_HEADS), ((0, 0), (N_HEADS, LANES - 2 * N_HEADS)))


def _cut_rows(rows, cols):
    return ((N_CHIPS, rows, cols), lambda tm, tn: (None, tm, tn),
            lambda i, j, tm, tn: (i // (rows // tm), i % (rows // tm), j))


def _cut_cols(rows, cols):
    return ((N_CHIPS, rows, cols), lambda tm, tn: (None, tm, tn),
            lambda i, j, tm, tn: (j // (cols // tn), i, j % (cols // tn)))


def _relu2(acc):
    r = jnp.maximum(acc, 0.0)
    return acc, r * r


def _relu2_bwd(acc, u):
    return (2.0 * jnp.maximum(u, 0.0) * acc,)


def _local_step(x, target, sp, wts, red):
    row = lambda v: v.reshape(1, -1)
    saved = []
    w_in, w_out, w_ff1, w_ff2 = [], [], [], []
    for l in range(DEPTH):
        par = dict(
            lbp=sp["lower_bounds"], hn=row(sp["hgrn_norm_w"][l]), cw=sp["gdn_conv_w"][l],
            alog=_gate_row(sp["gdn_a_log"][l]), dtb=_gate_row(sp["gdn_dt_bias"][l]), gn=row(sp["gdn_norm_w"][l]),
            glw=row(sp["gmlp_ln_w"][l]), glb=row(sp["gmlp_ln_b"][l]), ws=sp["gmlp_w_s"][l],
            bsc=jnp.pad(sp["gmlp_b_s"][l].T, ((0, 0), (0, LANES - N_HEADS))),
            dww=sp["conv_dw_w"][l], dwb=row(sp["conv_dw_b"][l]), clw=row(sp["conv_ln_w"][l]),
            clb=row(sp["conv_ln_b"][l]), n1=row(sp["norm_mix_pre"][l]), n2=row(sp["norm_mix_post"][l]),
            n3=row(sp["norm_ff_pre"][l]), n4=row(sp["norm_ff_post"][l]))
        h = _rms_fwd(f"norm_mix_pre{l}", x, par["n1"])
        w_in.append(wts.get("w_in", l))
        proj = _matmul(f"in_proj{l}", h, w_in[l], "nn", [F32])[0]
        wts.mark(f"proj{l}", proj)
        o_a, st_a = _hgrn_fwd(l, proj, par["lbp"], par["hn"])
        o_b, st_b = _gdn_fwd(l, proj, par["cw"], par["alog"], par["dtb"], par["gn"])
        wts.mark(f"gdn{l}", o_b)
        o_c = _gmlp_fwd(l, proj, par["glw"], par["glb"], par["ws"], par["bsc"])
        o_d = _conformer_fwd(l, proj, par["dww"], par["dwb"], par["clw"], par["clb"])
        wts.mark(f"conformer{l}", o_d)
        mix = jnp.concatenate([o_a, o_b, o_c, o_d], axis=1)
        w_out.append(wts.get("w_out", l))
        y = _matmul(f"out_proj{l}", mix, w_out[l], "nn", [F32])[0]
        wts.mark(f"y{l}", y)
        x1 = _resid_rms_fwd(f"norm_mix_post{l}", x, y, par["n2"])
        h2 = _rms_fwd(f"norm_ff_pre{l}", x1, par["n3"])
        w_ff1.append(wts.get("w_ff1", l))
        u, act = _matmul(f"ff1_{l}", h2, w_ff1[l], "nn", [F32, BF16], epilogue=_relu2)
        wts.mark(f"u{l}", u)
        w_ff2.append(wts.get("w_ff2", l))
        y2 = _matmul(f"ff2_{l}", act, w_ff2[l], "nn", [F32])[0]
        wts.mark(f"ff2_{l}", y2)
        x2 = _resid_rms_fwd(f"norm_ff_post{l}", x1, y2, par["n4"])
        saved.append(dict(par=par, x=x, h=h, proj=proj, st_a=st_a, st_b=st_b, mix=mix, y=y, x1=x1, h2=h2,
                          u=u, act=act, y2=y2))
        x = x2

    dx, loss_acc = _loss_head("loss_head", x, target)
    loss_part = loss_acc[0, 0]

    gs = {k: [None] * DEPTH for k in SMALL if k != "lower_bounds"}
    g_lb = jnp.zeros((DEPTH, GROUP_WIDTH), F32)
    def behind(row, token):
        return row if token is None else row + token[:1, :1]

    def join(a, b):
        return b if a is None else a if b is None else a + b

    tok = None
    for l in reversed(range(DEPTH)):
        sv = saved[l]
        par = sv["par"]
        dy2, dn4 = _resid_rms_bwd(f"norm_ff_post_bwd{l}", sv["y2"], behind(par["n4"], tok), dx)
        du = _matmul(f"ff2_dx{l}", dy2, w_ff2[l], "nt", [BF16], epilogue=_relu2_bwd, extras=(sv["u"],))[0]
        g_ff2 = _matmul(f"ff2_dw{l}", sv["act"], dy2, "tn", [F32], place=_cut_rows(D_FF // N_CHIPS, D_MODEL))[0]
        tok = red.mark(f"ff2_dw{l}", du, g_ff2)
        g_ff1 = _matmul(f"ff1_dw{l}", sv["h2"], du, "tn", [F32], place=_cut_cols(D_MODEL, D_FF // N_CHIPS))[0]
        tok = join(tok, red.mark(f"ff1_dw{l}", du, g_ff1))
        dh2 = _matmul(f"ff1_dx{l}", du, w_ff1[l], "nt", [F32])[0]
        dx1, dn3 = _rms_bwd(f"norm_ff_pre_bwd{l}", sv["x1"], behind(par["n3"], tok), dh2, dx)
        dy, dn2 = _resid_rms_bwd(f"norm_mix_post_bwd{l}", sv["y"], par["n2"], dx1)
        dmix = _matmul(f"out_proj_dx{l}", dy, w_out[l], "nt", [F32])[0]
        tok = red.mark(f"out_proj_dx{l}", dmix)
        g_out = _matmul(f"out_proj_dw{l}", sv["mix"], dy, "tn", [F32], max_tm=GROUP_WIDTH,
                        place=_cut_rows(GROUP_WIDTH, D_MODEL))[0]
        tok = join(tok, red.mark(f"out_proj_dw{l}", dmix, g_out))
        proj = sv["proj"]
        dp_a, dlb, dhn = _hgrn_bwd(l, proj, par["lbp"], behind(par["hn"], tok), sv["st_a"], dmix)
        dp_b, dp_ab, dcw, dalog, ddtb, dgn = _gdn_bwd(
            l, proj, par["cw"], par["alog"], par["dtb"], par["gn"], sv["st_b"], dmix)
        dp_c, dglw, dglb, dws, dbsc = _gmlp_bwd(l, proj, par["glw"], par["glb"], par["ws"], par["bsc"], dmix)
        dp_d, ddww, ddwb, dclw, dclb = _conformer_bwd(
            l, proj, par["dww"], par["dwb"], par["clw"], par["clb"], dmix)
        tok = red.mark(f"conformer_bwd{l}", dp_d)
        dproj = jnp.concatenate([dp_a, dp_b, dp_c, dp_d, dp_ab], axis=1)
        g_in = _matmul(f"in_proj_dw{l}", sv["h"], dproj, "tn", [F32])[0]
        dh = _matmul(f"in_proj_dx{l}", dproj, w_in[l], "nt", [F32])[0]
        red.mark(f"in_proj_dx{l}", dh)
        dx, dn1 = _rms_bwd(f"norm_mix_pre_bwd{l}", sv["x"], behind(par["n1"], tok), dh, dx1)
        tok = red.layer_done(l, dict(w_in=g_in, w_out=g_out, w_ff1=g_ff1, w_ff2=g_ff2), dx)
        g_lb = g_lb + dlb
        for k, v in dict(
                norm_mix_pre=dn1[0], norm_mix_post=dn2[0], norm_ff_pre=dn3[0], norm_ff_post=dn4[0],
                hgrn_norm_w=dhn[0], gdn_conv_w=dcw, gdn_a_log=dalog[0, N_HEADS:2 * N_HEADS],
                gdn_dt_bias=ddtb[0, N_HEADS:2 * N_HEADS], gdn_norm_w=dgn[0], gmlp_ln_w=dglw[0],
                gmlp_ln_b=dglb[0], gmlp_w_s=dws, gmlp_b_s=dbsc[:, :N_HEADS].T, conv_dw_w=ddww,
                conv_dw_b=ddwb[0], conv_ln_w=dclw[0], conv_ln_b=dclb[0]).items():
            gs[k][l] = v
    small_grads = {k: jnp.stack(v) for k, v in gs.items()}
    small_grads["lower_bounds"] = g_lb
    return loss_part, dx, small_grads


HBM = pl.BlockSpec(memory_space=pl.ANY)


def _place():
    return lax.axis_index("x"), lax.axis_index("y"), lax.axis_index("c")


def _other_chips(x, y):
    chips = [(1 - x, y), (x, 1 - y), (1 - x, 1 - y)]
    return [(px, py, 2 * px + py) for px, py in chips]


SEM = pl.BlockSpec(memory_space=pltpu.SEMAPHORE)
IN_HBM = pl.BlockSpec(memory_space=pltpu.HBM)
EFFECT = pltpu.SideEffectType.DATAFLOW_SIDE_EFFECTING


def _copies_start(name, plan, bufs, after, n=3):
    nb = len(bufs)

    def body(*refs):
        token = refs[-1]
        for started, _ in plan(refs[:nb], refs[nb + 1], refs[nb + 2]):
            started.start()
        token[...] = jnp.zeros(token.shape, token.dtype)

    out = pl.pallas_call(
        body, name=name,
        out_shape=(pltpu.SemaphoreType.DMA((n,)), pltpu.SemaphoreType.DMA((n,)))
        + tuple(pltpu.HBM(b.shape, b.dtype) for b in bufs) + (jax.ShapeDtypeStruct((8, LANES), F32),),
        in_specs=(IN_HBM,) * nb + (HBM,),
        out_specs=(SEM, SEM) + (IN_HBM,) * nb + (pl.BlockSpec(memory_space=pltpu.VMEM),),
        input_output_aliases={i: 2 + i for i in range(nb)},
        compiler_params=pltpu.CompilerParams(has_side_effects=EFFECT),
    )(*[pltpu.with_memory_space_constraint(b, pltpu.HBM) for b in bufs], after)
    return out[0], out[1], list(out[2:2 + nb]), out[-1]


def _copies_wait(name, plan, send_sems, recv_sems, bufs, after):
    nb = len(bufs)

    def body(*refs):
        for started, arriving in plan(refs[:nb], refs[nb], refs[nb + 1]):
            started.wait_send()
            arriving.wait_recv()

    out = pl.pallas_call(
        body, name=name, out_shape=tuple(pltpu.HBM(b.shape, b.dtype) for b in bufs),
        in_specs=(IN_HBM,) * nb + (SEM, SEM, HBM), out_specs=(IN_HBM,) * nb,
        input_output_aliases={i: i for i in range(nb)},
        compiler_params=pltpu.CompilerParams(has_side_effects=EFFECT),
    )(*bufs, send_sems, recv_sems, after)
    return list(out)


def _ici_plan(bufs, send_sems, recv_sems):
    blk, land = bufs
    x, y, c = _place()
    mine = 2 * x + y
    plan = []
    for j, (px, py, k) in enumerate(_other_chips(x, y)):
        def copy(dst, j=j, to=(px, py, c)):
            return pltpu.make_async_remote_copy(
                src_ref=blk.at[c], dst_ref=dst, send_sem=send_sems.at[j], recv_sem=recv_sems.at[j],
                device_id=to, device_id_type=MESH)
        plan.append((copy(land.at[mine, c]), copy(land.at[k, c])))
    return plan


def _d2d_plan(bufs, send_sems, recv_sems):
    (land,) = bufs
    x, y, c = _place()
    plan = []
    for j, (_, _, k) in enumerate(_other_chips(x, y)):
        def copy(layer, j=j, k=k):
            return pltpu.make_async_remote_copy(
                src_ref=land.at[k, c], dst_ref=land.at[k, layer], send_sem=send_sems.at[j],
                recv_sem=recv_sems.at[j], device_id=(x, y, 1 - c), device_id_type=MESH)
        plan.append((copy(c), copy(1 - c)))
    return plan


def _ici_rows_plan(bufs, send_sems, recv_sems):
    blk, land = bufs
    x, y, c = _place()
    mine = 2 * x + y
    half = blk.shape[0] // 2
    rows = pl.ds(c * half, half)
    plan = []
    for j, (px, py, k) in enumerate(_other_chips(x, y)):
        def copy(dst, j=j, to=(px, py, c)):
            return pltpu.make_async_remote_copy(
                src_ref=blk.at[rows], dst_ref=dst, send_sem=send_sems.at[j], recv_sem=recv_sems.at[j],
                device_id=to, device_id_type=MESH)
        plan.append((copy(land.at[mine, rows]), copy(land.at[k, rows])))
    return plan


def _d2d_rows_plan(bufs, send_sems, recv_sems):
    (land,) = bufs
    x, y, c = _place()
    half = land.shape[1] // 2
    plan = []
    for j, (_, _, k) in enumerate(_other_chips(x, y)):
        def copy(part, j=j, k=k):
            return pltpu.make_async_remote_copy(
                src_ref=land.at[k, pl.ds(c * half, half)], dst_ref=land.at[k, pl.ds(part * half, half)],
                send_sem=send_sems.at[j], recv_sem=recv_sems.at[j], device_id=(x, y, 1 - c), device_id_type=MESH)
        plan.append((copy(c), copy(1 - c)))
    return plan


class _GatheredWeights:
    STAGES = {"w_out": ("proj0", "conformer0"), "w_ff1": ("gdn0", "y0"), "w_ff2": ("u0", None),
              "w_in1": ("ff2_0", None)}
    ORDER = ("w_in0", "w_out", "w_ff1", "w_ff2", "w_in1")

    def __init__(self, chip, shards):
        self.chip = chip
        self.blk, self.ici, self.d2d, self.full, self.mats = {}, {}, {}, {}, {}
        token = jnp.zeros((8, LANES), F32)
        for k in self.ORDER:
            by_rows = k.startswith("w_in")
            blk = shards["w_in"][int(k[-1])] if by_rows else shards[k]
            self.blk[k] = blk.astype(BF16)
            land = lax.empty((N_CHIPS,) + self.blk[k].shape, BF16)
            plans = (_ici_rows_plan, _d2d_rows_plan) if by_rows else (_ici_plan, _d2d_plan)
            send, recv, bufs, token = _copies_start(f"gather_{k}_ici", plans[0], [self.blk[k], land], token)
            self.ici[k] = (send, recv, bufs, plans)
        self._hand_over("w_in0", token)
        self._finish("w_in0", token)

    def _hand_over(self, k, after):
        send, recv, bufs, plans = self.ici.pop(k)
        _, land = _copies_wait(f"gather_{k}_ici_done", plans[0], send, recv, bufs, after)
        send, recv, bufs, _ = _copies_start(f"gather_{k}_d2d", plans[1], [land], after)
        self.d2d[k] = (send, recv, bufs, plans)

    def _finish(self, k, after):
        send, recv, bufs, plans = self.d2d.pop(k)
        (land,) = _copies_wait(f"gather_{k}_d2d_done", plans[1], send, recv, bufs, after)
        self.full[k] = lax.dynamic_update_slice(land, self.blk[k][None], (self.chip,) + (0,) * self.blk[k].ndim)

    def mark(self, tag, value):
        for k, (first, second) in self.STAGES.items():
            if tag == first:
                self._hand_over(k, value)
                if second is None:
                    self._finish(k, value)
            elif tag == second:
                self._finish(k, value)

    def get(self, kind, l):
        if (kind, l) not in self.mats:
            a = self.full[f"w_in{l}" if kind == "w_in" else kind]
            if kind == "w_in":
                m = _assemble_in([a[k] for k in range(N_CHIPS)])
            elif kind == "w_ff1":
                m = jnp.concatenate([a[k, l] for k in range(N_CHIPS)], axis=-1)
            else:
                m = a[:, l].reshape(-1, D_MODEL)
            self.mats[kind, l] = m
        return self.mats[kind, l]


def _exchange_plan(bufs, send_sems, recv_sems):
    g, got = bufs
    x, y, c = _place()
    half = got.shape[1]
    cp = pltpu.make_async_remote_copy(
        src_ref=g.at[pl.ds(0, N_CHIPS), pl.ds((1 - c) * half, half)], dst_ref=got, send_sem=send_sems.at[0],
        recv_sem=recv_sems.at[0], device_id=(x, y, 1 - c), device_id_type=MESH)
    return [(cp, cp)]


def _scatter_plan(bufs, send_sems, recv_sems):
    p, rcv = bufs
    x, y, c = _place()
    plan = []
    for j, (px, py, _) in enumerate(_other_chips(x, y)):
        cp = pltpu.make_async_remote_copy(
            src_ref=p.at[j], dst_ref=rcv.at[j], send_sem=send_sems.at[j], recv_sem=recv_sems.at[j],
            device_id=(px, py, c), device_id_type=MESH)
        plan.append((cp, cp))
    return plan


def _share_plan(bufs, send_sems, recv_sems):
    mine, sib = bufs
    x, y, c = _place()
    cp = pltpu.make_async_remote_copy(
        src_ref=mine, dst_ref=sib, send_sem=send_sems.at[0], recv_sem=recv_sems.at[0],
        device_id=(x, y, 1 - c), device_id_type=MESH)
    return [(cp, cp)]


def _gather_small(name, pack):
    def body(p_ref, out_ref, send_sems, recv_sems, local_sem):
        x, y, c = _place()
        me = 4 * x + 2 * y + c
        local = pltpu.make_async_copy(p_ref, out_ref.at[me], local_sem)
        local.start()
        flips = [(fx, fy, fc) for fx in (0, 1) for fy in (0, 1) for fc in (0, 1)][1:]
        peers = [((1 - x) if fx else x, (1 - y) if fy else y, (1 - c) if fc else c) for fx, fy, fc in flips]

        def cp(j, slot, to):
            return pltpu.make_async_remote_copy(
                src_ref=p_ref, dst_ref=out_ref.at[slot], send_sem=send_sems.at[j], recv_sem=recv_sems.at[j],
                device_id=to, device_id_type=MESH)

        sends = [cp(j, me, to) for j, to in enumerate(peers)]
        for s in sends:
            s.start()
        for j, (px, py, pc) in enumerate(peers):
            cp(j, 4 * px + 2 * py + pc, (px, py, pc)).wait_recv()
        for s in sends:
            s.wait_send()
        local.wait()

    return pl.pallas_call(
        body, name=name, out_shape=jax.ShapeDtypeStruct((N_DEV,) + pack.shape, pack.dtype), in_specs=[HBM],
        out_specs=HBM,
        scratch_shapes=[pltpu.SemaphoreType.DMA((7,)), pltpu.SemaphoreType.DMA((7,)), pltpu.SemaphoreType.DMA],
    )(pack)


SLAB_ROWS = 256


I_CORE, I_CHIP, I_PEER = 0, 1, 2


def _peer_chip(pre, j):
    return jnp.where(j == 0, pre[I_PEER][0], jnp.where(j == 1, pre[I_PEER + 1][0], pre[I_PEER + 2][0]))


def _pair_sum(name, g, got, where):
    _, half, c = got.shape
    t = _pick_tile(half, (SLAB_ROWS, 128, 64, 8))
    per = half // t

    def body(i, v, car, pre):
        return [v[0] + v[1]], [], []

    return _seq_call(
        name, body, steps=3 * per, prefetch=where,
        ins=[(g, (None, t, c), lambda i, pre: (_peer_chip(pre, i // per), pre[I_CORE][0] * per + i % per, 0)),
             (got, (None, t, c), lambda i, pre: (_peer_chip(pre, i // per), i % per, 0))],
        outs=[((3, half, c), BF16, (None, t, c), lambda i, pre: (i // per, i % per, 0))])[0]


def _chip_sum(name, g, got, rcv, where):
    _, half, c = got.shape
    t = _pick_tile(half, (SLAB_ROWS, 128, 64, 8))
    per = half // t

    def body(i, v, car, pre):
        acc = v[0] + v[1]
        for part in v[2:]:
            acc = acc + part.astype(F32)
        return [acc], [], []

    return _seq_call(
        name, body, steps=per, prefetch=where,
        ins=[(g, (None, t, c), lambda i, pre: (pre[I_CHIP][0], pre[I_CORE][0] * per + i, 0)),
             (got, (None, t, c), lambda i, pre: (pre[I_CHIP][0], i, 0))]
        + [(rcv, (None, t, c), (lambda i, pre, _j=j: (_j, i, 0))) for j in range(3)],
        outs=[((half, c), F32, (t, c), lambda i, pre: (i, 0))])[0]


def _adamw_math(w, g, m, v):
    m = ADAM_B1 * m + (1.0 - ADAM_B1) * g
    v = ADAM_B2 * v + (1.0 - ADAM_B2) * (g * g)
    m_hat = m / (1.0 - ADAM_B1 ** ADAM_STEP)
    v_hat = v / (1.0 - ADAM_B2 ** ADAM_STEP)
    delta = -ADAM_LR * (m_hat / (jnp.sqrt(v_hat) + ADAM_EPS) + ADAM_WD * w)
    return delta, m, v


def _adamw(name, w, g, m, v):
    n, r, c = w.shape
    t = _pick_tile(r, (SLAB_ROWS, 128, 64, 8))
    per = r // t

    def body(i, vals, car):
        return list(_adamw_math(*vals)), [], []

    blk = lambda a: (a, (None, t, c), lambda i: (i // per, i % per, 0))
    out = ((n, r, c), F32, (None, t, c), lambda i: (i // per, i % per, 0))
    return _seq_call(name, body, steps=n * per, ins=[blk(w), blk(g), blk(m), blk(v)], outs=[out] * 3)


def _adamw_layer(name, layer, w, mine, sib, m, v, where, into):
    n, r, c = w.shape
    half = r // 2
    t = _pick_tile(half, (SLAB_ROWS, 128, 64, 8))
    per = half // t

    def body(i, vals, car, pre):
        g = jnp.where(i // per == pre[I_CORE][0], vals[1], vals[2])
        return [g] + list(_adamw_math(vals[0], g, vals[3], vals[4])), [], []

    of_layer = lambda a: (a, (None, t, c), lambda i, pre: (layer, i, 0))
    halves = lambda a: (a, (t, c), lambda i, pre: (i % per, 0))
    out = ((n, r, c), F32, (None, t, c), lambda i, pre: (layer, i, 0))
    return _seq_call(name, body, steps=2 * per, prefetch=where, into=into,
                     ins=[of_layer(w), halves(mine), halves(sib), of_layer(m), of_layer(v)], outs=[out] * 4)


def _ordered_sum(name, packs):
    n, r, c = packs.shape

    def body(i, v, car):
        acc = v[0][0]
        for k in range(1, n):
            acc = acc + v[0][k]
        return [acc], [], []

    return _seq_call(name, body, steps=1, ins=[_whole(packs)], outs=[((r, c), F32, (r, c), lambda i: (0, 0))])[0]


def _pack(arrays):
    flat = []
    for a in arrays:
        a = a.reshape(-1).astype(F32)
        pad = (-a.shape[0]) % LANES
        flat.append(jnp.pad(a, (0, pad)) if pad else a)
    v = jnp.concatenate(flat)
    pad = (-v.shape[0]) % (64 * LANES)
    if pad:
        v = jnp.pad(v, (0, pad))
    return v.reshape(-1, LANES)


def _unpack(pack, shapes):
    v = pack.reshape(-1)
    out, off = [], 0
    for s in shapes:
        n = math.prod(s)
        out.append(v[off:off + n].reshape(s))
        off += n + ((-n) % LANES)
    return out


GATE_LO = 8 * GROUP_WIDTH
GATE_HI = GATE_LO + 2 * N_HEADS
IN_CUT = D_IN_PROJ // N_CHIPS


def _padded_col(o):
    if o < GATE_LO:
        return o
    return 12 * GROUP_WIDTH + (o - GATE_LO) if o < GATE_HI else o - 2 * N_HEADS


def _runs(lo, hi, breaks):
    cuts = sorted({lo, hi} | {b for b in breaks if lo < b < hi})
    return list(zip(cuts[:-1], cuts[1:]))


def _assemble_in(blocks):
    shard_edges = [IN_CUT * k for k in range(1, N_CHIPS)]
    pieces = []
    for lo, hi in ((0, GATE_LO), (GATE_HI, D_IN_PROJ), (GATE_LO, GATE_HI)):
        for a, b in _runs(lo, hi, shard_edges):
            k = a // IN_CUT
            pieces.append(blocks[k][:, a - IN_CUT * k:b - IN_CUT * k])
    pieces.append(jnp.zeros((blocks[0].shape[0], LANES - 2 * N_HEADS), blocks[0].dtype))
    return jnp.concatenate(pieces, axis=-1)


def _cut_in(padded):
    cuts = []
    for k in range(N_CHIPS):
        runs = _runs(IN_CUT * k, IN_CUT * (k + 1), (GATE_LO, GATE_HI))
        cuts.append(jnp.concatenate([padded[:, _padded_col(a):_padded_col(a) + (b - a)] for a, b in runs], axis=-1))
    return jnp.stack(cuts)


BIG = ("w_in", "w_out", "w_ff1", "w_ff2")


class _GradReducer:
    EARLY = ("w_ff2", "w_ff1")

    def __init__(self, where, state):
        self.where, self.state = where, state
        self.out = {k: () for k in BIG}
        self.flight = {}

    @staticmethod
    def _join(tokens):
        tokens = [t for t in tokens if t is not None]
        return functools.reduce(lambda a, b: a + b, tokens) if tokens else None

    def mark(self, tag, value, grad=None):
        if tag == "ff2_dw0":
            token = self._add_pairs(1, BIG, value)
            return self._join([token, self._swap(0, "w_ff2", grad, token)])
        if tag == "ff1_dw0":
            return self._swap(0, "w_ff1", grad, value)
        if tag == "out_proj_dx0":
            return self._add_pairs(0, self.EARLY, value)
        if tag == "out_proj_dw0":
            return self._swap(0, "w_out", grad, value)
        if tag == "conformer_bwd0":
            return self._join([self._add_chips(1, BIG, value), self._add_pairs(0, ("w_out",), value)])
        if tag == "in_proj_dx0":
            self._finish(1, BIG, value)
        return None

    def layer_done(self, l, grads, after):
        rest = [k for k in BIG if (l, k) not in self.flight]
        token = self._join([self._swap(l, k, grads[k], after) for k in rest])
        if l == 0:
            self._add_pairs(0, rest, after)
            self._add_chips(0, BIG, after)
            self._finish(0, BIG, after)
        return token

    def _swap(self, l, k, g, after):
        if k == "w_in":
            g = _cut_in(g)
        got = lax.empty((N_CHIPS, g.shape[1] // 2, g.shape[2]), F32)
        send, recv, bufs, token = _copies_start(f"swap_halves_{k}{l}", _exchange_plan, [g, got], after, n=1)
        self.flight[l, k] = (send, recv, bufs)
        return token

    def _add_pairs(self, l, ks, after):
        tokens = []
        for k in ks:
            send, recv, bufs = self.flight[l, k]
            g, got = _copies_wait(f"swap_halves_{k}{l}_done", _exchange_plan, send, recv, bufs, after)
            pair = _pair_sum(f"pair_sum_{k}{l}", g, got, self.where)
            send, recv, bufs, token = _copies_start(
                f"scatter_{k}{l}", _scatter_plan, [pair, lax.empty(pair.shape, pair.dtype)], after)
            self.flight[l, k] = (send, recv, bufs, g, got)
            tokens.append(token)
        return self._join(tokens)

    def _add_chips(self, l, ks, after):
        tokens = []
        for k in ks:
            send, recv, bufs, g, got = self.flight[l, k]
            _, rcv = _copies_wait(f"scatter_{k}{l}_done", _scatter_plan, send, recv, bufs, after)
            mine = _chip_sum(f"chip_sum_{k}{l}", g, got, rcv, self.where)
            send, recv, bufs, token = _copies_start(
                f"share_{k}{l}", _share_plan, [mine, lax.empty(mine.shape, mine.dtype)], after, n=1)
            self.flight[l, k] = (send, recv, bufs)
            tokens.append(token)
        return self._join(tokens)

    def _finish(self, l, ks, after):
        for k in ks:
            send, recv, bufs = self.flight[l, k]
            mine, sib = _copies_wait(f"share_{k}{l}_done", _share_plan, send, recv, bufs, after)
            self.out[k] = _adamw_layer(f"adamw_{k}{l}", l, self.state[k][0], mine, sib, self.state[k][1],
                                       self.state[k][2], self.where, self.out[k])


def kernel(x, lower_bounds, norm_mix_pre, norm_mix_post, norm_ff_pre, norm_ff_post, w_in, w_out, hgrn_norm_w, gdn_conv_w, gdn_a_log, gdn_dt_bias, gdn_norm_w, gmlp_ln_w, gmlp_ln_b, gmlp_w_s, gmlp_b_s, conv_dw_w, conv_dw_b, conv_ln_w, conv_ln_b, w_ff1, w_ff2, loss_target, m_lower_bounds, m_norm_mix_pre, m_norm_mix_post, m_norm_ff_pre, m_norm_ff_post, m_w_in, m_w_out, m_hgrn_norm_w, m_gdn_conv_w, m_gdn_a_log, m_gdn_dt_bias, m_gdn_norm_w, m_gmlp_ln_w, m_gmlp_ln_b, m_gmlp_w_s, m_gmlp_b_s, m_conv_dw_w, m_conv_dw_b, m_conv_ln_w, m_conv_ln_b, m_w_ff1, m_w_ff2, v_lower_bounds, v_norm_mix_pre, v_norm_mix_post, v_norm_ff_pre, v_norm_ff_post, v_w_in, v_w_out, v_hgrn_norm_w, v_gdn_conv_w, v_gdn_a_log, v_gdn_dt_bias, v_gdn_norm_w, v_gmlp_ln_w, v_gmlp_ln_b, v_gmlp_w_s, v_gmlp_b_s, v_conv_dw_w, v_conv_dw_b, v_conv_ln_w, v_conv_ln_b, v_w_ff1, v_w_ff2):
    args = dict(locals())
    chip = 2 * lax.axis_index("x") + lax.axis_index("y")
    where = tuple(jnp.asarray(v, jnp.int32).reshape(1)
                  for v in (lax.axis_index("c"), chip, chip ^ 2, chip ^ 1, chip ^ 3))

    wts = _GatheredWeights(chip, dict(w_in=w_in, w_out=w_out, w_ff1=w_ff1, w_ff2=w_ff2))
    cut_shapes = [gdn_conv_w.shape, conv_dw_w.shape]
    cuts = _gather_small("gather_cut_small", _pack([gdn_conv_w, conv_dw_w]))
    cut_parts = [_unpack(cuts[2 * k], cut_shapes) for k in range(N_CHIPS)]
    sp = {k: args[k] for k in SMALL}
    sp["gdn_conv_w"] = jnp.concatenate([p[0] for p in cut_parts], axis=-1)
    sp["conv_dw_w"] = jnp.concatenate([p[1] for p in cut_parts], axis=-1)

    red = _GradReducer(where, {k: (args[k], args["m_" + k], args["v_" + k]) for k in BIG})
    loss_part, grad_x, small_g = _local_step(x[0], loss_target[0], sp, wts, red)

    names = SMALL + ["loss"]
    small_g["loss"] = loss_part.reshape(1)
    shapes = [small_g[k].shape for k in names]
    total = _ordered_sum("sum_small", _gather_small("gather_small", _pack([small_g[k] for k in names])))
    summed = dict(zip(names, _unpack(total, shapes)))
    loss = summed.pop("loss")[0]
    for k, width in (("gdn_conv_w", gdn_conv_w.shape[-1]), ("conv_dw_w", conv_dw_w.shape[-1])):
        summed[k] = lax.dynamic_slice_in_dim(summed[k], chip * width, width, axis=-1)

    grads, deltas, new_m, new_v = {}, {}, {}, {}
    for k in BIG:
        grads[k], deltas[k], new_m[k], new_v[k] = red.out[k]
    local_shapes = [args[k].shape for k in SMALL]
    packs = [_pack([src[k] for k in SMALL]) for src in (
        {k: args[k] for k in SMALL}, summed, {k: args["m_" + k] for k in SMALL}, {k: args["v_" + k] for k in SMALL})]
    d_s, m_s, v_s = _adamw("adamw_small", *[p[None] for p in packs])
    for k, d, mm, vv in zip(SMALL, _unpack(d_s[0], local_shapes), _unpack(m_s[0], local_shapes),
                            _unpack(v_s[0], local_shapes)):
        grads[k], deltas[k], new_m[k], new_v[k] = summed[k], d, mm, vv

    order = ["lower_bounds", "norm_mix_pre", "norm_mix_post", "norm_ff_pre", "norm_ff_post", "w_in", "w_out",
             "hgrn_norm_w", "gdn_conv_w", "gdn_a_log", "gdn_dt_bias", "gdn_norm_w", "gmlp_ln_w", "gmlp_ln_b",
             "gmlp_w_s", "gmlp_b_s", "conv_dw_w", "conv_dw_b", "conv_ln_w", "conv_ln_b", "w_ff1", "w_ff2"]
    return (loss, grad_x[None], *[grads[k] for k in order], *[deltas[k] for k in order],
            *[new_m[k] for k in order], *[new_v[k] for k in order])
```

```python
import functools
import math

import jax
import jax.numpy as jnp
from jax import lax
from jax.experimental import pallas as pl
from jax.experimental.pallas import tpu as pltpu

F32 = jnp.float32
BF16 = jnp.bfloat16

DEPTH = 2
D_MODEL = 2048
GROUP_WIDTH = 512
HEAD_DIM = 128
N_HEADS = 4
CHUNK = 64
SHORT_CONV = 4
MIX_CHUNK = 128
CONV_WIDTH = 31
D_FF = 4 * D_MODEL
D_IN_PROJ = 12 * GROUP_WIDTH + 2 * N_HEADS
EPS = 1e-6
NEG_BIG = -1e30
TINY = 1e-30
ADAM_LR = 0.001
ADAM_B1 = 0.9
ADAM_B2 = 0.999
ADAM_EPS = 1e-08
ADAM_WD = 0.01
ADAM_STEP = 10

LANES = 128
P_IN = 12 * GROUP_WIDTH + LANES
SUB_BLOCK = 16
HGRN_TILE = 128
GDN_TILE = 128
CONV_TILE = 128
CONV_HALO = 32
VMEM_LIMIT = 56 * 1024 * 1024
N_CHIPS = 4
N_DEV = 8
MESH = pl.DeviceIdType.MESH


_DIMS = {
    "nn": (((1,), (0,)), ((), ())),
    "nt": (((1,), (1,)), ((), ())),
    "tn": (((0,), (0,)), ((), ())),
}


def _split2(a):
    hi = a.astype(BF16)
    return hi, (a - hi.astype(F32)).astype(BF16)


def _raw_mm(a, b, mode, exact):
    dot = lambda p, q: lax.dot_general(p, q, _DIMS[mode], preferred_element_type=F32)
    if not exact:
        return dot(a.astype(BF16), b.astype(BF16))
    a_hi, a_lo = _split2(a)
    b_hi, b_lo = _split2(b)
    return dot(a_hi, b_hi) + (dot(a_hi, b_lo) + dot(a_lo, b_hi))


@functools.partial(jax.custom_vjp, nondiff_argnums=(2, 3))
def _mm(a, b, mode, exact):
    return _raw_mm(a, b, mode, exact)


def _mm_fwd(a, b, mode, exact):
    return _raw_mm(a, b, mode, exact), (a, b)


def _mm_bwd(mode, exact, res, g):
    a, b = res
    if mode == "nn":
        return _raw_mm(g, b, "nt", exact), _raw_mm(a, g, "tn", exact)
    if mode == "nt":
        return _raw_mm(g, b, "nn", exact), _raw_mm(g, a, "tn", exact)
    return _raw_mm(b, g, "nt", exact), _raw_mm(a, g, "nn", exact)


_mm.defvjp(_mm_fwd, _mm_bwd)


def _sig(x):
    return jax.nn.sigmoid(x)


def _silu(x):
    return x * jax.nn.sigmoid(x)


def _gelu(x):
    return 0.5 * x * (1.0 + lax.erf(x * (1.0 / math.sqrt(2.0))))


def _rms(x, w):
    return x * lax.rsqrt(jnp.mean(x * x, axis=-1, keepdims=True) + EPS) * w


def _ln(x, w, b):
    mu = jnp.mean(x, axis=-1, keepdims=True)
    xc = x - mu
    var = jnp.mean(xc * xc, axis=-1, keepdims=True)
    return xc * lax.rsqrt(var + EPS) * w + b


def _iota(shape, dim):
    return lax.broadcasted_iota(jnp.int32, shape, dim)


def _tri_mm(x, mode):
    n = x.shape[0]
    tri = (_iota((n, n), 0) >= _iota((n, n), 1)).astype(BF16)
    x1 = x.astype(BF16)
    r1 = x - x1.astype(F32)
    x2 = r1.astype(BF16)
    x3 = (r1 - x2.astype(F32)).astype(BF16)
    dot = lambda q: lax.dot_general(tri, q, _DIMS[mode], preferred_element_type=F32)
    return dot(x1) + (dot(x2) + dot(x3))


@jax.custom_vjp
def _cumsum_rows(x):
    return _tri_mm(x, "nn")


_cumsum_rows.defvjp(lambda x: (_tri_mm(x, "nn"), None), lambda _, g: (_tri_mm(g, "tn"),))


def _hgrn_head(q, k, v, b, st):
    n = q.shape[0]
    ii = _iota((n, 1), 0)
    zpad = jnp.zeros((SUB_BLOCK, HEAD_DIM), F32)
    k_ext = jnp.concatenate([zpad, k], axis=0)
    b_ext = jnp.concatenate([zpad, b], axis=0)
    v_ext = jnp.concatenate([zpad, v], axis=0)
    o = jnp.zeros((n, HEAD_DIM), F32)
    for d in range(SUB_BLOCK):
        ks = k_ext[SUB_BLOCK - d:SUB_BLOCK - d + n]
        bs = b_ext[SUB_BLOCK - d:SUB_BLOCK - d + n]
        vs = v_ext[SUB_BLOCK - d:SUB_BLOCK - d + n]
        e = jnp.exp(jnp.where((ii % SUB_BLOCK) >= d, b - bs, NEG_BIG))
        o = o + jnp.sum(q * ks * e, axis=-1, keepdims=True) * vs
    blocks = [o[0:SUB_BLOCK]]
    for blk in range(1, n // SUB_BLOCK):
        lo = SUB_BLOCK * blk
        r = b[lo - 1:lo]
        a_q = q[lo:lo + SUB_BLOCK] * jnp.exp(b[lo:lo + SUB_BLOCK] - r)
        b_k = jnp.where(ii < lo, k * jnp.exp(jnp.minimum(r - b, 0.0)), 0.0)
        sc = _mm(a_q, b_k, "nt", False)
        blocks.append(o[lo:lo + SUB_BLOCK] + _mm(sc, v, "nn", False))
    o = jnp.concatenate(blocks, axis=0)
    o = o + _mm(q * jnp.exp(b), st, "nt", False)
    b_end = b[n - 1:n]
    st_new = st * jnp.exp(b_end) + _mm(v, k * jnp.exp(b_end - b), "tn", False)
    return o, st_new


def _hgrn_chunk(layer, lbp, nw, aq, af, ai, ag, states):
    rows = [lbp[i:i + 1, :] for i in range(DEPTH)]
    mx = functools.reduce(jnp.maximum, rows)
    es = [jnp.exp(r - mx) for r in rows]
    den = functools.reduce(lambda p, s: p + s, es)
    soft = [e / den for e in es]
    lb = functools.reduce(lambda p, s: p + s, soft[:layer + 1]) - soft[0]
    f = lb + (1.0 - lb) * _sig(af)
    logf = jnp.log(jnp.maximum(f, TINY))
    k = (1.0 - lb) * _sig(-af)
    q = _silu(aq)
    b = _cumsum_rows(logf)
    outs, new_states = [], []
    for h in range(N_HEADS):
        sl = slice(HEAD_DIM * h, HEAD_DIM * (h + 1))
        o, st = _hgrn_head(q[:, sl], k[:, sl], ai[:, sl], b[:, sl], states[h])
        outs.append(_rms(o, nw) * _silu(ag[:, sl]))
        new_states.append(st)
    return jnp.concatenate(outs, axis=1), new_states


def _hgrn_tile(layer, lbp, nw, aq, af, ai, ag, states):
    outs = []
    for ci in range(aq.shape[0] // CHUNK):
        rs = slice(CHUNK * ci, CHUNK * (ci + 1))
        o, states = _hgrn_chunk(layer, lbp, nw, aq[rs], af[rs], ai[rs], ag[rs], states)
        outs.append(o)
    return jnp.concatenate(outs, axis=0), states


def _short_conv(prev, cur, w):
    n = cur.shape[0]
    ext = jnp.concatenate([prev[n - 8:n], cur], axis=0)
    y = jnp.zeros_like(cur)
    for t in range(SHORT_CONV):
        off = 8 - (SHORT_CONV - 1) + t
        y = y + w[t:t + 1, :] * ext[off:off + n]
    return _silu(y)


def _gdn_tile(cw, alog, dtb, nw, pq, pk, pv, cq, ck, cv, bz, ab, states):
    n = cq.shape[0]
    q_all = _short_conv(pq, cq, cw[:, 0:GROUP_WIDTH])
    k_all = _short_conv(pk, ck, cw[:, GROUP_WIDTH:2 * GROUP_WIDTH])
    v_all = _short_conv(pv, cv, cw[:, 2 * GROUP_WIDTH:3 * GROUP_WIDTH])
    beta_all = _sig(ab)
    g_all = -jnp.exp(alog) * jax.nn.softplus(ab + dtb)
    units = [(ci, h) for ci in range(n // CHUNK) for h in range(N_HEADS)]
    gc_all = [_cumsum_rows(g_all[CHUNK * ci:CHUNK * (ci + 1)]) for ci in range(n // CHUNK)]
    gc_t = [g.T for g in gc_all]
    ii = _iota((CHUNK, CHUNK), 0)
    jj = _iota((CHUNK, CHUNK), 1)
    eye = (ii == jj).astype(F32)

    def cut(a, ci, h):
        return a[CHUNK * ci:CHUNK * (ci + 1), HEAD_DIM * h:HEAD_DIM * (h + 1)]

    q = [cut(q_all, ci, h) for ci, h in units]
    k = [cut(k_all, ci, h) for ci, h in units]
    v = [cut(v_all, ci, h) for ci, h in units]
    q = [t * lax.rsqrt(jnp.sum(t * t, axis=-1, keepdims=True) + EPS) * (HEAD_DIM ** -0.5) for t in q]
    k = [t * lax.rsqrt(jnp.sum(t * t, axis=-1, keepdims=True) + EPS) for t in k]
    beta = [beta_all[CHUNK * ci:CHUNK * (ci + 1), h:h + 1] for ci, h in units]
    gc = [gc_all[ci][:, N_HEADS + h:N_HEADS + h + 1] for ci, h in units]
    gcr = [gc_t[ci][N_HEADS + h:N_HEADS + h + 1, :] for ci, h in units]
    gamma = [jnp.exp(jnp.where(ii >= jj, a - b, NEG_BIG)) for a, b in zip(gc, gcr)]
    kb = [a * b for a, b in zip(k, beta)]
    m = [jnp.where(ii > jj, _mm(a, b, "nt", False) * g, 0.0) for a, b, g in zip(kb, k, gamma)]
    inv = [eye - t for t in m]
    p = m
    for _ in range(max(1, int(math.ceil(math.log2(CHUNK))) - 1)):
        p = [_mm(t, t, "nn", True) for t in p]
        inv = [a + _mm(a, t, "nn", True) for a, t in zip(inv, p)]
    eg = [jnp.exp(t) for t in gc]
    u = [_mm(a, b * c, "nn", True) for a, b, c in zip(inv, v, beta)]
    w = [_mm(a, b * c, "nn", True) for a, b, c in zip(inv, kb, eg)]
    qk = [_mm(a, b, "nt", False) * g for a, b, g in zip(q, k, gamma)]
    qd = [a * b for a, b in zip(q, eg)]
    g_end = [t[CHUNK - 1:CHUNK] for t in gc]
    kd = [a * jnp.exp(e - g) for a, e, g in zip(k, g_end, gc)]
    states = list(states)
    outs = {}
    for i, (ci, h) in enumerate(units):
        st = states[h]
        v_new = u[i] - _mm(w[i], st, "nt", False)
        outs[ci, h] = _mm(qd[i], st, "nt", False) + _mm(qk[i], v_new, "nn", False)
        states[h] = st * jnp.exp(g_end[i]) + _mm(v_new, kd[i], "tn", False)
    rows = []
    for ci in range(n // CHUNK):
        rows.append(jnp.concatenate(
            [_rms(outs[ci, h], nw) * _silu(cut(bz, ci, h)) for h in range(N_HEADS)], axis=1))
    return jnp.concatenate(rows, axis=0), states


def _gmlp_tile(ln_w, ln_b, ws, bs_cols, cu, cv):
    u = _gelu(cu)
    v = _ln(_gelu(cv), ln_w, ln_b)
    n = cu.shape[0]
    tril = _iota((n, n), 0) >= _iota((n, n), 1)
    outs = []
    for h in range(N_HEADS):
        sl = slice(HEAD_DIM * h, HEAD_DIM * (h + 1))
        wc = jnp.where(tril, ws[h], 0.0)
        outs.append(_mm(wc, v[:, sl], "nn", False) + bs_cols[:, h:h + 1])
    return u * jnp.concatenate(outs, axis=1)


def _conformer_tile(dw_w, dw_b, ln_w, ln_b, pa, pg, ca, cg):
    n = ca.shape[0]
    yp = pa[n - CONV_HALO:n] * _sig(pg[n - CONV_HALO:n])
    ext = jnp.concatenate([yp, ca * _sig(cg)], axis=0)
    acc = jnp.zeros_like(ca)
    for t in range(CONV_WIDTH):
        off = CONV_HALO - (CONV_WIDTH - 1) + t
        acc = acc + dw_w[t:t + 1, :] * ext[off:off + n]
    return _silu(_ln(acc + dw_b, ln_w, ln_b))


def _seq_call(name, body, *, steps, ins, outs, accs=(), carries=(), reverse=False, prefetch=None, into=()):
    n_in, n_out, n_acc, n_car = len(ins), len(outs), len(accs), len(carries)
    n_into = len(into)
    n_pre = 0 if prefetch is None else len(prefetch)

    def logical(g):
        return (steps - 1 - g) if reverse else g

    def kern(*refs):
        pre = refs[:n_pre]
        refs = refs[n_pre:]
        in_refs = refs[:n_in]
        refs = refs[n_in + n_into:]
        out_refs = refs[:n_out]
        acc_refs = refs[n_out:n_out + n_acc]
        car_refs = refs[n_out + n_acc:]
        g = pl.program_id(0)

        @pl.when(g == 0)
        def _():
            for r in list(acc_refs) + list(car_refs):
                r[...] = jnp.zeros(r.shape, r.dtype)

        o, a, c = body(logical(g), [r[...] for r in in_refs], [r[...] for r in car_refs], *((pre,) if n_pre else ()))
        for r, v in zip(out_refs, o, strict=True):
            r[...] = v.astype(r.dtype)
        for r, v in zip(acc_refs, a, strict=True):
            r[...] += v
        for r, v in zip(car_refs, c, strict=True):
            r[...] = v

    def spec(block, fn):
        return pl.BlockSpec(block, lambda g, *pre: fn(logical(g), *((pre,) if n_pre else ())))

    in_specs = [spec(bs, fn) for (_, bs, fn) in ins] + [HBM] * n_into
    out_specs = [spec(bs, fn) for (_, _, bs, fn) in outs]
    out_specs += [pl.BlockSpec(shape, lambda g, *pre, _n=len(shape): (0,) * _n) for (shape, _) in accs]
    out_shape = [jax.ShapeDtypeStruct(s, d) for (s, d, _, _) in outs]
    out_shape += [jax.ShapeDtypeStruct(s, d) for (s, d) in accs]
    grid_spec = pltpu.PrefetchScalarGridSpec(
        num_scalar_prefetch=n_pre, grid=(steps,), in_specs=in_specs, out_specs=out_specs,
        scratch_shapes=[pltpu.VMEM(s, d) for (s, d) in carries])
    args = ([] if prefetch is None else list(prefetch)) + [a for (a, _, _) in ins] + list(into)
    return pl.pallas_call(
        kern, name=name, grid_spec=grid_spec, out_shape=out_shape,
        input_output_aliases={n_pre + n_in + i: i for i in range(n_into)},
        compiler_params=pltpu.CompilerParams(dimension_semantics=("arbitrary",), vmem_limit_bytes=VMEM_LIMIT),
    )(*args)


def _whole(a):
    nd = a.ndim
    return (a, a.shape, lambda i, *pre: (0,) * nd)


def _rows(a, tile, col=0, width=None, shift=0):
    width = a.shape[1] if width is None else width
    if shift:
        return (a, (tile, width), lambda i, *pre: (jnp.maximum(i + shift, 0), col))
    return (a, (tile, width), lambda i, *pre: (i, col))


def _row_out(n_rows, width, dtype, tile):
    return ((n_rows, width), dtype, (tile, width), lambda i, *pre: (i, 0))


def _pick_tile(n, prefs):
    for t in prefs:
        if n % t == 0:
            return t
    return n


def _matmul(name, a, b, mode, out_dtypes, epilogue=None, extras=(), place=None, max_tm=1024):
    if mode == "nn":
        (m, k), n = a.shape, b.shape[1]
    elif mode == "nt":
        (m, k), n = a.shape, b.shape[0]
    else:
        (k, m), n = a.shape, b.shape[1]
    tm = _pick_tile(m, tuple(t for t in (1024, 512, 256, 128) if t <= max_tm))
    tn = _pick_tile(n, (1024, 896, 512, 256, 128))
    tk = _pick_tile(k, (2048, 896, 512, 256, 128))
    nk = k // tk
    n_ex = len(extras)
    n_out = len(out_dtypes)

    def kern(*refs):
        a_ref, b_ref = refs[0], refs[1]
        ex_refs = refs[2:2 + n_ex]
        out_refs = refs[2 + n_ex:2 + n_ex + n_out]
        acc_ref = refs[2 + n_ex + n_out]
        kk = pl.program_id(2)

        @pl.when(kk == 0)
        def _():
            acc_ref[...] = jnp.zeros(acc_ref.shape, F32)

        acc_ref[...] += lax.dot_general(a_ref[...], b_ref[...], _DIMS[mode], preferred_element_type=F32)

        @pl.when(kk == nk - 1)
        def _():
            acc = acc_ref[...]
            vals = (acc,) if epilogue is None else epilogue(acc, *[r[...] for r in ex_refs])
            for r, v in zip(out_refs, vals, strict=True):
                r[...] = v.astype(r.dtype)

    if mode == "tn":
        a_spec = pl.BlockSpec((tk, tm), lambda i, j, kk: (kk, i))
    else:
        a_spec = pl.BlockSpec((tm, tk), lambda i, j, kk: (i, kk))
    if mode == "nt":
        b_spec = pl.BlockSpec((tn, tk), lambda i, j, kk: (j, kk))
    else:
        b_spec = pl.BlockSpec((tk, tn), lambda i, j, kk: (kk, j))
    tile = pl.BlockSpec((tm, tn), lambda i, j, kk: (i, j))
    out_specs = [tile] * n_out
    out_shape = [jax.ShapeDtypeStruct((m, n), d) for d in out_dtypes]
    if place is not None:
        shape, block_fn, index_fn = place
        out_specs = [pl.BlockSpec(block_fn(tm, tn), lambda i, j, kk: index_fn(i, j, tm, tn))]
        out_shape = [jax.ShapeDtypeStruct(shape, out_dtypes[0])]
    return pl.pallas_call(
        kern, name=name, grid=(m // tm, n // tn, nk),
        in_specs=[a_spec, b_spec] + [tile] * n_ex,
        out_specs=out_specs, out_shape=out_shape,
        scratch_shapes=[pltpu.VMEM((tm, tn), F32)],
        compiler_params=pltpu.CompilerParams(
            dimension_semantics=("parallel", "parallel", "arbitrary"), vmem_limit_bytes=VMEM_LIMIT),
    )(a, b, *extras)


ROW_TILE = 256


def _rms_fwd(name, x, w):
    s, d = x.shape
    t = _pick_tile(s, (ROW_TILE,))

    def body(i, v, c):
        return [_rms(v[0], v[1])], [], []

    return _seq_call(name, body, steps=s // t, ins=[_rows(x, t), _whole(w)], outs=[_row_out(s, d, BF16, t)])[0]


def _resid_rms_fwd(name, x, y, w):
    s, d = x.shape
    t = _pick_tile(s, (ROW_TILE,))

    def body(i, v, c):
        return [v[0] + _rms(v[1], v[2])], [], []

    return _seq_call(name, body, steps=s // t, ins=[_rows(x, t), _rows(y, t), _whole(w)],
                     outs=[_row_out(s, d, F32, t)])[0]


def _resid_rms_norm_fwd(name, x, y, w_post, w_pre):
    s, d = x.shape
    t = _pick_tile(s, (ROW_TILE,))

    def body(i, v, c):
        x1 = v[0] + _rms(v[1], v[2])
        return [x1, _rms(x1, v[3])], [], []

    return _seq_call(name, body, steps=s // t, ins=[_rows(x, t), _rows(y, t), _whole(w_post), _whole(w_pre)],
                     outs=[_row_out(s, d, F32, t), _row_out(s, d, BF16, t)])


def _rms_bwd(name, x, w, dh, dres):
    s, d = x.shape
    t = _pick_tile(s, (ROW_TILE,))

    def body(i, v, c):
        _, vjp = jax.vjp(_rms, v[0], v[1])
        dx, dw = vjp(v[2])
        return [dx + v[3]], [dw], []

    return _seq_call(name, body, steps=s // t, ins=[_rows(x, t), _whole(w), _rows(dh, t), _rows(dres, t)],
                     outs=[_row_out(s, d, F32, t)], accs=[((1, d), F32)])


def _resid_rms_bwd(name, y, w, dxo):
    s, d = y.shape
    t = _pick_tile(s, (ROW_TILE,))

    def body(i, v, c):
        _, vjp = jax.vjp(_rms, v[0], v[1])
        dy, dw = vjp(v[2])
        return [dy], [dw], []

    return _seq_call(name, body, steps=s // t, ins=[_rows(y, t), _whole(w), _rows(dxo, t)],
                     outs=[_row_out(s, d, BF16, t)], accs=[((1, d), F32)])


def _loss_head(name, y, target):
    s, d = y.shape
    t = _pick_tile(s, (ROW_TILE,))

    def body(i, v, c):
        err = v[0] - v[1]
        part = 0.5 * jnp.sum(jnp.mean(err * err, axis=-1, keepdims=True))
        return [err * (1.0 / d)], [jnp.full((1, LANES), part, F32)], []

    return _seq_call(name, body, steps=s // t, ins=[_rows(y, t), _rows(target, t)],
                     outs=[_row_out(s, d, F32, t)], accs=[((1, LANES), F32)])


SEG = {n: i for i, n in enumerate(
    ["a_q", "a_f", "a_i", "a_g", "b_q", "b_k", "b_v", "b_z", "c_u", "c_v", "d_a", "d_gate"])}
AB_COL = 12 * GROUP_WIDTH // LANES


def _seg(proj, name, tile, shift=0):
    return _rows(proj, tile, col=SEG[name], width=GROUP_WIDTH, shift=shift)


def _state_block():
    return (1, N_HEADS * HEAD_DIM, HEAD_DIM), lambda i, *pre: (i, 0, 0)


def _split_states(blk):
    return [blk[0, HEAD_DIM * h:HEAD_DIM * (h + 1), :] for h in range(N_HEADS)]


STATE_CARRIES = [((HEAD_DIM, HEAD_DIM), F32)] * N_HEADS


def _hgrn_fwd(layer, proj, lbp, nw):
    s = proj.shape[0]
    n = s // HGRN_TILE
    sb, sf = _state_block()

    def body(i, v, st):
        o, new = _hgrn_tile(layer, v[0], v[1], v[2], v[3], v[4], v[5], st)
        return [o, jnp.concatenate(st, axis=0)[None]], [], new

    return _seq_call(
        f"hgrn_fwd{layer}", body, steps=n,
        ins=[_whole(lbp), _whole(nw)] + [_seg(proj, k, HGRN_TILE) for k in ("a_q", "a_f", "a_i", "a_g")],
        outs=[_row_out(s, GROUP_WIDTH, BF16, HGRN_TILE), ((n, N_HEADS * HEAD_DIM, HEAD_DIM), F32, sb, sf)],
        carries=STATE_CARRIES)


def _hgrn_bwd(layer, proj, lbp, nw, states, dmix):
    s = proj.shape[0]
    n = s // HGRN_TILE
    sb, sf = _state_block()

    def body(i, v, dst):
        st = _split_states(v[6])

        def f(lbp_, nw_, aq, af, ai, ag, *st_):
            return _hgrn_tile(layer, lbp_, nw_, aq, af, ai, ag, list(st_))

        _, vjp = jax.vjp(f, v[0], v[1], v[2], v[3], v[4], v[5], *st)
        g = vjp((v[7], list(dst)))
        return [jnp.concatenate(g[2:6], axis=1)], [g[0], g[1]], list(g[6:])

    return _seq_call(
        f"hgrn_bwd{layer}", body, steps=n, reverse=True,
        ins=[_whole(lbp), _whole(nw)] + [_seg(proj, k, HGRN_TILE) for k in ("a_q", "a_f", "a_i", "a_g")]
        + [(states, sb, sf), _rows(dmix, HGRN_TILE, col=0, width=GROUP_WIDTH)],
        outs=[_row_out(s, 4 * GROUP_WIDTH, BF16, HGRN_TILE)],
        accs=[(lbp.shape, F32), (nw.shape, F32)], carries=STATE_CARRIES)


def _gdn_ins(proj, cw, alog, dtb, nw):
    return ([_whole(cw), _whole(alog), _whole(dtb), _whole(nw)]
            + [_seg(proj, k, GDN_TILE, shift=-1) for k in ("b_q", "b_k", "b_v")]
            + [_seg(proj, k, GDN_TILE) for k in ("b_q", "b_k", "b_v", "b_z")]
            + [_rows(proj, GDN_TILE, col=AB_COL, width=LANES)])


def _mask_prev(i, vals):
    keep = (i > 0).astype(F32)
    return [p * keep for p in vals]


def _gdn_fwd(layer, proj, cw, alog, dtb, nw):
    s = proj.shape[0]
    n = s // GDN_TILE
    sb, sf = _state_block()

    def body(i, v, st):
        prev = _mask_prev(i, v[4:7])
        o, new = _gdn_tile(v[0], v[1], v[2], v[3], *prev, *v[7:12], st)
        return [o, jnp.concatenate(st, axis=0)[None]], [], new

    return _seq_call(
        f"gdn_fwd{layer}", body, steps=n, ins=_gdn_ins(proj, cw, alog, dtb, nw),
        outs=[_row_out(s, GROUP_WIDTH, BF16, GDN_TILE), ((n, N_HEADS * HEAD_DIM, HEAD_DIM), F32, sb, sf)],
        carries=STATE_CARRIES)


def _gdn_bwd(layer, proj, cw, alog, dtb, nw, states, dmix):
    s = proj.shape[0]
    n = s // GDN_TILE
    sb, sf = _state_block()

    def body(i, v, car):
        dst, dprev = car[:N_HEADS], car[N_HEADS:]
        prev = _mask_prev(i, v[4:7])
        st = _split_states(v[12])

        def f(cw_, alog_, dtb_, nw_, pq, pk, pv, cq, ck, cv, bz, ab, *st_):
            return _gdn_tile(cw_, alog_, dtb_, nw_, pq, pk, pv, cq, ck, cv, bz, ab, list(st_))

        _, vjp = jax.vjp(f, v[0], v[1], v[2], v[3], *prev, *v[7:12], *st)
        g = vjp((v[13], list(dst)))
        dcur = [g[7] + dprev[0], g[8] + dprev[1], g[9] + dprev[2], g[10]]
        return ([jnp.concatenate(dcur, axis=1), g[11]], list(g[0:4]), list(g[12:]) + list(g[4:7]))

    return _seq_call(
        f"gdn_bwd{layer}", body, steps=n, reverse=True,
        ins=_gdn_ins(proj, cw, alog, dtb, nw) + [(states, sb, sf), _rows(dmix, GDN_TILE, col=1, width=GROUP_WIDTH)],
        outs=[_row_out(s, 4 * GROUP_WIDTH, BF16, GDN_TILE), _row_out(s, LANES, BF16, GDN_TILE)],
        accs=[(cw.shape, F32), (alog.shape, F32), (dtb.shape, F32), (nw.shape, F32)],
        carries=STATE_CARRIES + [((GDN_TILE, GROUP_WIDTH), F32)] * 3)


def _gmlp_fwd(layer, proj, ln_w, ln_b, ws, bs_cols):
    s = proj.shape[0]

    def body(i, v, c):
        return [_gmlp_tile(*v)], [], []

    return _seq_call(
        f"gmlp_fwd{layer}", body, steps=s // MIX_CHUNK,
        ins=[_whole(ln_w), _whole(ln_b), _whole(ws), _whole(bs_cols),
             _seg(proj, "c_u", MIX_CHUNK), _seg(proj, "c_v", MIX_CHUNK)],
        outs=[_row_out(s, GROUP_WIDTH, BF16, MIX_CHUNK)])[0]


def _gmlp_bwd(layer, proj, ln_w, ln_b, ws, bs_cols, dmix):
    s = proj.shape[0]

    def body(i, v, c):
        _, vjp = jax.vjp(_gmlp_tile, *v[:6])
        g = vjp(v[6])
        return [jnp.concatenate(g[4:6], axis=1)], list(g[0:4]), []

    return _seq_call(
        f"gmlp_bwd{layer}", body, steps=s // MIX_CHUNK,
        ins=[_whole(ln_w), _whole(ln_b), _whole(ws), _whole(bs_cols),
             _seg(proj, "c_u", MIX_CHUNK), _seg(proj, "c_v", MIX_CHUNK),
             _rows(dmix, MIX_CHUNK, col=2, width=GROUP_WIDTH)],
        outs=[_row_out(s, 2 * GROUP_WIDTH, BF16, MIX_CHUNK)],
        accs=[(ln_w.shape, F32), (ln_b.shape, F32), (ws.shape, F32), (bs_cols.shape, F32)])


def _conformer_ins(proj, dw_w, dw_b, ln_w, ln_b):
    return ([_whole(dw_w), _whole(dw_b), _whole(ln_w), _whole(ln_b)]
            + [_seg(proj, k, CONV_TILE, shift=-1) for k in ("d_a", "d_gate")]
            + [_seg(proj, k, CONV_TILE) for k in ("d_a", "d_gate")])


def _conformer_fwd(layer, proj, dw_w, dw_b, ln_w, ln_b):
    s = proj.shape[0]

    def body(i, v, c):
        prev = _mask_prev(i, v[4:6])
        return [_conformer_tile(v[0], v[1], v[2], v[3], *prev, v[6], v[7])], [], []

    return _seq_call(
        f"conformer_fwd{layer}", body, steps=s // CONV_TILE, ins=_conformer_ins(proj, dw_w, dw_b, ln_w, ln_b),
        outs=[_row_out(s, GROUP_WIDTH, BF16, CONV_TILE)])[0]


def _conformer_bwd(layer, proj, dw_w, dw_b, ln_w, ln_b, dmix):
    s = proj.shape[0]

    def body(i, v, dprev):
        prev = _mask_prev(i, v[4:6])
        _, vjp = jax.vjp(_conformer_tile, v[0], v[1], v[2], v[3], *prev, v[6], v[7])
        g = vjp(v[8])
        return [jnp.concatenate([g[6] + dprev[0], g[7] + dprev[1]], axis=1)], list(g[0:4]), [g[4], g[5]]

    return _seq_call(
        f"conformer_bwd{layer}", body, steps=s // CONV_TILE, reverse=True,
        ins=_conformer_ins(proj, dw_w, dw_b, ln_w, ln_b) + [_rows(dmix, CONV_TILE, col=3, width=GROUP_WIDTH)],
        outs=[_row_out(s, 2 * GROUP_WIDTH, BF16, CONV_TILE)],
        accs=[(dw_w.shape, F32), (dw_b.shape, F32), (ln_w.shape, F32), (ln_b.shape, F32)],
        carries=[((CONV_TILE, GROUP_WIDTH), F32)] * 2)


SMALL = ["lower_bounds", "norm_mix_pre", "norm_mix_post", "norm_ff_pre", "norm_ff_post", "hgrn_norm_w",
         "gdn_conv_w", "gdn_a_log", "gdn_dt_bias", "gdn_norm_w", "gmlp_ln_w", "gmlp_ln_b", "gmlp_w_s",
         "gmlp_b_s", "conv_dw_w", "conv_dw_b", "conv_ln_w", "conv_ln_b"]


def _gate_row(v):
    return jnp.pad(v.reshape(1, N_HEADS), ((0, 0), (N_HEADS, LANES - 2 * N_HEADS)))


def _cut_rows(rows, cols):
    return ((N_CHIPS, rows, cols), lambda tm, tn: (None, tm, tn),
            lambda i, j, tm, tn: (i // (rows // tm), i % (rows // tm), j))


def _cut_cols(rows, cols):
    return ((N_CHIPS, rows, cols), lambda tm, tn: (None, tm, tn),
            lambda i, j, tm, tn: (j // (cols // tn), i, j % (cols // tn)))


def _relu2(acc):
    r = jnp.maximum(acc, 0.0)
    return acc, r * r


def _relu2_bwd(acc, u):
    return (2.0 * jnp.maximum(u, 0.0) * acc,)


def _local_step(x, target, sp, wts, red):
    row = lambda v: v.reshape(1, -1)
    saved = []
    w_in, w_out, w_ff1, w_ff2 = [], [], [], []
    for l in range(DEPTH):
        par = dict(
            lbp=sp["lower_bounds"], hn=row(sp["hgrn_norm_w"][l]), cw=sp["gdn_conv_w"][l],
            alog=_gate_row(sp["gdn_a_log"][l]), dtb=_gate_row(sp["gdn_dt_bias"][l]), gn=row(sp["gdn_norm_w"][l]),
            glw=row(sp["gmlp_ln_w"][l]), glb=row(sp["gmlp_ln_b"][l]), ws=sp["gmlp_w_s"][l],
            bsc=jnp.pad(sp["gmlp_b_s"][l].T, ((0, 0), (0, LANES - N_HEADS))),
            dww=sp["conv_dw_w"][l], dwb=row(sp["conv_dw_b"][l]), clw=row(sp["conv_ln_w"][l]),
            clb=row(sp["conv_ln_b"][l]), n1=row(sp["norm_mix_pre"][l]), n2=row(sp["norm_mix_post"][l]),
            n3=row(sp["norm_ff_pre"][l]), n4=row(sp["norm_ff_post"][l]))
        h = _rms_fwd(f"norm_mix_pre{l}", x, par["n1"])
        w_in.append(wts.get("w_in", l))
        proj = _matmul(f"in_proj{l}", h, w_in[l], "nn", [F32])[0]
        wts.mark(f"proj{l}", proj)
        o_a, st_a = _hgrn_fwd(l, proj, par["lbp"], par["hn"])
        o_b, st_b = _gdn_fwd(l, proj, par["cw"], par["alog"], par["dtb"], par["gn"])
        wts.mark(f"gdn{l}", o_b)
        o_c = _gmlp_fwd(l, proj, par["glw"], par["glb"], par["ws"], par["bsc"])
        o_d = _conformer_fwd(l, proj, par["dww"], par["dwb"], par["clw"], par["clb"])
        wts.mark(f"conformer{l}", o_d)
        mix = jnp.concatenate([o_a, o_b, o_c, o_d], axis=1)
        w_out.append(wts.get("w_out", l))
        y = _matmul(f"out_proj{l}", mix, w_out[l], "nn", [F32])[0]
        wts.mark(f"y{l}", y)
        x1, h2 = _resid_rms_norm_fwd(f"norm_mix_post_ff_pre{l}", x, y, par["n2"], par["n3"])
        w_ff1.append(wts.get("w_ff1", l))
        u, act = _matmul(f"ff1_{l}", h2, w_ff1[l], "nn", [F32, BF16], epilogue=_relu2)
        wts.mark(f"u{l}", u)
        w_ff2.append(wts.get("w_ff2", l))
        y2 = _matmul(f"ff2_{l}", act, w_ff2[l], "nn", [F32])[0]
        wts.mark(f"ff2_{l}", y2)
        x2 = _resid_rms_fwd(f"norm_ff_post{l}", x1, y2, par["n4"])
        saved.append(dict(par=par, x=x, h=h, proj=proj, st_a=st_a, st_b=st_b, mix=mix, y=y, x1=x1, h2=h2,
                          u=u, act=act, y2=y2))
        x = x2

    dx, loss_acc = _loss_head("loss_head", x, target)
    loss_part = loss_acc[0, 0]

    gs = {k: [None] * DEPTH for k in SMALL if k != "lower_bounds"}
    g_lb = jnp.zeros((DEPTH, GROUP_WIDTH), F32)
    def behind(row, token):
        return row if token is None else row + token[:1, :1]

    def join(a, b):
        return b if a is None else a if b is None else a + b

    tok = None
    for l in reversed(range(DEPTH)):
        sv = saved[l]
        par = sv["par"]
        dy2, dn4 = _resid_rms_bwd(f"norm_ff_post_bwd{l}", sv["y2"], behind(par["n4"], tok), dx)
        du = _matmul(f"ff2_dx{l}", dy2, w_ff2[l], "nt", [BF16], epilogue=_relu2_bwd, extras=(sv["u"],))[0]
        g_ff2 = _matmul(f"ff2_dw{l}", sv["act"], dy2, "tn", [F32], place=_cut_rows(D_FF // N_CHIPS, D_MODEL))[0]
        tok = red.mark(f"ff2_dw{l}", du, g_ff2)
        g_ff1 = _matmul(f"ff1_dw{l}", sv["h2"], du, "tn", [F32], place=_cut_cols(D_MODEL, D_FF // N_CHIPS))[0]
        tok = join(tok, red.mark(f"ff1_dw{l}", du, g_ff1))
        dh2 = _matmul(f"ff1_dx{l}", du, w_ff1[l], "nt", [F32])[0]
        dx1, dn3 = _rms_bwd(f"norm_ff_pre_bwd{l}", sv["x1"], behind(par["n3"], tok), dh2, dx)
        dy, dn2 = _resid_rms_bwd(f"norm_mix_post_bwd{l}", sv["y"], par["n2"], dx1)
        dmix = _matmul(f"out_proj_dx{l}", dy, w_out[l], "nt", [F32])[0]
        tok = red.mark(f"out_proj_dx{l}", dmix)
        g_out = _matmul(f"out_proj_dw{l}", sv["mix"], dy, "tn", [F32], max_tm=GROUP_WIDTH,
                        place=_cut_rows(GROUP_WIDTH, D_MODEL))[0]
        tok = join(tok, red.mark(f"out_proj_dw{l}", dmix, g_out))
        proj = sv["proj"]
        dp_a, dlb, dhn = _hgrn_bwd(l, proj, par["lbp"], behind(par["hn"], tok), sv["st_a"], dmix)
        dp_b, dp_ab, dcw, dalog, ddtb, dgn = _gdn_bwd(
            l, proj, par["cw"], par["alog"], par["dtb"], par["gn"], sv["st_b"], dmix)
        dp_c, dglw, dglb, dws, dbsc = _gmlp_bwd(l, proj, par["glw"], par["glb"], par["ws"], par["bsc"], dmix)
        dp_d, ddww, ddwb, dclw, dclb = _conformer_bwd(
            l, proj, par["dww"], par["dwb"], par["clw"], par["clb"], dmix)
        tok = red.mark(f"conformer_bwd{l}", dp_d)
        dproj = jnp.concatenate([dp_a, dp_b, dp_c, dp_d, dp_ab], axis=1)
        g_in = _matmul(f"in_proj_dw{l}", sv["h"], dproj, "tn", [F32])[0]
        dh = _matmul(f"in_proj_dx{l}", dproj, w_in[l], "nt", [F32])[0]
        red.mark(f"in_proj_dx{l}", dh)
        dx, dn1 = _rms_bwd(f"norm_mix_pre_bwd{l}", sv["x"], behind(par["n1"], tok), dh, dx1)
        tok = red.layer_done(l, dict(w_in=g_in, w_out=g_out, w_ff1=g_ff1, w_ff2=g_ff2), dx)
        g_lb = g_lb + dlb
        for k, v in dict(
                norm_mix_pre=dn1[0], norm_mix_post=dn2[0], norm_ff_pre=dn3[0], norm_ff_post=dn4[0],
                hgrn_norm_w=dhn[0], gdn_conv_w=dcw, gdn_a_log=dalog[0, N_HEADS:2 * N_HEADS],
                gdn_dt_bias=ddtb[0, N_HEADS:2 * N_HEADS], gdn_norm_w=dgn[0], gmlp_ln_w=dglw[0],
                gmlp_ln_b=dglb[0], gmlp_w_s=dws, gmlp_b_s=dbsc[:, :N_HEADS].T, conv_dw_w=ddww,
                conv_dw_b=ddwb[0], conv_ln_w=dclw[0], conv_ln_b=dclb[0]).items():
            gs[k][l] = v
    small_grads = {k: jnp.stack(v) for k, v in gs.items()}
    small_grads["lower_bounds"] = g_lb
    return loss_part, dx, small_grads


HBM = pl.BlockSpec(memory_space=pl.ANY)


def _place():
    return lax.axis_index("x"), lax.axis_index("y"), lax.axis_index("c")


def _other_chips(x, y):
    chips = [(1 - x, y), (x, 1 - y), (1 - x, 1 - y)]
    return [(px, py, 2 * px + py) for px, py in chips]


SEM = pl.BlockSpec(memory_space=pltpu.SEMAPHORE)
IN_HBM = pl.BlockSpec(memory_space=pltpu.HBM)
EFFECT = pltpu.SideEffectType.DATAFLOW_SIDE_EFFECTING


def _copies_start(name, plan, bufs, after, n=3):
    nb = len(bufs)

    def body(*refs):
        token = refs[-1]
        for started, _ in plan(refs[:nb], refs[nb + 1], refs[nb + 2]):
            started.start()
        token[...] = jnp.zeros(token.shape, token.dtype)

    out = pl.pallas_call(
        body, name=name,
        out_shape=(pltpu.SemaphoreType.DMA((n,)), pltpu.SemaphoreType.DMA((n,)))
        + tuple(pltpu.HBM(b.shape, b.dtype) for b in bufs) + (jax.ShapeDtypeStruct((8, LANES), F32),),
        in_specs=(IN_HBM,) * nb + (HBM,),
        out_specs=(SEM, SEM) + (IN_HBM,) * nb + (pl.BlockSpec(memory_space=pltpu.VMEM),),
        input_output_aliases={i: 2 + i for i in range(nb)},
        compiler_params=pltpu.CompilerParams(has_side_effects=EFFECT),
    )(*[pltpu.with_memory_space_constraint(b, pltpu.HBM) for b in bufs], after)
    return out[0], out[1], list(out[2:2 + nb]), out[-1]


def _copies_wait(name, plan, send_sems, recv_sems, bufs, after):
    nb = len(bufs)

    def body(*refs):
        for started, arriving in plan(refs[:nb], refs[nb], refs[nb + 1]):
            started.wait_send()
            arriving.wait_recv()

    out = pl.pallas_call(
        body, name=name, out_shape=tuple(pltpu.HBM(b.shape, b.dtype) for b in bufs),
        in_specs=(IN_HBM,) * nb + (SEM, SEM, HBM), out_specs=(IN_HBM,) * nb,
        input_output_aliases={i: i for i in range(nb)},
        compiler_params=pltpu.CompilerParams(has_side_effects=EFFECT),
    )(*bufs, send_sems, recv_sems, after)
    return list(out)


def _ici_plan(bufs, send_sems, recv_sems):
    blk, land = bufs
    x, y, c = _place()
    mine = 2 * x + y
    plan = []
    for j, (px, py, k) in enumerate(_other_chips(x, y)):
        def copy(dst, j=j, to=(px, py, c)):
            return pltpu.make_async_remote_copy(
                src_ref=blk.at[c], dst_ref=dst, send_sem=send_sems.at[j], recv_sem=recv_sems.at[j],
                device_id=to, device_id_type=MESH)
        plan.append((copy(land.at[mine, c]), copy(land.at[k, c])))
    return plan


def _d2d_plan(bufs, send_sems, recv_sems):
    (land,) = bufs
    x, y, c = _place()
    plan = []
    for j, (_, _, k) in enumerate(_other_chips(x, y)):
        def copy(layer, j=j, k=k):
            return pltpu.make_async_remote_copy(
                src_ref=land.at[k, c], dst_ref=land.at[k, layer], send_sem=send_sems.at[j],
                recv_sem=recv_sems.at[j], device_id=(x, y, 1 - c), device_id_type=MESH)
        plan.append((copy(c), copy(1 - c)))
    return plan


def _ici_rows_plan(bufs, send_sems, recv_sems):
    blk, land = bufs
    x, y, c = _place()
    mine = 2 * x + y
    half = blk.shape[0] // 2
    rows = pl.ds(c * half, half)
    plan = []
    for j, (px, py, k) in enumerate(_other_chips(x, y)):
        def copy(dst, j=j, to=(px, py, c)):
            return pltpu.make_async_remote_copy(
                src_ref=blk.at[rows], dst_ref=dst, send_sem=send_sems.at[j], recv_sem=recv_sems.at[j],
                device_id=to, device_id_type=MESH)
        plan.append((copy(land.at[mine, rows]), copy(land.at[k, rows])))
    return plan


def _d2d_rows_plan(bufs, send_sems, recv_sems):
    (land,) = bufs
    x, y, c = _place()
    half = land.shape[1] // 2
    plan = []
    for j, (_, _, k) in enumerate(_other_chips(x, y)):
        def copy(part, j=j, k=k):
            return pltpu.make_async_remote_copy(
                src_ref=land.at[k, pl.ds(c * half, half)], dst_ref=land.at[k, pl.ds(part * half, half)],
                send_sem=send_sems.at[j], recv_sem=recv_sems.at[j], device_id=(x, y, 1 - c), device_id_type=MESH)
        plan.append((copy(c), copy(1 - c)))
    return plan


class _GatheredWeights:
    STAGES = {"w_out": ("proj0", "conformer0"), "w_ff1": ("gdn0", "y0"), "w_ff2": ("u0", None),
              "w_in1": ("ff2_0", None)}
    ORDER = ("w_in0", "w_out", "w_ff1", "w_ff2", "w_in1")

    def __init__(self, chip, shards):
        self.chip = chip
        self.blk, self.ici, self.d2d, self.full, self.mats = {}, {}, {}, {}, {}
        token = jnp.zeros((8, LANES), F32)
        for k in self.ORDER:
            by_rows = k.startswith("w_in")
            blk = shards["w_in"][int(k[-1])] if by_rows else shards[k]
            self.blk[k] = blk.astype(BF16)
            land = lax.empty((N_CHIPS,) + self.blk[k].shape, BF16)
            plans = (_ici_rows_plan, _d2d_rows_plan) if by_rows else (_ici_plan, _d2d_plan)
            send, recv, bufs, token = _copies_start(f"gather_{k}_ici", plans[0], [self.blk[k], land], token)
            self.ici[k] = (send, recv, bufs, plans)
        self._hand_over("w_in0", token)
        self._finish("w_in0", token)

    def _hand_over(self, k, after):
        send, recv, bufs, plans = self.ici.pop(k)
        _, land = _copies_wait(f"gather_{k}_ici_done", plans[0], send, recv, bufs, after)
        send, recv, bufs, _ = _copies_start(f"gather_{k}_d2d", plans[1], [land], after)
        self.d2d[k] = (send, recv, bufs, plans)

    def _finish(self, k, after):
        send, recv, bufs, plans = self.d2d.pop(k)
        (land,) = _copies_wait(f"gather_{k}_d2d_done", plans[1], send, recv, bufs, after)
        self.full[k] = lax.dynamic_update_slice(land, self.blk[k][None], (self.chip,) + (0,) * self.blk[k].ndim)

    def mark(self, tag, value):
        for k, (first, second) in self.STAGES.items():
            if tag == first:
                self._hand_over(k, value)
                if second is None:
                    self._finish(k, value)
            elif tag == second:
                self._finish(k, value)

    def get(self, kind, l):
        if (kind, l) not in self.mats:
            a = self.full[f"w_in{l}" if kind == "w_in" else kind]
            if kind == "w_in":
                m = _assemble_in([a[k] for k in range(N_CHIPS)])
            elif kind == "w_ff1":
                m = jnp.concatenate([a[k, l] for k in range(N_CHIPS)], axis=-1)
            else:
                m = a[:, l].reshape(-1, D_MODEL)
            self.mats[kind, l] = m
        return self.mats[kind, l]


def _exchange_plan(bufs, send_sems, recv_sems):
    g, got = bufs
    x, y, c = _place()
    half = got.shape[1]
    cp = pltpu.make_async_remote_copy(
        src_ref=g.at[pl.ds(0, N_CHIPS), pl.ds((1 - c) * half, half)], dst_ref=got, send_sem=send_sems.at[0],
        recv_sem=recv_sems.at[0], device_id=(x, y, 1 - c), device_id_type=MESH)
    return [(cp, cp)]


def _scatter_plan(bufs, send_sems, recv_sems):
    p, rcv = bufs
    x, y, c = _place()
    plan = []
    for j, (px, py, _) in enumerate(_other_chips(x, y)):
        cp = pltpu.make_async_remote_copy(
            src_ref=p.at[j], dst_ref=rcv.at[j], send_sem=send_sems.at[j], recv_sem=recv_sems.at[j],
            device_id=(px, py, c), device_id_type=MESH)
        plan.append((cp, cp))
    return plan


def _share_plan(bufs, send_sems, recv_sems):
    mine, sib = bufs
    x, y, c = _place()
    cp = pltpu.make_async_remote_copy(
        src_ref=mine, dst_ref=sib, send_sem=send_sems.at[0], recv_sem=recv_sems.at[0],
        device_id=(x, y, 1 - c), device_id_type=MESH)
    return [(cp, cp)]


def _gather_small(name, pack):
    def body(p_ref, out_ref, send_sems, recv_sems, local_sem):
        x, y, c = _place()
        me = 4 * x + 2 * y + c
        local = pltpu.make_async_copy(p_ref, out_ref.at[me], local_sem)
        local.start()
        flips = [(fx, fy, fc) for fx in (0, 1) for fy in (0, 1) for fc in (0, 1)][1:]
        peers = [((1 - x) if fx else x, (1 - y) if fy else y, (1 - c) if fc else c) for fx, fy, fc in flips]

        def cp(j, slot, to):
            return pltpu.make_async_remote_copy(
                src_ref=p_ref, dst_ref=out_ref.at[slot], send_sem=send_sems.at[j], recv_sem=recv_sems.at[j],
                device_id=to, device_id_type=MESH)

        sends = [cp(j, me, to) for j, to in enumerate(peers)]
        for s in sends:
            s.start()
        for j, (px, py, pc) in enumerate(peers):
            cp(j, 4 * px + 2 * py + pc, (px, py, pc)).wait_recv()
        for s in sends:
            s.wait_send()
        local.wait()

    return pl.pallas_call(
        body, name=name, out_shape=jax.ShapeDtypeStruct((N_DEV,) + pack.shape, pack.dtype), in_specs=[HBM],
        out_specs=HBM,
        scratch_shapes=[pltpu.SemaphoreType.DMA((7,)), pltpu.SemaphoreType.DMA((7,)), pltpu.SemaphoreType.DMA],
    )(pack)


SLAB_ROWS = 256


I_CORE, I_CHIP, I_PEER = 0, 1, 2


def _peer_chip(pre, j):
    return jnp.where(j == 0, pre[I_PEER][0], jnp.where(j == 1, pre[I_PEER + 1][0], pre[I_PEER + 2][0]))


def _pair_sum(name, g, got, where):
    _, half, c = got.shape
    t = _pick_tile(half, (SLAB_ROWS, 128, 64, 8))
    per = half // t

    def body(i, v, car, pre):
        return [v[0] + v[1]], [], []

    return _seq_call(
        name, body, steps=3 * per, prefetch=where,
        ins=[(g, (None, t, c), lambda i, pre: (_peer_chip(pre, i // per), pre[I_CORE][0] * per + i % per, 0)),
             (got, (None, t, c), lambda i, pre: (_peer_chip(pre, i // per), i % per, 0))],
        outs=[((3, half, c), BF16, (None, t, c), lambda i, pre: (i // per, i % per, 0))])[0]


def _chip_sum(name, g, got, rcv, where):
    _, half, c = got.shape
    t = _pick_tile(half, (SLAB_ROWS, 128, 64, 8))
    per = half // t

    def body(i, v, car, pre):
        acc = v[0] + v[1]
        for part in v[2:]:
            acc = acc + part.astype(F32)
        return [acc], [], []

    return _seq_call(
        name, body, steps=per, prefetch=where,
        ins=[(g, (None, t, c), lambda i, pre: (pre[I_CHIP][0], pre[I_CORE][0] * per + i, 0)),
             (got, (None, t, c), lambda i, pre: (pre[I_CHIP][0], i, 0))]
        + [(rcv, (None, t, c), (lambda i, pre, _j=j: (_j, i, 0))) for j in range(3)],
        outs=[((half, c), F32, (t, c), lambda i, pre: (i, 0))])[0]


def _adamw_math(w, g, m, v):
    m = ADAM_B1 * m + (1.0 - ADAM_B1) * g
    v = ADAM_B2 * v + (1.0 - ADAM_B2) * (g * g)
    m_hat = m / (1.0 - ADAM_B1 ** ADAM_STEP)
    v_hat = v / (1.0 - ADAM_B2 ** ADAM_STEP)
    delta = -ADAM_LR * (m_hat / (jnp.sqrt(v_hat) + ADAM_EPS) + ADAM_WD * w)
    return delta, m, v


def _adamw(name, w, g, m, v):
    n, r, c = w.shape
    t = _pick_tile(r, (SLAB_ROWS, 128, 64, 8))
    per = r // t

    def body(i, vals, car):
        return list(_adamw_math(*vals)), [], []

    blk = lambda a: (a, (None, t, c), lambda i: (i // per, i % per, 0))
    out = ((n, r, c), F32, (None, t, c), lambda i: (i // per, i % per, 0))
    return _seq_call(name, body, steps=n * per, ins=[blk(w), blk(g), blk(m), blk(v)], outs=[out] * 3)


def _adamw_layer(name, layer, w, mine, sib, m, v, where, into):
    n, r, c = w.shape
    half = r // 2
    t = _pick_tile(half, (SLAB_ROWS, 128, 64, 8))
    per = half // t

    def body(i, vals, car, pre):
        g = jnp.where(i // per == pre[I_CORE][0], vals[1], vals[2])
        return [g] + list(_adamw_math(vals[0], g, vals[3], vals[4])), [], []

    of_layer = lambda a: (a, (None, t, c), lambda i, pre: (layer, i, 0))
    halves = lambda a: (a, (t, c), lambda i, pre: (i % per, 0))
    out = ((n, r, c), F32, (None, t, c), lambda i, pre: (layer, i, 0))
    return _seq_call(name, body, steps=2 * per, prefetch=where, into=into,
                     ins=[of_layer(w), halves(mine), halves(sib), of_layer(m), of_layer(v)], outs=[out] * 4)


def _ordered_sum(name, packs):
    n, r, c = packs.shape

    def body(i, v, car):
        acc = v[0][0]
        for k in range(1, n):
            acc = acc + v[0][k]
        return [acc], [], []

    return _seq_call(name, body, steps=1, ins=[_whole(packs)], outs=[((r, c), F32, (r, c), lambda i: (0, 0))])[0]


def _pack(arrays):
    flat = []
    for a in arrays:
        a = a.reshape(-1).astype(F32)
        pad = (-a.shape[0]) % LANES
        flat.append(jnp.pad(a, (0, pad)) if pad else a)
    v = jnp.concatenate(flat)
    pad = (-v.shape[0]) % (64 * LANES)
    if pad:
        v = jnp.pad(v, (0, pad))
    return v.reshape(-1, LANES)


def _unpack(pack, shapes):
    v = pack.reshape(-1)
    out, off = [], 0
    for s in shapes:
        n = math.prod(s)
        out.append(v[off:off + n].reshape(s))
        off += n + ((-n) % LANES)
    return out


GATE_LO = 8 * GROUP_WIDTH
GATE_HI = GATE_LO + 2 * N_HEADS
IN_CUT = D_IN_PROJ // N_CHIPS


def _padded_col(o):
    if o < GATE_LO:
        return o
    return 12 * GROUP_WIDTH + (o - GATE_LO) if o < GATE_HI else o - 2 * N_HEADS


def _runs(lo, hi, breaks):
    cuts = sorted({lo, hi} | {b for b in breaks if lo < b < hi})
    return list(zip(cuts[:-1], cuts[1:]))


def _assemble_in(blocks):
    shard_edges = [IN_CUT * k for k in range(1, N_CHIPS)]
    pieces = []
    for lo, hi in ((0, GATE_LO), (GATE_HI, D_IN_PROJ), (GATE_LO, GATE_HI)):
        for a, b in _runs(lo, hi, shard_edges):
            k = a // IN_CUT
            pieces.append(blocks[k][:, a - IN_CUT * k:b - IN_CUT * k])
    pieces.append(jnp.zeros((blocks[0].shape[0], LANES - 2 * N_HEADS), blocks[0].dtype))
    return jnp.concatenate(pieces, axis=-1)


def _cut_in(padded):
    cuts = []
    for k in range(N_CHIPS):
        runs = _runs(IN_CUT * k, IN_CUT * (k + 1), (GATE_LO, GATE_HI))
        cuts.append(jnp.concatenate([padded[:, _padded_col(a):_padded_col(a) + (b - a)] for a, b in runs], axis=-1))
    return jnp.stack(cuts)


BIG = ("w_in", "w_out", "w_ff1", "w_ff2")


class _GradReducer:
    EARLY = ("w_ff2", "w_ff1")

    def __init__(self, where, state):
        self.where, self.state = where, state
        self.out = {k: () for k in BIG}
        self.flight = {}

    @staticmethod
    def _join(tokens):
        tokens = [t for t in tokens if t is not None]
        return functools.reduce(lambda a, b: a + b, tokens) if tokens else None

    def mark(self, tag, value, grad=None):
        if tag == "ff2_dw0":
            token = self._add_pairs(1, BIG, value)
            return self._join([token, self._swap(0, "w_ff2", grad, token)])
        if tag == "ff1_dw0":
            return self._swap(0, "w_ff1", grad, value)
        if tag == "out_proj_dx0":
            return self._add_pairs(0, self.EARLY, value)
        if tag == "out_proj_dw0":
            return self._swap(0, "w_out", grad, value)
        if tag == "conformer_bwd0":
            return self._join([self._add_chips(1, BIG, value), self._add_pairs(0, ("w_out",), value)])
        if tag == "in_proj_dx0":
            self._finish(1, BIG, value)
        return None

    def layer_done(self, l, grads, after):
        rest = [k for k in BIG if (l, k) not in self.flight]
        token = self._join([self._swap(l, k, grads[k], after) for k in rest])
        if l == 0:
            self._add_pairs(0, rest, after)
            self._add_chips(0, BIG, after)
            self._finish(0, BIG, after)
        return token

    def _swap(self, l, k, g, after):
        if k == "w_in":
            g = _cut_in(g)
        got = lax.empty((N_CHIPS, g.shape[1] // 2, g.shape[2]), F32)
        send, recv, bufs, token = _copies_start(f"swap_halves_{k}{l}", _exchange_plan, [g, got], after, n=1)
        self.flight[l, k] = (send, recv, bufs)
        return token

    def _add_pairs(self, l, ks, after):
        tokens = []
        for k in ks:
            send, recv, bufs = self.flight[l, k]
            g, got = _copies_wait(f"swap_halves_{k}{l}_done", _exchange_plan, send, recv, bufs, after)
            pair = _pair_sum(f"pair_sum_{k}{l}", g, got, self.where)
            send, recv, bufs, token = _copies_start(
                f"scatter_{k}{l}", _scatter_plan, [pair, lax.empty(pair.shape, pair.dtype)], after)
            self.flight[l, k] = (send, recv, bufs, g, got)
            tokens.append(token)
        return self._join(tokens)

    def _add_chips(self, l, ks, after):
        tokens = []
        for k in ks:
            send, recv, bufs, g, got = self.flight[l, k]
            _, rcv = _copies_wait(f"scatter_{k}{l}_done", _scatter_plan, send, recv, bufs, after)
            mine = _chip_sum(f"chip_sum_{k}{l}", g, got, rcv, self.where)
            send, recv, bufs, token = _copies_start(
                f"share_{k}{l}", _share_plan, [mine, lax.empty(mine.shape, mine.dtype)], after, n=1)
            self.flight[l, k] = (send, recv, bufs)
            tokens.append(token)
        return self._join(tokens)

    def _finish(self, l, ks, after):
        for k in ks:
            send, recv, bufs = self.flight[l, k]
            mine, sib = _copies_wait(f"share_{k}{l}_done", _share_plan, send, recv, bufs, after)
            self.out[k] = _adamw_layer(f"adamw_{k}{l}", l, self.state[k][0], mine, sib, self.state[k][1],
                                       self.state[k][2], self.where, self.out[k])


def kernel(x, lower_bounds, norm_mix_pre, norm_mix_post, norm_ff_pre, norm_ff_post, w_in, w_out, hgrn_norm_w, gdn_conv_w, gdn_a_log, gdn_dt_bias, gdn_norm_w, gmlp_ln_w, gmlp_ln_b, gmlp_w_s, gmlp_b_s, conv_dw_w, conv_dw_b, conv_ln_w, conv_ln_b, w_ff1, w_ff2, loss_target, m_lower_bounds, m_norm_mix_pre, m_norm_mix_post, m_norm_ff_pre, m_norm_ff_post, m_w_in, m_w_out, m_hgrn_norm_w, m_gdn_conv_w, m_gdn_a_log, m_gdn_dt_bias, m_gdn_norm_w, m_gmlp_ln_w, m_gmlp_ln_b, m_gmlp_w_s, m_gmlp_b_s, m_conv_dw_w, m_conv_dw_b, m_conv_ln_w, m_conv_ln_b, m_w_ff1, m_w_ff2, v_lower_bounds, v_norm_mix_pre, v_norm_mix_post, v_norm_ff_pre, v_norm_ff_post, v_w_in, v_w_out, v_hgrn_norm_w, v_gdn_conv_w, v_gdn_a_log, v_gdn_dt_bias, v_gdn_norm_w, v_gmlp_ln_w, v_gmlp_ln_b, v_gmlp_w_s, v_gmlp_b_s, v_conv_dw_w, v_conv_dw_b, v_conv_ln_w, v_conv_ln_b, v_w_ff1, v_w_ff2):
    args = dict(locals())
    chip = 2 * lax.axis_index("x") + lax.axis_index("y")
    where = tuple(jnp.asarray(v, jnp.int32).reshape(1)
                  for v in (lax.axis_index("c"), chip, chip ^ 2, chip ^ 1, chip ^ 3))

    wts = _GatheredWeights(chip, dict(w_in=w_in, w_out=w_out, w_ff1=w_ff1, w_ff2=w_ff2))
    cut_shapes = [gdn_conv_w.shape, conv_dw_w.shape]
    cuts = _gather_small("gather_cut_small", _pack([gdn_conv_w, conv_dw_w]))
    cut_parts = [_unpack(cuts[2 * k], cut_shapes) for k in range(N_CHIPS)]
    sp = {k: args[k] for k in SMALL}
    sp["gdn_conv_w"] = jnp.concatenate([p[0] for p in cut_parts], axis=-1)
    sp["conv_dw_w"] = jnp.concatenate([p[1] for p in cut_parts], axis=-1)

    red = _GradReducer(where, {k: (args[k], args["m_" + k], args["v_" + k]) for k in BIG})
    loss_part, grad_x, small_g = _local_step(x[0], loss_target[0], sp, wts, red)

    names = SMALL + ["loss"]
    small_g["loss"] = loss_part.reshape(1)
    shapes = [small_g[k].shape for k in names]
    total = _ordered_sum("sum_small", _gather_small("gather_small", _pack([small_g[k] for k in names])))
    summed = dict(zip(names, _unpack(total, shapes)))
    loss = summed.pop("loss")[0]
    for k, width in (("gdn_conv_w", gdn_conv_w.shape[-1]), ("conv_dw_w", conv_dw_w.shape[-1])):
        summed[k] = lax.dynamic_slice_in_dim(summed[k], chip * width, width, axis=-1)

    grads, deltas, new_m, new_v = {}, {}, {}, {}
    for k in BIG:
        grads[k], deltas[k], new_m[k], new_v[k] = red.out[k]
    local_shapes = [args[k].shape for k in SMALL]
    packs = [_pack([src[k] for k in SMALL]) for src in (
        {k: args[k] for k in SMALL}, summed, {k: args["m_" + k] for k in SMALL}, {k: args["v_" + k] for k in SMALL})]
    d_s, m_s, v_s = _adamw("adamw_small", *[p[None] for p in packs])
    for k, d, mm, vv in zip(SMALL, _unpack(d_s[0], local_shapes), _unpack(m_s[0], local_shapes),
                            _unpack(v_s[0], local_shapes)):
        grads[k], deltas[k], new_m[k], new_v[k] = summed[k], d, mm, vv

    order = ["lower_bounds", "norm_mix_pre", "norm_mix_post", "norm_ff_pre", "norm_ff_post", "w_in", "w_out",
             "hgrn_norm_w", "gdn_conv_w", "gdn_a_log", "gdn_dt_bias", "gdn_norm_w", "gmlp_ln_w", "gmlp_ln_b",
             "gmlp_w_s", "gmlp_b_s", "conv_dw_w", "conv_dw_b", "conv_ln_w", "conv_ln_b", "w_ff1", "w_ff2"]
    return (loss, grad_x[None], *[grads[k] for k in order], *[deltas[k] for k in order],
            *[new_m[k] for k in order], *[new_v[k] for k in order])
```
